```python
import jax, jax.numpy as jnp
from jax import lax
import numpy as np

D_MODEL = 2048
BATCH = 32
SEQ = 256
DEPTH = 2
DEC_BATCH = 8
DEC_SEQ = 1024
PAST_LEN = 512

GRID_W = 64
QBLK = 128
ROPE_BASE = 10000.0
EPS = 1e-6
NEG_INF = -1e30
MLA_HEADS = 8
MLA_Q_RANK = 512
MLA_KV_RANK = 512
MLA_NOPE = 128
MLA_ROPE = 64
MLA_V = 128
WIN_HEADS = 8
WIN_KV_HEADS = 2
WIN_GROUP = WIN_HEADS // WIN_KV_HEADS
WIN_HEAD_DIM = 128
WINDOW = 128
NAT_HEADS = 8
NAT_HEAD_DIM = 128
NAT_ROWS = 8
NAT_COLS = 16
NAT_BIAS_SIZE = (2 * NAT_ROWS - 1) * (2 * NAT_COLS - 1)
BRANCH_W = 1024
D_FF = 5632
N_EXPERTS = 8
TOP_K = 2
N_DENSE = (DEPTH + 1) // 2
N_MOE = DEPTH // 2
IN_SPLITS = (MLA_Q_RANK, MLA_KV_RANK, MLA_ROPE,
             WIN_HEADS * WIN_HEAD_DIM, WIN_KV_HEADS * WIN_HEAD_DIM, WIN_KV_HEADS * WIN_HEAD_DIM,
             NAT_HEADS * NAT_HEAD_DIM, NAT_HEADS * NAT_HEAD_DIM, NAT_HEADS * NAT_HEAD_DIM,
             D_MODEL, D_MODEL, D_MODEL)
D_IN = sum(IN_SPLITS)

kernel_name = 'hybrid_mla_window_natten_prefix_dit'

F32 = jnp.float32


def rmsnorm(x, g):
    x32 = x.astype(F32)
    y = x32 * lax.rsqrt(jnp.mean(x32 * x32, axis=-1, keepdims=True) + EPS)
    return (y * g.astype(F32)).astype(x.dtype)


def rope_axis(x, pos):
    n = x.shape[-1]
    half = n // 2
    inv = jnp.power(ROPE_BASE, -jnp.arange(half, dtype=F32) * (2.0 / n))
    ang = pos.astype(F32)[:, None] * inv[None, :]
    cos = jnp.cos(ang)[None, :, None, :]
    sin = jnp.sin(ang)[None, :, None, :]
    x32 = x.astype(F32)
    x1, x2 = x32[..., :half], x32[..., half:]
    return jnp.concatenate([x1 * cos - x2 * sin, x2 * cos + x1 * sin], axis=-1).astype(x.dtype)


def rope2d(x):
    t = jnp.arange(x.shape[1])
    n = x.shape[-1]
    return jnp.concatenate([rope_axis(x[..., :n // 2], t // GRID_W),
                            rope_axis(x[..., n // 2:], t % GRID_W)], axis=-1)


def softmax_sink(s, sink=None):
    m = jnp.max(s, axis=-1, keepdims=True)
    if sink is not None:
        m = jnp.maximum(m, sink)
    e = jnp.exp(s - m)
    den = jnp.sum(e, axis=-1, keepdims=True)
    if sink is not None:
        den = den + jnp.exp(sink - m)
    return e / den


def dense_attention(q, k, v, sink=None):
    B, Lq, Hk, G, dk = q.shape
    scale = dk ** -0.5
    nb = Lq // QBLK
    qb = jnp.moveaxis(q.reshape(B, nb, QBLK, Hk, G, dk), 1, 0)
    sk = None if sink is None else sink.astype(F32).reshape(1, Hk, G, 1, 1)

    def blk(qi):
        s = jnp.einsum('bqkgd,bpkd->bkgqp', qi, k, preferred_element_type=F32) * scale
        p = softmax_sink(s, sk)
        return jnp.einsum('bkgqp,bpkd->bqkgd', p.astype(v.dtype), v)

    o = lax.map(blk, qb)
    return jnp.moveaxis(o, 0, 1).reshape(B, Lq, -1)


def window_attention(q, k, v, k_ctx, v_ctx, sink):
    B, L, Hk, G, d = q.shape
    scale = d ** -0.5
    nb = L // QBLK
    P = k_ctx.shape[1]
    pad = ((0, 0), (QBLK, QBLK), (0, 0), (0, 0))
    kp = jnp.pad(k, pad)
    vp = jnp.pad(v, pad)
    qb = jnp.moveaxis(q.reshape(B, nb, QBLK, Hk, G, d), 1, 0)
    sk = sink.astype(F32).reshape(1, Hk, G, 1, 1)

    def blk(args):
        b, qi = args
        k_loc = lax.dynamic_slice_in_dim(kp, b * QBLK, 3 * QBLK, axis=1)
        v_loc = lax.dynamic_slice_in_dim(vp, b * QBLK, 3 * QBLK, axis=1)
        qpos = b * QBLK + jnp.arange(QBLK)
        kpos = b * QBLK - QBLK + jnp.arange(3 * QBLK)
        valid = ((kpos[None, :] >= 0) & (kpos[None, :] < L)
                 & (jnp.abs(qpos[:, None] - kpos[None, :]) <= WINDOW))
        s_loc = jnp.einsum('bqkgd,bpkd->bkgqp', qi, k_loc, preferred_element_type=F32) * scale
        s_loc = jnp.where(valid, s_loc, NEG_INF)
        s_ctx = jnp.einsum('bqkgd,bpkd->bkgqp', qi, k_ctx, preferred_element_type=F32) * scale
        p = softmax_sink(jnp.concatenate([s_ctx, s_loc], axis=-1), sk)
        return (jnp.einsum('bkgqp,bpkd->bqkgd', p[..., :P].astype(v.dtype), v_ctx)
                + jnp.einsum('bkgqp,bpkd->bqkgd', p[..., P:].astype(v.dtype), v_loc))

    o = lax.map(blk, (jnp.arange(nb), qb))
    return jnp.moveaxis(o, 0, 1).reshape(B, L, -1)


def neighbourhood_tables(rows):
    kr = min(NAT_ROWS, rows)
    r = jnp.arange(rows)[:, None, None, None]
    c = jnp.arange(GRID_W)[None, :, None, None]
    rs = jnp.clip(r - kr // 2, 0, rows - kr)
    cs = jnp.clip(c - NAT_COLS // 2, 0, GRID_W - NAT_COLS)
    key_row = rs + jnp.arange(kr)[None, None, :, None]
    key_col = cs + jnp.arange(NAT_COLS)[None, None, None, :]
    idx = key_row * GRID_W + key_col
    off = (key_row - r + NAT_ROWS - 1) * (2 * NAT_COLS - 1) + (key_col - c + NAT_COLS - 1)
    n = kr * NAT_COLS
    return idx.reshape(rows, GRID_W, n), jnp.broadcast_to(off, idx.shape).reshape(rows, GRID_W, n)


def neighbourhood_attention(q, k, v, k_ctx, v_ctx, rpb):
    B, L, H, d = q.shape
    scale = d ** -0.5
    rows = L // GRID_W
    P = k_ctx.shape[1]
    idx, off = neighbourhood_tables(rows)
    bias_tab = rpb.reshape(H, -1).astype(F32)

    def row(args):
        r, idx_r, off_r = args
        qi = lax.dynamic_slice_in_dim(q, r * GRID_W, GRID_W, axis=1)
        k_nb = jnp.take(k, idx_r, axis=1)
        v_nb = jnp.take(v, idx_r, axis=1)
        bias = jnp.take(bias_tab, off_r, axis=1)
        s_loc = jnp.einsum('bqhd,bqnhd->bhqn', qi, k_nb, preferred_element_type=F32) * scale + bias[None]
        s_ctx = jnp.einsum('bqhd,bphd->bhqp', qi, k_ctx, preferred_element_type=F32) * scale
        p = softmax_sink(jnp.concatenate([s_ctx, s_loc], axis=-1))
        return (jnp.einsum('bhqp,bphd->bqhd', p[..., :P].astype(v.dtype), v_ctx)
                + jnp.einsum('bhqn,bqnhd->bqhd', p[..., P:].astype(v.dtype), v_nb))

    o = lax.map(row, (jnp.arange(rows), idx, off))
    return jnp.moveaxis(o, 0, 1).reshape(B, L, H * d)


def swiglu(x, wg, wu, wd):
    return (jax.nn.silu(x @ wg) * (x @ wu)) @ wd


def moe(x, w_router, wg, wu, wd):
    logits = (x @ w_router).astype(F32)
    top_v, top_i = lax.top_k(logits, TOP_K)
    top_w = jax.nn.softmax(top_v, axis=-1)
    combine = jnp.sum(jax.nn.one_hot(top_i, N_EXPERTS, dtype=F32) * top_w[..., None], axis=-2)
    y = jnp.zeros_like(x)
    for e in range(N_EXPERTS):
        y = y + combine[..., e:e + 1].astype(x.dtype) * swiglu(x, wg[e], wu[e], wd[e])
    return y


def _modulation(cond, lp):
    m = jax.nn.silu(cond) @ lp['w_mod'] + lp['b_mod']
    return jnp.split(m, 6, axis=-1)


def _mixer_in(h, g, shift, scale, w_in):
    hn = rmsnorm(h, g) * (1 + scale) + shift
    points, acc = [], 0
    for size in IN_SPLITS[:-1]:
        acc += size
        points.append(acc)
    return jnp.split(hn @ w_in, points, axis=-1)


def _mla_query(qd, lp):
    B, L, _ = qd.shape
    return (rmsnorm(qd, lp['mla_q_norm']) @ lp['w_mla_q_up']).reshape(B, L, MLA_HEADS, MLA_NOPE + MLA_ROPE)


def _mla_kv(ckv, krope, w_kv_up):
    B, L, _ = ckv.shape
    kv = (ckv @ w_kv_up).reshape(B, L, MLA_HEADS, MLA_NOPE + MLA_V)
    k = jnp.concatenate([kv[..., :MLA_NOPE],
                         jnp.broadcast_to(krope[:, :, None, :], (B, L, MLA_HEADS, MLA_ROPE))], axis=-1)
    return k, kv[..., MLA_NOPE:]


def _merge(o_a, o_b, o_c, ga, gb, gc, lp):
    m = (jax.nn.sigmoid(ga) * (o_a @ lp['w_br_mla'])
         + jax.nn.sigmoid(gb) * (o_b @ lp['w_br_win'])
         + jax.nn.sigmoid(gc) * (o_c @ lp['w_br_nat']))
    return m @ lp['w_out']


def _channel_sublayer(h, shift, scale, gate, lp):
    hn = rmsnorm(h, lp['norm2']) * (1 + scale) + shift
    if 'w_router' in lp:
        y = moe(hn, lp['w_router'], lp['w_ex_gate'], lp['w_ex_up'], lp['w_ex_down'])
    else:
        y = swiglu(hn, lp['w_ff_gate'], lp['w_ff_up'], lp['w_ff_down'])
    return h + gate * y


def _context_layer(h, c_ctx, lp):
    B, S, _ = h.shape
    sh1, sc1, g1, sh2, sc2, g2 = _modulation(c_ctx, lp)
    qd, kvd, kr, wq, wk, wv, nq, nk, nv, ga, gb, gc = _mixer_in(h, lp['norm1'], sh1, sc1, lp['w_in'])
    q_a = _mla_query(qd, lp)
    ckv = rmsnorm(kvd, lp['mla_kv_norm'])
    k_a, v_a = _mla_kv(ckv, kr, lp['w_mla_kv_up'])
    o_a = dense_attention(q_a[:, :, :, None, :], k_a, v_a)
    k_b = wk.reshape(B, S, WIN_KV_HEADS, WIN_HEAD_DIM)
    v_b = wv.reshape(B, S, WIN_KV_HEADS, WIN_HEAD_DIM)
    o_b = dense_attention(wq.reshape(B, S, WIN_KV_HEADS, WIN_GROUP, WIN_HEAD_DIM), k_b, v_b, lp['win_sink'])
    k_c = nk.reshape(B, S, NAT_HEADS, NAT_HEAD_DIM)
    v_c = nv.reshape(B, S, NAT_HEADS, NAT_HEAD_DIM)
    o_c = dense_attention(nq.reshape(B, S, NAT_HEADS, 1, NAT_HEAD_DIM), k_c, v_c)
    h = h + g1 * _merge(o_a, o_b, o_c, ga, gb, gc, lp)
    h = _channel_sublayer(h, sh2, sc2, g2, lp)
    return h, (ckv, kr, k_b, v_b, k_c, v_c)


def _latent_layer(h, c, cache, lp):
    ck_ckv, ck_kr, ck_wk, ck_wv, ck_nk, ck_nv = cache
    B, L, _ = h.shape
    sh1, sc1, g1, sh2, sc2, g2 = [m[:, None, :] for m in _modulation(c, lp)]
    qd, kvd, kr, wq, wk, wv, nq, nk, nv, ga, gb, gc = _mixer_in(h, lp['norm1'], sh1, sc1, lp['w_in'])
    q_a = _mla_query(qd, lp)
    q_a = jnp.concatenate([q_a[..., :MLA_NOPE], rope2d(q_a[..., MLA_NOPE:])], axis=-1)
    ckv = rmsnorm(kvd, lp['mla_kv_norm'])
    k_lat, v_lat = _mla_kv(ckv, rope2d(kr[:, :, None, :])[:, :, 0, :], lp['w_mla_kv_up'])
    k_ctx, v_ctx = _mla_kv(ck_ckv, ck_kr, lp['w_mla_kv_up'])
    o_a = dense_attention(q_a[:, :, :, None, :], jnp.concatenate([k_ctx, k_lat], axis=1),
                          jnp.concatenate([v_ctx, v_lat], axis=1))
    q_b = rope2d(wq.reshape(B, L, WIN_HEADS, WIN_HEAD_DIM)).reshape(B, L, WIN_KV_HEADS, WIN_GROUP, WIN_HEAD_DIM)
    k_b = rope2d(wk.reshape(B, L, WIN_KV_HEADS, WIN_HEAD_DIM))
    v_b = wv.reshape(B, L, WIN_KV_HEADS, WIN_HEAD_DIM)
    o_b = window_attention(q_b, k_b, v_b, ck_wk, ck_wv, lp['win_sink'])
    o_c = neighbourhood_attention(nq.reshape(B, L, NAT_HEADS, NAT_HEAD_DIM), nk.reshape(B, L, NAT_HEADS, NAT_HEAD_DIM),
                                  nv.reshape(B, L, NAT_HEADS, NAT_HEAD_DIM), ck_nk, ck_nv, lp['nat_rpb'])
    h = h + g1 * _merge(o_a, o_b, o_c, ga, gb, gc, lp)
    return _channel_sublayer(h, sh2, sc2, g2, lp)


def setup_inputs(seed: int = 0) -> dict:
    key = jax.random.key(seed)
    ks = list(jax.random.split(key, 40))
    D, F, E = D_MODEL, D_FF, N_EXPERTS

    def nrm(shape, scale=1.0):
        return jax.random.normal(ks.pop(), shape, F32) * scale

    def gain(shape):
        return 1.0 + nrm(shape, 0.05)

    return {
        'x_prompt': nrm((BATCH, SEQ, D)),
        'x_sample': nrm((DEC_BATCH, DEC_SEQ, D)),
        'cache_mla_ckv': nrm((DEC_BATCH, DEPTH, PAST_LEN, MLA_KV_RANK)),
        'cache_mla_krope': nrm((DEC_BATCH, DEPTH, PAST_LEN, MLA_ROPE)),
        'cache_win_k': nrm((DEC_BATCH, DEPTH, PAST_LEN, WIN_KV_HEADS, WIN_HEAD_DIM)),
        'cache_win_v': nrm((DEC_BATCH, DEPTH, PAST_LEN, WIN_KV_HEADS, WIN_HEAD_DIM)),
        'cache_nat_k': nrm((DEC_BATCH, DEPTH, PAST_LEN, NAT_HEADS, NAT_HEAD_DIM)),
        'cache_nat_v': nrm((DEC_BATCH, DEPTH, PAST_LEN, NAT_HEADS, NAT_HEAD_DIM)),
        'c': nrm((DEC_BATCH, D)),
        'c_ctx': nrm((D,)),
        'w_mod': nrm((DEPTH, D, 6 * D), 0.5 * D ** -0.5),
        'b_mod': nrm((DEPTH, 6 * D), 0.01),
        'norm1': gain((DEPTH, D)),
        'norm2': gain((DEPTH, D)),
        'w_in': nrm((DEPTH, D, D_IN), D ** -0.5),
        'mla_q_norm': gain((DEPTH, MLA_Q_RANK)),
        'mla_kv_norm': gain((DEPTH, MLA_KV_RANK)),
        'w_mla_q_up': nrm((DEPTH, MLA_Q_RANK, MLA_HEADS * (MLA_NOPE + MLA_ROPE)), MLA_Q_RANK ** -0.5),
        'w_mla_kv_up': nrm((DEPTH, MLA_KV_RANK, MLA_HEADS * (MLA_NOPE + MLA_V)), MLA_KV_RANK ** -0.5),
        'win_sink': nrm((DEPTH, WIN_HEADS)),
        'nat_rpb': nrm((DEPTH, NAT_HEADS, 2 * NAT_ROWS - 1, 2 * NAT_COLS - 1), 0.1),
        'w_br_mla': nrm((DEPTH, BRANCH_W, D), BRANCH_W ** -0.5),
        'w_br_win': nrm((DEPTH, BRANCH_W, D), BRANCH_W ** -0.5),
        'w_br_nat': nrm((DEPTH, BRANCH_W, D), BRANCH_W ** -0.5),
        'w_out': nrm((DEPTH, D, D), D ** -0.5),
        'w_ff_gate': nrm((N_DENSE, D, F), D ** -0.5),
        'w_ff_up': nrm((N_DENSE, D, F), D ** -0.5),
        'w_ff_down': nrm((N_DENSE, F, D), F ** -0.5),
        'w_router': nrm((N_MOE, D, E), D ** -0.5),
        'w_ex_gate': nrm((N_MOE, E, D, F), D ** -0.5),
        'w_ex_up': nrm((N_MOE, E, D, F), D ** -0.5),
        'w_ex_down': nrm((N_MOE, E, F, D), F ** -0.5),
        'final_norm': gain((D,)),
    }


def reference(x_prompt, x_sample, cache_mla_ckv, cache_mla_krope, cache_win_k, cache_win_v, cache_nat_k,
              cache_nat_v, c, c_ctx, w_mod, b_mod, norm1, norm2, w_in, mla_q_norm, mla_kv_norm, w_mla_q_up,
              w_mla_kv_up, win_sink, nat_rpb, w_br_mla, w_br_win, w_br_nat, w_out, w_ff_gate, w_ff_up, w_ff_down,
              w_router, w_ex_gate, w_ex_up, w_ex_down, final_norm):
    h_ctx = x_prompt
    h_lat = x_sample
    st = ([], [], [], [], [], [])
    for l in range(DEPTH):
        lp = {'w_mod': w_mod[l], 'b_mod': b_mod[l], 'norm1': norm1[l], 'norm2': norm2[l], 'w_in': w_in[l],
              'mla_q_norm': mla_q_norm[l], 'mla_kv_norm': mla_kv_norm[l], 'w_mla_q_up': w_mla_q_up[l],
              'w_mla_kv_up': w_mla_kv_up[l], 'win_sink': win_sink[l], 'nat_rpb': nat_rpb[l],
              'w_br_mla': w_br_mla[l], 'w_br_win': w_br_win[l], 'w_br_nat': w_br_nat[l], 'w_out': w_out[l]}
        if l % 2 == 0:
            lp['w_ff_gate'] = w_ff_gate[l // 2]
            lp['w_ff_up'] = w_ff_up[l // 2]
            lp['w_ff_down'] = w_ff_down[l // 2]
        else:
            lp['w_router'] = w_router[l // 2]
            lp['w_ex_gate'] = w_ex_gate[l // 2]
            lp['w_ex_up'] = w_ex_up[l // 2]
            lp['w_ex_down'] = w_ex_down[l // 2]
        h_ctx, layer_state = _context_layer(h_ctx, c_ctx, lp)
        for lst, arr in zip(st, layer_state):
            lst.append(arr)
        cache = (cache_mla_ckv[:, l], cache_mla_krope[:, l], cache_win_k[:, l], cache_win_v[:, l],
                 cache_nat_k[:, l], cache_nat_v[:, l])
        h_lat = _latent_layer(h_lat, c, cache, lp)
    y_prompt = rmsnorm(h_ctx, final_norm)
    y_sample = rmsnorm(h_lat, final_norm)
    state_mla_ckv = jnp.stack(st[0], axis=1)
    state_mla_krope = jnp.stack(st[1], axis=1)
    state_win_k = jnp.stack(st[2], axis=1)
    state_win_v = jnp.stack(st[3], axis=1)
    state_nat_k = jnp.stack(st[4], axis=1)
    state_nat_v = jnp.stack(st[5], axis=1)
    return (y_prompt, y_sample, state_mla_ckv, state_mla_krope, state_win_k, state_win_v, state_nat_k, state_nat_v)
```

```python
import functools

import jax
import jax.numpy as jnp
import numpy as np
from jax import lax
from jax.experimental import pallas as pl
from jax.experimental.pallas import tpu as pltpu

F32 = jnp.float32
BF16 = jnp.bfloat16

D_MODEL = 2048
DEPTH = 2
GRID_W = 64
ROPE_BASE = 10000.0
EPS = 1e-6
NEG_INF = -1e30
MLA_HEADS = 8
MLA_Q_RANK = 512
MLA_KV_RANK = 512
MLA_NOPE = 128
MLA_ROPE = 64
MLA_V = 128
WIN_HEADS = 8
WIN_KV_HEADS = 2
WIN_GROUP = WIN_HEADS // WIN_KV_HEADS
WIN_HEAD_DIM = 128
WINDOW = 128
NAT_HEADS = 8
NAT_HEAD_DIM = 128
NAT_ROWS = 8
NAT_COLS = 16
BRANCH_W = 1024
D_FF = 5632
N_EXPERTS = 8
TOP_K = 2

P_WQ, P_NQ, P_NK, P_NV = 0, 1024, 2048, 3072
P_WK, P_WV = 4096, 4352
P_QD, P_KVD, P_KR = 4608, 5120, 5632
P_COLS = 5760
P_TN = 1152
LANES = 128

VMEM_LIMIT = 56 * 1024 * 1024

ROW_TILE = 1024
FFN_TM = 512
FFN_TF = 512
MOE_TM = 512
GATHER_ROWS = 256
NAT_QROWS = 4
NAT_KROWS = 12


def _cparams(sem):
    return pltpu.CompilerParams(dimension_semantics=sem, vmem_limit_bytes=VMEM_LIMIT)


def _rms(x, g):
    ms = jnp.mean(x * x, axis=-1, keepdims=True)
    return x * lax.rsqrt(ms + EPS) * g


def _dot(a, b):
    return jnp.dot(a, b, preferred_element_type=F32)


def _dot_nt(a, b):
    return lax.dot_general(a, b, (((1,), (1,)), ((), ())), preferred_element_type=F32)


def _rope(x, tab_ref_or_val, shift):
    n = x.shape[-1]
    t = tab_ref_or_val
    c, s_up, s_dn = t[:, 0:n], t[:, n:2 * n], t[:, 2 * n:3 * n]
    up = pltpu.roll(x, n - shift, axis=1)
    dn = pltpu.roll(x, shift, axis=1)
    return x * c + up * s_up + dn * s_dn


def _softmax_pv(s, v, sink=None):
    m = jnp.max(s, axis=-1, keepdims=True)
    if sink is not None:
        m = jnp.maximum(m, sink)
    e = jnp.exp(s - m)
    den = jnp.sum(e, axis=-1, keepdims=True)
    if sink is not None:
        den = den + jnp.exp(sink - m)
    return _dot(e.astype(BF16), v) * (1.0 / den)


def _mod_kernel(c_ref, w_ref, b_ref, o_ref):
    c = c_ref[...]
    s = (c * jax.nn.sigmoid(c)).astype(BF16)
    o_ref[...] = _dot(s, w_ref[...].astype(BF16)) + b_ref[...]


def _modulation(cond, w_mod, b_mod):
    depth, d, n = w_mod.shape
    nc = cond.shape[0]
    tn = 1024
    return pl.pallas_call(
        _mod_kernel,
        grid=(depth, n // tn),
        in_specs=[
            pl.BlockSpec((nc, d), lambda l, j: (0, 0)),
            pl.BlockSpec((None, d, tn), lambda l, j: (l, 0, j)),
            pl.BlockSpec((None, 1, tn), lambda l, j: (l, 0, j)),
        ],
        out_specs=pl.BlockSpec((None, nc, tn), lambda l, j: (l, 0, j)),
        out_shape=jax.ShapeDtypeStruct((depth, nc, n), F32),
        compiler_params=_cparams(("arbitrary", "arbitrary")),
        name="modulation",
    )(cond, w_mod, b_mod.reshape(depth, 1, n))


def _mixer_in_kernel(h_ref, mod_ref, g_ref, w_ref, p_ref, hn_ref):
    @pl.when(pl.program_id(1) == 0)
    def _():
        hn = _rms(h_ref[...], g_ref[...]) * (1.0 + mod_ref[1:2, :]) + mod_ref[0:1, :]
        hn_ref[...] = hn.astype(BF16)

    p_ref[...] = _dot(hn_ref[...], w_ref[...])


def _mixer_in(h, mod, g, w, cond_of_tile):
    rows, d = h.shape
    n = w.shape[1]
    tm, tn = ROW_TILE, P_TN
    return pl.pallas_call(
        _mixer_in_kernel,
        grid=(rows // tm, n // tn),
        in_specs=[
            pl.BlockSpec((tm, d), lambda i, j: (i, 0)),
            pl.BlockSpec((None, 6, d), lambda i, j: (cond_of_tile(i), 0, 0)),
            pl.BlockSpec((1, d), lambda i, j: (0, 0)),
            pl.BlockSpec((d, tn), lambda i, j: (0, j)),
        ],
        out_specs=[
            pl.BlockSpec((tm, tn), lambda i, j: (i, j)),
            pl.BlockSpec((tm, d), lambda i, j: (i, 0)),
        ],
        out_shape=[
            jax.ShapeDtypeStruct((rows, n), F32),
            jax.ShapeDtypeStruct((rows, d), BF16),
        ],
        compiler_params=_cparams(("arbitrary", "arbitrary")),
        name="mixer_in",
    )(h, mod, g, w)


def _ctx_attn_kernel(p_ref, qn_ref, kvn_ref, wq_ref, wkv_ref, sink_ref,
                     o_ref, ckv_ref, kr_ref, kb_ref, vb_ref, kc_ref, vc_ref):
    q = _dot(_rms(p_ref[:, P_QD:P_QD + MLA_Q_RANK], qn_ref[...]).astype(BF16), wq_ref[...])
    ckv = _rms(p_ref[:, P_KVD:P_KVD + MLA_KV_RANK], kvn_ref[...])
    ckv_ref[...] = ckv
    kv = _dot(ckv.astype(BF16), wkv_ref[...]).astype(BF16)
    kr = p_ref[:, P_KR:P_KR + MLA_ROPE]
    kr_ref[...] = kr
    krb = kr.astype(BF16)
    scale_a = (MLA_NOPE + MLA_ROPE) ** -0.5
    nope_w = MLA_HEADS * MLA_NOPE
    for h in range(MLA_HEADS):
        qn = q[:, h * MLA_NOPE:(h + 1) * MLA_NOPE].astype(BF16)
        qr = q[:, nope_w + h * MLA_ROPE:nope_w + (h + 1) * MLA_ROPE].astype(BF16)
        kn = kv[:, h * 256:h * 256 + MLA_NOPE]
        v = kv[:, h * 256 + MLA_NOPE:(h + 1) * 256]
        s = (_dot_nt(qn, kn) + _dot_nt(qr, krb)) * scale_a
        o_ref[:, h * MLA_V:(h + 1) * MLA_V] = _softmax_pv(s, v).astype(BF16)
    kb_ref[...] = p_ref[:, P_WK:P_WK + 256]
    vb_ref[...] = p_ref[:, P_WV:P_WV + 256]
    scale_b = WIN_HEAD_DIM ** -0.5
    for h in range(WIN_HEADS):
        kvh = h // WIN_GROUP
        qh = p_ref[:, P_WQ + h * 128:P_WQ + (h + 1) * 128].astype(BF16)
        k = p_ref[:, P_WK + kvh * 128:P_WK + (kvh + 1) * 128].astype(BF16)
        v = p_ref[:, P_WV + kvh * 128:P_WV + (kvh + 1) * 128].astype(BF16)
        s = _dot_nt(qh, k) * scale_b
        o = _softmax_pv(s, v, sink_ref[0:1, h:h + 1])
        o_ref[:, BRANCH_W + h * 128:BRANCH_W + (h + 1) * 128] = o.astype(BF16)
    kc_ref[...] = p_ref[:, P_NK:P_NK + 1024]
    vc_ref[...] = p_ref[:, P_NV:P_NV + 1024]
    scale_c = NAT_HEAD_DIM ** -0.5
    for h in range(NAT_HEADS):
        qh = p_ref[:, P_NQ + h * 128:P_NQ + (h + 1) * 128].astype(BF16)
        k = p_ref[:, P_NK + h * 128:P_NK + (h + 1) * 128].astype(BF16)
        v = p_ref[:, P_NV + h * 128:P_NV + (h + 1) * 128].astype(BF16)
        s = _dot_nt(qh, k) * scale_c
        o_ref[:, 2 * BRANCH_W + h * 128:2 * BRANCH_W + (h + 1) * 128] = _softmax_pv(s, v).astype(BF16)


def _ctx_attention(p, nb, seq, qn, kvn, wq, wkv, sink):
    full = lambda a: pl.BlockSpec(a.shape, lambda b: (0,) * a.ndim)
    row = lambda w: pl.BlockSpec((seq, w), lambda b: (b, 0))
    widths = (3 * BRANCH_W, MLA_KV_RANK, MLA_ROPE, 256, 256, 1024, 1024)
    dtypes = (BF16, F32, F32, F32, F32, F32, F32)
    return pl.pallas_call(
        _ctx_attn_kernel,
        grid=(nb,),
        in_specs=[pl.BlockSpec((seq, P_COLS), lambda b: (b, 0)),
                  full(qn), full(kvn), full(wq), full(wkv), full(sink)],
        out_specs=[row(w) for w in widths],
        out_shape=[jax.ShapeDtypeStruct((nb * seq, w), dt) for w, dt in zip(widths, dtypes)],
        compiler_params=_cparams(("arbitrary",)),
        name="ctx_attention",
    )(p, qn, kvn, wq, wkv, sink)


MLA_QT = 256


def _lat_mla_kernel(p_ref, cckv_ref, ckr_ref, qn_ref, kvn_ref, wq_ref, wkv_ref, tq_ref, tk_ref,
                    o_ref, kv_scr, kr_scr, *, past, seq):
    qt = pl.program_id(1)

    @pl.when(qt == 0)
    def _():
        kv_scr[0:past, :] = _dot(cckv_ref[...].astype(BF16), wkv_ref[...]).astype(BF16)
        kr_scr[0:past, :] = ckr_ref[...].astype(BF16)
        for c in range(seq // MLA_QT):
            r0 = c * MLA_QT
            ckv = _rms(p_ref[r0:r0 + MLA_QT, MLA_Q_RANK:MLA_Q_RANK + MLA_KV_RANK], kvn_ref[...])
            kv_scr[past + r0:past + r0 + MLA_QT, :] = _dot(ckv.astype(BF16), wkv_ref[...]).astype(BF16)
            krp = p_ref[r0:r0 + MLA_QT, 2 * MLA_Q_RANK:2 * MLA_Q_RANK + LANES]
            kr = _rope(krp, tk_ref[r0:r0 + MLA_QT, :], MLA_ROPE // 4)
            kr_scr[past + r0:past + r0 + MLA_QT, :] = kr[:, 0:MLA_ROPE].astype(BF16)

    r0 = pl.multiple_of(qt * MLA_QT, MLA_QT)
    qd = p_ref[pl.ds(r0, MLA_QT), 0:MLA_Q_RANK]
    q = _dot(_rms(qd, qn_ref[...]).astype(BF16), wq_ref[...])
    nope_w = MLA_HEADS * MLA_NOPE
    q_rope = _rope(q[:, nope_w:], tq_ref[pl.ds(r0, MLA_QT), :], MLA_ROPE // 4)
    scale = (MLA_NOPE + MLA_ROPE) ** -0.5
    krb = kr_scr[...]
    for h in range(MLA_HEADS):
        qn = q[:, h * MLA_NOPE:(h + 1) * MLA_NOPE].astype(BF16)
        qr = q_rope[:, h * MLA_ROPE:(h + 1) * MLA_ROPE].astype(BF16)
        kn = kv_scr[:, h * 256:h * 256 + MLA_NOPE]
        v = kv_scr[:, h * 256 + MLA_NOPE:(h + 1) * 256]
        s = (_dot_nt(qn, kn) + _dot_nt(qr, krb)) * scale
        o_ref[:, h * MLA_V:(h + 1) * MLA_V] = _softmax_pv(s, v).astype(BF16)


def _lat_mla(p, row_blk0, nb, seq, cache_ckv, cache_kr, layer, qn, kvn, wq, wkv, tq, tk):
    past = cache_ckv.shape[2]
    full = lambda a: pl.BlockSpec(a.shape, lambda b, t: (0,) * a.ndim)
    nqt = seq // MLA_QT
    return pl.pallas_call(
        functools.partial(_lat_mla_kernel, past=past, seq=seq),
        grid=(nb, nqt),
        in_specs=[
            pl.BlockSpec((seq, P_TN), lambda b, t: (row_blk0 + b, P_QD // P_TN)),
            pl.BlockSpec((None, None, past, MLA_KV_RANK), lambda b, t: (b, layer, 0, 0)),
            pl.BlockSpec((None, None, past, MLA_ROPE), lambda b, t: (b, layer, 0, 0)),
            full(qn), full(kvn), full(wq), full(wkv), full(tq), full(tk),
        ],
        out_specs=pl.BlockSpec((MLA_QT, BRANCH_W), lambda b, t: (b * nqt + t, 0)),
        out_shape=jax.ShapeDtypeStruct((nb * seq, BRANCH_W), BF16),
        scratch_shapes=[pltpu.VMEM((past + seq, MLA_HEADS * 256), BF16),
                        pltpu.VMEM((past + seq, MLA_ROPE), BF16)],
        compiler_params=_cparams(("arbitrary", "arbitrary")),
        name="lat_mla",
    )(p, cache_ckv, cache_kr, qn, kvn, wq, wkv, tq, tk)


WIN_QB = 128


def _lat_win_kernel(q_ref, k0_ref, k1_ref, k2_ref, v0_ref, v1_ref, v2_ref, ck_ref, cv_ref,
                    t0_ref, t1_ref, t2_ref, sink_ref, o_ref, *, past, seq):
    qb = pl.program_id(1)
    scale = WIN_HEAD_DIM ** -0.5
    n_loc = 3 * WIN_QB
    rows = WIN_GROUP * WIN_QB
    qpos = qb * WIN_QB + lax.broadcasted_iota(jnp.int32, (rows, n_loc), 0) % WIN_QB
    kpos = (qb - 1) * WIN_QB + lax.broadcasted_iota(jnp.int32, (rows, n_loc), 1)
    valid = (kpos >= 0) & (kpos < seq) & (jnp.abs(qpos - kpos) <= WINDOW)
    t1 = t1_ref[...]
    for kvh in range(WIN_KV_HEADS):
        cs = slice(kvh * 128, (kvh + 1) * 128)
        k_loc = jnp.concatenate([
            _rope(k0_ref[:, cs], t0_ref[...], 32),
            _rope(k1_ref[:, cs], t1, 32),
            _rope(k2_ref[:, cs], t2_ref[...], 32)], axis=0).astype(BF16)
        v_loc = jnp.concatenate([v0_ref[:, cs], v1_ref[:, cs], v2_ref[:, cs]], axis=0).astype(BF16)
        k_ctx = ck_ref[:, cs].astype(BF16)
        v_ctx = cv_ref[:, cs].astype(BF16)
        heads = range(kvh * WIN_GROUP, (kvh + 1) * WIN_GROUP)
        q = jnp.concatenate([_rope(q_ref[:, h * 128:(h + 1) * 128], t1, 32) for h in heads],
                            axis=0).astype(BF16)
        sink = jnp.concatenate([jnp.broadcast_to(sink_ref[0:1, h:h + 1], (WIN_QB, 1)) for h in heads], axis=0)
        s_ctx = _dot_nt(q, k_ctx) * scale
        s_loc = jnp.where(valid, _dot_nt(q, k_loc) * scale, NEG_INF)
        m = jnp.maximum(jnp.maximum(jnp.max(s_ctx, axis=-1, keepdims=True),
                                    jnp.max(s_loc, axis=-1, keepdims=True)), sink)
        e_ctx = jnp.exp(s_ctx - m)
        e_loc = jnp.exp(s_loc - m)
        den = (jnp.sum(e_ctx, axis=-1, keepdims=True) + jnp.sum(e_loc, axis=-1, keepdims=True)
               + jnp.exp(sink - m))
        o = (_dot(e_ctx.astype(BF16), v_ctx) + _dot(e_loc.astype(BF16), v_loc)) * (1.0 / den)
        for g, h in enumerate(heads):
            o_ref[:, h * 128:(h + 1) * 128] = o[g * WIN_QB:(g + 1) * WIN_QB, :].astype(BF16)


def _lat_win(p, row0, nb, seq, cache_k, cache_v, layer, tab, sink):
    past = cache_k.shape[2]
    nqb = seq // WIN_QB
    rb0 = row0 // WIN_QB

    def kblk(off, col_blk):
        return pl.BlockSpec(
            (WIN_QB, 256),
            lambda b, t: (rb0 + b * nqb + jnp.clip(t + off, 0, nqb - 1), col_blk))

    def tblk(off):
        return pl.BlockSpec((WIN_QB, 3 * 128), lambda b, t: (jnp.clip(t + off, 0, nqb - 1), 0))

    cache = pl.BlockSpec((None, None, past, 256), lambda b, t: (b, layer, 0, 0))
    return pl.pallas_call(
        functools.partial(_lat_win_kernel, past=past, seq=seq),
        grid=(nb, nqb),
        in_specs=[
            pl.BlockSpec((WIN_QB, 1024), lambda b, t: (rb0 + b * nqb + t, P_WQ // 1024)),
            kblk(-1, P_WK // 256), kblk(0, P_WK // 256), kblk(1, P_WK // 256),
            kblk(-1, P_WV // 256), kblk(0, P_WV // 256), kblk(1, P_WV // 256),
            cache, cache,
            tblk(-1), tblk(0), tblk(1),
            pl.BlockSpec(sink.shape, lambda b, t: (0, 0)),
        ],
        out_specs=pl.BlockSpec((WIN_QB, BRANCH_W), lambda b, t: (b * nqb + t, 0)),
        out_shape=jax.ShapeDtypeStruct((nb * seq, BRANCH_W), BF16),
        compiler_params=_cparams(("arbitrary", "arbitrary")),
        name="lat_window",
    )(p, p, p, p, p, p, p, cache_k, cache_v, tab, tab, tab, sink)


def _lat_nat_kernel(q_ref, k0_ref, k1_ref, k2_ref, v0_ref, v1_ref, v2_ref, ck_ref, cv_ref, bias_ref, o_ref):
    scale = NAT_HEAD_DIM ** -0.5
    for h in range(NAT_HEADS):
        cs = slice(h * 128, (h + 1) * 128)
        q = q_ref[:, cs].astype(BF16)
        k_loc = jnp.concatenate([k0_ref[:, cs], k1_ref[:, cs], k2_ref[:, cs]], axis=0).astype(BF16)
        v_loc = jnp.concatenate([v0_ref[:, cs], v1_ref[:, cs], v2_ref[:, cs]], axis=0).astype(BF16)
        s_ctx = _dot_nt(q, ck_ref[:, cs].astype(BF16)) * scale
        s_loc = _dot_nt(q, k_loc) * scale + bias_ref[h]
        m = jnp.maximum(jnp.max(s_ctx, axis=-1, keepdims=True), jnp.max(s_loc, axis=-1, keepdims=True))
        e_ctx = jnp.exp(s_ctx - m)
        e_loc = jnp.exp(s_loc - m)
        den = jnp.sum(e_ctx, axis=-1, keepdims=True) + jnp.sum(e_loc, axis=-1, keepdims=True)
        o = (_dot(e_ctx.astype(BF16), cv_ref[:, cs].astype(BF16)) + _dot(e_loc.astype(BF16), v_loc)) * (1.0 / den)
        o_ref[:, cs] = o.astype(BF16)


def _nat_key_start(g):
    return g // 2


def _lat_nat(p, row0, nb, seq, cache_k, cache_v, layer, bias):
    past = cache_k.shape[2]
    qrows = NAT_QROWS * GRID_W
    ng = seq // qrows
    rb0 = row0 // qrows

    def kblk(off, col_blk):
        return pl.BlockSpec((qrows, 1024), lambda g, b: (rb0 + b * ng + _nat_key_start(g) + off, col_blk))

    cache = pl.BlockSpec((None, None, past, 1024), lambda g, b: (b, layer, 0, 0))
    return pl.pallas_call(
        _lat_nat_kernel,
        grid=(ng, nb),
        in_specs=[
            pl.BlockSpec((qrows, 1024), lambda g, b: (rb0 + b * ng + g, P_NQ // 1024)),
            kblk(0, P_NK // 1024), kblk(1, P_NK // 1024), kblk(2, P_NK // 1024),
            kblk(0, P_NV // 1024), kblk(1, P_NV // 1024), kblk(2, P_NV // 1024),
            cache, cache,
            pl.BlockSpec((None, NAT_HEADS, qrows, NAT_KROWS * GRID_W), lambda g, b: (g, 0, 0, 0)),
        ],
        out_specs=pl.BlockSpec((qrows, BRANCH_W), lambda g, b: (b * ng + g, 0)),
        out_shape=jax.ShapeDtypeStruct((nb * seq, BRANCH_W), BF16),
        compiler_params=_cparams(("arbitrary", "arbitrary")),
        name="lat_neighbourhood",
    )(p, p, p, p, p, p, p, cache_k, cache_v, bias)


def _nat_bias_table(rpb, seq):
    rows = seq // GRID_W
    kr = min(NAT_ROWS, rows)
    ng = rows // NAT_QROWS
    g = np.arange(ng)[:, None, None]
    q = np.arange(NAT_QROWS * GRID_W)[None, :, None]
    k = np.arange(NAT_KROWS * GRID_W)[None, None, :]
    r = NAT_QROWS * g + q // GRID_W
    c = q % GRID_W
    key_row = NAT_QROWS * (g // 2) + k // GRID_W
    key_col = k % GRID_W
    rs = np.clip(r - kr // 2, 0, rows - kr)
    cs = np.clip(c - NAT_COLS // 2, 0, GRID_W - NAT_COLS)
    valid = (key_row >= rs) & (key_row < rs + kr) & (key_col >= cs) & (key_col < cs + NAT_COLS)
    off = (key_row - r + NAT_ROWS - 1) * (2 * NAT_COLS - 1) + (key_col - c + NAT_COLS - 1)
    off = np.where(valid, off, 0).astype(np.int32)
    tab = rpb.reshape(NAT_HEADS, -1).astype(F32)
    vals = jnp.take(tab, jnp.asarray(off.reshape(-1)), axis=1).reshape((NAT_HEADS,) + off.shape)
    vals = jnp.where(jnp.asarray(valid)[None], vals, NEG_INF)
    return jnp.transpose(vals, (1, 0, 2, 3))


def _merge_kernel(hn_ref, o_ref, wga_ref, wgb_ref, wgc_ref, wba_ref, wbb_ref, wbc_ref, m_ref):
    hn = hn_ref[...]
    acc = None
    for k, (wg, wb) in enumerate(((wga_ref, wba_ref), (wgb_ref, wbb_ref), (wgc_ref, wbc_ref))):
        gate = jax.nn.sigmoid(_dot(hn, wg[...]))
        term = gate * _dot(o_ref[:, k * BRANCH_W:(k + 1) * BRANCH_W], wb[...])
        acc = term if acc is None else acc + term
    m_ref[...] = acc.astype(BF16)


def _merge(hn, o, w_gate, w_br):
    rows, d = hn.shape
    tm, tn = ROW_TILE, 512
    nj = d // tn
    gate = lambda k: pl.BlockSpec((d, tn), lambda i, j: (0, k * nj + j))
    br = lambda k: pl.BlockSpec((None, BRANCH_W, tn), lambda i, j: (k, 0, j))
    return pl.pallas_call(
        _merge_kernel,
        grid=(rows // tm, nj),
        in_specs=[pl.BlockSpec((tm, d), lambda i, j: (i, 0)),
                  pl.BlockSpec((tm, 3 * BRANCH_W), lambda i, j: (i, 0)),
                  gate(0), gate(1), gate(2), br(0), br(1), br(2)],
        out_specs=pl.BlockSpec((tm, tn), lambda i, j: (i, j)),
        out_shape=jax.ShapeDtypeStruct((rows, d), BF16),
        compiler_params=_cparams(("arbitrary", "arbitrary")),
        name="merge",
    )(hn, o, w_gate, w_gate, w_gate, w_br, w_br, w_br)


ROUTE_IDX = 8
ROUTE_W = 10


def _out_proj_kernel(m_ref, w_ref, h_ref, mod_ref, g_ref, *rest, routed):
    if routed:
        wr_ref, h1_ref, hn2_ref, route_ref = rest
    else:
        h1_ref, hn2_ref = rest
    h1 = h_ref[...] + mod_ref[2:3, :] * _dot(m_ref[...], w_ref[...])
    h1_ref[...] = h1
    hn2 = _rms(h1, g_ref[...]) * (1.0 + mod_ref[4:5, :]) + mod_ref[3:4, :]
    hn2_ref[...] = hn2.astype(hn2_ref.dtype)
    if routed:
        logits = _dot(hn2.astype(BF16), wr_ref[...])
        lane = lax.broadcasted_iota(jnp.int32, logits.shape, 1).astype(F32)
        lg = jnp.where(lane < N_EXPERTS, logits, -jnp.inf)
        m1 = jnp.max(lg, axis=-1, keepdims=True)
        i1 = jnp.min(jnp.where(lg == m1, lane, float(LANES)), axis=-1, keepdims=True)
        lg2 = jnp.where(lane == i1, -jnp.inf, lg)
        m2 = jnp.max(lg2, axis=-1, keepdims=True)
        i2 = jnp.min(jnp.where(lg2 == m2, lane, float(LANES)), axis=-1, keepdims=True)
        e2 = jnp.exp(m2 - m1)
        w1 = 1.0 / (1.0 + e2)
        w2 = e2 / (1.0 + e2)
        route = (jnp.where(lane == ROUTE_IDX, i1, 0.0)
                 + jnp.where(lane == ROUTE_IDX + 1, i2, 0.0)
                 + jnp.where(lane == ROUTE_W, w1, 0.0)
                 + jnp.where(lane == ROUTE_W + 1, w2, 0.0))
        route_ref[...] = route


def _out_proj(m, w_out, h, mod, g, cond_of_tile, w_router=None):
    rows, d = h.shape
    tm = 512
    sub = ROW_TILE // tm
    routed = w_router is not None
    in_specs = [
        pl.BlockSpec((tm, d), lambda i: (i, 0)),
        pl.BlockSpec((d, d), lambda i: (0, 0)),
        pl.BlockSpec((tm, d), lambda i: (i, 0)),
        pl.BlockSpec((None, 6, d), lambda i: (cond_of_tile(i // sub), 0, 0)),
        pl.BlockSpec((1, d), lambda i: (0, 0)),
    ]
    out_specs = [pl.BlockSpec((tm, d), lambda i: (i, 0)), pl.BlockSpec((tm, d), lambda i: (i, 0))]
    out_shape = [jax.ShapeDtypeStruct((rows, d), F32),
                 jax.ShapeDtypeStruct((rows, d), F32 if routed else BF16)]
    args = [m, w_out, h, mod, g]
    if routed:
        in_specs.append(pl.BlockSpec((d, LANES), lambda i: (0, 0)))
        out_specs.append(pl.BlockSpec((tm, LANES), lambda i: (i, 0)))
        out_shape.append(jax.ShapeDtypeStruct((rows, LANES), F32))
        args.append(w_router)
    return pl.pallas_call(
        functools.partial(_out_proj_kernel, routed=routed),
        grid=(rows // tm,),
        in_specs=in_specs, out_specs=out_specs, out_shape=out_shape,
        compiler_params=_cparams(("arbitrary",)),
        name="out_proj",
    )(*args)


def _swiglu_chunk(x, wg_ref, wu_ref, wd_ref):
    g = _dot(x, wg_ref[...])
    u = _dot(x, wu_ref[...])
    a = (g * jax.nn.sigmoid(g) * u).astype(BF16)
    return _dot(a, wd_ref[...])


def _ffn_kernel(x_ref, wg_ref, wu_ref, wd_ref, h_ref, mod_ref, o_ref):
    f = pl.program_id(1)
    part = _swiglu_chunk(x_ref[...], wg_ref, wu_ref, wd_ref)

    @pl.when(f == 0)
    def _():
        o_ref[...] = part

    @pl.when(f > 0)
    def _():
        o_ref[...] += part

    @pl.when(f == pl.num_programs(1) - 1)
    def _():
        o_ref[...] = h_ref[...] + mod_ref[5:6, :] * o_ref[...]


def _ffn(x, wg, wu, wd, h, mod, cond_of_tile):
    rows, d = h.shape
    ff = wg.shape[1]
    tm, tf = FFN_TM, FFN_TF
    sub = ROW_TILE // tm
    return pl.pallas_call(
        _ffn_kernel,
        grid=(rows // tm, ff // tf),
        in_specs=[
            pl.BlockSpec((tm, d), lambda i, f: (i, 0)),
            pl.BlockSpec((d, tf), lambda i, f: (0, f)),
            pl.BlockSpec((d, tf), lambda i, f: (0, f)),
            pl.BlockSpec((tf, d), lambda i, f: (f, 0)),
            pl.BlockSpec((tm, d), lambda i, f: (i, 0)),
            pl.BlockSpec((None, 6, d), lambda i, f: (cond_of_tile(i // sub), 0, 0)),
        ],
        out_specs=pl.BlockSpec((tm, d), lambda i, f: (i, 0)),
        out_shape=jax.ShapeDtypeStruct((rows, d), F32),
        compiler_params=_cparams(("arbitrary", "arbitrary")),
        name="ffn_dense",
    )(x, wg, wu, wd, h, mod)


def _gather_kernel(idx_ref, src_ref, o_ref, buf, sem):
    n = buf.shape[0]

    def issue(r, carry):
        t = idx_ref[0, r]
        pltpu.make_async_copy(src_ref.at[pl.ds(t, 1), :], buf.at[pl.ds(r, 1), :], sem).start()
        return carry

    lax.fori_loop(0, n, issue, 0)
    pltpu.make_async_copy(src_ref.at[pl.ds(0, n), :], buf, sem).wait()
    o_ref[...] = buf[...].astype(o_ref.dtype)


def _gather_rows(src, idx, out_dtype):
    n = idx.shape[0]
    d = src.shape[1]
    tg = GATHER_ROWS
    return pl.pallas_call(
        _gather_kernel,
        grid=(n // tg,),
        in_specs=[pl.BlockSpec((None, 1, tg), lambda i: (i, 0, 0), memory_space=pltpu.SMEM),
                  pl.BlockSpec(memory_space=pl.ANY)],
        out_specs=pl.BlockSpec((tg, d), lambda i: (i, 0)),
        out_shape=jax.ShapeDtypeStruct((n, d), out_dtype),
        scratch_shapes=[pltpu.VMEM((tg, d), src.dtype), pltpu.SemaphoreType.DMA(())],
        compiler_params=_cparams(("arbitrary",)),
        name="moe_gather",
    )(idx.reshape(n // tg, 1, tg), src)


def _moe_ffn_kernel(te_ref, tv_ref, x_ref, wg_ref, wu_ref, wd_ref, o_ref):
    i = pl.program_id(0)
    f = pl.program_id(1)
    valid = tv_ref[i] > 0

    @pl.when(valid)
    def _():
        part = _swiglu_chunk(x_ref[...], wg_ref, wu_ref, wd_ref)

        @pl.when(f == 0)
        def _():
            o_ref[...] = part

        @pl.when(f > 0)
        def _():
            o_ref[...] += part

    @pl.when(jnp.logical_not(valid) & (f == 0))
    def _():
        o_ref[...] = jnp.zeros_like(o_ref)


def _moe_ffn(xs, wg, wu, wd, tile_expert, tile_valid):
    n, d = xs.shape
    ff = wg.shape[2]
    tm, tf = MOE_TM, FFN_TF
    nf = ff // tf

    def fidx(i, f, tv):
        return jnp.where(tv[i] > 0, f, nf - 1)

    grid_spec = pltpu.PrefetchScalarGridSpec(
        num_scalar_prefetch=2,
        grid=(n // tm, nf),
        in_specs=[
            pl.BlockSpec((tm, d), lambda i, f, te, tv: (i, 0)),
            pl.BlockSpec((None, d, tf), lambda i, f, te, tv: (te[i], 0, fidx(i, f, tv))),
            pl.BlockSpec((None, d, tf), lambda i, f, te, tv: (te[i], 0, fidx(i, f, tv))),
            pl.BlockSpec((None, tf, d), lambda i, f, te, tv: (te[i], fidx(i, f, tv), 0)),
        ],
        out_specs=pl.BlockSpec((tm, d), lambda i, f, te, tv: (i, 0)),
    )
    return pl.pallas_call(
        _moe_ffn_kernel,
        grid_spec=grid_spec,
        out_shape=jax.ShapeDtypeStruct((n, d), F32),
        compiler_params=_cparams(("arbitrary", "arbitrary")),
        name="moe_ffn",
    )(tile_expert, tile_valid, xs, wg, wu, wd)


def _combine_kernel(p0_ref, p1_ref, ys_ref, route_ref, h_ref, mod_ref, fn_ref, o_ref, buf0, buf1, sem):
    n = buf0.shape[0]

    def issue(r, carry):
        pltpu.make_async_copy(ys_ref.at[pl.ds(p0_ref[0, r], 1), :], buf0.at[pl.ds(r, 1), :], sem.at[0]).start()
        pltpu.make_async_copy(ys_ref.at[pl.ds(p1_ref[0, r], 1), :], buf1.at[pl.ds(r, 1), :], sem.at[1]).start()
        return carry

    lax.fori_loop(0, n, issue, 0)
    pltpu.make_async_copy(ys_ref.at[pl.ds(0, n), :], buf0, sem.at[0]).wait()
    pltpu.make_async_copy(ys_ref.at[pl.ds(0, n), :], buf1, sem.at[1]).wait()
    w0 = route_ref[:, ROUTE_W:ROUTE_W + 1]
    w1 = route_ref[:, ROUTE_W + 1:ROUTE_W + 2]
    y = w0 * buf0[...] + w1 * buf1[...]
    h2 = h_ref[...] + mod_ref[5:6, :] * y
    o_ref[...] = _rms(h2, fn_ref[...])


def _moe_combine(ys, pos0, pos1, route, h, mod, final_norm, cond_of_tile):
    rows, d = h.shape
    tc = GATHER_ROWS
    sub = ROW_TILE // tc
    smem = lambda: pl.BlockSpec((None, 1, tc), lambda i: (i, 0, 0), memory_space=pltpu.SMEM)
    return pl.pallas_call(
        _combine_kernel,
        grid=(rows // tc,),
        in_specs=[smem(), smem(),
                  pl.BlockSpec(memory_space=pl.ANY),
                  pl.BlockSpec((tc, LANES), lambda i: (i, 0)),
                  pl.BlockSpec((tc, d), lambda i: (i, 0)),
                  pl.BlockSpec((None, 6, d), lambda i: (cond_of_tile(i // sub), 0, 0)),
                  pl.BlockSpec((1, d), lambda i: (0, 0))],
        out_specs=pl.BlockSpec((tc, d), lambda i: (i, 0)),
        out_shape=jax.ShapeDtypeStruct((rows, d), F32),
        scratch_shapes=[pltpu.VMEM((tc, d), F32), pltpu.VMEM((tc, d), F32), pltpu.SemaphoreType.DMA((2,))],
        compiler_params=_cparams(("arbitrary",)),
        name="moe_combine",
    )(pos0.reshape(rows // tc, 1, tc), pos1.reshape(rows // tc, 1, tc), ys, route, h, mod, final_norm)


def _dispatch_plan(route, tm):
    rows = route.shape[0]
    ids = route[:, ROUTE_IDX:ROUTE_IDX + TOP_K].astype(jnp.int32)
    flat = ids.reshape(-1)
    onehot = (flat[:, None] == jnp.arange(N_EXPERTS)[None, :]).astype(jnp.int32)
    rank = jnp.sum((jnp.cumsum(onehot, axis=0) - onehot) * onehot, axis=1)
    counts = jnp.sum(onehot, axis=0)
    tiles = (counts + tm - 1) // tm
    tile_end = jnp.cumsum(tiles)
    start = (tile_end - tiles) * tm
    slot = start[flat] + rank
    n_tiles = (rows * TOP_K) // tm + N_EXPERTS
    n_slots = n_tiles * tm
    slot_token = jnp.zeros((n_slots,), jnp.int32).at[slot].set(jnp.arange(rows * TOP_K, dtype=jnp.int32) // TOP_K)
    t = jnp.arange(n_tiles)
    tile_valid = (t < tile_end[-1]).astype(jnp.int32)
    tile_expert = jnp.minimum(jnp.sum((t[:, None] >= tile_end[None, :]).astype(jnp.int32), axis=1), N_EXPERTS - 1)
    last_expert = tile_expert[jnp.maximum(tile_end[-1] - 1, 0)]
    tile_expert = jnp.where(tile_valid > 0, tile_expert, last_expert).astype(jnp.int32)
    pos = slot.reshape(rows, TOP_K)
    return slot_token, tile_expert, tile_valid, pos[:, 0], pos[:, 1]


def _rope_table(seq, n):
    quarter = n // 4
    t = np.arange(seq)
    inv = jnp.power(ROPE_BASE, -jnp.arange(quarter, dtype=F32) * (2.0 / (n // 2)))
    ang_r = jnp.asarray(t // GRID_W, F32)[:, None] * inv[None, :]
    ang_c = jnp.asarray(t % GRID_W, F32)[:, None] * inv[None, :]
    zero = jnp.zeros((seq, quarter), F32)
    cos = jnp.concatenate([jnp.cos(ang_r)] * 2 + [jnp.cos(ang_c)] * 2, axis=1)
    up = jnp.concatenate([-jnp.sin(ang_r), zero, -jnp.sin(ang_c), zero], axis=1)
    dn = jnp.concatenate([zero, jnp.sin(ang_r), zero, jnp.sin(ang_c)], axis=1)
    return cos, up, dn


def _tile_cols(parts, reps, pad_to=None):
    out = [jnp.tile(p, (1, reps)) for p in parts]
    if pad_to is not None:
        out = [jnp.pad(p, ((0, 0), (0, pad_to - p.shape[1]))) for p in out]
    return jnp.concatenate(out, axis=1)


def kernel(x_prompt, x_sample, cache_mla_ckv, cache_mla_krope, cache_win_k, cache_win_v, cache_nat_k, cache_nat_v, c, c_ctx, w_mod, b_mod, norm1, norm2, w_in, mla_q_norm, mla_kv_norm, w_mla_q_up, w_mla_kv_up, win_sink, nat_rpb, w_br_mla, w_br_win, w_br_nat, w_out, w_ff_gate, w_ff_up, w_ff_down, w_router, w_ex_gate, w_ex_up, w_ex_down, final_norm):
    nb_ctx, seq_ctx, d = x_prompt.shape
    nb_lat, seq_lat, _ = x_sample.shape
    past = cache_mla_ckv.shape[2]
    rows_ctx = nb_ctx * seq_ctx
    rows_lat = nb_lat * seq_lat
    assert d == D_MODEL and seq_lat == ROW_TILE and rows_ctx % ROW_TILE == 0
    assert seq_lat // GRID_W == 4 * NAT_QROWS
    assert w_mod.shape[0] == DEPTH == 2
    ctx_tiles = rows_ctx // ROW_TILE

    def cond_of_tile(i):
        return jnp.where(i < ctx_tiles, 0, i - ctx_tiles + 1)

    n_cond = 16
    cond = jnp.zeros((n_cond, d), F32).at[0].set(c_ctx).at[1:1 + nb_lat].set(c)
    mod_all = _modulation(cond, w_mod, b_mod).reshape(DEPTH, n_cond, 6, d)

    splits = np.cumsum([0, MLA_Q_RANK, MLA_KV_RANK, MLA_ROPE, 1024, 256, 256, 1024, 1024, 1024])
    seg = lambda k: w_in[:, :, splits[k]:splits[k + 1]]
    qd_w, kvd_w, kr_w, wq_w, wk_w, wv_w, nq_w, nk_w, nv_w = [seg(k) for k in range(9)]
    w_p = jnp.concatenate(
        [wq_w, nq_w, nk_w, nv_w, wk_w, wv_w, qd_w, kvd_w, kr_w,
         jnp.zeros((DEPTH, d, P_COLS - P_KR - MLA_ROPE), F32)], axis=2).astype(BF16)
    w_gate = w_in[:, :, splits[9]:].astype(BF16)
    wq_up = w_mla_q_up.reshape(DEPTH, MLA_Q_RANK, MLA_HEADS, MLA_NOPE + MLA_ROPE)
    wq_up = jnp.concatenate([wq_up[..., :MLA_NOPE].reshape(DEPTH, MLA_Q_RANK, -1),
                             wq_up[..., MLA_NOPE:].reshape(DEPTH, MLA_Q_RANK, -1)], axis=2).astype(BF16)
    wkv_up = w_mla_kv_up.astype(BF16)
    w_br = jnp.stack([w_br_mla, w_br_win, w_br_nat], axis=1).astype(BF16)
    w_out_b = w_out.astype(BF16)
    w_router_p = jnp.pad(w_router, ((0, 0), (0, 0), (0, LANES - N_EXPERTS))).astype(BF16)

    t128 = jnp.concatenate(_rope_table(seq_lat, 128), axis=1)
    t64 = _rope_table(seq_lat, MLA_ROPE)
    t_q = _tile_cols(t64, MLA_HEADS)
    t_k = _tile_cols(t64, 1, pad_to=LANES)

    ck_win = cache_win_k.reshape(nb_lat, DEPTH, past, 256)
    cv_win = cache_win_v.reshape(nb_lat, DEPTH, past, 256)
    ck_nat = cache_nat_k.reshape(nb_lat, DEPTH, past, 1024)
    cv_nat = cache_nat_v.reshape(nb_lat, DEPTH, past, 1024)

    h = jnp.concatenate([x_prompt.reshape(rows_ctx, d), x_sample.reshape(rows_lat, d)], axis=0)
    states = [[] for _ in range(6)]
    for l in range(DEPTH):
        mod = mod_all[l]
        qn = mla_q_norm[l].reshape(1, -1)
        kvn = mla_kv_norm[l].reshape(1, -1)
        sink = win_sink[l].reshape(1, -1)
        p, hn = _mixer_in(h, mod, norm1[l].reshape(1, d), w_p[l], cond_of_tile)
        o_ctx, *st = _ctx_attention(p, nb_ctx, seq_ctx, qn, kvn, wq_up[l], wkv_up[l], sink)
        for lst, arr in zip(states, st):
            lst.append(arr)
        o_a = _lat_mla(p, rows_ctx // seq_lat, nb_lat, seq_lat, cache_mla_ckv, cache_mla_krope, l,
                       qn, kvn, wq_up[l], wkv_up[l], t_q, t_k)
        o_b = _lat_win(p, rows_ctx, nb_lat, seq_lat, ck_win, cv_win, l, t128, sink)
        o_c = _lat_nat(p, rows_ctx, nb_lat, seq_lat, ck_nat, cv_nat, l, _nat_bias_table(nat_rpb[l], seq_lat))
        o = jnp.concatenate([o_ctx, jnp.concatenate([o_a, o_b, o_c], axis=1)], axis=0)
        m = _merge(hn, o, w_gate[l], w_br[l])
        if l % 2 == 0:
            h1, hn2 = _out_proj(m, w_out_b[l], h, mod, norm2[l].reshape(1, d), cond_of_tile)
            h = _ffn(hn2, w_ff_gate[l // 2].astype(BF16), w_ff_up[l // 2].astype(BF16),
                     w_ff_down[l // 2].astype(BF16), h1, mod, cond_of_tile)
        else:
            h1, hn2, route = _out_proj(m, w_out_b[l], h, mod, norm2[l].reshape(1, d), cond_of_tile,
                                       w_router=w_router_p[l // 2])
            slot_token, tile_expert, tile_valid, pos0, pos1 = _dispatch_plan(route, MOE_TM)
            xs = _gather_rows(hn2, slot_token, BF16)
            ys = _moe_ffn(xs, w_ex_gate[l // 2].astype(BF16), w_ex_up[l // 2].astype(BF16),
                          w_ex_down[l // 2].astype(BF16), tile_expert, tile_valid)
            h = _moe_combine(ys, pos0, pos1, route, h1, mod, final_norm.reshape(1, d), cond_of_tile)

    y_prompt = h[:rows_ctx].reshape(nb_ctx, seq_ctx, d)
    y_sample = h[rows_ctx:].reshape(nb_lat, seq_lat, d)
    st = [jnp.stack(s, axis=1) for s in states]
    shp = lambda a, tail: a.reshape(nb_ctx, seq_ctx, DEPTH, -1).transpose(0, 2, 1, 3).reshape((nb_ctx, DEPTH, seq_ctx) + tail)
    return (y_prompt, y_sample,
            shp(st[0], (MLA_KV_RANK,)), shp(st[1], (MLA_ROPE,)),
            shp(st[2], (WIN_KV_HEADS, WIN_HEAD_DIM)), shp(st[3], (WIN_KV_HEADS, WIN_HEAD_DIM)),
            shp(st[4], (NAT_HEADS, NAT_HEAD_DIM)), shp(st[5], (NAT_HEADS, NAT_HEAD_DIM)))
```

```python
import functools

import jax
import jax.numpy as jnp
import numpy as np
from jax import lax
from jax.experimental import pallas as pl
from jax.experimental.pallas import tpu as pltpu

F32 = jnp.float32
BF16 = jnp.bfloat16

D_MODEL = 2048
DEPTH = 2
GRID_W = 64
ROPE_BASE = 10000.0
EPS = 1e-6
NEG_INF = -1e30
MLA_HEADS = 8
MLA_Q_RANK = 512
MLA_KV_RANK = 512
MLA_NOPE = 128
MLA_ROPE = 64
MLA_V = 128
WIN_HEADS = 8
WIN_KV_HEADS = 2
WIN_GROUP = WIN_HEADS // WIN_KV_HEADS
WIN_HEAD_DIM = 128
WINDOW = 128
NAT_HEADS = 8
NAT_HEAD_DIM = 128
NAT_ROWS = 8
NAT_COLS = 16
BRANCH_W = 1024
D_FF = 5632
N_EXPERTS = 8
TOP_K = 2

P_WQ, P_NQ, P_NK, P_NV = 0, 1024, 2048, 3072
P_WK, P_WV = 4096, 4352
P_QD, P_KVD, P_KR = 4608, 5120, 5632
P_COLS = 5760
P_TN = 1152
LANES = 128

VMEM_LIMIT = 56 * 1024 * 1024

ROW_TILE = 1024
FFN_TM = 512
FFN_TF = 512
MOE_TM = 1024
MOE_SLAB = 512
GATHER_ROWS = 256
NAT_QROWS = 4
NAT_KROWS = 12


def _cparams(sem):
    return pltpu.CompilerParams(dimension_semantics=sem, vmem_limit_bytes=VMEM_LIMIT)


def _rms(x, g):
    ms = jnp.mean(x * x, axis=-1, keepdims=True)
    return x * lax.rsqrt(ms + EPS) * g


def _dot(a, b):
    return jnp.dot(a, b, preferred_element_type=F32)


def _dot_nt(a, b):
    return lax.dot_general(a, b, (((1,), (1,)), ((), ())), preferred_element_type=F32)


def _rope(x, tab_ref_or_val, shift):
    n = x.shape[-1]
    t = tab_ref_or_val
    c, s_up, s_dn = t[:, 0:n], t[:, n:2 * n], t[:, 2 * n:3 * n]
    up = pltpu.roll(x, n - shift, axis=1)
    dn = pltpu.roll(x, shift, axis=1)
    return x * c + up * s_up + dn * s_dn


def _softmax_pv(s, v, sink=None):
    m = jnp.max(s, axis=-1, keepdims=True)
    if sink is not None:
        m = jnp.maximum(m, sink)
    e = jnp.exp(s - m)
    den = jnp.sum(e, axis=-1, keepdims=True)
    if sink is not None:
        den = den + jnp.exp(sink - m)
    return _dot(e.astype(BF16), v) * (1.0 / den)


def _mod_kernel(c_ref, w_ref, b_ref, o_ref):
    c = c_ref[...]
    s = (c * jax.nn.sigmoid(c)).astype(BF16)
    o_ref[...] = _dot(s, w_ref[...].astype(BF16)) + b_ref[...]


def _modulation(cond, w_mod, b_mod):
    depth, d, n = w_mod.shape
    nc = cond.shape[0]
    tn = 1024
    return pl.pallas_call(
        _mod_kernel,
        grid=(depth, n // tn),
        in_specs=[
            pl.BlockSpec((nc, d), lambda l, j: (0, 0)),
            pl.BlockSpec((None, d, tn), lambda l, j: (l, 0, j)),
            pl.BlockSpec((None, 1, tn), lambda l, j: (l, 0, j)),
        ],
        out_specs=pl.BlockSpec((None, nc, tn), lambda l, j: (l, 0, j)),
        out_shape=jax.ShapeDtypeStruct((depth, nc, n), F32),
        compiler_params=_cparams(("arbitrary", "arbitrary")),
        name="modulation",
    )(cond, w_mod, b_mod.reshape(depth, 1, n))


def _mixer_in_kernel(h_ref, mod_ref, g_ref, w_ref, p_ref, hn_ref):
    @pl.when(pl.program_id(1) == 0)
    def _():
        hn = _rms(h_ref[...], g_ref[...]) * (1.0 + mod_ref[1:2, :]) + mod_ref[0:1, :]
        hn_ref[...] = hn.astype(BF16)

    p_ref[...] = _dot(hn_ref[...], w_ref[...])


def _mixer_in(h, mod, g, w, cond_of_tile):
    rows, d = h.shape
    n = w.shape[1]
    tm, tn = ROW_TILE, P_TN
    return pl.pallas_call(
        _mixer_in_kernel,
        grid=(rows // tm, n // tn),
        in_specs=[
            pl.BlockSpec((tm, d), lambda i, j: (i, 0)),
            pl.BlockSpec((None, 6, d), lambda i, j: (cond_of_tile(i), 0, 0)),
            pl.BlockSpec((1, d), lambda i, j: (0, 0)),
            pl.BlockSpec((d, tn), lambda i, j: (0, j)),
        ],
        out_specs=[
            pl.BlockSpec((tm, tn), lambda i, j: (i, j)),
            pl.BlockSpec((tm, d), lambda i, j: (i, 0)),
        ],
        out_shape=[
            jax.ShapeDtypeStruct((rows, n), F32),
            jax.ShapeDtypeStruct((rows, d), BF16),
        ],
        compiler_params=_cparams(("arbitrary", "arbitrary")),
        name="mixer_in",
    )(h, mod, g, w)


def _ctx_attn_kernel(p_ref, qn_ref, kvn_ref, wq_ref, wkv_ref, sink_ref,
                     o_ref, ckv_ref, kr_ref, kb_ref, vb_ref, kc_ref, vc_ref):
    q = _dot(_rms(p_ref[:, P_QD:P_QD + MLA_Q_RANK], qn_ref[...]).astype(BF16), wq_ref[...])
    ckv = _rms(p_ref[:, P_KVD:P_KVD + MLA_KV_RANK], kvn_ref[...])
    ckv_ref[...] = ckv
    kv = _dot(ckv.astype(BF16), wkv_ref[...]).astype(BF16)
    kr = p_ref[:, P_KR:P_KR + MLA_ROPE]
    kr_ref[...] = kr
    krb = kr.astype(BF16)
    scale_a = (MLA_NOPE + MLA_ROPE) ** -0.5
    nope_w = MLA_HEADS * MLA_NOPE
    for h in range(MLA_HEADS):
        qn = q[:, h * MLA_NOPE:(h + 1) * MLA_NOPE].astype(BF16)
        qr = q[:, nope_w + h * MLA_ROPE:nope_w + (h + 1) * MLA_ROPE].astype(BF16)
        kn = kv[:, h * 256:h * 256 + MLA_NOPE]
        v = kv[:, h * 256 + MLA_NOPE:(h + 1) * 256]
        s = (_dot_nt(qn, kn) + _dot_nt(qr, krb)) * scale_a
        o_ref[:, h * MLA_V:(h + 1) * MLA_V] = _softmax_pv(s, v).astype(BF16)
    kb_ref[...] = p_ref[:, P_WK:P_WK + 256]
    vb_ref[...] = p_ref[:, P_WV:P_WV + 256]
    scale_b = WIN_HEAD_DIM ** -0.5
    for h in range(WIN_HEADS):
        kvh = h // WIN_GROUP
        qh = p_ref[:, P_WQ + h * 128:P_WQ + (h + 1) * 128].astype(BF16)
        k = p_ref[:, P_WK + kvh * 128:P_WK + (kvh + 1) * 128].astype(BF16)
        v = p_ref[:, P_WV + kvh * 128:P_WV + (kvh + 1) * 128].astype(BF16)
        s = _dot_nt(qh, k) * scale_b
        o = _softmax_pv(s, v, sink_ref[0:1, h:h + 1])
        o_ref[:, BRANCH_W + h * 128:BRANCH_W + (h + 1) * 128] = o.astype(BF16)
    kc_ref[...] = p_ref[:, P_NK:P_NK + 1024]
    vc_ref[...] = p_ref[:, P_NV:P_NV + 1024]
    scale_c = NAT_HEAD_DIM ** -0.5
    for h in range(NAT_HEADS):
        qh = p_ref[:, P_NQ + h * 128:P_NQ + (h + 1) * 128].astype(BF16)
        k = p_ref[:, P_NK + h * 128:P_NK + (h + 1) * 128].astype(BF16)
        v = p_ref[:, P_NV + h * 128:P_NV + (h + 1) * 128].astype(BF16)
        s = _dot_nt(qh, k) * scale_c
        o_ref[:, 2 * BRANCH_W + h * 128:2 * BRANCH_W + (h + 1) * 128] = _softmax_pv(s, v).astype(BF16)


def _ctx_attention(p, nb, seq, qn, kvn, wq, wkv, sink):
    full = lambda a: pl.BlockSpec(a.shape, lambda b: (0,) * a.ndim)
    row = lambda w: pl.BlockSpec((seq, w), lambda b: (b, 0))
    widths = (3 * BRANCH_W, MLA_KV_RANK, MLA_ROPE, 256, 256, 1024, 1024)
    dtypes = (BF16, F32, F32, F32, F32, F32, F32)
    return pl.pallas_call(
        _ctx_attn_kernel,
        grid=(nb,),
        in_specs=[pl.BlockSpec((seq, P_COLS), lambda b: (b, 0)),
                  full(qn), full(kvn), full(wq), full(wkv), full(sink)],
        out_specs=[row(w) for w in widths],
        out_shape=[jax.ShapeDtypeStruct((nb * seq, w), dt) for w, dt in zip(widths, dtypes)],
        compiler_params=_cparams(("arbitrary",)),
        name="ctx_attention",
    )(p, qn, kvn, wq, wkv, sink)


MLA_QT = 256


def _lat_mla_kernel(p_ref, cckv_ref, ckr_ref, qn_ref, kvn_ref, wq_ref, wkv_ref, tq_ref, tk_ref,
                    o_ref, kv_scr, kr_scr, *, past, seq):
    qt = pl.program_id(1)

    @pl.when(qt == 0)
    def _():
        kv_scr[0:past, :] = _dot(cckv_ref[...].astype(BF16), wkv_ref[...]).astype(BF16)
        kr_scr[0:past, :] = ckr_ref[...].astype(BF16)
        for c in range(seq // MLA_QT):
            r0 = c * MLA_QT
            ckv = _rms(p_ref[r0:r0 + MLA_QT, MLA_Q_RANK:MLA_Q_RANK + MLA_KV_RANK], kvn_ref[...])
            kv_scr[past + r0:past + r0 + MLA_QT, :] = _dot(ckv.astype(BF16), wkv_ref[...]).astype(BF16)
            krp = p_ref[r0:r0 + MLA_QT, 2 * MLA_Q_RANK:2 * MLA_Q_RANK + LANES]
            kr = _rope(krp, tk_ref[r0:r0 + MLA_QT, :], MLA_ROPE // 4)
            kr_scr[past + r0:past + r0 + MLA_QT, :] = kr[:, 0:MLA_ROPE].astype(BF16)

    r0 = pl.multiple_of(qt * MLA_QT, MLA_QT)
    qd = p_ref[pl.ds(r0, MLA_QT), 0:MLA_Q_RANK]
    q = _dot(_rms(qd, qn_ref[...]).astype(BF16), wq_ref[...])
    nope_w = MLA_HEADS * MLA_NOPE
    q_rope = _rope(q[:, nope_w:], tq_ref[pl.ds(r0, MLA_QT), :], MLA_ROPE // 4)
    scale = (MLA_NOPE + MLA_ROPE) ** -0.5
    krb = kr_scr[...]
    for h in range(MLA_HEADS):
        qn = q[:, h * MLA_NOPE:(h + 1) * MLA_NOPE].astype(BF16)
        qr = q_rope[:, h * MLA_ROPE:(h + 1) * MLA_ROPE].astype(BF16)
        kn = kv_scr[:, h * 256:h * 256 + MLA_NOPE]
        v = kv_scr[:, h * 256 + MLA_NOPE:(h + 1) * 256]
        s = (_dot_nt(qn, kn) + _dot_nt(qr, krb)) * scale
        o_ref[:, h * MLA_V:(h + 1) * MLA_V] = _softmax_pv(s, v).astype(BF16)


def _lat_mla(p, row_blk0, nb, seq, cache_ckv, cache_kr, layer, qn, kvn, wq, wkv, tq, tk):
    past = cache_ckv.shape[2]
    full = lambda a: pl.BlockSpec(a.shape, lambda b, t: (0,) * a.ndim)
    nqt = seq // MLA_QT
    return pl.pallas_call(
        functools.partial(_lat_mla_kernel, past=past, seq=seq),
        grid=(nb, nqt),
        in_specs=[
            pl.BlockSpec((seq, P_TN), lambda b, t: (row_blk0 + b, P_QD // P_TN)),
            pl.BlockSpec((None, None, past, MLA_KV_RANK), lambda b, t: (b, layer, 0, 0)),
            pl.BlockSpec((None, None, past, MLA_ROPE), lambda b, t: (b, layer, 0, 0)),
            full(qn), full(kvn), full(wq), full(wkv), full(tq), full(tk),
        ],
        out_specs=pl.BlockSpec((MLA_QT, BRANCH_W), lambda b, t: (b * nqt + t, 0)),
        out_shape=jax.ShapeDtypeStruct((nb * seq, BRANCH_W), BF16),
        scratch_shapes=[pltpu.VMEM((past + seq, MLA_HEADS * 256), BF16),
                        pltpu.VMEM((past + seq, MLA_ROPE), BF16)],
        compiler_params=_cparams(("arbitrary", "arbitrary")),
        name="lat_mla",
    )(p, cache_ckv, cache_kr, qn, kvn, wq, wkv, tq, tk)


WIN_QB = 128


def _lat_win_kernel(q_ref, k0_ref, k1_ref, k2_ref, v0_ref, v1_ref, v2_ref, ck_ref, cv_ref,
                    t0_ref, t1_ref, t2_ref, sink_ref, o_ref, *, past, seq):
    qb = pl.program_id(1)
    scale = WIN_HEAD_DIM ** -0.5
    n_loc = 3 * WIN_QB
    rows = WIN_GROUP * WIN_QB
    qpos = qb * WIN_QB + lax.broadcasted_iota(jnp.int32, (rows, n_loc), 0) % WIN_QB
    kpos = (qb - 1) * WIN_QB + lax.broadcasted_iota(jnp.int32, (rows, n_loc), 1)
    valid = (kpos >= 0) & (kpos < seq) & (jnp.abs(qpos - kpos) <= WINDOW)
    t1 = t1_ref[...]
    for kvh in range(WIN_KV_HEADS):
        cs = slice(kvh * 128, (kvh + 1) * 128)
        k_loc = jnp.concatenate([
            _rope(k0_ref[:, cs], t0_ref[...], 32),
            _rope(k1_ref[:, cs], t1, 32),
            _rope(k2_ref[:, cs], t2_ref[...], 32)], axis=0).astype(BF16)
        v_loc = jnp.concatenate([v0_ref[:, cs], v1_ref[:, cs], v2_ref[:, cs]], axis=0).astype(BF16)
        k_ctx = ck_ref[:, cs].astype(BF16)
        v_ctx = cv_ref[:, cs].astype(BF16)
        heads = range(kvh * WIN_GROUP, (kvh + 1) * WIN_GROUP)
        q = jnp.concatenate([_rope(q_ref[:, h * 128:(h + 1) * 128], t1, 32) for h in heads],
                            axis=0).astype(BF16)
        sink = jnp.concatenate([jnp.broadcast_to(sink_ref[0:1, h:h + 1], (WIN_QB, 1)) for h in heads], axis=0)
        s_ctx = _dot_nt(q, k_ctx) * scale
        s_loc = jnp.where(valid, _dot_nt(q, k_loc) * scale, NEG_INF)
        m = jnp.maximum(jnp.maximum(jnp.max(s_ctx, axis=-1, keepdims=True),
                                    jnp.max(s_loc, axis=-1, keepdims=True)), sink)
        e_ctx = jnp.exp(s_ctx - m)
        e_loc = jnp.exp(s_loc - m)
        den = (jnp.sum(e_ctx, axis=-1, keepdims=True) + jnp.sum(e_loc, axis=-1, keepdims=True)
               + jnp.exp(sink - m))
        o = (_dot(e_ctx.astype(BF16), v_ctx) + _dot(e_loc.astype(BF16), v_loc)) * (1.0 / den)
        for g, h in enumerate(heads):
            o_ref[:, h * 128:(h + 1) * 128] = o[g * WIN_QB:(g + 1) * WIN_QB, :].astype(BF16)


def _lat_win(p, row0, nb, seq, cache_k, cache_v, layer, tab, sink):
    past = cache_k.shape[2]
    nqb = seq // WIN_QB
    rb0 = row0 // WIN_QB

    def kblk(off, col_blk):
        return pl.BlockSpec(
            (WIN_QB, 256),
            lambda b, t: (rb0 + b * nqb + jnp.clip(t + off, 0, nqb - 1), col_blk))

    def tblk(off):
        return pl.BlockSpec((WIN_QB, 3 * 128), lambda b, t: (jnp.clip(t + off, 0, nqb - 1), 0))

    cache = pl.BlockSpec((None, None, past, 256), lambda b, t: (b, layer, 0, 0))
    return pl.pallas_call(
        functools.partial(_lat_win_kernel, past=past, seq=seq),
        grid=(nb, nqb),
        in_specs=[
            pl.BlockSpec((WIN_QB, 1024), lambda b, t: (rb0 + b * nqb + t, P_WQ // 1024)),
            kblk(-1, P_WK // 256), kblk(0, P_WK // 256), kblk(1, P_WK // 256),
            kblk(-1, P_WV // 256), kblk(0, P_WV // 256), kblk(1, P_WV // 256),
            cache, cache,
            tblk(-1), tblk(0), tblk(1),
            pl.BlockSpec(sink.shape, lambda b, t: (0, 0)),
        ],
        out_specs=pl.BlockSpec((WIN_QB, BRANCH_W), lambda b, t: (b * nqb + t, 0)),
        out_shape=jax.ShapeDtypeStruct((nb * seq, BRANCH_W), BF16),
        compiler_params=_cparams(("arbitrary", "arbitrary")),
        name="lat_window",
    )(p, p, p, p, p, p, p, cache_k, cache_v, tab, tab, tab, sink)


def _lat_nat_kernel(q_ref, k0_ref, k1_ref, k2_ref, v0_ref, v1_ref, v2_ref, ck_ref, cv_ref, bias_ref, o_ref):
    scale = NAT_HEAD_DIM ** -0.5
    for h in range(NAT_HEADS):
        cs = slice(h * 128, (h + 1) * 128)
        q = q_ref[:, cs].astype(BF16)
        k_loc = jnp.concatenate([k0_ref[:, cs], k1_ref[:, cs], k2_ref[:, cs]], axis=0).astype(BF16)
        v_loc = jnp.concatenate([v0_ref[:, cs], v1_ref[:, cs], v2_ref[:, cs]], axis=0).astype(BF16)
        s_ctx = _dot_nt(q, ck_ref[:, cs].astype(BF16)) * scale
        s_loc = _dot_nt(q, k_loc) * scale + bias_ref[h]
        m = jnp.maximum(jnp.max(s_ctx, axis=-1, keepdims=True), jnp.max(s_loc, axis=-1, keepdims=True))
        e_ctx = jnp.exp(s_ctx - m)
        e_loc = jnp.exp(s_loc - m)
        den = jnp.sum(e_ctx, axis=-1, keepdims=True) + jnp.sum(e_loc, axis=-1, keepdims=True)
        o = (_dot(e_ctx.astype(BF16), cv_ref[:, cs].astype(BF16)) + _dot(e_loc.astype(BF16), v_loc)) * (1.0 / den)
        o_ref[:, cs] = o.astype(BF16)


def _nat_key_start(g):
    return g // 2


def _lat_nat(p, row0, nb, seq, cache_k, cache_v, layer, bias):
    past = cache_k.shape[2]
    qrows = NAT_QROWS * GRID_W
    ng = seq // qrows
    rb0 = row0 // qrows

    def kblk(off, col_blk):
        return pl.BlockSpec((qrows, 1024), lambda g, b: (rb0 + b * ng + _nat_key_start(g) + off, col_blk))

    cache = pl.BlockSpec((None, None, past, 1024), lambda g, b: (b, layer, 0, 0))
    return pl.pallas_call(
        _lat_nat_kernel,
        grid=(ng, nb),
        in_specs=[
            pl.BlockSpec((qrows, 1024), lambda g, b: (rb0 + b * ng + g, P_NQ // 1024)),
            kblk(0, P_NK // 1024), kblk(1, P_NK // 1024), kblk(2, P_NK // 1024),
            kblk(0, P_NV // 1024), kblk(1, P_NV // 1024), kblk(2, P_NV // 1024),
            cache, cache,
            pl.BlockSpec((None, NAT_HEADS, qrows, NAT_KROWS * GRID_W), lambda g, b: (g, 0, 0, 0)),
        ],
        out_specs=pl.BlockSpec((qrows, BRANCH_W), lambda g, b: (b * ng + g, 0)),
        out_shape=jax.ShapeDtypeStruct((nb * seq, BRANCH_W), BF16),
        compiler_params=_cparams(("arbitrary", "arbitrary")),
        name="lat_neighbourhood",
    )(p, p, p, p, p, p, p, cache_k, cache_v, bias)


def _nat_bias_table(rpb, seq):
    rows = seq // GRID_W
    kr = min(NAT_ROWS, rows)
    ng = rows // NAT_QROWS
    g = np.arange(ng)[:, None, None]
    q = np.arange(NAT_QROWS * GRID_W)[None, :, None]
    k = np.arange(NAT_KROWS * GRID_W)[None, None, :]
    r = NAT_QROWS * g + q // GRID_W
    c = q % GRID_W
    key_row = NAT_QROWS * (g // 2) + k // GRID_W
    key_col = k % GRID_W
    rs = np.clip(r - kr // 2, 0, rows - kr)
    cs = np.clip(c - NAT_COLS // 2, 0, GRID_W - NAT_COLS)
    valid = (key_row >= rs) & (key_row < rs + kr) & (key_col >= cs) & (key_col < cs + NAT_COLS)
    n_dr, n_dc = 2 * NAT_ROWS - 1, 2 * NAT_COLS - 1
    cc = np.arange(GRID_W)
    dc = cc[None, :] - cc[:, None] + NAT_COLS - 1
    oh_c = (dc[None] == np.arange(n_dc)[:, None, None]).astype(np.float32)
    rq = np.arange(NAT_QROWS)
    rk = np.arange(NAT_KROWS)
    dr = (NAT_QROWS * (np.arange(ng)[:, None, None] // 2) + rk[None, None, :]
          - NAT_QROWS * np.arange(ng)[:, None, None] - rq[None, :, None] + NAT_ROWS - 1)
    oh_r = (dr[..., None] == np.arange(n_dr)).astype(np.float32)
    hp = lax.Precision.HIGHEST
    toep = jnp.einsum('had,dck->hack', rpb.astype(F32), jnp.asarray(oh_c), precision=hp)
    vals = jnp.einsum('hack,gqra->ghqcrk', toep, jnp.asarray(oh_r), precision=hp)
    vals = vals.reshape(ng, NAT_HEADS, NAT_QROWS * GRID_W, NAT_KROWS * GRID_W)
    return jnp.where(jnp.asarray(valid)[:, None], vals, NEG_INF)


def _merge_kernel(hn_ref, o_ref, wga_ref, wgb_ref, wgc_ref, wba_ref, wbb_ref, wbc_ref, m_ref):
    hn = hn_ref[...]
    acc = None
    for k, (wg, wb) in enumerate(((wga_ref, wba_ref), (wgb_ref, wbb_ref), (wgc_ref, wbc_ref))):
        gate = jax.nn.sigmoid(_dot(hn, wg[...]))
        term = gate * _dot(o_ref[:, k * BRANCH_W:(k + 1) * BRANCH_W], wb[...])
        acc = term if acc is None else acc + term
    m_ref[...] = acc.astype(BF16)


def _merge(hn, o, w_gate, w_br):
    rows, d = hn.shape
    tm, tn = ROW_TILE, 512
    nj = d // tn
    gate = lambda k: pl.BlockSpec((d, tn), lambda i, j: (0, k * nj + j))
    br = lambda k: pl.BlockSpec((None, BRANCH_W, tn), lambda i, j: (k, 0, j))
    return pl.pallas_call(
        _merge_kernel,
        grid=(rows // tm, nj),
        in_specs=[pl.BlockSpec((tm, d), lambda i, j: (i, 0)),
                  pl.BlockSpec((tm, 3 * BRANCH_W), lambda i, j: (i, 0)),
                  gate(0), gate(1), gate(2), br(0), br(1), br(2)],
        out_specs=pl.BlockSpec((tm, tn), lambda i, j: (i, j)),
        out_shape=jax.ShapeDtypeStruct((rows, d), BF16),
        compiler_params=_cparams(("arbitrary", "arbitrary")),
        name="merge",
    )(hn, o, w_gate, w_gate, w_gate, w_br, w_br, w_br)


ROUTE_IDX = 8
ROUTE_W = 10


def _out_proj_kernel(m_ref, w_ref, h_ref, mod_ref, g_ref, *rest, routed):
    if routed:
        wr_ref, h1_ref, hn2_ref, route_ref = rest
    else:
        h1_ref, hn2_ref = rest
    h1 = h_ref[...] + mod_ref[2:3, :] * _dot(m_ref[...], w_ref[...])
    h1_ref[...] = h1
    hn2 = _rms(h1, g_ref[...]) * (1.0 + mod_ref[4:5, :]) + mod_ref[3:4, :]
    hn2_ref[...] = hn2.astype(hn2_ref.dtype)
    if routed:
        logits = _dot(hn2.astype(BF16), wr_ref[...])
        lane = lax.broadcasted_iota(jnp.int32, logits.shape, 1).astype(F32)
        lg = jnp.where(lane < N_EXPERTS, logits, -jnp.inf)
        m1 = jnp.max(lg, axis=-1, keepdims=True)
        i1 = jnp.min(jnp.where(lg == m1, lane, float(LANES)), axis=-1, keepdims=True)
        lg2 = jnp.where(lane == i1, -jnp.inf, lg)
        m2 = jnp.max(lg2, axis=-1, keepdims=True)
        i2 = jnp.min(jnp.where(lg2 == m2, lane, float(LANES)), axis=-1, keepdims=True)
        e2 = jnp.exp(m2 - m1)
        w1 = 1.0 / (1.0 + e2)
        w2 = e2 / (1.0 + e2)
        route = (jnp.where(lane == ROUTE_IDX, i1, 0.0)
                 + jnp.where(lane == ROUTE_IDX + 1, i2, 0.0)
                 + jnp.where(lane == ROUTE_W, w1, 0.0)
                 + jnp.where(lane == ROUTE_W + 1, w2, 0.0))
        route_ref[...] = route


def _out_proj(m, w_out, h, mod, g, cond_of_tile, w_router=None):
    rows, d = h.shape
    tm = 512
    sub = ROW_TILE // tm
    routed = w_router is not None
    in_specs = [
        pl.BlockSpec((tm, d), lambda i: (i, 0)),
        pl.BlockSpec((d, d), lambda i: (0, 0)),
        pl.BlockSpec((tm, d), lambda i: (i, 0)),
        pl.BlockSpec((None, 6, d), lambda i: (cond_of_tile(i // sub), 0, 0)),
        pl.BlockSpec((1, d), lambda i: (0, 0)),
    ]
    out_specs = [pl.BlockSpec((tm, d), lambda i: (i, 0)), pl.BlockSpec((tm, d), lambda i: (i, 0))]
    out_shape = [jax.ShapeDtypeStruct((rows, d), F32),
                 jax.ShapeDtypeStruct((rows, d), F32 if routed else BF16)]
    args = [m, w_out, h, mod, g]
    if routed:
        in_specs.append(pl.BlockSpec((d, LANES), lambda i: (0, 0)))
        out_specs.append(pl.BlockSpec((tm, LANES), lambda i: (i, 0)))
        out_shape.append(jax.ShapeDtypeStruct((rows, LANES), F32))
        args.append(w_router)
    return pl.pallas_call(
        functools.partial(_out_proj_kernel, routed=routed),
        grid=(rows // tm,),
        in_specs=in_specs, out_specs=out_specs, out_shape=out_shape,
        compiler_params=_cparams(("arbitrary",)),
        name="out_proj",
    )(*args)


def _swiglu_chunk(x, wg_ref, wu_ref, wd_ref):
    g = _dot(x, wg_ref[...])
    u = _dot(x, wu_ref[...])
    a = (g * jax.nn.sigmoid(g) * u).astype(BF16)
    return _dot(a, wd_ref[...])


def _ffn_kernel(x_ref, wg_ref, wu_ref, wd_ref, h_ref, mod_ref, o_ref):
    f = pl.program_id(1)

    @pl.when(f == 0)
    def _():
        o_ref[...] = jnp.zeros_like(o_ref)

    o_ref[...] += _swiglu_chunk(x_ref[...], wg_ref, wu_ref, wd_ref)

    @pl.when(f == pl.num_programs(1) - 1)
    def _():
        o_ref[...] = h_ref[...] + mod_ref[5:6, :] * o_ref[...]


def _ffn(x, wg, wu, wd, h, mod, cond_of_tile):
    rows, d = h.shape
    ff = wg.shape[1]
    tm, tf = FFN_TM, FFN_TF
    sub = ROW_TILE // tm
    return pl.pallas_call(
        _ffn_kernel,
        grid=(rows // tm, ff // tf),
        in_specs=[
            pl.BlockSpec((tm, d), lambda i, f: (i, 0)),
            pl.BlockSpec((d, tf), lambda i, f: (0, f)),
            pl.BlockSpec((d, tf), lambda i, f: (0, f)),
            pl.BlockSpec((tf, d), lambda i, f: (f, 0)),
            pl.BlockSpec((tm, d), lambda i, f: (i, 0)),
            pl.BlockSpec((None, 6, d), lambda i, f: (cond_of_tile(i // sub), 0, 0)),
        ],
        out_specs=pl.BlockSpec((tm, d), lambda i, f: (i, 0)),
        out_shape=jax.ShapeDtypeStruct((rows, d), F32),
        compiler_params=_cparams(("arbitrary", "arbitrary")),
        name="ffn_dense",
    )(x, wg, wu, wd, h, mod)


def _gather_kernel(idx_ref, src_ref, o_ref, buf, sem):
    n = buf.shape[0]

    def issue(r, carry):
        t = idx_ref[0, r]
        pltpu.make_async_copy(src_ref.at[pl.ds(t, 1), :], buf.at[pl.ds(r, 1), :], sem).start()
        return carry

    lax.fori_loop(0, n, issue, 0, unroll=8)
    pltpu.make_async_copy(src_ref.at[pl.ds(0, n), :], buf, sem).wait()
    o_ref[...] = buf[...].astype(o_ref.dtype)


def _gather_rows(src, idx, out_dtype):
    n = idx.shape[0]
    d = src.shape[1]
    tg = GATHER_ROWS
    return pl.pallas_call(
        _gather_kernel,
        grid=(n // tg,),
        in_specs=[pl.BlockSpec((None, 1, tg), lambda i: (i, 0, 0), memory_space=pltpu.SMEM),
                  pl.BlockSpec(memory_space=pl.ANY)],
        out_specs=pl.BlockSpec((tg, d), lambda i: (i, 0)),
        out_shape=jax.ShapeDtypeStruct((n, d), out_dtype),
        scratch_shapes=[pltpu.VMEM((tg, d), src.dtype), pltpu.SemaphoreType.DMA(())],
        compiler_params=_cparams(("arbitrary",)),
        name="moe_gather",
    )(idx.reshape(n // tg, 1, tg), src)


def _moe_ffn_kernel(te_ref, tv_ref, x_ref, wg_ref, wu_ref, wd_ref, o_ref):
    i = pl.program_id(0)
    f = pl.program_id(1)
    rows_used = tv_ref[i]
    for s in range(MOE_TM // MOE_SLAB):
        rs = slice(s * MOE_SLAB, (s + 1) * MOE_SLAB)
        used = rows_used > s * MOE_SLAB

        @pl.when(f == 0)
        def _():
            o_ref[rs, :] = jnp.zeros((MOE_SLAB, o_ref.shape[1]), o_ref.dtype)

        @pl.when(used)
        def _():
            o_ref[rs, :] += _swiglu_chunk(x_ref[rs, :], wg_ref, wu_ref, wd_ref)


def _moe_ffn(xs, wg, wu, wd, tile_expert, tile_valid):
    n, d = xs.shape
    ff = wg.shape[2]
    tm, tf = MOE_TM, FFN_TF
    nf = ff // tf

    def fidx(i, f, tv):
        return jnp.where(tv[i] > 0, f, nf - 1)

    grid_spec = pltpu.PrefetchScalarGridSpec(
        num_scalar_prefetch=2,
        grid=(n // tm, nf),
        in_specs=[
            pl.BlockSpec((tm, d), lambda i, f, te, tv: (i, 0)),
            pl.BlockSpec((None, d, tf), lambda i, f, te, tv: (te[i], 0, fidx(i, f, tv))),
            pl.BlockSpec((None, d, tf), lambda i, f, te, tv: (te[i], 0, fidx(i, f, tv))),
            pl.BlockSpec((None, tf, d), lambda i, f, te, tv: (te[i], fidx(i, f, tv), 0)),
        ],
        out_specs=pl.BlockSpec((tm, d), lambda i, f, te, tv: (i, 0)),
    )
    return pl.pallas_call(
        _moe_ffn_kernel,
        grid_spec=grid_spec,
        out_shape=jax.ShapeDtypeStruct((n, d), F32),
        compiler_params=_cparams(("arbitrary", "arbitrary")),
        name="moe_ffn",
    )(tile_expert, tile_valid, xs, wg, wu, wd)


def _combine_kernel(p0_ref, p1_ref, ys_ref, route_ref, h_ref, mod_ref, fn_ref, o_ref, buf0, buf1, sem):
    n = buf0.shape[0]

    def issue(r, carry):
        pltpu.make_async_copy(ys_ref.at[pl.ds(p0_ref[0, r], 1), :], buf0.at[pl.ds(r, 1), :], sem.at[0]).start()
        pltpu.make_async_copy(ys_ref.at[pl.ds(p1_ref[0, r], 1), :], buf1.at[pl.ds(r, 1), :], sem.at[1]).start()
        return carry

    lax.fori_loop(0, n, issue, 0, unroll=8)
    pltpu.make_async_copy(ys_ref.at[pl.ds(0, n), :], buf0, sem.at[0]).wait()
    pltpu.make_async_copy(ys_ref.at[pl.ds(0, n), :], buf1, sem.at[1]).wait()
    w0 = route_ref[:, ROUTE_W:ROUTE_W + 1]
    w1 = route_ref[:, ROUTE_W + 1:ROUTE_W + 2]
    y = w0 * buf0[...] + w1 * buf1[...]
    h2 = h_ref[...] + mod_ref[5:6, :] * y
    o_ref[...] = _rms(h2, fn_ref[...])


def _moe_combine(ys, pos0, pos1, route, h, mod, final_norm, cond_of_tile):
    rows, d = h.shape
    tc = GATHER_ROWS
    sub = ROW_TILE // tc
    smem = lambda: pl.BlockSpec((None, 1, tc), lambda i: (i, 0, 0), memory_space=pltpu.SMEM)
    return pl.pallas_call(
        _combine_kernel,
        grid=(rows // tc,),
        in_specs=[smem(), smem(),
                  pl.BlockSpec(memory_space=pl.ANY),
                  pl.BlockSpec((tc, LANES), lambda i: (i, 0)),
                  pl.BlockSpec((tc, d), lambda i: (i, 0)),
                  pl.BlockSpec((None, 6, d), lambda i: (cond_of_tile(i // sub), 0, 0)),
                  pl.BlockSpec((1, d), lambda i: (0, 0))],
        out_specs=pl.BlockSpec((tc, d), lambda i: (i, 0)),
        out_shape=jax.ShapeDtypeStruct((rows, d), F32),
        scratch_shapes=[pltpu.VMEM((tc, d), F32), pltpu.VMEM((tc, d), F32), pltpu.SemaphoreType.DMA((2,))],
        compiler_params=_cparams(("arbitrary",)),
        name="moe_combine",
    )(pos0.reshape(rows // tc, 1, tc), pos1.reshape(rows // tc, 1, tc), ys, route, h, mod, final_norm)


def _dispatch_plan(route, tm):
    rows = route.shape[0]
    ids = route[:, ROUTE_IDX:ROUTE_IDX + TOP_K].astype(jnp.int32)
    flat = ids.reshape(-1)
    onehot = (flat[:, None] == jnp.arange(N_EXPERTS)[None, :]).astype(jnp.int32)
    rank = jnp.sum((jnp.cumsum(onehot, axis=0) - onehot) * onehot, axis=1)
    counts = jnp.sum(onehot, axis=0)
    tiles = (counts + tm - 1) // tm
    tile_end = jnp.cumsum(tiles)
    start = (tile_end - tiles) * tm
    slot = start[flat] + rank
    n_tiles = (rows * TOP_K) // tm + N_EXPERTS
    n_slots = n_tiles * tm
    slot_token = jnp.zeros((n_slots,), jnp.int32).at[slot].set(jnp.arange(rows * TOP_K, dtype=jnp.int32) // TOP_K)
    t = jnp.arange(n_tiles)
    tile_expert = jnp.minimum(jnp.sum((t[:, None] >= tile_end[None, :]).astype(jnp.int32), axis=1), N_EXPERTS - 1)
    tile_in_expert = t - (tile_end - tiles)[tile_expert]
    tile_rows = jnp.where(t < tile_end[-1], jnp.clip(counts[tile_expert] - tile_in_expert * tm, 0, tm), 0)
    last_expert = tile_expert[jnp.maximum(tile_end[-1] - 1, 0)]
    tile_expert = jnp.where(tile_rows > 0, tile_expert, last_expert).astype(jnp.int32)
    pos = slot.reshape(rows, TOP_K)
    return slot_token, tile_expert, tile_rows.astype(jnp.int32), pos[:, 0], pos[:, 1]


def _rope_table(seq, n):
    quarter = n // 4
    t = np.arange(seq)
    inv = jnp.power(ROPE_BASE, -jnp.arange(quarter, dtype=F32) * (2.0 / (n // 2)))
    ang_r = jnp.asarray(t // GRID_W, F32)[:, None] * inv[None, :]
    ang_c = jnp.asarray(t % GRID_W, F32)[:, None] * inv[None, :]
    zero = jnp.zeros((seq, quarter), F32)
    cos = jnp.concatenate([jnp.cos(ang_r)] * 2 + [jnp.cos(ang_c)] * 2, axis=1)
    up = jnp.concatenate([-jnp.sin(ang_r), zero, -jnp.sin(ang_c), zero], axis=1)
    dn = jnp.concatenate([zero, jnp.sin(ang_r), zero, jnp.sin(ang_c)], axis=1)
    return cos, up, dn


def _tile_cols(parts, reps, pad_to=None):
    out = [jnp.tile(p, (1, reps)) for p in parts]
    if pad_to is not None:
        out = [jnp.pad(p, ((0, 0), (0, pad_to - p.shape[1]))) for p in out]
    return jnp.concatenate(out, axis=1)


def kernel(x_prompt, x_sample, cache_mla_ckv, cache_mla_krope, cache_win_k, cache_win_v, cache_nat_k, cache_nat_v, c, c_ctx, w_mod, b_mod, norm1, norm2, w_in, mla_q_norm, mla_kv_norm, w_mla_q_up, w_mla_kv_up, win_sink, nat_rpb, w_br_mla, w_br_win, w_br_nat, w_out, w_ff_gate, w_ff_up, w_ff_down, w_router, w_ex_gate, w_ex_up, w_ex_down, final_norm):
    nb_ctx, seq_ctx, d = x_prompt.shape
    nb_lat, seq_lat, _ = x_sample.shape
    past = cache_mla_ckv.shape[2]
    rows_ctx = nb_ctx * seq_ctx
    rows_lat = nb_lat * seq_lat
    assert d == D_MODEL and seq_lat == ROW_TILE and rows_ctx % ROW_TILE == 0
    assert seq_lat // GRID_W == 4 * NAT_QROWS
    assert w_mod.shape[0] == DEPTH == 2
    ctx_tiles = rows_ctx // ROW_TILE

    def cond_of_tile(i):
        return jnp.where(i < ctx_tiles, 0, i - ctx_tiles + 1)

    n_cond = 16
    cond = jnp.zeros((n_cond, d), F32).at[0].set(c_ctx).at[1:1 + nb_lat].set(c)
    mod_all = _modulation(cond, w_mod, b_mod).reshape(DEPTH, n_cond, 6, d)

    splits = np.cumsum([0, MLA_Q_RANK, MLA_KV_RANK, MLA_ROPE, 1024, 256, 256, 1024, 1024, 1024])
    seg = lambda k: w_in[:, :, splits[k]:splits[k + 1]]
    qd_w, kvd_w, kr_w, wq_w, wk_w, wv_w, nq_w, nk_w, nv_w = [seg(k) for k in range(9)]
    w_p = jnp.concatenate(
        [wq_w, nq_w, nk_w, nv_w, wk_w, wv_w, qd_w, kvd_w, kr_w,
         jnp.zeros((DEPTH, d, P_COLS - P_KR - MLA_ROPE), F32)], axis=2).astype(BF16)
    w_gate = w_in[:, :, splits[9]:].astype(BF16)
    wq_up = w_mla_q_up.reshape(DEPTH, MLA_Q_RANK, MLA_HEADS, MLA_NOPE + MLA_ROPE)
    wq_up = jnp.concatenate([wq_up[..., :MLA_NOPE].reshape(DEPTH, MLA_Q_RANK, -1),
                             wq_up[..., MLA_NOPE:].reshape(DEPTH, MLA_Q_RANK, -1)], axis=2).astype(BF16)
    wkv_up = w_mla_kv_up.astype(BF16)
    w_br = jnp.stack([w_br_mla, w_br_win, w_br_nat], axis=1).astype(BF16)
    w_out_b = w_out.astype(BF16)
    w_router_p = jnp.pad(w_router, ((0, 0), (0, 0), (0, LANES - N_EXPERTS))).astype(BF16)

    t128 = jnp.concatenate(_rope_table(seq_lat, 128), axis=1)
    t64 = _rope_table(seq_lat, MLA_ROPE)
    t_q = _tile_cols(t64, MLA_HEADS)
    t_k = _tile_cols(t64, 1, pad_to=LANES)

    ck_win = cache_win_k.reshape(nb_lat, DEPTH, past, 256)
    cv_win = cache_win_v.reshape(nb_lat, DEPTH, past, 256)
    ck_nat = cache_nat_k.reshape(nb_lat, DEPTH, past, 1024)
    cv_nat = cache_nat_v.reshape(nb_lat, DEPTH, past, 1024)

    h = jnp.concatenate([x_prompt.reshape(rows_ctx, d), x_sample.reshape(rows_lat, d)], axis=0)
    states = [[] for _ in range(6)]
    for l in range(DEPTH):
        mod = mod_all[l]
        qn = mla_q_norm[l].reshape(1, -1)
        kvn = mla_kv_norm[l].reshape(1, -1)
        sink = win_sink[l].reshape(1, -1)
        p, hn = _mixer_in(h, mod, norm1[l].reshape(1, d), w_p[l], cond_of_tile)
        o_ctx, *st = _ctx_attention(p, nb_ctx, seq_ctx, qn, kvn, wq_up[l], wkv_up[l], sink)
        for lst, arr in zip(states, st):
            lst.append(arr)
        o_a = _lat_mla(p, rows_ctx // seq_lat, nb_lat, seq_lat, cache_mla_ckv, cache_mla_krope, l,
                       qn, kvn, wq_up[l], wkv_up[l], t_q, t_k)
        o_b = _lat_win(p, rows_ctx, nb_lat, seq_lat, ck_win, cv_win, l, t128, sink)
        o_c = _lat_nat(p, rows_ctx, nb_lat, seq_lat, ck_nat, cv_nat, l, _nat_bias_table(nat_rpb[l], seq_lat))
        o = jnp.concatenate([o_ctx, jnp.concatenate([o_a, o_b, o_c], axis=1)], axis=0)
        m = _merge(hn, o, w_gate[l], w_br[l])
        if l % 2 == 0:
            h1, hn2 = _out_proj(m, w_out_b[l], h, mod, norm2[l].reshape(1, d), cond_of_tile)
            h = _ffn(hn2, w_ff_gate[l // 2].astype(BF16), w_ff_up[l // 2].astype(BF16),
                     w_ff_down[l // 2].astype(BF16), h1, mod, cond_of_tile)
        else:
            h1, hn2, route = _out_proj(m, w_out_b[l], h, mod, norm2[l].reshape(1, d), cond_of_tile,
                                       w_router=w_router_p[l // 2])
            slot_token, tile_expert, tile_valid, pos0, pos1 = _dispatch_plan(route, MOE_TM)
            xs = _gather_rows(hn2, slot_token, BF16)
            ys = _moe_ffn(xs, w_ex_gate[l // 2].astype(BF16), w_ex_up[l // 2].astype(BF16),
                          w_ex_down[l // 2].astype(BF16), tile_expert, tile_valid)
            h = _moe_combine(ys, pos0, pos1, route, h1, mod, final_norm.reshape(1, d), cond_of_tile)

    y_prompt = h[:rows_ctx].reshape(nb_ctx, seq_ctx, d)
    y_sample = h[rows_ctx:].reshape(nb_lat, seq_lat, d)
    st = [jnp.stack(s, axis=1) for s in states]
    shp = lambda a, tail: a.reshape(nb_ctx, seq_ctx, DEPTH, -1).transpose(0, 2, 1, 3).reshape((nb_ctx, DEPTH, seq_ctx) + tail)
    return (y_prompt, y_sample,
            shp(st[0], (MLA_KV_RANK,)), shp(st[1], (MLA_ROPE,)),
            shp(st[2], (WIN_KV_HEADS, WIN_HEAD_DIM)), shp(st[3], (WIN_KV_HEADS, WIN_HEAD_DIM)),
            shp(st[4], (NAT_HEADS, NAT_HEAD_DIM)), shp(st[5], (NAT_HEADS, NAT_HEAD_DIM)))
```

```python
import functools

import jax
import jax.numpy as jnp
import numpy as np
from jax import lax
from jax.experimental import pallas as pl
from jax.experimental.pallas import tpu as pltpu

F32 = jnp.float32
BF16 = jnp.bfloat16

D_MODEL = 2048
DEPTH = 2
GRID_W = 64
ROPE_BASE = 10000.0
EPS = 1e-6
NEG_INF = -1e30
MLA_HEADS = 8
MLA_Q_RANK = 512
MLA_KV_RANK = 512
MLA_NOPE = 128
MLA_ROPE = 64
MLA_V = 128
WIN_HEADS = 8
WIN_KV_HEADS = 2
WIN_GROUP = WIN_HEADS // WIN_KV_HEADS
WIN_HEAD_DIM = 128
WINDOW = 128
NAT_HEADS = 8
NAT_HEAD_DIM = 128
NAT_ROWS = 8
NAT_COLS = 16
BRANCH_W = 1024
D_FF = 5632
N_EXPERTS = 8
TOP_K = 2

P_WQ, P_NQ, P_NK, P_NV = 0, 1024, 2048, 3072
P_WK, P_WV = 4096, 4352
P_QD, P_KVD, P_KR = 4608, 5120, 5632
P_COLS = 5760
P_TN = 1152
LANES = 128

VMEM_LIMIT = 56 * 1024 * 1024

ROW_TILE = 1024
FFN_TM = 512
FFN_TF = 512
MOE_TM = 1024
MOE_SLAB = 512
GATHER_ROWS = 256
NAT_QROWS = 4
NAT_KROWS = 12


def _cparams(sem):
    return pltpu.CompilerParams(dimension_semantics=sem, vmem_limit_bytes=VMEM_LIMIT)


def _rms(x, g):
    ms = jnp.mean(x * x, axis=-1, keepdims=True)
    return x * lax.rsqrt(ms + EPS) * g


def _dot(a, b):
    return jnp.dot(a, b, preferred_element_type=F32)


def _dot_nt(a, b):
    return lax.dot_general(a, b, (((1,), (1,)), ((), ())), preferred_element_type=F32)


def _rope(x, tab_ref_or_val, shift):
    n = x.shape[-1]
    t = tab_ref_or_val
    c, s_up, s_dn = t[:, 0:n], t[:, n:2 * n], t[:, 2 * n:3 * n]
    up = pltpu.roll(x, n - shift, axis=1)
    dn = pltpu.roll(x, shift, axis=1)
    return x * c + up * s_up + dn * s_dn


def _softmax_pv(s, v, sink=None):
    m = jnp.max(s, axis=-1, keepdims=True)
    if sink is not None:
        m = jnp.maximum(m, sink)
    e = jnp.exp(s - m)
    den = jnp.sum(e, axis=-1, keepdims=True)
    if sink is not None:
        den = den + jnp.exp(sink - m)
    return _dot(e.astype(BF16), v) * (1.0 / den)


def _mod_kernel(c_ref, w_ref, b_ref, o_ref):
    c = c_ref[...]
    s = (c * jax.nn.sigmoid(c)).astype(BF16)
    o_ref[...] = _dot(s, w_ref[...].astype(BF16)) + b_ref[...]


def _modulation(cond, w_mod, b_mod):
    depth, d, n = w_mod.shape
    nc = cond.shape[0]
    tn = 1024
    return pl.pallas_call(
        _mod_kernel,
        grid=(depth, n // tn),
        in_specs=[
            pl.BlockSpec((nc, d), lambda l, j: (0, 0)),
            pl.BlockSpec((None, d, tn), lambda l, j: (l, 0, j)),
            pl.BlockSpec((None, 1, tn), lambda l, j: (l, 0, j)),
        ],
        out_specs=pl.BlockSpec((None, nc, tn), lambda l, j: (l, 0, j)),
        out_shape=jax.ShapeDtypeStruct((depth, nc, n), F32),
        compiler_params=_cparams(("arbitrary", "arbitrary")),
        name="modulation",
    )(cond, w_mod, b_mod.reshape(depth, 1, n))


def _mixer_in_kernel(h_ref, mod_ref, g_ref, w_ref, p_ref, hn_ref):
    @pl.when(pl.program_id(1) == 0)
    def _():
        hn = _rms(h_ref[...], g_ref[...]) * (1.0 + mod_ref[1:2, :]) + mod_ref[0:1, :]
        hn_ref[...] = hn.astype(BF16)

    p_ref[...] = _dot(hn_ref[...], w_ref[...])


def _mixer_in(h, mod, g, w, cond_of_tile):
    rows, d = h.shape
    n = w.shape[1]
    tm, tn = ROW_TILE, P_TN
    return pl.pallas_call(
        _mixer_in_kernel,
        grid=(rows // tm, n // tn),
        in_specs=[
            pl.BlockSpec((tm, d), lambda i, j: (i, 0)),
            pl.BlockSpec((None, 6, d), lambda i, j: (cond_of_tile(i), 0, 0)),
            pl.BlockSpec((1, d), lambda i, j: (0, 0)),
            pl.BlockSpec((d, tn), lambda i, j: (0, j)),
        ],
        out_specs=[
            pl.BlockSpec((tm, tn), lambda i, j: (i, j)),
            pl.BlockSpec((tm, d), lambda i, j: (i, 0)),
        ],
        out_shape=[
            jax.ShapeDtypeStruct((rows, n), F32),
            jax.ShapeDtypeStruct((rows, d), BF16),
        ],
        compiler_params=_cparams(("arbitrary", "arbitrary")),
        name="mixer_in",
    )(h, mod, g, w)


def _ctx_attn_kernel(p_ref, qn_ref, kvn_ref, wq_ref, wkv_ref, sink_ref,
                     o_ref, ckv_ref, kr_ref, kb_ref, vb_ref, kc_ref, vc_ref):
    q = _dot(_rms(p_ref[:, P_QD:P_QD + MLA_Q_RANK], qn_ref[...]).astype(BF16), wq_ref[...])
    ckv = _rms(p_ref[:, P_KVD:P_KVD + MLA_KV_RANK], kvn_ref[...])
    ckv_ref[...] = ckv
    kv = _dot(ckv.astype(BF16), wkv_ref[...]).astype(BF16)
    kr = p_ref[:, P_KR:P_KR + MLA_ROPE]
    kr_ref[...] = kr
    krb = kr.astype(BF16)
    scale_a = (MLA_NOPE + MLA_ROPE) ** -0.5
    nope_w = MLA_HEADS * MLA_NOPE
    for h in range(MLA_HEADS):
        qn = q[:, h * MLA_NOPE:(h + 1) * MLA_NOPE].astype(BF16)
        qr = q[:, nope_w + h * MLA_ROPE:nope_w + (h + 1) * MLA_ROPE].astype(BF16)
        kn = kv[:, h * 256:h * 256 + MLA_NOPE]
        v = kv[:, h * 256 + MLA_NOPE:(h + 1) * 256]
        s = (_dot_nt(qn, kn) + _dot_nt(qr, krb)) * scale_a
        o_ref[:, h * MLA_V:(h + 1) * MLA_V] = _softmax_pv(s, v).astype(BF16)
    kb_ref[...] = p_ref[:, P_WK:P_WK + 256]
    vb_ref[...] = p_ref[:, P_WV:P_WV + 256]
    scale_b = WIN_HEAD_DIM ** -0.5
    for h in range(WIN_HEADS):
        kvh = h // WIN_GROUP
        qh = p_ref[:, P_WQ + h * 128:P_WQ + (h + 1) * 128].astype(BF16)
        k = p_ref[:, P_WK + kvh * 128:P_WK + (kvh + 1) * 128].astype(BF16)
        v = p_ref[:, P_WV + kvh * 128:P_WV + (kvh + 1) * 128].astype(BF16)
        s = _dot_nt(qh, k) * scale_b
        o = _softmax_pv(s, v, sink_ref[0:1, h:h + 1])
        o_ref[:, BRANCH_W + h * 128:BRANCH_W + (h + 1) * 128] = o.astype(BF16)
    kc_ref[...] = p_ref[:, P_NK:P_NK + 1024]
    vc_ref[...] = p_ref[:, P_NV:P_NV + 1024]
    scale_c = NAT_HEAD_DIM ** -0.5
    for h in range(NAT_HEADS):
        qh = p_ref[:, P_NQ + h * 128:P_NQ + (h + 1) * 128].astype(BF16)
        k = p_ref[:, P_NK + h * 128:P_NK + (h + 1) * 128].astype(BF16)
        v = p_ref[:, P_NV + h * 128:P_NV + (h + 1) * 128].astype(BF16)
        s = _dot_nt(qh, k) * scale_c
        o_ref[:, 2 * BRANCH_W + h * 128:2 * BRANCH_W + (h + 1) * 128] = _softmax_pv(s, v).astype(BF16)


def _ctx_attention(p, nb, seq, qn, kvn, wq, wkv, sink):
    full = lambda a: pl.BlockSpec(a.shape, lambda b: (0,) * a.ndim)
    row = lambda w: pl.BlockSpec((seq, w), lambda b: (b, 0))
    widths = (3 * BRANCH_W, MLA_KV_RANK, MLA_ROPE, 256, 256, 1024, 1024)
    dtypes = (BF16, F32, F32, F32, F32, F32, F32)
    return pl.pallas_call(
        _ctx_attn_kernel,
        grid=(nb,),
        in_specs=[pl.BlockSpec((seq, P_COLS), lambda b: (b, 0)),
                  full(qn), full(kvn), full(wq), full(wkv), full(sink)],
        out_specs=[row(w) for w in widths],
        out_shape=[jax.ShapeDtypeStruct((nb * seq, w), dt) for w, dt in zip(widths, dtypes)],
        compiler_params=_cparams(("arbitrary",)),
        name="ctx_attention",
    )(p, qn, kvn, wq, wkv, sink)


MLA_QT = 256


def _lat_mla_kernel(p_ref, cckv_ref, ckr_ref, qn_ref, kvn_ref, wq_ref, wkv_ref, tq_ref, tk_ref,
                    o_ref, kv_scr, kr_scr, *, past, seq):
    qt = pl.program_id(1)

    @pl.when(qt == 0)
    def _():
        kv_scr[0:past, :] = _dot(cckv_ref[...].astype(BF16), wkv_ref[...]).astype(BF16)
        kr_scr[0:past, :] = ckr_ref[...].astype(BF16)
        for c in range(seq // MLA_QT):
            r0 = c * MLA_QT
            ckv = _rms(p_ref[r0:r0 + MLA_QT, MLA_Q_RANK:MLA_Q_RANK + MLA_KV_RANK], kvn_ref[...])
            kv_scr[past + r0:past + r0 + MLA_QT, :] = _dot(ckv.astype(BF16), wkv_ref[...]).astype(BF16)
            krp = p_ref[r0:r0 + MLA_QT, 2 * MLA_Q_RANK:2 * MLA_Q_RANK + LANES]
            kr = _rope(krp, tk_ref[r0:r0 + MLA_QT, :], MLA_ROPE // 4)
            kr_scr[past + r0:past + r0 + MLA_QT, :] = kr[:, 0:MLA_ROPE].astype(BF16)

    r0 = pl.multiple_of(qt * MLA_QT, MLA_QT)
    qd = p_ref[pl.ds(r0, MLA_QT), 0:MLA_Q_RANK]
    q = _dot(_rms(qd, qn_ref[...]).astype(BF16), wq_ref[...])
    nope_w = MLA_HEADS * MLA_NOPE
    q_rope = _rope(q[:, nope_w:], tq_ref[pl.ds(r0, MLA_QT), :], MLA_ROPE // 4)
    scale = (MLA_NOPE + MLA_ROPE) ** -0.5
    krb = kr_scr[...]
    for h in range(MLA_HEADS):
        qn = q[:, h * MLA_NOPE:(h + 1) * MLA_NOPE].astype(BF16)
        qr = q_rope[:, h * MLA_ROPE:(h + 1) * MLA_ROPE].astype(BF16)
        kn = kv_scr[:, h * 256:h * 256 + MLA_NOPE]
        v = kv_scr[:, h * 256 + MLA_NOPE:(h + 1) * 256]
        s = (_dot_nt(qn, kn) + _dot_nt(qr, krb)) * scale
        o_ref[:, h * MLA_V:(h + 1) * MLA_V] = _softmax_pv(s, v).astype(BF16)


def _lat_mla(p, row_blk0, nb, seq, cache_ckv, cache_kr, layer, qn, kvn, wq, wkv, tq, tk):
    past = cache_ckv.shape[2]
    full = lambda a: pl.BlockSpec(a.shape, lambda b, t: (0,) * a.ndim)
    nqt = seq // MLA_QT
    return pl.pallas_call(
        functools.partial(_lat_mla_kernel, past=past, seq=seq),
        grid=(nb, nqt),
        in_specs=[
            pl.BlockSpec((seq, P_TN), lambda b, t: (row_blk0 + b, P_QD // P_TN)),
            pl.BlockSpec((None, None, past, MLA_KV_RANK), lambda b, t: (b, layer, 0, 0)),
            pl.BlockSpec((None, None, past, MLA_ROPE), lambda b, t: (b, layer, 0, 0)),
            full(qn), full(kvn), full(wq), full(wkv), full(tq), full(tk),
        ],
        out_specs=pl.BlockSpec((MLA_QT, BRANCH_W), lambda b, t: (b * nqt + t, 0)),
        out_shape=jax.ShapeDtypeStruct((nb * seq, BRANCH_W), BF16),
        scratch_shapes=[pltpu.VMEM((past + seq, MLA_HEADS * 256), BF16),
                        pltpu.VMEM((past + seq, MLA_ROPE), BF16)],
        compiler_params=_cparams(("arbitrary", "arbitrary")),
        name="lat_mla",
    )(p, cache_ckv, cache_kr, qn, kvn, wq, wkv, tq, tk)


WIN_QB = 128


def _lat_win_kernel(q_ref, k0_ref, k1_ref, k2_ref, v0_ref, v1_ref, v2_ref, ck_ref, cv_ref,
                    t0_ref, t1_ref, t2_ref, sink_ref, o_ref, *, past, seq):
    qb = pl.program_id(1)
    scale = WIN_HEAD_DIM ** -0.5
    n_loc = 3 * WIN_QB
    rows = WIN_GROUP * WIN_QB
    qpos = qb * WIN_QB + lax.broadcasted_iota(jnp.int32, (rows, n_loc), 0) % WIN_QB
    kpos = (qb - 1) * WIN_QB + lax.broadcasted_iota(jnp.int32, (rows, n_loc), 1)
    valid = (kpos >= 0) & (kpos < seq) & (jnp.abs(qpos - kpos) <= WINDOW)
    t1 = t1_ref[...]
    for kvh in range(WIN_KV_HEADS):
        cs = slice(kvh * 128, (kvh + 1) * 128)
        k_loc = jnp.concatenate([
            _rope(k0_ref[:, cs], t0_ref[...], 32),
            _rope(k1_ref[:, cs], t1, 32),
            _rope(k2_ref[:, cs], t2_ref[...], 32)], axis=0).astype(BF16)
        v_loc = jnp.concatenate([v0_ref[:, cs], v1_ref[:, cs], v2_ref[:, cs]], axis=0).astype(BF16)
        k_ctx = ck_ref[:, cs].astype(BF16)
        v_ctx = cv_ref[:, cs].astype(BF16)
        heads = range(kvh * WIN_GROUP, (kvh + 1) * WIN_GROUP)
        q = jnp.concatenate([_rope(q_ref[:, h * 128:(h + 1) * 128], t1, 32) for h in heads],
                            axis=0).astype(BF16)
        sink = jnp.concatenate([jnp.broadcast_to(sink_ref[0:1, h:h + 1], (WIN_QB, 1)) for h in heads], axis=0)
        s_ctx = _dot_nt(q, k_ctx) * scale
        s_loc = jnp.where(valid, _dot_nt(q, k_loc) * scale, NEG_INF)
        m = jnp.maximum(jnp.maximum(jnp.max(s_ctx, axis=-1, keepdims=True),
                                    jnp.max(s_loc, axis=-1, keepdims=True)), sink)
        e_ctx = jnp.exp(s_ctx - m)
        e_loc = jnp.exp(s_loc - m)
        den = (jnp.sum(e_ctx, axis=-1, keepdims=True) + jnp.sum(e_loc, axis=-1, keepdims=True)
               + jnp.exp(sink - m))
        o = (_dot(e_ctx.astype(BF16), v_ctx) + _dot(e_loc.astype(BF16), v_loc)) * (1.0 / den)
        for g, h in enumerate(heads):
            o_ref[:, h * 128:(h + 1) * 128] = o[g * WIN_QB:(g + 1) * WIN_QB, :].astype(BF16)


def _lat_win(p, row0, nb, seq, cache_k, cache_v, layer, tab, sink):
    past = cache_k.shape[2]
    nqb = seq // WIN_QB
    rb0 = row0 // WIN_QB

    def kblk(off, col_blk):
        return pl.BlockSpec(
            (WIN_QB, 256),
            lambda b, t: (rb0 + b * nqb + jnp.clip(t + off, 0, nqb - 1), col_blk))

    def tblk(off):
        return pl.BlockSpec((WIN_QB, 3 * 128), lambda b, t: (jnp.clip(t + off, 0, nqb - 1), 0))

    cache = pl.BlockSpec((None, None, past, 256), lambda b, t: (b, layer, 0, 0))
    return pl.pallas_call(
        functools.partial(_lat_win_kernel, past=past, seq=seq),
        grid=(nb, nqb),
        in_specs=[
            pl.BlockSpec((WIN_QB, 1024), lambda b, t: (rb0 + b * nqb + t, P_WQ // 1024)),
            kblk(-1, P_WK // 256), kblk(0, P_WK // 256), kblk(1, P_WK // 256),
            kblk(-1, P_WV // 256), kblk(0, P_WV // 256), kblk(1, P_WV // 256),
            cache, cache,
            tblk(-1), tblk(0), tblk(1),
            pl.BlockSpec(sink.shape, lambda b, t: (0, 0)),
        ],
        out_specs=pl.BlockSpec((WIN_QB, BRANCH_W), lambda b, t: (b * nqb + t, 0)),
        out_shape=jax.ShapeDtypeStruct((nb * seq, BRANCH_W), BF16),
        compiler_params=_cparams(("arbitrary", "arbitrary")),
        name="lat_window",
    )(p, p, p, p, p, p, p, cache_k, cache_v, tab, tab, tab, sink)


def _lat_nat_kernel(q_ref, k0_ref, k1_ref, k2_ref, v0_ref, v1_ref, v2_ref, ck_ref, cv_ref, bias_ref, o_ref):
    scale = NAT_HEAD_DIM ** -0.5
    for h in range(NAT_HEADS):
        cs = slice(h * 128, (h + 1) * 128)
        q = q_ref[:, cs].astype(BF16)
        k_loc = jnp.concatenate([k0_ref[:, cs], k1_ref[:, cs], k2_ref[:, cs]], axis=0).astype(BF16)
        v_loc = jnp.concatenate([v0_ref[:, cs], v1_ref[:, cs], v2_ref[:, cs]], axis=0).astype(BF16)
        s_ctx = _dot_nt(q, ck_ref[:, cs].astype(BF16)) * scale
        s_loc = _dot_nt(q, k_loc) * scale + bias_ref[h]
        m = jnp.maximum(jnp.max(s_ctx, axis=-1, keepdims=True), jnp.max(s_loc, axis=-1, keepdims=True))
        e_ctx = jnp.exp(s_ctx - m)
        e_loc = jnp.exp(s_loc - m)
        den = jnp.sum(e_ctx, axis=-1, keepdims=True) + jnp.sum(e_loc, axis=-1, keepdims=True)
        o = (_dot(e_ctx.astype(BF16), cv_ref[:, cs].astype(BF16)) + _dot(e_loc.astype(BF16), v_loc)) * (1.0 / den)
        o_ref[:, cs] = o.astype(BF16)


def _nat_key_start(g):
    return g // 2


def _lat_nat(p, row0, nb, seq, cache_k, cache_v, layer, bias):
    past = cache_k.shape[2]
    qrows = NAT_QROWS * GRID_W
    ng = seq // qrows
    rb0 = row0 // qrows

    def kblk(off, col_blk):
        return pl.BlockSpec((qrows, 1024), lambda g, b: (rb0 + b * ng + _nat_key_start(g) + off, col_blk))

    cache = pl.BlockSpec((None, None, past, 1024), lambda g, b: (b, layer, 0, 0))
    return pl.pallas_call(
        _lat_nat_kernel,
        grid=(ng, nb),
        in_specs=[
            pl.BlockSpec((qrows, 1024), lambda g, b: (rb0 + b * ng + g, P_NQ // 1024)),
            kblk(0, P_NK // 1024), kblk(1, P_NK // 1024), kblk(2, P_NK // 1024),
            kblk(0, P_NV // 1024), kblk(1, P_NV // 1024), kblk(2, P_NV // 1024),
            cache, cache,
            pl.BlockSpec((None, NAT_HEADS, qrows, NAT_KROWS * GRID_W), lambda g, b: (g, 0, 0, 0)),
        ],
        out_specs=pl.BlockSpec((qrows, BRANCH_W), lambda g, b: (b * ng + g, 0)),
        out_shape=jax.ShapeDtypeStruct((nb * seq, BRANCH_W), BF16),
        compiler_params=_cparams(("arbitrary", "arbitrary")),
        name="lat_neighbourhood",
    )(p, p, p, p, p, p, p, cache_k, cache_v, bias)


def _nat_bias_table(rpb, seq):
    rows = seq // GRID_W
    kr = min(NAT_ROWS, rows)
    ng = rows // NAT_QROWS
    g = np.arange(ng)[:, None, None]
    q = np.arange(NAT_QROWS * GRID_W)[None, :, None]
    k = np.arange(NAT_KROWS * GRID_W)[None, None, :]
    r = NAT_QROWS * g + q // GRID_W
    c = q % GRID_W
    key_row = NAT_QROWS * (g // 2) + k // GRID_W
    key_col = k % GRID_W
    rs = np.clip(r - kr // 2, 0, rows - kr)
    cs = np.clip(c - NAT_COLS // 2, 0, GRID_W - NAT_COLS)
    valid = (key_row >= rs) & (key_row < rs + kr) & (key_col >= cs) & (key_col < cs + NAT_COLS)
    n_dr, n_dc = 2 * NAT_ROWS - 1, 2 * NAT_COLS - 1
    cc = np.arange(GRID_W)
    dc = cc[None, :] - cc[:, None] + NAT_COLS - 1
    oh_c = (dc[None] == np.arange(n_dc)[:, None, None]).astype(np.float32)
    rq = np.arange(NAT_QROWS)
    rk = np.arange(NAT_KROWS)
    dr = (NAT_QROWS * (np.arange(ng)[:, None, None] // 2) + rk[None, None, :]
          - NAT_QROWS * np.arange(ng)[:, None, None] - rq[None, :, None] + NAT_ROWS - 1)
    oh_r = (dr[..., None] == np.arange(n_dr)).astype(np.float32)
    hp = lax.Precision.HIGHEST
    toep = jnp.einsum('had,dck->hack', rpb.astype(F32), jnp.asarray(oh_c), precision=hp)
    vals = jnp.einsum('hack,gqra->ghqcrk', toep, jnp.asarray(oh_r), precision=hp)
    vals = vals.reshape(ng, NAT_HEADS, NAT_QROWS * GRID_W, NAT_KROWS * GRID_W)
    return jnp.where(jnp.asarray(valid)[:, None], vals, NEG_INF)


def _merge_kernel(hn_ref, octx_ref, oa_ref, ob_ref, oc_ref, wga_ref, wgb_ref, wgc_ref,
                  wba_ref, wbb_ref, wbc_ref, m_ref, *, ctx_tiles):
    i = pl.program_id(0)
    weights = ((wga_ref, wba_ref), (wgb_ref, wbb_ref), (wgc_ref, wbc_ref))

    def merged(branch):
        hn = hn_ref[...]
        acc = None
        for k, (wg, wb) in enumerate(weights):
            term = jax.nn.sigmoid(_dot(hn, wg[...])) * _dot(branch(k), wb[...])
            acc = term if acc is None else acc + term
        m_ref[...] = acc.astype(BF16)

    @pl.when(i < ctx_tiles)
    def _():
        merged(lambda k: octx_ref[:, k * BRANCH_W:(k + 1) * BRANCH_W])

    @pl.when(i >= ctx_tiles)
    def _():
        lat = (oa_ref, ob_ref, oc_ref)
        merged(lambda k: lat[k][...])


def _merge(hn, o_ctx, o_a, o_b, o_c, w_gate, w_br):
    rows, d = hn.shape
    tm, tn = ROW_TILE, 256
    nj = d // tn
    ctx_tiles = o_ctx.shape[0] // tm
    gate = lambda k: pl.BlockSpec((d, tn), lambda i, j: (0, k * nj + j))
    br = lambda k: pl.BlockSpec((None, BRANCH_W, tn), lambda i, j: (k, 0, j))
    lat = pl.BlockSpec((tm, BRANCH_W), lambda i, j: (jnp.maximum(i - ctx_tiles, 0), 0))
    return pl.pallas_call(
        functools.partial(_merge_kernel, ctx_tiles=ctx_tiles),
        grid=(rows // tm, nj),
        in_specs=[pl.BlockSpec((tm, d), lambda i, j: (i, 0)),
                  pl.BlockSpec((tm, 3 * BRANCH_W), lambda i, j: (jnp.minimum(i, ctx_tiles - 1), 0)),
                  lat, lat, lat,
                  gate(0), gate(1), gate(2), br(0), br(1), br(2)],
        out_specs=pl.BlockSpec((tm, tn), lambda i, j: (i, j)),
        out_shape=jax.ShapeDtypeStruct((rows, d), BF16),
        compiler_params=_cparams(("arbitrary", "arbitrary")),
        name="merge",
    )(hn, o_ctx, o_a, o_b, o_c, w_gate, w_gate, w_gate, w_br, w_br, w_br)


ROUTE_IDX = 8
ROUTE_W = 10


ROW_SLABS = D_MODEL // LANES


def _store_row_major(ref2, val2, row0=0):
    n = val2.shape[0]
    for s in range(ROW_SLABS):
        ref2[pl.ds(row0 * ROW_SLABS + s, n, stride=ROW_SLABS), :] = (
            val2[:, s * LANES:(s + 1) * LANES].astype(ref2.dtype))


def _load_row_major(ref2):
    n = ref2.shape[0] // ROW_SLABS
    return jnp.concatenate([ref2[pl.ds(s, n, stride=ROW_SLABS), :] for s in range(ROW_SLABS)], axis=1)


def _row_slab(ref2, r):
    return ref2.at[pl.ds(pl.multiple_of(r * ROW_SLABS, ROW_SLABS), ROW_SLABS), :]


def _out_proj_kernel(m_ref, w_ref, h_ref, mod_ref, g_ref, *rest, routed):
    if routed:
        wr_ref, h1_ref, hn2_ref, route_ref = rest
    else:
        h1_ref, hn2_ref = rest
    h1 = h_ref[...] + mod_ref[2:3, :] * _dot(m_ref[...], w_ref[...])
    h1_ref[...] = h1
    hn2 = _rms(h1, g_ref[...]) * (1.0 + mod_ref[4:5, :]) + mod_ref[3:4, :]
    if routed:
        _store_row_major(hn2_ref, hn2)
    else:
        hn2_ref[...] = hn2.astype(hn2_ref.dtype)
    if routed:
        logits = _dot(hn2.astype(BF16), wr_ref[...])
        lane = lax.broadcasted_iota(jnp.int32, logits.shape, 1).astype(F32)
        lg = jnp.where(lane < N_EXPERTS, logits, -jnp.inf)
        m1 = jnp.max(lg, axis=-1, keepdims=True)
        i1 = jnp.min(jnp.where(lg == m1, lane, float(LANES)), axis=-1, keepdims=True)
        lg2 = jnp.where(lane == i1, -jnp.inf, lg)
        m2 = jnp.max(lg2, axis=-1, keepdims=True)
        i2 = jnp.min(jnp.where(lg2 == m2, lane, float(LANES)), axis=-1, keepdims=True)
        e2 = jnp.exp(m2 - m1)
        w1 = 1.0 / (1.0 + e2)
        w2 = e2 / (1.0 + e2)
        route = (jnp.where(lane == ROUTE_IDX, i1, 0.0)
                 + jnp.where(lane == ROUTE_IDX + 1, i2, 0.0)
                 + jnp.where(lane == ROUTE_W, w1, 0.0)
                 + jnp.where(lane == ROUTE_W + 1, w2, 0.0))
        route_ref[...] = route


def _out_proj(m, w_out, h, mod, g, cond_of_tile, w_router=None):
    rows, d = h.shape
    tm = 512
    sub = ROW_TILE // tm
    routed = w_router is not None
    in_specs = [
        pl.BlockSpec((tm, d), lambda i: (i, 0)),
        pl.BlockSpec((d, d), lambda i: (0, 0)),
        pl.BlockSpec((tm, d), lambda i: (i, 0)),
        pl.BlockSpec((None, 6, d), lambda i: (cond_of_tile(i // sub), 0, 0)),
        pl.BlockSpec((1, d), lambda i: (0, 0)),
    ]
    out_specs = [pl.BlockSpec((tm, d), lambda i: (i, 0)), pl.BlockSpec((tm, d), lambda i: (i, 0))]
    out_shape = [jax.ShapeDtypeStruct((rows, d), F32), jax.ShapeDtypeStruct((rows, d), BF16)]
    args = [m, w_out, h, mod, g]
    if routed:
        out_specs[1] = pl.BlockSpec((tm * ROW_SLABS, LANES), lambda i: (i, 0))
        out_shape[1] = jax.ShapeDtypeStruct((rows * ROW_SLABS, LANES), F32)
        in_specs.append(pl.BlockSpec((d, LANES), lambda i: (0, 0)))
        out_specs.append(pl.BlockSpec((tm, LANES), lambda i: (i, 0)))
        out_shape.append(jax.ShapeDtypeStruct((rows, LANES), F32))
        args.append(w_router)
    return pl.pallas_call(
        functools.partial(_out_proj_kernel, routed=routed),
        grid=(rows // tm,),
        in_specs=in_specs, out_specs=out_specs, out_shape=out_shape,
        compiler_params=_cparams(("arbitrary",)),
        name="out_proj",
    )(*args)


def _swiglu_chunk(x, wg_ref, wu_ref, wd_ref):
    g = _dot(x, wg_ref[...])
    u = _dot(x, wu_ref[...])
    a = (g * jax.nn.sigmoid(g) * u).astype(BF16)
    return _dot(a, wd_ref[...])


def _ffn_kernel(x_ref, wg_ref, wu_ref, wd_ref, h_ref, mod_ref, o_ref):
    f = pl.program_id(1)

    @pl.when(f == 0)
    def _():
        o_ref[...] = jnp.zeros_like(o_ref)

    o_ref[...] += _swiglu_chunk(x_ref[...], wg_ref, wu_ref, wd_ref)

    @pl.when(f == pl.num_programs(1) - 1)
    def _():
        o_ref[...] = h_ref[...] + mod_ref[5:6, :] * o_ref[...]


def _ffn(x, wg, wu, wd, h, mod, cond_of_tile):
    rows, d = h.shape
    ff = wg.shape[1]
    tm, tf = FFN_TM, FFN_TF
    sub = ROW_TILE // tm
    return pl.pallas_call(
        _ffn_kernel,
        grid=(rows // tm, ff // tf),
        in_specs=[
            pl.BlockSpec((tm, d), lambda i, f: (i, 0)),
            pl.BlockSpec((d, tf), lambda i, f: (0, f)),
            pl.BlockSpec((d, tf), lambda i, f: (0, f)),
            pl.BlockSpec((tf, d), lambda i, f: (f, 0)),
            pl.BlockSpec((tm, d), lambda i, f: (i, 0)),
            pl.BlockSpec((None, 6, d), lambda i, f: (cond_of_tile(i // sub), 0, 0)),
        ],
        out_specs=pl.BlockSpec((tm, d), lambda i, f: (i, 0)),
        out_shape=jax.ShapeDtypeStruct((rows, d), F32),
        compiler_params=_cparams(("arbitrary", "arbitrary")),
        name="ffn_dense",
    )(x, wg, wu, wd, h, mod)


def _gather_kernel(tr_ref, idx_ref, src_ref, o_ref, buf, sem):
    i = pl.program_id(0)
    n = o_ref.shape[0]
    per_tile = MOE_TM // n
    used = tr_ref[i // per_tile] > (i % per_tile) * n

    @pl.when(used)
    def _():
        def issue(r, carry):
            pltpu.make_async_copy(_row_slab(src_ref, idx_ref[0, r]), _row_slab(buf, r), sem).start()
            return carry

        lax.fori_loop(0, n, issue, 0, unroll=8)
        pltpu.make_async_copy(src_ref.at[pl.ds(0, n * ROW_SLABS), :], buf, sem).wait()
        o_ref[...] = _load_row_major(buf).astype(o_ref.dtype)

    @pl.when(jnp.logical_not(used))
    def _():
        o_ref[...] = jnp.zeros_like(o_ref)


def _gather_rows(src, idx, tile_rows, out_dtype):
    n = idx.shape[0]
    d = ROW_SLABS * LANES
    tg = GATHER_ROWS
    grid_spec = pltpu.PrefetchScalarGridSpec(
        num_scalar_prefetch=1,
        grid=(n // tg,),
        in_specs=[pl.BlockSpec((None, 1, tg), lambda i, tr: (i, 0, 0), memory_space=pltpu.SMEM),
                  pl.BlockSpec(memory_space=pl.ANY)],
        out_specs=pl.BlockSpec((tg, d), lambda i, tr: (i, 0)),
        scratch_shapes=[pltpu.VMEM((tg * ROW_SLABS, LANES), src.dtype), pltpu.SemaphoreType.DMA(())],
    )
    return pl.pallas_call(
        _gather_kernel,
        grid_spec=grid_spec,
        out_shape=jax.ShapeDtypeStruct((n, d), out_dtype),
        compiler_params=_cparams(("arbitrary",)),
        name="moe_gather",
    )(tile_rows, idx.reshape(n // tg, 1, tg), src)


def _moe_ffn_kernel(te_ref, tv_ref, x_ref, wg_ref, wu_ref, wd_ref, o_ref, acc_ref):
    i = pl.program_id(0)
    f = pl.program_id(1)
    rows_used = tv_ref[i]
    for s in range(MOE_TM // MOE_SLAB):
        rs = slice(s * MOE_SLAB, (s + 1) * MOE_SLAB)
        used = rows_used > s * MOE_SLAB

        @pl.when(f == 0)
        def _():
            acc_ref[rs, :] = jnp.zeros((MOE_SLAB, acc_ref.shape[1]), acc_ref.dtype)

        @pl.when(used)
        def _():
            acc_ref[rs, :] += _swiglu_chunk(x_ref[rs, :], wg_ref, wu_ref, wd_ref)

        @pl.when(f == pl.num_programs(1) - 1)
        def _():
            _store_row_major(o_ref, acc_ref[rs, :], s * MOE_SLAB)


def _moe_ffn(xs, wg, wu, wd, tile_expert, tile_valid):
    n, d = xs.shape
    ff = wg.shape[2]
    tm, tf = MOE_TM, FFN_TF
    nf = ff // tf

    def fidx(i, f, tv):
        return jnp.where(tv[i] > 0, f, nf - 1)

    grid_spec = pltpu.PrefetchScalarGridSpec(
        num_scalar_prefetch=2,
        grid=(n // tm, nf),
        in_specs=[
            pl.BlockSpec((tm, d), lambda i, f, te, tv: (i, 0)),
            pl.BlockSpec((None, d, tf), lambda i, f, te, tv: (te[i], 0, fidx(i, f, tv))),
            pl.BlockSpec((None, d, tf), lambda i, f, te, tv: (te[i], 0, fidx(i, f, tv))),
            pl.BlockSpec((None, tf, d), lambda i, f, te, tv: (te[i], fidx(i, f, tv), 0)),
        ],
        out_specs=pl.BlockSpec((tm * ROW_SLABS, LANES), lambda i, f, te, tv: (i, 0)),
        scratch_shapes=[pltpu.VMEM((tm, d), F32)],
    )
    return pl.pallas_call(
        _moe_ffn_kernel,
        grid_spec=grid_spec,
        out_shape=jax.ShapeDtypeStruct((n * ROW_SLABS, LANES), F32),
        compiler_params=_cparams(("arbitrary", "arbitrary")),
        name="moe_ffn",
    )(tile_expert, tile_valid, xs, wg, wu, wd)


def _combine_kernel(p0_ref, p1_ref, ys_ref, route_ref, h_ref, mod_ref, fn_ref, o_ref, buf0, buf1, sem):
    n = o_ref.shape[0]

    def issue(r, carry):
        pltpu.make_async_copy(_row_slab(ys_ref, p0_ref[0, r]), _row_slab(buf0, r), sem.at[0]).start()
        pltpu.make_async_copy(_row_slab(ys_ref, p1_ref[0, r]), _row_slab(buf1, r), sem.at[1]).start()
        return carry

    lax.fori_loop(0, n, issue, 0, unroll=8)
    pltpu.make_async_copy(ys_ref.at[pl.ds(0, n * ROW_SLABS), :], buf0, sem.at[0]).wait()
    pltpu.make_async_copy(ys_ref.at[pl.ds(0, n * ROW_SLABS), :], buf1, sem.at[1]).wait()
    w0 = route_ref[:, ROUTE_W:ROUTE_W + 1]
    w1 = route_ref[:, ROUTE_W + 1:ROUTE_W + 2]
    y = w0 * _load_row_major(buf0) + w1 * _load_row_major(buf1)
    h2 = h_ref[...] + mod_ref[5:6, :] * y
    o_ref[...] = _rms(h2, fn_ref[...])


def _moe_combine(ys, pos0, pos1, route, h, mod, final_norm, cond_of_tile):
    rows, d = h.shape
    tc = GATHER_ROWS
    sub = ROW_TILE // tc
    smem = lambda: pl.BlockSpec((None, 1, tc), lambda i: (i, 0, 0), memory_space=pltpu.SMEM)
    return pl.pallas_call(
        _combine_kernel,
        grid=(rows // tc,),
        in_specs=[smem(), smem(),
                  pl.BlockSpec(memory_space=pl.ANY),
                  pl.BlockSpec((tc, LANES), lambda i: (i, 0)),
                  pl.BlockSpec((tc, d), lambda i: (i, 0)),
                  pl.BlockSpec((None, 6, d), lambda i: (cond_of_tile(i // sub), 0, 0)),
                  pl.BlockSpec((1, d), lambda i: (0, 0))],
        out_specs=pl.BlockSpec((tc, d), lambda i: (i, 0)),
        out_shape=jax.ShapeDtypeStruct((rows, d), F32),
        scratch_shapes=[pltpu.VMEM((tc * ROW_SLABS, LANES), F32), pltpu.VMEM((tc * ROW_SLABS, LANES), F32),
                        pltpu.SemaphoreType.DMA((2,))],
        compiler_params=_cparams(("arbitrary",)),
        name="moe_combine",
    )(pos0.reshape(rows // tc, 1, tc), pos1.reshape(rows // tc, 1, tc), ys, route, h, mod, final_norm)


def _dispatch_plan(route, tm):
    rows = route.shape[0]
    ids = route[:, ROUTE_IDX:ROUTE_IDX + TOP_K].astype(jnp.int32)
    flat = ids.reshape(-1)
    onehot = (flat[:, None] == jnp.arange(N_EXPERTS)[None, :]).astype(jnp.int32)
    rank = jnp.sum((jnp.cumsum(onehot, axis=0) - onehot) * onehot, axis=1)
    counts = jnp.sum(onehot, axis=0)
    tiles = (counts + tm - 1) // tm
    tile_end = jnp.cumsum(tiles)
    start = (tile_end - tiles) * tm
    slot = start[flat] + rank
    n_tiles = (rows * TOP_K) // tm + N_EXPERTS
    n_slots = n_tiles * tm
    slot_token = (jnp.arange(n_slots, dtype=jnp.int32) % rows).at[slot].set(
        jnp.arange(rows * TOP_K, dtype=jnp.int32) // TOP_K)
    t = jnp.arange(n_tiles)
    tile_expert = jnp.minimum(jnp.sum((t[:, None] >= tile_end[None, :]).astype(jnp.int32), axis=1), N_EXPERTS - 1)
    tile_in_expert = t - (tile_end - tiles)[tile_expert]
    tile_rows = jnp.where(t < tile_end[-1], jnp.clip(counts[tile_expert] - tile_in_expert * tm, 0, tm), 0)
    last_expert = tile_expert[jnp.maximum(tile_end[-1] - 1, 0)]
    tile_expert = jnp.where(tile_rows > 0, tile_expert, last_expert).astype(jnp.int32)
    pos = slot.reshape(rows, TOP_K)
    return slot_token, tile_expert, tile_rows.astype(jnp.int32), pos[:, 0], pos[:, 1]


def _rope_table(seq, n):
    quarter = n // 4
    t = np.arange(seq)
    inv = jnp.power(ROPE_BASE, -jnp.arange(quarter, dtype=F32) * (2.0 / (n // 2)))
    ang_r = jnp.asarray(t // GRID_W, F32)[:, None] * inv[None, :]
    ang_c = jnp.asarray(t % GRID_W, F32)[:, None] * inv[None, :]
    zero = jnp.zeros((seq, quarter), F32)
    cos = jnp.concatenate([jnp.cos(ang_r)] * 2 + [jnp.cos(ang_c)] * 2, axis=1)
    up = jnp.concatenate([-jnp.sin(ang_r), zero, -jnp.sin(ang_c), zero], axis=1)
    dn = jnp.concatenate([zero, jnp.sin(ang_r), zero, jnp.sin(ang_c)], axis=1)
    return cos, up, dn


def _tile_cols(parts, reps, pad_to=None):
    out = [jnp.tile(p, (1, reps)) for p in parts]
    if pad_to is not None:
        out = [jnp.pad(p, ((0, 0), (0, pad_to - p.shape[1]))) for p in out]
    return jnp.concatenate(out, axis=1)


def kernel(x_prompt, x_sample, cache_mla_ckv, cache_mla_krope, cache_win_k, cache_win_v, cache_nat_k, cache_nat_v, c, c_ctx, w_mod, b_mod, norm1, norm2, w_in, mla_q_norm, mla_kv_norm, w_mla_q_up, w_mla_kv_up, win_sink, nat_rpb, w_br_mla, w_br_win, w_br_nat, w_out, w_ff_gate, w_ff_up, w_ff_down, w_router, w_ex_gate, w_ex_up, w_ex_down, final_norm):
    nb_ctx, seq_ctx, d = x_prompt.shape
    nb_lat, seq_lat, _ = x_sample.shape
    past = cache_mla_ckv.shape[2]
    rows_ctx = nb_ctx * seq_ctx
    rows_lat = nb_lat * seq_lat
    assert d == D_MODEL and seq_lat == ROW_TILE and rows_ctx % ROW_TILE == 0
    assert seq_lat // GRID_W == 4 * NAT_QROWS
    assert w_mod.shape[0] == DEPTH == 2
    ctx_tiles = rows_ctx // ROW_TILE

    def cond_of_tile(i):
        return jnp.where(i < ctx_tiles, 0, i - ctx_tiles + 1)

    n_cond = 16
    cond = jnp.zeros((n_cond, d), F32).at[0].set(c_ctx).at[1:1 + nb_lat].set(c)
    mod_all = _modulation(cond, w_mod, b_mod).reshape(DEPTH, n_cond, 6, d)

    splits = np.cumsum([0, MLA_Q_RANK, MLA_KV_RANK, MLA_ROPE, 1024, 256, 256, 1024, 1024, 1024])
    seg = lambda k: w_in[:, :, splits[k]:splits[k + 1]]
    qd_w, kvd_w, kr_w, wq_w, wk_w, wv_w, nq_w, nk_w, nv_w = [seg(k) for k in range(9)]
    w_p = jnp.concatenate(
        [wq_w, nq_w, nk_w, nv_w, wk_w, wv_w, qd_w, kvd_w, kr_w,
         jnp.zeros((DEPTH, d, P_COLS - P_KR - MLA_ROPE), F32)], axis=2).astype(BF16)
    w_gate = w_in[:, :, splits[9]:].astype(BF16)
    wq_up = w_mla_q_up.reshape(DEPTH, MLA_Q_RANK, MLA_HEADS, MLA_NOPE + MLA_ROPE)
    wq_up = jnp.concatenate([wq_up[..., :MLA_NOPE].reshape(DEPTH, MLA_Q_RANK, -1),
                             wq_up[..., MLA_NOPE:].reshape(DEPTH, MLA_Q_RANK, -1)], axis=2).astype(BF16)
    wkv_up = w_mla_kv_up.astype(BF16)
    w_br = jnp.stack([w_br_mla, w_br_win, w_br_nat], axis=1).astype(BF16)
    w_out_b = w_out.astype(BF16)
    w_router_p = jnp.pad(w_router, ((0, 0), (0, 0), (0, LANES - N_EXPERTS))).astype(BF16)

    t128 = jnp.concatenate(_rope_table(seq_lat, 128), axis=1)
    t64 = _rope_table(seq_lat, MLA_ROPE)
    t_q = _tile_cols(t64, MLA_HEADS)
    t_k = _tile_cols(t64, 1, pad_to=LANES)

    ck_win = cache_win_k.reshape(nb_lat, DEPTH, past, 256)
    cv_win = cache_win_v.reshape(nb_lat, DEPTH, past, 256)
    ck_nat = cache_nat_k.reshape(nb_lat, DEPTH, past, 1024)
    cv_nat = cache_nat_v.reshape(nb_lat, DEPTH, past, 1024)

    h = jnp.concatenate([x_prompt.reshape(rows_ctx, d), x_sample.reshape(rows_lat, d)], axis=0)
    states = [[] for _ in range(6)]
    for l in range(DEPTH):
        mod = mod_all[l]
        qn = mla_q_norm[l].reshape(1, -1)
        kvn = mla_kv_norm[l].reshape(1, -1)
        sink = win_sink[l].reshape(1, -1)
        p, hn = _mixer_in(h, mod, norm1[l].reshape(1, d), w_p[l], cond_of_tile)
        o_ctx, *st = _ctx_attention(p, nb_ctx, seq_ctx, qn, kvn, wq_up[l], wkv_up[l], sink)
        for lst, arr in zip(states, st):
            lst.append(arr)
        o_a = _lat_mla(p, rows_ctx // seq_lat, nb_lat, seq_lat, cache_mla_ckv, cache_mla_krope, l,
                       qn, kvn, wq_up[l], wkv_up[l], t_q, t_k)
        o_b = _lat_win(p, rows_ctx, nb_lat, seq_lat, ck_win, cv_win, l, t128, sink)
        o_c = _lat_nat(p, rows_ctx, nb_lat, seq_lat, ck_nat, cv_nat, l, _nat_bias_table(nat_rpb[l], seq_lat))
        m = _merge(hn, o_ctx, o_a, o_b, o_c, w_gate[l], w_br[l])
        if l % 2 == 0:
            h1, hn2 = _out_proj(m, w_out_b[l], h, mod, norm2[l].reshape(1, d), cond_of_tile)
            h = _ffn(hn2, w_ff_gate[l // 2].astype(BF16), w_ff_up[l // 2].astype(BF16),
                     w_ff_down[l // 2].astype(BF16), h1, mod, cond_of_tile)
        else:
            h1, hn2, route = _out_proj(m, w_out_b[l], h, mod, norm2[l].reshape(1, d), cond_of_tile,
                                       w_router=w_router_p[l // 2])
            slot_token, tile_expert, tile_valid, pos0, pos1 = _dispatch_plan(route, MOE_TM)
            xs = _gather_rows(hn2, slot_token, tile_valid, BF16)
            ys = _moe_ffn(xs, w_ex_gate[l // 2].astype(BF16), w_ex_up[l // 2].astype(BF16),
                          w_ex_down[l // 2].astype(BF16), tile_expert, tile_valid)
            h = _moe_combine(ys, pos0, pos1, route, h1, mod, final_norm.reshape(1, d), cond_of_tile)

    y_prompt = h[:rows_ctx].reshape(nb_ctx, seq_ctx, d)
    y_sample = h[rows_ctx:].reshape(nb_lat, seq_lat, d)
    st = [jnp.stack(s, axis=1) for s in states]
    shp = lambda a, tail: a.reshape(nb_ctx, seq_ctx, DEPTH, -1).transpose(0, 2, 1, 3).reshape((nb_ctx, DEPTH, seq_ctx) + tail)
    return (y_prompt, y_sample,
            shp(st[0], (MLA_KV_RANK,)), shp(st[1], (MLA_ROPE,)),
            shp(st[2], (WIN_KV_HEADS, WIN_HEAD_DIM)), shp(st[3], (WIN_KV_HEADS, WIN_HEAD_DIM)),
            shp(st[4], (NAT_HEADS, NAT_HEAD_DIM)), shp(st[5], (NAT_HEADS, NAT_HEAD_DIM)))
```

```python
import functools

import jax
import jax.numpy as jnp
import numpy as np
from jax import lax
from jax.experimental import pallas as pl
from jax.experimental.pallas import tpu as pltpu

F32 = jnp.float32
BF16 = jnp.bfloat16

D_MODEL = 2048
DEPTH = 2
GRID_W = 64
ROPE_BASE = 10000.0
EPS = 1e-6
NEG_INF = -1e30
MLA_HEADS = 8
MLA_Q_RANK = 512
MLA_KV_RANK = 512
MLA_NOPE = 128
MLA_ROPE = 64
MLA_V = 128
WIN_HEADS = 8
WIN_KV_HEADS = 2
WIN_GROUP = WIN_HEADS // WIN_KV_HEADS
WIN_HEAD_DIM = 128
WINDOW = 128
NAT_HEADS = 8
NAT_HEAD_DIM = 128
NAT_ROWS = 8
NAT_COLS = 16
BRANCH_W = 1024
D_FF = 5632
N_EXPERTS = 8
TOP_K = 2

P_WQ, P_NQ, P_NK, P_NV = 0, 1024, 2048, 3072
P_WK, P_WV = 4096, 4352
P_QD, P_KVD, P_KR = 4608, 5120, 5632
P_COLS = 5760
P_TN = 1152
LANES = 128

VMEM_LIMIT = 56 * 1024 * 1024

ROW_TILE = 1024
FFN_TM = 512
FFN_TF = 512
MOE_TM = 1024
MOE_SLAB = 512
GATHER_ROWS = 256
NAT_QROWS = 4
NAT_KROWS = 12


def _cparams(sem):
    return pltpu.CompilerParams(dimension_semantics=sem, vmem_limit_bytes=VMEM_LIMIT)


def _rms(x, g):
    ms = jnp.mean(x * x, axis=-1, keepdims=True)
    return x * lax.rsqrt(ms + EPS) * g


def _dot(a, b):
    return jnp.dot(a, b, preferred_element_type=F32)


def _dot_nt(a, b):
    return lax.dot_general(a, b, (((1,), (1,)), ((), ())), preferred_element_type=F32)


def _rope(x, tab_ref_or_val, shift):
    n = x.shape[-1]
    t = tab_ref_or_val
    c, s_up, s_dn = t[:, 0:n], t[:, n:2 * n], t[:, 2 * n:3 * n]
    up = pltpu.roll(x, n - shift, axis=1)
    dn = pltpu.roll(x, shift, axis=1)
    return x * c + up * s_up + dn * s_dn


def _softmax_pv(s, v, sink=None):
    m = jnp.max(s, axis=-1, keepdims=True)
    if sink is not None:
        m = jnp.maximum(m, sink)
    e = jnp.exp(s - m)
    den = jnp.sum(e, axis=-1, keepdims=True)
    if sink is not None:
        den = den + jnp.exp(sink - m)
    return _dot(e.astype(BF16), v) * (1.0 / den)


def _attend_heads(n_heads, score, value, o_ref, col0, sink=None, group=None, emit=None):
    group = group or n_heads
    for h0 in range(0, n_heads, group):
        heads = range(h0, h0 + group)
        s = [score(h) for h in heads]
        m = [jnp.max(x, axis=-1, keepdims=True) for x in s]
        if sink is not None:
            sk = [sink(h) for h in heads]
            m = [jnp.maximum(a, b) for a, b in zip(m, sk)]
        e = [jnp.exp(x - a) for x, a in zip(s, m)]
        den = [jnp.sum(x, axis=-1, keepdims=True) for x in e]
        if sink is not None:
            den = [d + jnp.exp(b - a) for d, a, b in zip(den, m, sk)]
        pv = [_dot(x.astype(BF16), value(h)) for x, h in zip(e, heads)]
        for i, h in enumerate(heads):
            out = pv[i] * (1.0 / den[i])
            if emit is not None:
                emit(h, out)
            else:
                o_ref[:, col0 + h * LANES:col0 + (h + 1) * LANES] = out.astype(o_ref.dtype)


def _mod_kernel(c_ref, w_ref, b_ref, o_ref):
    c = c_ref[...]
    s = (c * jax.nn.sigmoid(c)).astype(BF16)
    o_ref[...] = _dot(s, w_ref[...].astype(BF16)) + b_ref[...]


def _modulation(cond, w_mod, b_mod):
    depth, d, n = w_mod.shape
    nc = cond.shape[0]
    tn = 1024
    return pl.pallas_call(
        _mod_kernel,
        grid=(depth, n // tn),
        in_specs=[
            pl.BlockSpec((nc, d), lambda l, j: (0, 0)),
            pl.BlockSpec((None, d, tn), lambda l, j: (l, 0, j)),
            pl.BlockSpec((None, 1, tn), lambda l, j: (l, 0, j)),
        ],
        out_specs=pl.BlockSpec((None, nc, tn), lambda l, j: (l, 0, j)),
        out_shape=jax.ShapeDtypeStruct((depth, nc, n), F32),
        compiler_params=_cparams(("arbitrary", "arbitrary")),
        name="modulation",
    )(cond, w_mod, b_mod.reshape(depth, 1, n))


_W_IN_SEGMENTS = ((1088, 1024), (2624, 1024), (3648, 1024), (4672, 1024), (2112, 256), (2368, 256),
                  (0, 512), (512, 512), (1024, 64))
W_IN_GATES = 5696


def _w_in_prep_kernel(w_ref, p_ref, g_ref):
    col = 0
    for src, width in _W_IN_SEGMENTS:
        p_ref[:, col:col + width] = w_ref[:, src:src + width].astype(BF16)
        col += width
    p_ref[:, col:] = jnp.zeros((p_ref.shape[0], P_COLS - col), BF16)
    g_ref[...] = w_ref[:, W_IN_GATES:].astype(BF16)


def _w_in_prep(w_in):
    depth, d, n = w_in.shape
    tr = 128
    n_gate = n - W_IN_GATES
    return pl.pallas_call(
        _w_in_prep_kernel,
        grid=(depth, d // tr),
        in_specs=[pl.BlockSpec((None, tr, n), lambda l, r: (l, r, 0))],
        out_specs=[pl.BlockSpec((None, tr, P_COLS), lambda l, r: (l, r, 0)),
                   pl.BlockSpec((None, tr, n_gate), lambda l, r: (l, r, 0))],
        out_shape=[jax.ShapeDtypeStruct((depth, d, P_COLS), BF16),
                   jax.ShapeDtypeStruct((depth, d, n_gate), BF16)],
        compiler_params=_cparams(("arbitrary", "arbitrary")),
        name="w_in_prep",
    )(w_in)


def _mixer_in_kernel(h_ref, mod_ref, g_ref, w_ref, p_ref, hn_ref):
    @pl.when(pl.program_id(1) == 0)
    def _():
        hn = _rms(h_ref[...], g_ref[...]) * (1.0 + mod_ref[1:2, :]) + mod_ref[0:1, :]
        hn_ref[...] = hn.astype(BF16)

    p_ref[...] = _dot(hn_ref[...], w_ref[...])


def _mixer_in(h, mod, g, w, cond_of_tile):
    rows, d = h.shape
    n = w.shape[1]
    tm, tn = ROW_TILE, P_TN
    return pl.pallas_call(
        _mixer_in_kernel,
        grid=(rows // tm, n // tn),
        in_specs=[
            pl.BlockSpec((tm, d), lambda i, j: (i, 0)),
            pl.BlockSpec((None, 6, d), lambda i, j: (cond_of_tile(i), 0, 0)),
            pl.BlockSpec((1, d), lambda i, j: (0, 0)),
            pl.BlockSpec((d, tn), lambda i, j: (0, j)),
        ],
        out_specs=[
            pl.BlockSpec((tm, tn), lambda i, j: (i, j)),
            pl.BlockSpec((tm, d), lambda i, j: (i, 0)),
        ],
        out_shape=[
            jax.ShapeDtypeStruct((rows, n), F32),
            jax.ShapeDtypeStruct((rows, d), BF16),
        ],
        compiler_params=_cparams(("arbitrary", "arbitrary")),
        name="mixer_in",
    )(h, mod, g, w)


def _ctx_attn_kernel(p_ref, qn_ref, kvn_ref, wq_ref, wkv_ref, sink_ref,
                     o_ref, ckv_ref, kr_ref, kb_ref, vb_ref, kc_ref, vc_ref):
    q = _dot(_rms(p_ref[:, P_QD:P_QD + MLA_Q_RANK], qn_ref[...]).astype(BF16), wq_ref[...])
    ckv = _rms(p_ref[:, P_KVD:P_KVD + MLA_KV_RANK], kvn_ref[...])
    ckv_ref[...] = ckv
    kv = _dot(ckv.astype(BF16), wkv_ref[...]).astype(BF16)
    kr = p_ref[:, P_KR:P_KR + MLA_ROPE]
    kr_ref[...] = kr
    krb = kr.astype(BF16)
    scale_a = (MLA_NOPE + MLA_ROPE) ** -0.5
    nope_w = MLA_HEADS * MLA_NOPE

    def score_a(h):
        qn = q[:, h * MLA_NOPE:(h + 1) * MLA_NOPE].astype(BF16)
        qr = q[:, nope_w + h * MLA_ROPE:nope_w + (h + 1) * MLA_ROPE].astype(BF16)
        return (_dot_nt(qn, kv[:, h * 256:h * 256 + MLA_NOPE]) + _dot_nt(qr, krb)) * scale_a

    _attend_heads(MLA_HEADS, score_a, lambda h: kv[:, h * 256 + MLA_NOPE:(h + 1) * 256], o_ref, 0)
    kb_ref[...] = p_ref[:, P_WK:P_WK + 256]
    vb_ref[...] = p_ref[:, P_WV:P_WV + 256]
    scale_b = WIN_HEAD_DIM ** -0.5

    def score_b(h):
        kvh = h // WIN_GROUP
        qh = p_ref[:, P_WQ + h * 128:P_WQ + (h + 1) * 128].astype(BF16)
        return _dot_nt(qh, p_ref[:, P_WK + kvh * 128:P_WK + (kvh + 1) * 128].astype(BF16)) * scale_b

    def value_b(h):
        kvh = h // WIN_GROUP
        return p_ref[:, P_WV + kvh * 128:P_WV + (kvh + 1) * 128].astype(BF16)

    _attend_heads(WIN_HEADS, score_b, value_b, o_ref, BRANCH_W, lambda h: sink_ref[0:1, h:h + 1])
    kc_ref[...] = p_ref[:, P_NK:P_NK + 1024]
    vc_ref[...] = p_ref[:, P_NV:P_NV + 1024]
    scale_c = NAT_HEAD_DIM ** -0.5

    def score_c(h):
        qh = p_ref[:, P_NQ + h * 128:P_NQ + (h + 1) * 128].astype(BF16)
        return _dot_nt(qh, p_ref[:, P_NK + h * 128:P_NK + (h + 1) * 128].astype(BF16)) * scale_c

    _attend_heads(NAT_HEADS, score_c, lambda h: p_ref[:, P_NV + h * 128:P_NV + (h + 1) * 128].astype(BF16),
                  o_ref, 2 * BRANCH_W)


def _ctx_attention(p, nb, seq, qn, kvn, wq, wkv, sink):
    full = lambda a: pl.BlockSpec(a.shape, lambda b: (0,) * a.ndim)
    row = lambda w: pl.BlockSpec((seq, w), lambda b: (b, 0))
    widths = (3 * BRANCH_W, MLA_KV_RANK, MLA_ROPE, 256, 256, 1024, 1024)
    dtypes = (BF16, F32, F32, F32, F32, F32, F32)
    return pl.pallas_call(
        _ctx_attn_kernel,
        grid=(nb,),
        in_specs=[pl.BlockSpec((seq, P_COLS), lambda b: (b, 0)),
                  full(qn), full(kvn), full(wq), full(wkv), full(sink)],
        out_specs=[row(w) for w in widths],
        out_shape=[jax.ShapeDtypeStruct((nb * seq, w), dt) for w, dt in zip(widths, dtypes)],
        compiler_params=_cparams(("arbitrary",)),
        name="ctx_attention",
    )(p, qn, kvn, wq, wkv, sink)


MLA_QT = 512
MLA_KC = 256
MLA_DK = MLA_NOPE + MLA_ROPE


def _lat_mla_kernel(p_ref, cckv_ref, ckr_ref, qn_ref, kvn_ref, wq_ref, wkv_ref, tq_ref, tk_ref,
                    o_ref, k_scr, v_scr, *, past, seq):
    qt = pl.program_id(1)

    def put_keys(r0, n, kv, kr):
        krb = kr.astype(BF16)
        for h in range(MLA_HEADS):
            k_scr[h, r0:r0 + n, 0:MLA_NOPE] = kv[:, h * 256:h * 256 + MLA_NOPE].astype(BF16)
            k_scr[h, r0:r0 + n, MLA_NOPE:MLA_DK] = krb
            v_scr[r0:r0 + n, h * MLA_V:(h + 1) * MLA_V] = kv[:, h * 256 + MLA_NOPE:(h + 1) * 256].astype(BF16)

    @pl.when(qt == 0)
    def _():
        for c in range(past // MLA_KC):
            r0 = c * MLA_KC
            kv = _dot(cckv_ref[r0:r0 + MLA_KC, :].astype(BF16), wkv_ref[...])
            put_keys(r0, MLA_KC, kv, ckr_ref[r0:r0 + MLA_KC, :])
        for c in range(seq // MLA_KC):
            r0 = c * MLA_KC
            ckv = _rms(p_ref[r0:r0 + MLA_KC, MLA_Q_RANK:MLA_Q_RANK + MLA_KV_RANK], kvn_ref[...])
            kv = _dot(ckv.astype(BF16), wkv_ref[...])
            krp = p_ref[r0:r0 + MLA_KC, 2 * MLA_Q_RANK:2 * MLA_Q_RANK + LANES]
            kr = _rope(krp, tk_ref[r0:r0 + MLA_KC, :], MLA_ROPE // 4)
            put_keys(past + r0, MLA_KC, kv, kr[:, 0:MLA_ROPE])

    r0 = pl.multiple_of(qt * MLA_QT, MLA_QT)
    qd = p_ref[pl.ds(r0, MLA_QT), 0:MLA_Q_RANK]
    q = _dot(_rms(qd, qn_ref[...]).astype(BF16), wq_ref[...])
    nope_w = MLA_HEADS * MLA_NOPE
    q_rope = _rope(q[:, nope_w:], tq_ref[pl.ds(r0, MLA_QT), :], MLA_ROPE // 4)
    scale = MLA_DK ** -0.5

    def score(h):
        qh = jnp.concatenate([q[:, h * MLA_NOPE:(h + 1) * MLA_NOPE],
                              q_rope[:, h * MLA_ROPE:(h + 1) * MLA_ROPE]], axis=1).astype(BF16)
        return _dot_nt(qh, k_scr[h]) * scale

    _attend_heads(MLA_HEADS, score, lambda h: v_scr[:, h * MLA_V:(h + 1) * MLA_V], o_ref, 0, group=2)


def _lat_mla(p, row_blk0, nb, seq, cache_ckv, cache_kr, layer, qn, kvn, wq, wkv, tq, tk):
    past = cache_ckv.shape[2]
    full = lambda a: pl.BlockSpec(a.shape, lambda b, t: (0,) * a.ndim)
    nqt = seq // MLA_QT
    return pl.pallas_call(
        functools.partial(_lat_mla_kernel, past=past, seq=seq),
        grid=(nb, nqt),
        in_specs=[
            pl.BlockSpec((seq, P_TN), lambda b, t: (row_blk0 + b, P_QD // P_TN)),
            pl.BlockSpec((None, None, past, MLA_KV_RANK), lambda b, t: (b, layer, 0, 0)),
            pl.BlockSpec((None, None, past, MLA_ROPE), lambda b, t: (b, layer, 0, 0)),
            full(qn), full(kvn), full(wq), full(wkv), full(tq), full(tk),
        ],
        out_specs=pl.BlockSpec((MLA_QT, BRANCH_W), lambda b, t: (b * nqt + t, 0)),
        out_shape=jax.ShapeDtypeStruct((nb * seq, BRANCH_W), BF16),
        scratch_shapes=[pltpu.VMEM((MLA_HEADS, past + seq, MLA_DK), BF16),
                        pltpu.VMEM((past + seq, MLA_HEADS * MLA_V), BF16)],
        compiler_params=_cparams(("arbitrary", "arbitrary")),
        name="lat_mla",
    )(p, cache_ckv, cache_kr, qn, kvn, wq, wkv, tq, tk)


WIN_QB = 128


def _lat_win_kernel(q_ref, k0_ref, k1_ref, k2_ref, v0_ref, v1_ref, v2_ref, ck_ref, cv_ref,
                    t0_ref, t1_ref, t2_ref, sink_ref, o_ref, *, past, seq):
    qb = pl.program_id(1)
    scale = WIN_HEAD_DIM ** -0.5
    n_loc = 3 * WIN_QB
    rows = WIN_GROUP * WIN_QB
    qpos = qb * WIN_QB + lax.broadcasted_iota(jnp.int32, (rows, n_loc), 0) % WIN_QB
    kpos = (qb - 1) * WIN_QB + lax.broadcasted_iota(jnp.int32, (rows, n_loc), 1)
    valid = (kpos >= 0) & (kpos < seq) & (jnp.abs(qpos - kpos) <= WINDOW)
    t1 = t1_ref[...]
    q_heads = lambda kvh: range(kvh * WIN_GROUP, (kvh + 1) * WIN_GROUP)

    def score(kvh):
        cs = slice(kvh * 128, (kvh + 1) * 128)
        keys = jnp.concatenate([
            ck_ref[:, cs],
            _rope(k0_ref[:, cs], t0_ref[...], 32),
            _rope(k1_ref[:, cs], t1, 32),
            _rope(k2_ref[:, cs], t2_ref[...], 32)], axis=0).astype(BF16)
        q = jnp.concatenate([_rope(q_ref[:, h * 128:(h + 1) * 128], t1, 32) for h in q_heads(kvh)],
                            axis=0).astype(BF16)
        s = _dot_nt(q, keys) * scale
        return jnp.concatenate([s[:, :past], jnp.where(valid, s[:, past:], NEG_INF)], axis=1)

    def value(kvh):
        cs = slice(kvh * 128, (kvh + 1) * 128)
        return jnp.concatenate([cv_ref[:, cs], v0_ref[:, cs], v1_ref[:, cs], v2_ref[:, cs]], axis=0).astype(BF16)

    def sink(kvh):
        return jnp.concatenate([jnp.broadcast_to(sink_ref[0:1, h:h + 1], (WIN_QB, 1)) for h in q_heads(kvh)],
                               axis=0)

    def emit(kvh, out):
        for g, h in enumerate(q_heads(kvh)):
            o_ref[:, h * 128:(h + 1) * 128] = out[g * WIN_QB:(g + 1) * WIN_QB, :].astype(BF16)

    _attend_heads(WIN_KV_HEADS, score, value, o_ref, 0, sink=sink, emit=emit)


def _lat_win(p, row0, nb, seq, cache_k, cache_v, layer, tab, sink):
    past = cache_k.shape[2]
    nqb = seq // WIN_QB
    rb0 = row0 // WIN_QB

    def kblk(off, col_blk):
        return pl.BlockSpec(
            (WIN_QB, 256),
            lambda b, t: (rb0 + b * nqb + jnp.clip(t + off, 0, nqb - 1), col_blk))

    def tblk(off):
        return pl.BlockSpec((WIN_QB, 3 * 128), lambda b, t: (jnp.clip(t + off, 0, nqb - 1), 0))

    cache = pl.BlockSpec((None, None, past, 256), lambda b, t: (b, layer, 0, 0))
    return pl.pallas_call(
        functools.partial(_lat_win_kernel, past=past, seq=seq),
        grid=(nb, nqb),
        in_specs=[
            pl.BlockSpec((WIN_QB, 1024), lambda b, t: (rb0 + b * nqb + t, P_WQ // 1024)),
            kblk(-1, P_WK // 256), kblk(0, P_WK // 256), kblk(1, P_WK // 256),
            kblk(-1, P_WV // 256), kblk(0, P_WV // 256), kblk(1, P_WV // 256),
            cache, cache,
            tblk(-1), tblk(0), tblk(1),
            pl.BlockSpec(sink.shape, lambda b, t: (0, 0)),
        ],
        out_specs=pl.BlockSpec((WIN_QB, BRANCH_W), lambda b, t: (b * nqb + t, 0)),
        out_shape=jax.ShapeDtypeStruct((nb * seq, BRANCH_W), BF16),
        compiler_params=_cparams(("arbitrary", "arbitrary")),
        name="lat_window",
    )(p, p, p, p, p, p, p, cache_k, cache_v, tab, tab, tab, sink)


def _lat_nat_kernel(q_ref, k0_ref, k1_ref, k2_ref, v0_ref, v1_ref, v2_ref, ck_ref, cv_ref, bias_ref, o_ref):
    scale = NAT_HEAD_DIM ** -0.5
    past = ck_ref.shape[0]

    def rows_of(h, refs):
        cs = slice(h * 128, (h + 1) * 128)
        return jnp.concatenate([r[:, cs] for r in refs], axis=0).astype(BF16)

    def score(h):
        q = q_ref[:, h * 128:(h + 1) * 128].astype(BF16)
        s = _dot_nt(q, rows_of(h, (ck_ref, k0_ref, k1_ref, k2_ref))) * scale
        return jnp.concatenate([s[:, :past], s[:, past:] + bias_ref[h]], axis=1)

    _attend_heads(NAT_HEADS, score, lambda h: rows_of(h, (cv_ref, v0_ref, v1_ref, v2_ref)), o_ref, 0, group=2)


def _nat_key_start(g):
    return g // 2


def _lat_nat(p, row0, nb, seq, cache_k, cache_v, layer, bias):
    past = cache_k.shape[2]
    qrows = NAT_QROWS * GRID_W
    ng = seq // qrows
    rb0 = row0 // qrows

    def kblk(off, col_blk):
        return pl.BlockSpec((qrows, 1024), lambda g, b: (rb0 + b * ng + _nat_key_start(g) + off, col_blk))

    cache = pl.BlockSpec((None, None, past, 1024), lambda g, b: (b, layer, 0, 0))
    return pl.pallas_call(
        _lat_nat_kernel,
        grid=(ng, nb),
        in_specs=[
            pl.BlockSpec((qrows, 1024), lambda g, b: (rb0 + b * ng + g, P_NQ // 1024)),
            kblk(0, P_NK // 1024), kblk(1, P_NK // 1024), kblk(2, P_NK // 1024),
            kblk(0, P_NV // 1024), kblk(1, P_NV // 1024), kblk(2, P_NV // 1024),
            cache, cache,
            pl.BlockSpec((None, NAT_HEADS, qrows, NAT_KROWS * GRID_W), lambda g, b: (g, 0, 0, 0)),
        ],
        out_specs=pl.BlockSpec((qrows, BRANCH_W), lambda g, b: (b * ng + g, 0)),
        out_shape=jax.ShapeDtypeStruct((nb * seq, BRANCH_W), BF16),
        compiler_params=_cparams(("arbitrary", "arbitrary")),
        name="lat_neighbourhood",
    )(p, p, p, p, p, p, p, cache_k, cache_v, bias)


def _nat_bias_table(rpb, seq):
    rows = seq // GRID_W
    kr = min(NAT_ROWS, rows)
    ng = rows // NAT_QROWS
    g = np.arange(ng)[:, None, None]
    q = np.arange(NAT_QROWS * GRID_W)[None, :, None]
    k = np.arange(NAT_KROWS * GRID_W)[None, None, :]
    r = NAT_QROWS * g + q // GRID_W
    c = q % GRID_W
    key_row = NAT_QROWS * (g // 2) + k // GRID_W
    key_col = k % GRID_W
    rs = np.clip(r - kr // 2, 0, rows - kr)
    cs = np.clip(c - NAT_COLS // 2, 0, GRID_W - NAT_COLS)
    valid = (key_row >= rs) & (key_row < rs + kr) & (key_col >= cs) & (key_col < cs + NAT_COLS)
    n_dr, n_dc = 2 * NAT_ROWS - 1, 2 * NAT_COLS - 1
    cc = np.arange(GRID_W)
    dc = cc[None, :] - cc[:, None] + NAT_COLS - 1
    oh_c = (dc[None] == np.arange(n_dc)[:, None, None]).astype(np.float32)
    rq = np.arange(NAT_QROWS)
    rk = np.arange(NAT_KROWS)
    dr = (NAT_QROWS * (np.arange(ng)[:, None, None] // 2) + rk[None, None, :]
          - NAT_QROWS * np.arange(ng)[:, None, None] - rq[None, :, None] + NAT_ROWS - 1)
    oh_r = (dr[..., None] == np.arange(n_dr)).astype(np.float32)
    hp = lax.Precision.HIGHEST
    toep = jnp.einsum('had,dck->hack', rpb.astype(F32), jnp.asarray(oh_c), precision=hp)
    vals = jnp.einsum('hack,gqra->ghqcrk', toep, jnp.asarray(oh_r), precision=hp)
    vals = vals.reshape(ng, NAT_HEADS, NAT_QROWS * GRID_W, NAT_KROWS * GRID_W)
    return jnp.where(jnp.asarray(valid)[:, None], vals, NEG_INF)


def _merge_kernel(hn_ref, octx_ref, oa_ref, ob_ref, oc_ref, wga_ref, wgb_ref, wgc_ref,
                  wba_ref, wbb_ref, wbc_ref, m_ref, *, ctx_tiles):
    i = pl.program_id(0)
    weights = ((wga_ref, wba_ref), (wgb_ref, wbb_ref), (wgc_ref, wbc_ref))

    def merged(branch):
        hn = hn_ref[...]
        acc = None
        for k, (wg, wb) in enumerate(weights):
            term = jax.nn.sigmoid(_dot(hn, wg[...])) * _dot(branch(k), wb[...])
            acc = term if acc is None else acc + term
        m_ref[...] = acc.astype(BF16)

    @pl.when(i < ctx_tiles)
    def _():
        merged(lambda k: octx_ref[:, k * BRANCH_W:(k + 1) * BRANCH_W])

    @pl.when(i >= ctx_tiles)
    def _():
        lat = (oa_ref, ob_ref, oc_ref)
        merged(lambda k: lat[k][...])


def _merge(hn, o_ctx, o_a, o_b, o_c, w_gate, w_br):
    rows, d = hn.shape
    tm, tn = ROW_TILE, 256
    nj = d // tn
    ctx_tiles = o_ctx.shape[0] // tm
    gate = lambda k: pl.BlockSpec((d, tn), lambda i, j: (0, k * nj + j))
    br = lambda k: pl.BlockSpec((None, BRANCH_W, tn), lambda i, j: (k, 0, j))
    lat = pl.BlockSpec((tm, BRANCH_W), lambda i, j: (jnp.maximum(i - ctx_tiles, 0), 0))
    return pl.pallas_call(
        functools.partial(_merge_kernel, ctx_tiles=ctx_tiles),
        grid=(rows // tm, nj),
        in_specs=[pl.BlockSpec((tm, d), lambda i, j: (i, 0)),
                  pl.BlockSpec((tm, 3 * BRANCH_W), lambda i, j: (jnp.minimum(i, ctx_tiles - 1), 0)),
                  lat, lat, lat,
                  gate(0), gate(1), gate(2), br(0), br(1), br(2)],
        out_specs=pl.BlockSpec((tm, tn), lambda i, j: (i, j)),
        out_shape=jax.ShapeDtypeStruct((rows, d), BF16),
        compiler_params=_cparams(("arbitrary", "arbitrary")),
        name="merge",
    )(hn, o_ctx, o_a, o_b, o_c, w_gate, w_gate, w_gate, w_br, w_br, w_br)


ROUTE_IDX = 8
ROUTE_W = 10


def _out_proj_kernel(m_ref, w_ref, h_ref, mod_ref, g_ref, *rest, routed):
    if routed:
        wr_ref, h1_ref, hn2_ref, route_ref = rest
    else:
        h1_ref, hn2_ref = rest
    h1 = h_ref[...] + mod_ref[2:3, :] * _dot(m_ref[...], w_ref[...])
    h1_ref[...] = h1
    hn2 = _rms(h1, g_ref[...]) * (1.0 + mod_ref[4:5, :]) + mod_ref[3:4, :]
    hn2_ref[...] = hn2.astype(hn2_ref.dtype)
    if routed:
        logits = _dot(hn2.astype(BF16), wr_ref[...])
        lane = lax.broadcasted_iota(jnp.int32, logits.shape, 1).astype(F32)
        lg = jnp.where(lane < N_EXPERTS, logits, -jnp.inf)
        m1 = jnp.max(lg, axis=-1, keepdims=True)
        i1 = jnp.min(jnp.where(lg == m1, lane, float(LANES)), axis=-1, keepdims=True)
        lg2 = jnp.where(lane == i1, -jnp.inf, lg)
        m2 = jnp.max(lg2, axis=-1, keepdims=True)
        i2 = jnp.min(jnp.where(lg2 == m2, lane, float(LANES)), axis=-1, keepdims=True)
        e2 = jnp.exp(m2 - m1)
        w1 = 1.0 / (1.0 + e2)
        w2 = e2 / (1.0 + e2)
        route = (jnp.where(lane == ROUTE_IDX, i1, 0.0)
                 + jnp.where(lane == ROUTE_IDX + 1, i2, 0.0)
                 + jnp.where(lane == ROUTE_W, w1, 0.0)
                 + jnp.where(lane == ROUTE_W + 1, w2, 0.0))
        route_ref[...] = route


def _out_proj(m, w_out, h, mod, g, cond_of_tile, w_router=None):
    rows, d = h.shape
    tm = 512
    sub = ROW_TILE // tm
    routed = w_router is not None
    in_specs = [
        pl.BlockSpec((tm, d), lambda i: (i, 0)),
        pl.BlockSpec((d, d), lambda i: (0, 0)),
        pl.BlockSpec((tm, d), lambda i: (i, 0)),
        pl.BlockSpec((None, 6, d), lambda i: (cond_of_tile(i // sub), 0, 0)),
        pl.BlockSpec((1, d), lambda i: (0, 0)),
    ]
    out_specs = [pl.BlockSpec((tm, d), lambda i: (i, 0)), pl.BlockSpec((tm, d), lambda i: (i, 0))]
    out_shape = [jax.ShapeDtypeStruct((rows, d), F32), jax.ShapeDtypeStruct((rows, d), BF16)]
    args = [m, w_out, h, mod, g]
    if routed:
        out_shape[1] = jax.ShapeDtypeStruct((rows, d), F32)
        in_specs.append(pl.BlockSpec((d, LANES), lambda i: (0, 0)))
        out_specs.append(pl.BlockSpec((tm, LANES), lambda i: (i, 0)))
        out_shape.append(jax.ShapeDtypeStruct((rows, LANES), F32))
        args.append(w_router)
    return pl.pallas_call(
        functools.partial(_out_proj_kernel, routed=routed),
        grid=(rows // tm,),
        in_specs=in_specs, out_specs=out_specs, out_shape=out_shape,
        compiler_params=_cparams(("arbitrary",)),
        name="out_proj",
    )(*args)


def _swiglu_chunk(x, wg_ref, wu_ref, wd_ref):
    g = _dot(x, wg_ref[...])
    u = _dot(x, wu_ref[...])
    a = (g * jax.nn.sigmoid(g) * u).astype(BF16)
    return _dot(a, wd_ref[...])


def _ffn_kernel(x_ref, wg_ref, wu_ref, wd_ref, h_ref, mod_ref, o_ref):
    f = pl.program_id(1)

    @pl.when(f == 0)
    def _():
        o_ref[...] = jnp.zeros_like(o_ref)

    o_ref[...] += _swiglu_chunk(x_ref[...], wg_ref, wu_ref, wd_ref)

    @pl.when(f == pl.num_programs(1) - 1)
    def _():
        o_ref[...] = h_ref[...] + mod_ref[5:6, :] * o_ref[...]


def _ffn(x, wg, wu, wd, h, mod, cond_of_tile):
    rows, d = h.shape
    ff = wg.shape[1]
    tm, tf = FFN_TM, FFN_TF
    sub = ROW_TILE // tm
    return pl.pallas_call(
        _ffn_kernel,
        grid=(rows // tm, ff // tf),
        in_specs=[
            pl.BlockSpec((tm, d), lambda i, f: (i, 0)),
            pl.BlockSpec((d, tf), lambda i, f: (0, f)),
            pl.BlockSpec((d, tf), lambda i, f: (0, f)),
            pl.BlockSpec((tf, d), lambda i, f: (f, 0)),
            pl.BlockSpec((tm, d), lambda i, f: (i, 0)),
            pl.BlockSpec((None, 6, d), lambda i, f: (cond_of_tile(i // sub), 0, 0)),
        ],
        out_specs=pl.BlockSpec((tm, d), lambda i, f: (i, 0)),
        out_shape=jax.ShapeDtypeStruct((rows, d), F32),
        compiler_params=_cparams(("arbitrary", "arbitrary")),
        name="ffn_dense",
    )(x, wg, wu, wd, h, mod)


def _gather_kernel(tr_ref, idx_ref, src_ref, o_ref, buf, sem):
    i = pl.program_id(0)
    n = o_ref.shape[0]
    per_tile = MOE_TM // n
    used = tr_ref[i // per_tile] > (i % per_tile) * n

    @pl.when(used)
    def _():
        def issue(r, carry):
            pltpu.make_async_copy(src_ref.at[pl.ds(idx_ref[0, r], 1), :], buf.at[pl.ds(r, 1), :], sem).start()
            return carry

        lax.fori_loop(0, n, issue, 0, unroll=8)
        pltpu.make_async_copy(src_ref.at[pl.ds(0, n), :], buf, sem).wait()
        o_ref[...] = buf[...].astype(o_ref.dtype)

    @pl.when(jnp.logical_not(used))
    def _():
        o_ref[...] = jnp.zeros_like(o_ref)


def _gather_rows(src, idx, tile_rows, out_dtype):
    n = idx.shape[0]
    d = src.shape[1]
    tg = GATHER_ROWS
    grid_spec = pltpu.PrefetchScalarGridSpec(
        num_scalar_prefetch=1,
        grid=(n // tg,),
        in_specs=[pl.BlockSpec((None, 1, tg), lambda i, tr: (i, 0, 0), memory_space=pltpu.SMEM),
                  pl.BlockSpec(memory_space=pl.ANY)],
        out_specs=pl.BlockSpec((tg, d), lambda i, tr: (i, 0)),
        scratch_shapes=[pltpu.VMEM((tg, d), src.dtype), pltpu.SemaphoreType.DMA(())],
    )
    return pl.pallas_call(
        _gather_kernel,
        grid_spec=grid_spec,
        out_shape=jax.ShapeDtypeStruct((n, d), out_dtype),
        compiler_params=_cparams(("arbitrary",)),
        name="moe_gather",
    )(tile_rows, idx.reshape(n // tg, 1, tg), src)


def _moe_ffn_kernel(te_ref, tv_ref, x_ref, wg_ref, wu_ref, wd_ref, o_ref):
    i = pl.program_id(0)
    f = pl.program_id(1)
    rows_used = tv_ref[i]
    for s in range(MOE_TM // MOE_SLAB):
        rs = slice(s * MOE_SLAB, (s + 1) * MOE_SLAB)
        used = rows_used > s * MOE_SLAB

        @pl.when(f == 0)
        def _():
            o_ref[rs, :] = jnp.zeros((MOE_SLAB, o_ref.shape[1]), o_ref.dtype)

        @pl.when(used)
        def _():
            o_ref[rs, :] += _swiglu_chunk(x_ref[rs, :], wg_ref, wu_ref, wd_ref)


def _moe_ffn(xs, wg, wu, wd, tile_expert, tile_valid):
    n, d = xs.shape
    ff = wg.shape[2]
    tm, tf = MOE_TM, FFN_TF
    nf = ff // tf

    def fidx(i, f, tv):
        return jnp.where(tv[i] > 0, f, nf - 1)

    grid_spec = pltpu.PrefetchScalarGridSpec(
        num_scalar_prefetch=2,
        grid=(n // tm, nf),
        in_specs=[
            pl.BlockSpec((tm, d), lambda i, f, te, tv: (i, 0)),
            pl.BlockSpec((None, d, tf), lambda i, f, te, tv: (te[i], 0, fidx(i, f, tv))),
            pl.BlockSpec((None, d, tf), lambda i, f, te, tv: (te[i], 0, fidx(i, f, tv))),
            pl.BlockSpec((None, tf, d), lambda i, f, te, tv: (te[i], fidx(i, f, tv), 0)),
        ],
        out_specs=pl.BlockSpec((tm, d), lambda i, f, te, tv: (i, 0)),
    )
    return pl.pallas_call(
        _moe_ffn_kernel,
        grid_spec=grid_spec,
        out_shape=jax.ShapeDtypeStruct((n, d), F32),
        compiler_params=_cparams(("arbitrary", "arbitrary")),
        name="moe_ffn",
    )(tile_expert, tile_valid, xs, wg, wu, wd)


def _combine_kernel(p0_ref, p1_ref, ys_ref, route_ref, h_ref, mod_ref, fn_ref, octx_ref, olat_ref,
                    buf0, buf1, sem, *, ctx_steps):
    n = octx_ref.shape[0]

    def issue(r, carry):
        pltpu.make_async_copy(ys_ref.at[pl.ds(p0_ref[0, r], 1), :], buf0.at[pl.ds(r, 1), :], sem.at[0]).start()
        pltpu.make_async_copy(ys_ref.at[pl.ds(p1_ref[0, r], 1), :], buf1.at[pl.ds(r, 1), :], sem.at[1]).start()
        return carry

    lax.fori_loop(0, n, issue, 0, unroll=8)
    pltpu.make_async_copy(ys_ref.at[pl.ds(0, n), :], buf0, sem.at[0]).wait()
    pltpu.make_async_copy(ys_ref.at[pl.ds(0, n), :], buf1, sem.at[1]).wait()
    w0 = route_ref[:, ROUTE_W:ROUTE_W + 1]
    w1 = route_ref[:, ROUTE_W + 1:ROUTE_W + 2]
    y = w0 * buf0[...] + w1 * buf1[...]
    h2 = h_ref[...] + mod_ref[5:6, :] * y
    out = _rms(h2, fn_ref[...])
    i = pl.program_id(0)

    @pl.when(i < ctx_steps)
    def _():
        octx_ref[...] = out

    @pl.when(i >= ctx_steps)
    def _():
        olat_ref[...] = out


def _moe_combine(ys, pos0, pos1, route, h, mod, final_norm, cond_of_tile, rows_ctx):
    rows, d = h.shape
    tc = GATHER_ROWS
    sub = ROW_TILE // tc
    ctx_steps = rows_ctx // tc
    smem = lambda: pl.BlockSpec((None, 1, tc), lambda i: (i, 0, 0), memory_space=pltpu.SMEM)
    return pl.pallas_call(
        functools.partial(_combine_kernel, ctx_steps=ctx_steps),
        grid=(rows // tc,),
        in_specs=[smem(), smem(),
                  pl.BlockSpec(memory_space=pl.ANY),
                  pl.BlockSpec((tc, LANES), lambda i: (i, 0)),
                  pl.BlockSpec((tc, d), lambda i: (i, 0)),
                  pl.BlockSpec((None, 6, d), lambda i: (cond_of_tile(i // sub), 0, 0)),
                  pl.BlockSpec((1, d), lambda i: (0, 0))],
        out_specs=[pl.BlockSpec((tc, d), lambda i: (jnp.minimum(i, ctx_steps - 1), 0)),
                   pl.BlockSpec((tc, d), lambda i: (jnp.maximum(i - ctx_steps, 0), 0))],
        out_shape=[jax.ShapeDtypeStruct((rows_ctx, d), F32), jax.ShapeDtypeStruct((rows - rows_ctx, d), F32)],
        scratch_shapes=[pltpu.VMEM((tc, d), F32), pltpu.VMEM((tc, d), F32), pltpu.SemaphoreType.DMA((2,))],
        compiler_params=_cparams(("arbitrary",)),
        name="moe_combine",
    )(pos0.reshape(rows // tc, 1, tc), pos1.reshape(rows // tc, 1, tc), ys, route, h, mod, final_norm)


def _dispatch_plan(route, tm):
    rows = route.shape[0]
    ids = route[:, ROUTE_IDX:ROUTE_IDX + TOP_K].astype(jnp.int32)
    flat = ids.reshape(-1)
    onehot = (flat[:, None] == jnp.arange(N_EXPERTS)[None, :]).astype(jnp.int32)
    rank = jnp.sum((jnp.cumsum(onehot, axis=0) - onehot) * onehot, axis=1)
    counts = jnp.sum(onehot, axis=0)
    tiles = (counts + tm - 1) // tm
    tile_end = jnp.cumsum(tiles)
    start = (tile_end - tiles) * tm
    slot = start[flat] + rank
    n_tiles = (rows * TOP_K) // tm + N_EXPERTS
    n_slots = n_tiles * tm
    slot_token = (jnp.arange(n_slots, dtype=jnp.int32) % rows).at[slot].set(
        jnp.arange(rows * TOP_K, dtype=jnp.int32) // TOP_K)
    t = jnp.arange(n_tiles)
    tile_expert = jnp.minimum(jnp.sum((t[:, None] >= tile_end[None, :]).astype(jnp.int32), axis=1), N_EXPERTS - 1)
    tile_in_expert = t - (tile_end - tiles)[tile_expert]
    tile_rows = jnp.where(t < tile_end[-1], jnp.clip(counts[tile_expert] - tile_in_expert * tm, 0, tm), 0)
    last_expert = tile_expert[jnp.maximum(tile_end[-1] - 1, 0)]
    tile_expert = jnp.where(tile_rows > 0, tile_expert, last_expert).astype(jnp.int32)
    pos = slot.reshape(rows, TOP_K)
    return slot_token, tile_expert, tile_rows.astype(jnp.int32), pos[:, 0], pos[:, 1]


def _rope_table(seq, n):
    quarter = n // 4
    t = np.arange(seq)
    inv = jnp.power(ROPE_BASE, -jnp.arange(quarter, dtype=F32) * (2.0 / (n // 2)))
    ang_r = jnp.asarray(t // GRID_W, F32)[:, None] * inv[None, :]
    ang_c = jnp.asarray(t % GRID_W, F32)[:, None] * inv[None, :]
    zero = jnp.zeros((seq, quarter), F32)
    cos = jnp.concatenate([jnp.cos(ang_r)] * 2 + [jnp.cos(ang_c)] * 2, axis=1)
    up = jnp.concatenate([-jnp.sin(ang_r), zero, -jnp.sin(ang_c), zero], axis=1)
    dn = jnp.concatenate([zero, jnp.sin(ang_r), zero, jnp.sin(ang_c)], axis=1)
    return cos, up, dn


def _tile_cols(parts, reps, pad_to=None):
    out = [jnp.tile(p, (1, reps)) for p in parts]
    if pad_to is not None:
        out = [jnp.pad(p, ((0, 0), (0, pad_to - p.shape[1]))) for p in out]
    return jnp.concatenate(out, axis=1)


def kernel(x_prompt, x_sample, cache_mla_ckv, cache_mla_krope, cache_win_k, cache_win_v, cache_nat_k, cache_nat_v, c, c_ctx, w_mod, b_mod, norm1, norm2, w_in, mla_q_norm, mla_kv_norm, w_mla_q_up, w_mla_kv_up, win_sink, nat_rpb, w_br_mla, w_br_win, w_br_nat, w_out, w_ff_gate, w_ff_up, w_ff_down, w_router, w_ex_gate, w_ex_up, w_ex_down, final_norm):
    nb_ctx, seq_ctx, d = x_prompt.shape
    nb_lat, seq_lat, _ = x_sample.shape
    past = cache_mla_ckv.shape[2]
    rows_ctx = nb_ctx * seq_ctx
    rows_lat = nb_lat * seq_lat
    assert d == D_MODEL and seq_lat == ROW_TILE and rows_ctx % ROW_TILE == 0
    assert seq_lat // GRID_W == 4 * NAT_QROWS
    assert w_mod.shape[0] == DEPTH == 2
    ctx_tiles = rows_ctx // ROW_TILE

    def cond_of_tile(i):
        return jnp.where(i < ctx_tiles, 0, i - ctx_tiles + 1)

    n_cond = 16
    cond = jnp.zeros((n_cond, d), F32).at[0].set(c_ctx).at[1:1 + nb_lat].set(c)
    mod_all = _modulation(cond, w_mod, b_mod).reshape(DEPTH, n_cond, 6, d)

    w_p, w_gate = _w_in_prep(w_in)
    wq_up = w_mla_q_up.reshape(DEPTH, MLA_Q_RANK, MLA_HEADS, MLA_NOPE + MLA_ROPE)
    wq_up = jnp.concatenate([wq_up[..., :MLA_NOPE].reshape(DEPTH, MLA_Q_RANK, -1),
                             wq_up[..., MLA_NOPE:].reshape(DEPTH, MLA_Q_RANK, -1)], axis=2).astype(BF16)
    wkv_up = w_mla_kv_up.astype(BF16)
    w_br = jnp.stack([w_br_mla, w_br_win, w_br_nat], axis=1).astype(BF16)
    w_out_b = w_out.astype(BF16)
    w_router_p = jnp.pad(w_router, ((0, 0), (0, 0), (0, LANES - N_EXPERTS))).astype(BF16)

    t128 = jnp.concatenate(_rope_table(seq_lat, 128), axis=1)
    t64 = _rope_table(seq_lat, MLA_ROPE)
    t_q = _tile_cols(t64, MLA_HEADS)
    t_k = _tile_cols(t64, 1, pad_to=LANES)

    ck_win = cache_win_k.reshape(nb_lat, DEPTH, past, 256)
    cv_win = cache_win_v.reshape(nb_lat, DEPTH, past, 256)
    ck_nat = cache_nat_k.reshape(nb_lat, DEPTH, past, 1024)
    cv_nat = cache_nat_v.reshape(nb_lat, DEPTH, past, 1024)

    h = jnp.concatenate([x_prompt.reshape(rows_ctx, d), x_sample.reshape(rows_lat, d)], axis=0)
    states = [[] for _ in range(6)]
    for l in range(DEPTH):
        mod = mod_all[l]
        qn = mla_q_norm[l].reshape(1, -1)
        kvn = mla_kv_norm[l].reshape(1, -1)
        sink = win_sink[l].reshape(1, -1)
        p, hn = _mixer_in(h, mod, norm1[l].reshape(1, d), w_p[l], cond_of_tile)
        o_ctx, *st = _ctx_attention(p, nb_ctx, seq_ctx, qn, kvn, wq_up[l], wkv_up[l], sink)
        for lst, arr in zip(states, st):
            lst.append(arr)
        o_a = _lat_mla(p, rows_ctx // seq_lat, nb_lat, seq_lat, cache_mla_ckv, cache_mla_krope, l,
                       qn, kvn, wq_up[l], wkv_up[l], t_q, t_k)
        o_b = _lat_win(p, rows_ctx, nb_lat, seq_lat, ck_win, cv_win, l, t128, sink)
        o_c = _lat_nat(p, rows_ctx, nb_lat, seq_lat, ck_nat, cv_nat, l, _nat_bias_table(nat_rpb[l], seq_lat))
        m = _merge(hn, o_ctx, o_a, o_b, o_c, w_gate[l], w_br[l])
        if l % 2 == 0:
            h1, hn2 = _out_proj(m, w_out_b[l], h, mod, norm2[l].reshape(1, d), cond_of_tile)
            h = _ffn(hn2, w_ff_gate[l // 2].astype(BF16), w_ff_up[l // 2].astype(BF16),
                     w_ff_down[l // 2].astype(BF16), h1, mod, cond_of_tile)
        else:
            h1, hn2, route = _out_proj(m, w_out_b[l], h, mod, norm2[l].reshape(1, d), cond_of_tile,
                                       w_router=w_router_p[l // 2])
            slot_token, tile_expert, tile_valid, pos0, pos1 = _dispatch_plan(route, MOE_TM)
            xs = _gather_rows(hn2, slot_token, tile_valid, BF16)
            ys = _moe_ffn(xs, w_ex_gate[l // 2].astype(BF16), w_ex_up[l // 2].astype(BF16),
                          w_ex_down[l // 2].astype(BF16), tile_expert, tile_valid)
            y_ctx, y_lat = _moe_combine(ys, pos0, pos1, route, h1, mod, final_norm.reshape(1, d),
                                        cond_of_tile, rows_ctx)

    y_prompt = y_ctx.reshape(nb_ctx, seq_ctx, d)
    y_sample = y_lat.reshape(nb_lat, seq_lat, d)
    st = [jnp.stack(s, axis=1) for s in states]
    shp = lambda a, tail: a.reshape(nb_ctx, seq_ctx, DEPTH, -1).transpose(0, 2, 1, 3).reshape((nb_ctx, DEPTH, seq_ctx) + tail)
    return (y_prompt, y_sample,
            shp(st[0], (MLA_KV_RANK,)), shp(st[1], (MLA_ROPE,)),
            shp(st[2], (WIN_KV_HEADS, WIN_HEAD_DIM)), shp(st[3], (WIN_KV_HEADS, WIN_HEAD_DIM)),
            shp(st[4], (NAT_HEADS, NAT_HEAD_DIM)), shp(st[5], (NAT_HEADS, NAT_HEAD_DIM)))
```

```python
import functools

import jax
import jax.numpy as jnp
import numpy as np
from jax import lax
from jax.experimental import pallas as pl
from jax.experimental.pallas import tpu as pltpu

F32 = jnp.float32
BF16 = jnp.bfloat16

D_MODEL = 2048
DEPTH = 2
GRID_W = 64
ROPE_BASE = 10000.0
EPS = 1e-6
NEG_INF = -1e30
MLA_HEADS = 8
MLA_Q_RANK = 512
MLA_KV_RANK = 512
MLA_NOPE = 128
MLA_ROPE = 64
MLA_V = 128
WIN_HEADS = 8
WIN_KV_HEADS = 2
WIN_GROUP = WIN_HEADS // WIN_KV_HEADS
WIN_HEAD_DIM = 128
WINDOW = 128
NAT_HEADS = 8
NAT_HEAD_DIM = 128
NAT_ROWS = 8
NAT_COLS = 16
BRANCH_W = 1024
D_FF = 5632
N_EXPERTS = 8
TOP_K = 2

P_WQ, P_NQ, P_NK, P_NV = 0, 1024, 2048, 3072
P_WK, P_WV = 4096, 4352
P_QD, P_KVD, P_KR = 4608, 5120, 5632
P_COLS = 5760
P_TN = 1152
LANES = 128

VMEM_LIMIT = 56 * 1024 * 1024

ROW_TILE = 1024
FFN_TM = 512
FFN_TF = 512
MOE_TM = 1024
MOE_SLAB = 512
GATHER_ROWS = 256
NAT_QROWS = 4
NAT_KROWS = 12


def _cparams(sem):
    return pltpu.CompilerParams(dimension_semantics=sem, vmem_limit_bytes=VMEM_LIMIT)


def _rms(x, g):
    ms = jnp.mean(x * x, axis=-1, keepdims=True)
    return x * lax.rsqrt(ms + EPS) * g


def _dot(a, b):
    return jnp.dot(a, b, preferred_element_type=F32)


def _dot_nt(a, b):
    return lax.dot_general(a, b, (((1,), (1,)), ((), ())), preferred_element_type=F32)


def _rope(x, tab_ref_or_val, shift):
    n = x.shape[-1]
    t = tab_ref_or_val
    c, s_up, s_dn = t[:, 0:n], t[:, n:2 * n], t[:, 2 * n:3 * n]
    up = pltpu.roll(x, n - shift, axis=1)
    dn = pltpu.roll(x, shift, axis=1)
    return x * c + up * s_up + dn * s_dn


def _softmax_pv(s, v, sink=None):
    m = jnp.max(s, axis=-1, keepdims=True)
    if sink is not None:
        m = jnp.maximum(m, sink)
    e = jnp.exp(s - m)
    den = jnp.sum(e, axis=-1, keepdims=True)
    if sink is not None:
        den = den + jnp.exp(sink - m)
    return _dot(e.astype(BF16), v) * (1.0 / den)


def _attend_heads(n_heads, score, value, o_ref, col0, sink=None, group=None, emit=None):
    group = group or n_heads
    for h0 in range(0, n_heads, group):
        heads = range(h0, h0 + group)
        s = [score(h) for h in heads]
        m = [jnp.max(x, axis=-1, keepdims=True) for x in s]
        if sink is not None:
            sk = [sink(h) for h in heads]
            m = [jnp.maximum(a, b) for a, b in zip(m, sk)]
        e = [jnp.exp(x - a) for x, a in zip(s, m)]
        den = [jnp.sum(x, axis=-1, keepdims=True) for x in e]
        if sink is not None:
            den = [d + jnp.exp(b - a) for d, a, b in zip(den, m, sk)]
        pv = [_dot(x.astype(BF16), value(h)) for x, h in zip(e, heads)]
        for i, h in enumerate(heads):
            out = pv[i] * (1.0 / den[i])
            if emit is not None:
                emit(h, out)
            else:
                o_ref[:, col0 + h * LANES:col0 + (h + 1) * LANES] = out.astype(o_ref.dtype)


def _mod_kernel(c_ref, w_ref, b_ref, o_ref):
    c = c_ref[...]
    s = (c * jax.nn.sigmoid(c)).astype(BF16)
    o_ref[...] = _dot(s, w_ref[...].astype(BF16)) + b_ref[...]


def _modulation(cond, w_mod, b_mod):
    depth, d, n = w_mod.shape
    nc = cond.shape[0]
    tn = 1024
    return pl.pallas_call(
        _mod_kernel,
        grid=(depth, n // tn),
        in_specs=[
            pl.BlockSpec((nc, d), lambda l, j: (0, 0)),
            pl.BlockSpec((None, d, tn), lambda l, j: (l, 0, j)),
            pl.BlockSpec((None, 1, tn), lambda l, j: (l, 0, j)),
        ],
        out_specs=pl.BlockSpec((None, nc, tn), lambda l, j: (l, 0, j)),
        out_shape=jax.ShapeDtypeStruct((depth, nc, n), F32),
        compiler_params=_cparams(("arbitrary", "arbitrary")),
        name="modulation",
    )(cond, w_mod, b_mod.reshape(depth, 1, n))


_W_IN_SEGMENTS = ((1088, 1024), (2624, 1024), (3648, 1024), (4672, 1024), (2112, 256), (2368, 256),
                  (0, 512), (512, 512), (1024, 64))
W_IN_GATES = 5696


def _w_in_prep_kernel(w_ref, p_ref, g_ref):
    col = 0
    for src, width in _W_IN_SEGMENTS:
        p_ref[:, col:col + width] = w_ref[:, src:src + width].astype(BF16)
        col += width
    p_ref[:, col:] = jnp.zeros((p_ref.shape[0], P_COLS - col), BF16)
    g_ref[...] = w_ref[:, W_IN_GATES:].astype(BF16)


def _w_in_prep(w_in):
    depth, d, n = w_in.shape
    tr = 128
    n_gate = n - W_IN_GATES
    return pl.pallas_call(
        _w_in_prep_kernel,
        grid=(depth, d // tr),
        in_specs=[pl.BlockSpec((None, tr, n), lambda l, r: (l, r, 0))],
        out_specs=[pl.BlockSpec((None, tr, P_COLS), lambda l, r: (l, r, 0)),
                   pl.BlockSpec((None, tr, n_gate), lambda l, r: (l, r, 0))],
        out_shape=[jax.ShapeDtypeStruct((depth, d, P_COLS), BF16),
                   jax.ShapeDtypeStruct((depth, d, n_gate), BF16)],
        compiler_params=_cparams(("arbitrary", "arbitrary")),
        name="w_in_prep",
    )(w_in)


def _split_rows_specs(h, tm):
    if not isinstance(h, tuple):
        return [h], [pl.BlockSpec((tm, h.shape[1]), lambda i, *_: (i, 0))], None
    ctx_tiles = h[0].shape[0] // tm
    d = h[0].shape[1]
    return (list(h),
            [pl.BlockSpec((tm, d), lambda i, *_: (jnp.minimum(i, ctx_tiles - 1), 0)),
             pl.BlockSpec((tm, d), lambda i, *_: (jnp.maximum(i - ctx_tiles, 0), 0))],
            ctx_tiles)


def _row_group_value(h_refs, ctx_tiles):
    if ctx_tiles is None:
        return h_refs[0][...]
    return jnp.where(pl.program_id(0) < ctx_tiles, h_refs[0][...], h_refs[1][...])


def _mixer_in_kernel(*refs, ctx_tiles):
    nh = 1 if ctx_tiles is None else 2
    h_refs = refs[:nh]
    mod_ref, g_ref, w_ref, p_ref, hn_ref = refs[nh:]

    @pl.when(pl.program_id(1) == 0)
    def _():
        h = _row_group_value(h_refs, ctx_tiles)
        hn = _rms(h, g_ref[...]) * (1.0 + mod_ref[1:2, :]) + mod_ref[0:1, :]
        hn_ref[...] = hn.astype(BF16)

    p_ref[...] = _dot(hn_ref[...], w_ref[...])


def _mixer_in(h, mod, g, w, layer, cond_of_tile):
    tm, tn = (ROW_TILE // 2 if isinstance(h, tuple) else ROW_TILE), P_TN
    sub = ROW_TILE // tm
    h_args, h_specs, ctx_tiles = _split_rows_specs(h, tm)
    rows = sum(a.shape[0] for a in h_args)
    d = h_args[0].shape[1]
    n = w.shape[2]
    return pl.pallas_call(
        functools.partial(_mixer_in_kernel, ctx_tiles=ctx_tiles),
        grid=(rows // tm, n // tn),
        in_specs=h_specs + [
            pl.BlockSpec((None, 6, d), lambda i, j: (cond_of_tile(i // sub), 0, 0)),
            pl.BlockSpec((1, d), lambda i, j: (0, 0)),
            pl.BlockSpec((None, d, tn), lambda i, j: (layer, 0, j)),
        ],
        out_specs=[
            pl.BlockSpec((tm, tn), lambda i, j: (i, j)),
            pl.BlockSpec((tm, d), lambda i, j: (i, 0)),
        ],
        out_shape=[
            jax.ShapeDtypeStruct((rows, n), F32),
            jax.ShapeDtypeStruct((rows, d), BF16),
        ],
        compiler_params=_cparams(("arbitrary", "arbitrary")),
        name="mixer_in",
    )(*h_args, mod, g, w)


_STATE_WIDTHS = (MLA_KV_RANK, MLA_ROPE, 256, 256, 1024, 1024)
_STATE_HEADS = (None, None, WIN_KV_HEADS, WIN_KV_HEADS, NAT_HEADS, NAT_HEADS)


def _ctx_attn_kernel(p_ref, qn_ref, kvn_ref, wq_ref, wkv_ref, sink_ref, *refs, stacked):
    n_state = len(_STATE_WIDTHS)
    if stacked:
        prev_refs, o_ref, state_refs = refs[:n_state], refs[n_state], refs[n_state + 1:]
    else:
        prev_refs, o_ref, state_refs = None, refs[0], refs[1:]

    def put_state(k, val, slot=DEPTH - 1):
        ref, heads = state_refs[k], _STATE_HEADS[k]
        if not stacked:
            ref[...] = val
        elif heads is None:
            ref[slot] = val
        else:
            for j in range(heads):
                ref[slot, :, j, :] = val[:, j * LANES:(j + 1) * LANES]

    if stacked:
        for k in range(n_state):
            put_state(k, prev_refs[k][...], slot=0)

    q = _dot(_rms(p_ref[:, P_QD:P_QD + MLA_Q_RANK], qn_ref[...]).astype(BF16), wq_ref[...])
    ckv = _rms(p_ref[:, P_KVD:P_KVD + MLA_KV_RANK], kvn_ref[...])
    put_state(0, ckv)
    kv = _dot(ckv.astype(BF16), wkv_ref[...]).astype(BF16)
    kr = p_ref[:, P_KR:P_KR + MLA_ROPE]
    put_state(1, kr)
    krb = kr.astype(BF16)
    scale_a = (MLA_NOPE + MLA_ROPE) ** -0.5
    nope_w = MLA_HEADS * MLA_NOPE

    def score_a(h):
        qn = q[:, h * MLA_NOPE:(h + 1) * MLA_NOPE].astype(BF16)
        qr = q[:, nope_w + h * MLA_ROPE:nope_w + (h + 1) * MLA_ROPE].astype(BF16)
        return (_dot_nt(qn, kv[:, h * 256:h * 256 + MLA_NOPE]) + _dot_nt(qr, krb)) * scale_a

    _attend_heads(MLA_HEADS, score_a, lambda h: kv[:, h * 256 + MLA_NOPE:(h + 1) * 256], o_ref, 0)
    put_state(2, p_ref[:, P_WK:P_WK + 256])
    put_state(3, p_ref[:, P_WV:P_WV + 256])
    scale_b = WIN_HEAD_DIM ** -0.5

    def score_b(h):
        kvh = h // WIN_GROUP
        qh = p_ref[:, P_WQ + h * 128:P_WQ + (h + 1) * 128].astype(BF16)
        return _dot_nt(qh, p_ref[:, P_WK + kvh * 128:P_WK + (kvh + 1) * 128].astype(BF16)) * scale_b

    def value_b(h):
        kvh = h // WIN_GROUP
        return p_ref[:, P_WV + kvh * 128:P_WV + (kvh + 1) * 128].astype(BF16)

    _attend_heads(WIN_HEADS, score_b, value_b, o_ref, BRANCH_W, lambda h: sink_ref[0:1, h:h + 1])
    put_state(4, p_ref[:, P_NK:P_NK + 1024])
    put_state(5, p_ref[:, P_NV:P_NV + 1024])
    scale_c = NAT_HEAD_DIM ** -0.5

    def score_c(h):
        qh = p_ref[:, P_NQ + h * 128:P_NQ + (h + 1) * 128].astype(BF16)
        return _dot_nt(qh, p_ref[:, P_NK + h * 128:P_NK + (h + 1) * 128].astype(BF16)) * scale_c

    _attend_heads(NAT_HEADS, score_c, lambda h: p_ref[:, P_NV + h * 128:P_NV + (h + 1) * 128].astype(BF16),
                  o_ref, 2 * BRANCH_W)


def _ctx_attention(p, nb, seq, qn, kvn, wq, wkv, sink, prev_states=None):
    full = lambda a: pl.BlockSpec(a.shape, lambda b: (0,) * a.ndim)
    row = lambda w: pl.BlockSpec((seq, w), lambda b: (b, 0))
    stacked = prev_states is not None
    in_specs = [pl.BlockSpec((seq, P_COLS), lambda b: (b, 0)),
                full(qn), full(kvn), full(wq), full(wkv), full(sink)]
    args = [p, qn, kvn, wq, wkv, sink]
    out_specs = [row(3 * BRANCH_W)]
    out_shape = [jax.ShapeDtypeStruct((nb * seq, 3 * BRANCH_W), BF16)]
    for w, heads in zip(_STATE_WIDTHS, _STATE_HEADS):
        if not stacked:
            out_specs.append(row(w))
            out_shape.append(jax.ShapeDtypeStruct((nb * seq, w), F32))
        else:
            tail = (w,) if heads is None else (heads, w // heads)
            out_specs.append(pl.BlockSpec((None, DEPTH, seq) + tail, lambda b, n=len(tail): (b,) + (0,) * (n + 2)))
            out_shape.append(jax.ShapeDtypeStruct((nb, DEPTH, seq) + tail, F32))
    if stacked:
        in_specs += [row(w) for w in _STATE_WIDTHS]
        args += list(prev_states)
    outs = pl.pallas_call(
        functools.partial(_ctx_attn_kernel, stacked=stacked),
        grid=(nb,),
        in_specs=in_specs, out_specs=out_specs, out_shape=out_shape,
        compiler_params=_cparams(("arbitrary",)),
        name="ctx_attention",
    )(*args)
    return outs[0], outs[1:]


MLA_QT = 512
MLA_KC = 256
MLA_DK = MLA_NOPE + MLA_ROPE


def _lat_mla_kernel(p_ref, cckv_ref, ckr_ref, qn_ref, kvn_ref, wq_ref, wkv_ref, tq_ref, tk_ref,
                    o_ref, k_scr, v_scr, *, past, seq):
    qt = pl.program_id(1)

    def put_keys(r0, n, kv, kr):
        krb = kr.astype(BF16)
        for h in range(MLA_HEADS):
            k_scr[h, r0:r0 + n, 0:MLA_NOPE] = kv[:, h * 256:h * 256 + MLA_NOPE].astype(BF16)
            k_scr[h, r0:r0 + n, MLA_NOPE:MLA_DK] = krb
            v_scr[r0:r0 + n, h * MLA_V:(h + 1) * MLA_V] = kv[:, h * 256 + MLA_NOPE:(h + 1) * 256].astype(BF16)

    @pl.when(qt == 0)
    def _():
        for c in range(past // MLA_KC):
            r0 = c * MLA_KC
            kv = _dot(cckv_ref[r0:r0 + MLA_KC, :].astype(BF16), wkv_ref[...])
            put_keys(r0, MLA_KC, kv, ckr_ref[r0:r0 + MLA_KC, :])
        for c in range(seq // MLA_KC):
            r0 = c * MLA_KC
            ckv = _rms(p_ref[r0:r0 + MLA_KC, MLA_Q_RANK:MLA_Q_RANK + MLA_KV_RANK], kvn_ref[...])
            kv = _dot(ckv.astype(BF16), wkv_ref[...])
            krp = p_ref[r0:r0 + MLA_KC, 2 * MLA_Q_RANK:2 * MLA_Q_RANK + LANES]
            kr = _rope(krp, tk_ref[r0:r0 + MLA_KC, :], MLA_ROPE // 4)
            put_keys(past + r0, MLA_KC, kv, kr[:, 0:MLA_ROPE])

    r0 = pl.multiple_of(qt * MLA_QT, MLA_QT)
    qd = p_ref[pl.ds(r0, MLA_QT), 0:MLA_Q_RANK]
    q = _dot(_rms(qd, qn_ref[...]).astype(BF16), wq_ref[...])
    nope_w = MLA_HEADS * MLA_NOPE
    q_rope = _rope(q[:, nope_w:], tq_ref[pl.ds(r0, MLA_QT), :], MLA_ROPE // 4)
    scale = MLA_DK ** -0.5

    def score(h):
        qh = jnp.concatenate([q[:, h * MLA_NOPE:(h + 1) * MLA_NOPE],
                              q_rope[:, h * MLA_ROPE:(h + 1) * MLA_ROPE]], axis=1).astype(BF16)
        return _dot_nt(qh, k_scr[h]) * scale

    _attend_heads(MLA_HEADS, score, lambda h: v_scr[:, h * MLA_V:(h + 1) * MLA_V], o_ref, 0, group=2)


def _lat_mla(p, row_blk0, nb, seq, cache_ckv, cache_kr, layer, qn, kvn, wq, wkv, tq, tk):
    past = cache_ckv.shape[2]
    full = lambda a: pl.BlockSpec(a.shape, lambda b, t: (0,) * a.ndim)
    nqt = seq // MLA_QT
    return pl.pallas_call(
        functools.partial(_lat_mla_kernel, past=past, seq=seq),
        grid=(nb, nqt),
        in_specs=[
            pl.BlockSpec((seq, P_TN), lambda b, t: (row_blk0 + b, P_QD // P_TN)),
            pl.BlockSpec((None, None, past, MLA_KV_RANK), lambda b, t: (b, layer, 0, 0)),
            pl.BlockSpec((None, None, past, MLA_ROPE), lambda b, t: (b, layer, 0, 0)),
            full(qn), full(kvn), full(wq), full(wkv), full(tq), full(tk),
        ],
        out_specs=pl.BlockSpec((MLA_QT, BRANCH_W), lambda b, t: (b * nqt + t, 0)),
        out_shape=jax.ShapeDtypeStruct((nb * seq, BRANCH_W), BF16),
        scratch_shapes=[pltpu.VMEM((MLA_HEADS, past + seq, MLA_DK), BF16),
                        pltpu.VMEM((past + seq, MLA_HEADS * MLA_V), BF16)],
        compiler_params=_cparams(("arbitrary", "arbitrary")),
        name="lat_mla",
    )(p, cache_ckv, cache_kr, qn, kvn, wq, wkv, tq, tk)


WIN_QB = 128


def _lat_win_kernel(q_ref, k0_ref, k1_ref, k2_ref, v0_ref, v1_ref, v2_ref, ck_ref, cv_ref,
                    t0_ref, t1_ref, t2_ref, sink_ref, o_ref, *, past, seq):
    qb = pl.program_id(1)
    scale = WIN_HEAD_DIM ** -0.5
    n_loc = 3 * WIN_QB
    rows = WIN_GROUP * WIN_QB
    qpos = qb * WIN_QB + lax.broadcasted_iota(jnp.int32, (rows, n_loc), 0) % WIN_QB
    kpos = (qb - 1) * WIN_QB + lax.broadcasted_iota(jnp.int32, (rows, n_loc), 1)
    valid = (kpos >= 0) & (kpos < seq) & (jnp.abs(qpos - kpos) <= WINDOW)
    t1 = t1_ref[...]
    q_heads = lambda kvh: range(kvh * WIN_GROUP, (kvh + 1) * WIN_GROUP)

    def score(kvh):
        cs = slice(kvh * 128, (kvh + 1) * 128)
        keys = jnp.concatenate([
            ck_ref[:, cs],
            _rope(k0_ref[:, cs], t0_ref[...], 32),
            _rope(k1_ref[:, cs], t1, 32),
            _rope(k2_ref[:, cs], t2_ref[...], 32)], axis=0).astype(BF16)
        q = jnp.concatenate([_rope(q_ref[:, h * 128:(h + 1) * 128], t1, 32) for h in q_heads(kvh)],
                            axis=0).astype(BF16)
        s = _dot_nt(q, keys) * scale
        return jnp.concatenate([s[:, :past], jnp.where(valid, s[:, past:], NEG_INF)], axis=1)

    def value(kvh):
        cs = slice(kvh * 128, (kvh + 1) * 128)
        return jnp.concatenate([cv_ref[:, cs], v0_ref[:, cs], v1_ref[:, cs], v2_ref[:, cs]], axis=0).astype(BF16)

    def sink(kvh):
        return jnp.concatenate([jnp.broadcast_to(sink_ref[0:1, h:h + 1], (WIN_QB, 1)) for h in q_heads(kvh)],
                               axis=0)

    def emit(kvh, out):
        for g, h in enumerate(q_heads(kvh)):
            o_ref[:, h * 128:(h + 1) * 128] = out[g * WIN_QB:(g + 1) * WIN_QB, :].astype(BF16)

    _attend_heads(WIN_KV_HEADS, score, value, o_ref, 0, sink=sink, emit=emit)


def _lat_win(p, row0, nb, seq, cache_k, cache_v, layer, tab, sink):
    past = cache_k.shape[2]
    nqb = seq // WIN_QB
    rb0 = row0 // WIN_QB

    def kblk(off, col_blk):
        return pl.BlockSpec(
            (WIN_QB, 256),
            lambda b, t: (rb0 + b * nqb + jnp.clip(t + off, 0, nqb - 1), col_blk))

    def tblk(off):
        return pl.BlockSpec((WIN_QB, 3 * 128), lambda b, t: (jnp.clip(t + off, 0, nqb - 1), 0))

    cache = pl.BlockSpec((None, None, past, 256), lambda b, t: (b, layer, 0, 0))
    return pl.pallas_call(
        functools.partial(_lat_win_kernel, past=past, seq=seq),
        grid=(nb, nqb),
        in_specs=[
            pl.BlockSpec((WIN_QB, 1024), lambda b, t: (rb0 + b * nqb + t, P_WQ // 1024)),
            kblk(-1, P_WK // 256), kblk(0, P_WK // 256), kblk(1, P_WK // 256),
            kblk(-1, P_WV // 256), kblk(0, P_WV // 256), kblk(1, P_WV // 256),
            cache, cache,
            tblk(-1), tblk(0), tblk(1),
            pl.BlockSpec(sink.shape, lambda b, t: (0, 0)),
        ],
        out_specs=pl.BlockSpec((WIN_QB, BRANCH_W), lambda b, t: (b * nqb + t, 0)),
        out_shape=jax.ShapeDtypeStruct((nb * seq, BRANCH_W), BF16),
        compiler_params=_cparams(("arbitrary", "arbitrary")),
        name="lat_window",
    )(p, p, p, p, p, p, p, cache_k, cache_v, tab, tab, tab, sink)


def _lat_nat_kernel(q_ref, k0_ref, k1_ref, k2_ref, v0_ref, v1_ref, v2_ref, ck_ref, cv_ref, bias_ref, o_ref,
                    *, grid_rows):
    g = pl.program_id(0)
    scale = NAT_HEAD_DIM ** -0.5
    past = ck_ref.shape[0]
    win_rows = min(NAT_ROWS, grid_rows)
    lane = lax.broadcasted_iota(jnp.int32, (GRID_W, 2 * GRID_W), 1)

    def rows_of(h, refs):
        cs = slice(h * 128, (h + 1) * 128)
        return jnp.concatenate([r[:, cs] for r in refs], axis=0).astype(BF16)

    def pair_plan(rq, rkp):
        r = NAT_QROWS * g + rq
        kr0 = NAT_QROWS * _nat_key_start(g) + 2 * rkp
        rs = jnp.clip(r - win_rows // 2, 0, grid_rows - win_rows)
        in_win = lambda kr: ((kr >= rs) & (kr < rs + win_rows)).astype(jnp.int32)
        ok = jnp.where(lane < GRID_W, in_win(kr0), in_win(kr0 + 1)) > 0
        return jnp.clip(kr0 - r + NAT_ROWS - 1, 0, 2 * NAT_ROWS - 1), ok

    plans = [[pair_plan(rq, rkp) for rkp in range(NAT_KROWS // 2)] for rq in range(NAT_QROWS)]

    def bias(h):
        return jnp.concatenate(
            [jnp.concatenate([jnp.where(ok, bias_ref[h, a], NEG_INF) for a, ok in row], axis=1) for row in plans],
            axis=0)

    def score(h):
        q = q_ref[:, h * 128:(h + 1) * 128].astype(BF16)
        s = _dot_nt(q, rows_of(h, (ck_ref, k0_ref, k1_ref, k2_ref))) * scale
        return jnp.concatenate([s[:, :past], s[:, past:] + bias(h)], axis=1)

    _attend_heads(NAT_HEADS, score, lambda h: rows_of(h, (cv_ref, v0_ref, v1_ref, v2_ref)), o_ref, 0, group=2)


def _nat_key_start(g):
    return g // 2


def _lat_nat(p, row0, nb, seq, cache_k, cache_v, layer, bias):
    past = cache_k.shape[2]
    qrows = NAT_QROWS * GRID_W
    ng = seq // qrows
    rb0 = row0 // qrows

    def kblk(off, col_blk):
        return pl.BlockSpec((qrows, 1024), lambda g, b: (rb0 + b * ng + _nat_key_start(g) + off, col_blk))

    cache = pl.BlockSpec((None, None, past, 1024), lambda g, b: (b, layer, 0, 0))
    return pl.pallas_call(
        functools.partial(_lat_nat_kernel, grid_rows=seq // GRID_W),
        grid=(ng, nb),
        in_specs=[
            pl.BlockSpec((qrows, 1024), lambda g, b: (rb0 + b * ng + g, P_NQ // 1024)),
            kblk(0, P_NK // 1024), kblk(1, P_NK // 1024), kblk(2, P_NK // 1024),
            kblk(0, P_NV // 1024), kblk(1, P_NV // 1024), kblk(2, P_NV // 1024),
            cache, cache,
            pl.BlockSpec(bias.shape, lambda g, b: (0, 0, 0, 0)),
        ],
        out_specs=pl.BlockSpec((qrows, BRANCH_W), lambda g, b: (b * ng + g, 0)),
        out_shape=jax.ShapeDtypeStruct((nb * seq, BRANCH_W), BF16),
        compiler_params=_cparams(("arbitrary", "arbitrary")),
        name="lat_neighbourhood",
    )(p, p, p, p, p, p, p, cache_k, cache_v, bias)


def _nat_bias_blocks(rpb):
    n_dr, n_dc = 2 * NAT_ROWS - 1, 2 * NAT_COLS - 1
    cc = np.arange(GRID_W)
    dc = cc[None, :] - cc[:, None] + NAT_COLS - 1
    cs = np.clip(cc - NAT_COLS // 2, 0, GRID_W - NAT_COLS)[:, None]
    col_ok = (cc[None, :] >= cs) & (cc[None, :] < cs + NAT_COLS)
    onehot = ((dc[None] == np.arange(n_dc)[:, None, None]) & col_ok[None]).astype(np.float32)
    toep = jnp.einsum('had,dck->hack', rpb.astype(F32), jnp.asarray(onehot), precision=lax.Precision.HIGHEST)
    toep = jnp.where(jnp.asarray(col_ok), toep, NEG_INF)
    fill = jnp.full((NAT_HEADS, 2 * NAT_ROWS + 1 - n_dr, GRID_W, GRID_W), NEG_INF, F32)
    toep = jnp.concatenate([toep, fill], axis=1)
    return jnp.concatenate([toep[:, :-1], toep[:, 1:]], axis=-1)


def _merge_kernel(hn_ref, octx_ref, oa_ref, ob_ref, oc_ref, wga_ref, wgb_ref, wgc_ref,
                  wba_ref, wbb_ref, wbc_ref, m_ref, *, ctx_tiles):
    i = pl.program_id(0)
    weights = ((wga_ref, wba_ref), (wgb_ref, wbb_ref), (wgc_ref, wbc_ref))

    def merged(branch):
        hn = hn_ref[...]
        acc = None
        for k, (wg, wb) in enumerate(weights):
            term = jax.nn.sigmoid(_dot(hn, wg[...])) * _dot(branch(k), wb[...])
            acc = term if acc is None else acc + term
        m_ref[...] = acc.astype(BF16)

    @pl.when(i < ctx_tiles)
    def _():
        merged(lambda k: octx_ref[:, k * BRANCH_W:(k + 1) * BRANCH_W])

    @pl.when(i >= ctx_tiles)
    def _():
        lat = (oa_ref, ob_ref, oc_ref)
        merged(lambda k: lat[k][...])


def _merge(hn, o_ctx, o_a, o_b, o_c, w_gate, w_br, layer):
    rows, d = hn.shape
    tm, tn = ROW_TILE, 256
    nj = d // tn
    ctx_tiles = o_ctx.shape[0] // tm
    gate = lambda k: pl.BlockSpec((None, d, tn), lambda i, j: (layer, 0, k * nj + j))
    br = lambda k: pl.BlockSpec((None, None, BRANCH_W, tn), lambda i, j: (layer, k, 0, j))
    lat = pl.BlockSpec((tm, BRANCH_W), lambda i, j: (jnp.maximum(i - ctx_tiles, 0), 0))
    return pl.pallas_call(
        functools.partial(_merge_kernel, ctx_tiles=ctx_tiles),
        grid=(rows // tm, nj),
        in_specs=[pl.BlockSpec((tm, d), lambda i, j: (i, 0)),
                  pl.BlockSpec((tm, 3 * BRANCH_W), lambda i, j: (jnp.minimum(i, ctx_tiles - 1), 0)),
                  lat, lat, lat,
                  gate(0), gate(1), gate(2), br(0), br(1), br(2)],
        out_specs=pl.BlockSpec((tm, tn), lambda i, j: (i, j)),
        out_shape=jax.ShapeDtypeStruct((rows, d), BF16),
        compiler_params=_cparams(("arbitrary", "arbitrary")),
        name="merge",
    )(hn, o_ctx, o_a, o_b, o_c, w_gate, w_gate, w_gate, w_br, w_br, w_br)


ROUTE_IDX = 8
ROUTE_W = 10


def _out_proj_kernel(m_ref, w_ref, mod_ref, g_ref, *rest, routed, ctx_tiles):
    nh = 1 if ctx_tiles is None else 2
    h_refs, rest = rest[:nh], rest[nh:]
    if routed:
        wr_ref, h1_ref, hn2_ref, route_ref = rest
    else:
        h1_ref, hn2_ref = rest
    h1 = _row_group_value(h_refs, ctx_tiles) + mod_ref[2:3, :] * _dot(m_ref[...], w_ref[...])
    h1_ref[...] = h1
    hn2 = _rms(h1, g_ref[...]) * (1.0 + mod_ref[4:5, :]) + mod_ref[3:4, :]
    hn2_ref[...] = hn2.astype(hn2_ref.dtype)
    if routed:
        logits = _dot(hn2.astype(BF16), wr_ref[...])
        lane = lax.broadcasted_iota(jnp.int32, logits.shape, 1).astype(F32)
        lg = jnp.where(lane < N_EXPERTS, logits, -jnp.inf)
        m1 = jnp.max(lg, axis=-1, keepdims=True)
        i1 = jnp.min(jnp.where(lg == m1, lane, float(LANES)), axis=-1, keepdims=True)
        lg2 = jnp.where(lane == i1, -jnp.inf, lg)
        m2 = jnp.max(lg2, axis=-1, keepdims=True)
        i2 = jnp.min(jnp.where(lg2 == m2, lane, float(LANES)), axis=-1, keepdims=True)
        e2 = jnp.exp(m2 - m1)
        w1 = 1.0 / (1.0 + e2)
        w2 = e2 / (1.0 + e2)
        route = (jnp.where(lane == ROUTE_IDX, i1, 0.0)
                 + jnp.where(lane == ROUTE_IDX + 1, i2, 0.0)
                 + jnp.where(lane == ROUTE_W, w1, 0.0)
                 + jnp.where(lane == ROUTE_W + 1, w2, 0.0))
        route_ref[...] = route


def _out_proj(m, w_out, h, mod, g, layer, cond_of_tile, w_router=None):
    rows, d = m.shape
    tm = 512
    sub = ROW_TILE // tm
    routed = w_router is not None
    h_args, h_specs, ctx_tiles = _split_rows_specs(h, tm)
    in_specs = [
        pl.BlockSpec((tm, d), lambda i: (i, 0)),
        pl.BlockSpec((None, d, d), lambda i: (layer, 0, 0)),
        pl.BlockSpec((None, 6, d), lambda i: (cond_of_tile(i // sub), 0, 0)),
        pl.BlockSpec((1, d), lambda i: (0, 0)),
    ] + h_specs
    out_specs = [pl.BlockSpec((tm, d), lambda i: (i, 0)), pl.BlockSpec((tm, d), lambda i: (i, 0))]
    out_shape = [jax.ShapeDtypeStruct((rows, d), F32), jax.ShapeDtypeStruct((rows, d), BF16)]
    args = [m, w_out, mod, g] + h_args
    if routed:
        out_shape[1] = jax.ShapeDtypeStruct((rows, d), F32)
        in_specs.append(pl.BlockSpec((d, LANES), lambda i: (0, 0)))
        out_specs.append(pl.BlockSpec((tm, LANES), lambda i: (i, 0)))
        out_shape.append(jax.ShapeDtypeStruct((rows, LANES), F32))
        args.append(w_router)
    return pl.pallas_call(
        functools.partial(_out_proj_kernel, routed=routed, ctx_tiles=ctx_tiles),
        grid=(rows // tm,),
        in_specs=in_specs, out_specs=out_specs, out_shape=out_shape,
        compiler_params=_cparams(("arbitrary",)),
        name="out_proj",
    )(*args)


def _swiglu_chunk(x, wg_ref, wu_ref, wd_ref):
    g = _dot(x, wg_ref[...])
    u = _dot(x, wu_ref[...])
    a = (g * jax.nn.sigmoid(g) * u).astype(BF16)
    return _dot(a, wd_ref[...])


def _ffn_kernel(x_ref, wg_ref, wu_ref, wd_ref, h_ref, mod_ref, o_ref):
    f = pl.program_id(1)

    @pl.when(f == 0)
    def _():
        o_ref[...] = jnp.zeros_like(o_ref)

    o_ref[...] += _swiglu_chunk(x_ref[...], wg_ref, wu_ref, wd_ref)

    @pl.when(f == pl.num_programs(1) - 1)
    def _():
        o_ref[...] = h_ref[...] + mod_ref[5:6, :] * o_ref[...]


def _ffn(x, wg, wu, wd, h, mod, cond_of_tile):
    rows, d = h.shape
    ff = wg.shape[1]
    tm, tf = FFN_TM, FFN_TF
    sub = ROW_TILE // tm
    return pl.pallas_call(
        _ffn_kernel,
        grid=(rows // tm, ff // tf),
        in_specs=[
            pl.BlockSpec((tm, d), lambda i, f: (i, 0)),
            pl.BlockSpec((d, tf), lambda i, f: (0, f)),
            pl.BlockSpec((d, tf), lambda i, f: (0, f)),
            pl.BlockSpec((tf, d), lambda i, f: (f, 0)),
            pl.BlockSpec((tm, d), lambda i, f: (i, 0)),
            pl.BlockSpec((None, 6, d), lambda i, f: (cond_of_tile(i // sub), 0, 0)),
        ],
        out_specs=pl.BlockSpec((tm, d), lambda i, f: (i, 0)),
        out_shape=jax.ShapeDtypeStruct((rows, d), F32),
        compiler_params=_cparams(("arbitrary", "arbitrary")),
        name="ffn_dense",
    )(x, wg, wu, wd, h, mod)


def _gather_kernel(tr_ref, idx_ref, src_ref, o_ref, buf, sem):
    i = pl.program_id(0)
    n = o_ref.shape[0]
    per_tile = MOE_TM // n
    used = tr_ref[i // per_tile] > (i % per_tile) * n

    @pl.when(used)
    def _():
        def issue(r, carry):
            pltpu.make_async_copy(src_ref.at[pl.ds(idx_ref[0, r], 1), :], buf.at[pl.ds(r, 1), :], sem).start()
            return carry

        lax.fori_loop(0, n, issue, 0, unroll=8)
        pltpu.make_async_copy(src_ref.at[pl.ds(0, n), :], buf, sem).wait()
        o_ref[...] = buf[...].astype(o_ref.dtype)

    @pl.when(jnp.logical_not(used))
    def _():
        o_ref[...] = jnp.zeros_like(o_ref)


def _gather_rows(src, idx, tile_rows, out_dtype):
    n = idx.shape[0]
    d = src.shape[1]
    tg = GATHER_ROWS
    grid_spec = pltpu.PrefetchScalarGridSpec(
        num_scalar_prefetch=1,
        grid=(n // tg,),
        in_specs=[pl.BlockSpec((None, 1, tg), lambda i, tr: (i, 0, 0), memory_space=pltpu.SMEM),
                  pl.BlockSpec(memory_space=pl.ANY)],
        out_specs=pl.BlockSpec((tg, d), lambda i, tr: (i, 0)),
        scratch_shapes=[pltpu.VMEM((tg, d), src.dtype), pltpu.SemaphoreType.DMA(())],
    )
    return pl.pallas_call(
        _gather_kernel,
        grid_spec=grid_spec,
        out_shape=jax.ShapeDtypeStruct((n, d), out_dtype),
        compiler_params=_cparams(("arbitrary",)),
        name="moe_gather",
    )(tile_rows, idx.reshape(n // tg, 1, tg), src)


def _moe_ffn_kernel(te_ref, tv_ref, x_ref, wg_ref, wu_ref, wd_ref, o_ref):
    i = pl.program_id(0)
    f = pl.program_id(1)
    rows_used = tv_ref[i]
    for s in range(MOE_TM // MOE_SLAB):
        rs = slice(s * MOE_SLAB, (s + 1) * MOE_SLAB)
        used = rows_used > s * MOE_SLAB

        @pl.when(f == 0)
        def _():
            o_ref[rs, :] = jnp.zeros((MOE_SLAB, o_ref.shape[1]), o_ref.dtype)

        @pl.when(used)
        def _():
            o_ref[rs, :] += _swiglu_chunk(x_ref[rs, :], wg_ref, wu_ref, wd_ref)


def _moe_ffn(xs, wg, wu, wd, tile_expert, tile_valid):
    n, d = xs.shape
    ff = wg.shape[2]
    tm, tf = MOE_TM, FFN_TF
    nf = ff // tf

    def fidx(i, f, tv):
        return jnp.where(tv[i] > 0, f, nf - 1)

    grid_spec = pltpu.PrefetchScalarGridSpec(
        num_scalar_prefetch=2,
        grid=(n // tm, nf),
        in_specs=[
            pl.BlockSpec((tm, d), lambda i, f, te, tv: (i, 0)),
            pl.BlockSpec((None, d, tf), lambda i, f, te, tv: (te[i], 0, fidx(i, f, tv))),
            pl.BlockSpec((None, d, tf), lambda i, f, te, tv: (te[i], 0, fidx(i, f, tv))),
            pl.BlockSpec((None, tf, d), lambda i, f, te, tv: (te[i], fidx(i, f, tv), 0)),
        ],
        out_specs=pl.BlockSpec((tm, d), lambda i, f, te, tv: (i, 0)),
    )
    return pl.pallas_call(
        _moe_ffn_kernel,
        grid_spec=grid_spec,
        out_shape=jax.ShapeDtypeStruct((n, d), F32),
        compiler_params=_cparams(("arbitrary", "arbitrary")),
        name="moe_ffn",
    )(tile_expert, tile_valid, xs, wg, wu, wd)


def _combine_kernel(p0_ref, p1_ref, ys_ref, route_ref, h_ref, mod_ref, fn_ref, octx_ref, olat_ref,
                    buf0, buf1, sem, *, ctx_steps):
    n = octx_ref.shape[0]

    def issue(r, carry):
        pltpu.make_async_copy(ys_ref.at[pl.ds(p0_ref[0, r], 1), :], buf0.at[pl.ds(r, 1), :], sem.at[0]).start()
        pltpu.make_async_copy(ys_ref.at[pl.ds(p1_ref[0, r], 1), :], buf1.at[pl.ds(r, 1), :], sem.at[1]).start()
        return carry

    lax.fori_loop(0, n, issue, 0, unroll=8)
    pltpu.make_async_copy(ys_ref.at[pl.ds(0, n), :], buf0, sem.at[0]).wait()
    pltpu.make_async_copy(ys_ref.at[pl.ds(0, n), :], buf1, sem.at[1]).wait()
    w0 = route_ref[:, ROUTE_W:ROUTE_W + 1]
    w1 = route_ref[:, ROUTE_W + 1:ROUTE_W + 2]
    y = w0 * buf0[...] + w1 * buf1[...]
    h2 = h_ref[...] + mod_ref[5:6, :] * y
    out = _rms(h2, fn_ref[...])
    i = pl.program_id(0)

    @pl.when(i < ctx_steps)
    def _():
        octx_ref[...] = out

    @pl.when(i >= ctx_steps)
    def _():
        olat_ref[...] = out


def _moe_combine(ys, pos0, pos1, route, h, mod, final_norm, cond_of_tile, rows_ctx):
    rows, d = h.shape
    tc = GATHER_ROWS
    sub = ROW_TILE // tc
    ctx_steps = rows_ctx // tc
    smem = lambda: pl.BlockSpec((None, 1, tc), lambda i: (i, 0, 0), memory_space=pltpu.SMEM)
    return pl.pallas_call(
        functools.partial(_combine_kernel, ctx_steps=ctx_steps),
        grid=(rows // tc,),
        in_specs=[smem(), smem(),
                  pl.BlockSpec(memory_space=pl.ANY),
                  pl.BlockSpec((tc, LANES), lambda i: (i, 0)),
                  pl.BlockSpec((tc, d), lambda i: (i, 0)),
                  pl.BlockSpec((None, 6, d), lambda i: (cond_of_tile(i // sub), 0, 0)),
                  pl.BlockSpec((1, d), lambda i: (0, 0))],
        out_specs=[pl.BlockSpec((tc, d), lambda i: (jnp.minimum(i, ctx_steps - 1), 0)),
                   pl.BlockSpec((tc, d), lambda i: (jnp.maximum(i - ctx_steps, 0), 0))],
        out_shape=[jax.ShapeDtypeStruct((rows_ctx, d), F32), jax.ShapeDtypeStruct((rows - rows_ctx, d), F32)],
        scratch_shapes=[pltpu.VMEM((tc, d), F32), pltpu.VMEM((tc, d), F32), pltpu.SemaphoreType.DMA((2,))],
        compiler_params=_cparams(("arbitrary",)),
        name="moe_combine",
    )(pos0.reshape(rows // tc, 1, tc), pos1.reshape(rows // tc, 1, tc), ys, route, h, mod, final_norm)


def _dispatch_plan(route, tm):
    rows = route.shape[0]
    ids = route[:, ROUTE_IDX:ROUTE_IDX + TOP_K].astype(jnp.int32)
    flat = ids.reshape(-1)
    onehot = (flat[:, None] == jnp.arange(N_EXPERTS)[None, :]).astype(jnp.int32)
    rank = jnp.sum((jnp.cumsum(onehot, axis=0) - onehot) * onehot, axis=1)
    counts = jnp.sum(onehot, axis=0)
    tiles = (counts + tm - 1) // tm
    tile_end = jnp.cumsum(tiles)
    start = (tile_end - tiles) * tm
    slot = start[flat] + rank
    n_tiles = (rows * TOP_K) // tm + N_EXPERTS
    n_slots = n_tiles * tm
    slot_token = (jnp.arange(n_slots, dtype=jnp.int32) % rows).at[slot].set(
        jnp.arange(rows * TOP_K, dtype=jnp.int32) // TOP_K)
    t = jnp.arange(n_tiles)
    tile_expert = jnp.minimum(jnp.sum((t[:, None] >= tile_end[None, :]).astype(jnp.int32), axis=1), N_EXPERTS - 1)
    tile_in_expert = t - (tile_end - tiles)[tile_expert]
    tile_rows = jnp.where(t < tile_end[-1], jnp.clip(counts[tile_expert] - tile_in_expert * tm, 0, tm), 0)
    last_expert = tile_expert[jnp.maximum(tile_end[-1] - 1, 0)]
    tile_expert = jnp.where(tile_rows > 0, tile_expert, last_expert).astype(jnp.int32)
    pos = slot.reshape(rows, TOP_K)
    return slot_token, tile_expert, tile_rows.astype(jnp.int32), pos[:, 0], pos[:, 1]


def _rope_table(seq, n):
    quarter = n // 4
    t = np.arange(seq)
    inv = jnp.power(ROPE_BASE, -jnp.arange(quarter, dtype=F32) * (2.0 / (n // 2)))
    ang_r = jnp.asarray(t // GRID_W, F32)[:, None] * inv[None, :]
    ang_c = jnp.asarray(t % GRID_W, F32)[:, None] * inv[None, :]
    zero = jnp.zeros((seq, quarter), F32)
    cos = jnp.concatenate([jnp.cos(ang_r)] * 2 + [jnp.cos(ang_c)] * 2, axis=1)
    up = jnp.concatenate([-jnp.sin(ang_r), zero, -jnp.sin(ang_c), zero], axis=1)
    dn = jnp.concatenate([zero, jnp.sin(ang_r), zero, jnp.sin(ang_c)], axis=1)
    return cos, up, dn


def _tile_cols(parts, reps, pad_to=None):
    out = [jnp.tile(p, (1, reps)) for p in parts]
    if pad_to is not None:
        out = [jnp.pad(p, ((0, 0), (0, pad_to - p.shape[1]))) for p in out]
    return jnp.concatenate(out, axis=1)


def kernel(x_prompt, x_sample, cache_mla_ckv, cache_mla_krope, cache_win_k, cache_win_v, cache_nat_k, cache_nat_v, c, c_ctx, w_mod, b_mod, norm1, norm2, w_in, mla_q_norm, mla_kv_norm, w_mla_q_up, w_mla_kv_up, win_sink, nat_rpb, w_br_mla, w_br_win, w_br_nat, w_out, w_ff_gate, w_ff_up, w_ff_down, w_router, w_ex_gate, w_ex_up, w_ex_down, final_norm):
    nb_ctx, seq_ctx, d = x_prompt.shape
    nb_lat, seq_lat, _ = x_sample.shape
    past = cache_mla_ckv.shape[2]
    rows_ctx = nb_ctx * seq_ctx
    rows_lat = nb_lat * seq_lat
    assert d == D_MODEL and seq_lat == ROW_TILE and rows_ctx % ROW_TILE == 0
    assert seq_lat // GRID_W == 4 * NAT_QROWS
    assert w_mod.shape[0] == DEPTH == 2
    ctx_tiles = rows_ctx // ROW_TILE

    def cond_of_tile(i):
        return jnp.where(i < ctx_tiles, 0, i - ctx_tiles + 1)

    n_cond = 16
    cond = jnp.zeros((n_cond, d), F32).at[0].set(c_ctx).at[1:1 + nb_lat].set(c)
    mod_all = _modulation(cond, w_mod, b_mod).reshape(DEPTH, n_cond, 6, d)

    w_p, w_gate = _w_in_prep(w_in)
    wq_up = w_mla_q_up.reshape(DEPTH, MLA_Q_RANK, MLA_HEADS, MLA_NOPE + MLA_ROPE)
    wq_up = jnp.concatenate([wq_up[..., :MLA_NOPE].reshape(DEPTH, MLA_Q_RANK, -1),
                             wq_up[..., MLA_NOPE:].reshape(DEPTH, MLA_Q_RANK, -1)], axis=2).astype(BF16)
    wkv_up = w_mla_kv_up.astype(BF16)
    w_br = jnp.stack([w_br_mla, w_br_win, w_br_nat], axis=1).astype(BF16)
    w_out_b = w_out.astype(BF16)
    w_router_p = jnp.pad(w_router, ((0, 0), (0, 0), (0, LANES - N_EXPERTS))).astype(BF16)

    t128 = jnp.concatenate(_rope_table(seq_lat, 128), axis=1)
    t64 = _rope_table(seq_lat, MLA_ROPE)
    t_q = _tile_cols(t64, MLA_HEADS)
    t_k = _tile_cols(t64, 1, pad_to=LANES)

    ck_win = cache_win_k.reshape(nb_lat, DEPTH, past, 256)
    cv_win = cache_win_v.reshape(nb_lat, DEPTH, past, 256)
    ck_nat = cache_nat_k.reshape(nb_lat, DEPTH, past, 1024)
    cv_nat = cache_nat_v.reshape(nb_lat, DEPTH, past, 1024)

    h = (x_prompt.reshape(rows_ctx, d), x_sample.reshape(rows_lat, d))
    states = None
    for l in range(DEPTH):
        mod = mod_all[l]
        qn = mla_q_norm[l].reshape(1, -1)
        kvn = mla_kv_norm[l].reshape(1, -1)
        sink = win_sink[l].reshape(1, -1)
        p, hn = _mixer_in(h, mod, norm1[l].reshape(1, d), w_p, l, cond_of_tile)
        o_ctx, states = _ctx_attention(p, nb_ctx, seq_ctx, qn, kvn, wq_up[l], wkv_up[l], sink,
                                       prev_states=states)
        o_a = _lat_mla(p, rows_ctx // seq_lat, nb_lat, seq_lat, cache_mla_ckv, cache_mla_krope, l,
                       qn, kvn, wq_up[l], wkv_up[l], t_q, t_k)
        o_b = _lat_win(p, rows_ctx, nb_lat, seq_lat, ck_win, cv_win, l, t128, sink)
        o_c = _lat_nat(p, rows_ctx, nb_lat, seq_lat, ck_nat, cv_nat, l, _nat_bias_blocks(nat_rpb[l]))
        m = _merge(hn, o_ctx, o_a, o_b, o_c, w_gate, w_br, l)
        if l % 2 == 0:
            h1, hn2 = _out_proj(m, w_out_b, h, mod, norm2[l].reshape(1, d), l, cond_of_tile)
            h = _ffn(hn2, w_ff_gate[l // 2].astype(BF16), w_ff_up[l // 2].astype(BF16),
                     w_ff_down[l // 2].astype(BF16), h1, mod, cond_of_tile)
        else:
            h1, hn2, route = _out_proj(m, w_out_b, h, mod, norm2[l].reshape(1, d), l, cond_of_tile,
                                       w_router=w_router_p[l // 2])
            slot_token, tile_expert, tile_valid, pos0, pos1 = _dispatch_plan(route, MOE_TM)
            xs = _gather_rows(hn2, slot_token, tile_valid, BF16)
            ys = _moe_ffn(xs, w_ex_gate[l // 2].astype(BF16), w_ex_up[l // 2].astype(BF16),
                          w_ex_down[l // 2].astype(BF16), tile_expert, tile_valid)
            y_ctx, y_lat = _moe_combine(ys, pos0, pos1, route, h1, mod, final_norm.reshape(1, d),
                                        cond_of_tile, rows_ctx)

    y_prompt = y_ctx.reshape(nb_ctx, seq_ctx, d)
    y_sample = y_lat.reshape(nb_lat, seq_lat, d)
    return (y_prompt, y_sample) + tuple(states)
```

```python
import functools

import jax
import jax.numpy as jnp
import numpy as np
from jax import lax
from jax.experimental import pallas as pl
from jax.experimental.pallas import tpu as pltpu

F32 = jnp.float32
BF16 = jnp.bfloat16

D_MODEL = 2048
DEPTH = 2
GRID_W = 64
ROPE_BASE = 10000.0
EPS = 1e-6
NEG_INF = -1e30
MLA_HEADS = 8
MLA_Q_RANK = 512
MLA_KV_RANK = 512
MLA_NOPE = 128
MLA_ROPE = 64
MLA_V = 128
WIN_HEADS = 8
WIN_KV_HEADS = 2
WIN_GROUP = WIN_HEADS // WIN_KV_HEADS
WIN_HEAD_DIM = 128
WINDOW = 128
NAT_HEADS = 8
NAT_HEAD_DIM = 128
NAT_ROWS = 8
NAT_COLS = 16
BRANCH_W = 1024
D_FF = 5632
N_EXPERTS = 8
TOP_K = 2

P_WQ, P_NQ, P_NK, P_NV = 0, 1024, 2048, 3072
P_WK, P_WV = 4096, 4352
P_QD, P_KVD, P_KR = 4608, 5120, 5632
P_COLS = 5760
P_TN = 1152
LANES = 128

VMEM_LIMIT = 56 * 1024 * 1024

ROW_TILE = 1024
FFN_TM = 512
FFN_TF = 512
MOE_TM = 1024
MOE_SLAB = 512
GATHER_ROWS = 256
NAT_QROWS = 4
NAT_KROWS = 12


def _cparams(sem):
    return pltpu.CompilerParams(dimension_semantics=sem, vmem_limit_bytes=VMEM_LIMIT)


def _rms(x, g):
    ms = jnp.mean(x * x, axis=-1, keepdims=True)
    return x * lax.rsqrt(ms + EPS) * g


def _dot(a, b):
    return jnp.dot(a, b, preferred_element_type=F32)


def _dot_nt(a, b):
    return lax.dot_general(a, b, (((1,), (1,)), ((), ())), preferred_element_type=F32)


def _rope(x, tab_ref_or_val, shift):
    n = x.shape[-1]
    t = tab_ref_or_val
    c, s_up, s_dn = t[:, 0:n], t[:, n:2 * n], t[:, 2 * n:3 * n]
    up = pltpu.roll(x, n - shift, axis=1)
    dn = pltpu.roll(x, shift, axis=1)
    return x * c + up * s_up + dn * s_dn


def _softmax_pv(s, v, sink=None):
    m = jnp.max(s, axis=-1, keepdims=True)
    if sink is not None:
        m = jnp.maximum(m, sink)
    e = jnp.exp(s - m)
    den = jnp.sum(e, axis=-1, keepdims=True)
    if sink is not None:
        den = den + jnp.exp(sink - m)
    return _dot(e.astype(BF16), v) * (1.0 / den)


def _attend_heads(n_heads, score, value, o_ref, col0, sink=None, group=None, emit=None):
    group = group or n_heads
    for h0 in range(0, n_heads, group):
        heads = range(h0, h0 + group)
        s = [score(h) for h in heads]
        m = [jnp.max(x, axis=-1, keepdims=True) for x in s]
        if sink is not None:
            sk = [sink(h) for h in heads]
            m = [jnp.maximum(a, b) for a, b in zip(m, sk)]
        e = [jnp.exp(x - a) for x, a in zip(s, m)]
        den = [jnp.sum(x, axis=-1, keepdims=True) for x in e]
        if sink is not None:
            den = [d + jnp.exp(b - a) for d, a, b in zip(den, m, sk)]
        pv = [_dot(x.astype(BF16), value(h)) for x, h in zip(e, heads)]
        for i, h in enumerate(heads):
            out = pv[i] * (1.0 / den[i])
            if emit is not None:
                emit(h, out)
            else:
                o_ref[:, col0 + h * LANES:col0 + (h + 1) * LANES] = out.astype(o_ref.dtype)


def _mod_kernel(c_ref, w_ref, b_ref, o_ref):
    c = c_ref[...]
    s = (c * jax.nn.sigmoid(c)).astype(BF16)
    o_ref[...] = _dot(s, w_ref[...].astype(BF16)) + b_ref[...]


def _modulation(cond, w_mod, b_mod):
    depth, d, n = w_mod.shape
    nc = cond.shape[0]
    tn = 1024
    return pl.pallas_call(
        _mod_kernel,
        grid=(depth, n // tn),
        in_specs=[
            pl.BlockSpec((nc, d), lambda l, j: (0, 0)),
            pl.BlockSpec((None, d, tn), lambda l, j: (l, 0, j)),
            pl.BlockSpec((None, 1, tn), lambda l, j: (l, 0, j)),
        ],
        out_specs=pl.BlockSpec((None, nc, tn), lambda l, j: (l, 0, j)),
        out_shape=jax.ShapeDtypeStruct((depth, nc, n), F32),
        compiler_params=_cparams(("arbitrary", "arbitrary")),
        name="modulation",
    )(cond, w_mod, b_mod.reshape(depth, 1, n))


_W_IN_SEGMENTS = ((1088, 1024), (2624, 1024), (3648, 1024), (4672, 1024), (2112, 256), (2368, 256),
                  (0, 512), (512, 512), (1024, 64))
W_IN_GATES = 5696


def _w_in_prep_kernel(w_ref, p_ref, g_ref):
    col = 0
    for src, width in _W_IN_SEGMENTS:
        p_ref[:, col:col + width] = w_ref[:, src:src + width].astype(BF16)
        col += width
    p_ref[:, col:] = jnp.zeros((p_ref.shape[0], P_COLS - col), BF16)
    g_ref[...] = w_ref[:, W_IN_GATES:].astype(BF16)


def _w_in_prep(w_in):
    depth, d, n = w_in.shape
    tr = 128
    n_gate = n - W_IN_GATES
    return pl.pallas_call(
        _w_in_prep_kernel,
        grid=(depth, d // tr),
        in_specs=[pl.BlockSpec((None, tr, n), lambda l, r: (l, r, 0))],
        out_specs=[pl.BlockSpec((None, tr, P_COLS), lambda l, r: (l, r, 0)),
                   pl.BlockSpec((None, tr, n_gate), lambda l, r: (l, r, 0))],
        out_shape=[jax.ShapeDtypeStruct((depth, d, P_COLS), BF16),
                   jax.ShapeDtypeStruct((depth, d, n_gate), BF16)],
        compiler_params=_cparams(("arbitrary", "arbitrary")),
        name="w_in_prep",
    )(w_in)


def _split_rows_specs(h, tm):
    if not isinstance(h, tuple):
        return [h], [pl.BlockSpec((tm, h.shape[1]), lambda i, *_: (i, 0))], None
    ctx_tiles = h[0].shape[0] // tm
    d = h[0].shape[1]
    return (list(h),
            [pl.BlockSpec((tm, d), lambda i, *_: (jnp.minimum(i, ctx_tiles - 1), 0)),
             pl.BlockSpec((tm, d), lambda i, *_: (jnp.maximum(i - ctx_tiles, 0), 0))],
            ctx_tiles)


def _row_group_value(h_refs, ctx_tiles):
    if ctx_tiles is None:
        return h_refs[0][...]
    return jnp.where(pl.program_id(0) < ctx_tiles, h_refs[0][...], h_refs[1][...])


def _mixer_in_kernel(*refs, ctx_tiles):
    nh = 1 if ctx_tiles is None else 2
    h_refs = refs[:nh]
    mod_ref, g_ref, w_ref, p_ref, hn_ref = refs[nh:]

    @pl.when(pl.program_id(1) == 0)
    def _():
        h = _row_group_value(h_refs, ctx_tiles)
        hn = _rms(h, g_ref[...]) * (1.0 + mod_ref[1:2, :]) + mod_ref[0:1, :]
        hn_ref[...] = hn.astype(BF16)

    p_ref[...] = _dot(hn_ref[...], w_ref[...])


def _mixer_in(h, mod, g, w, layer, cond_of_tile):
    tm, tn = (ROW_TILE // 2 if isinstance(h, tuple) else ROW_TILE), P_TN
    sub = ROW_TILE // tm
    h_args, h_specs, ctx_tiles = _split_rows_specs(h, tm)
    rows = sum(a.shape[0] for a in h_args)
    d = h_args[0].shape[1]
    n = w.shape[2]
    return pl.pallas_call(
        functools.partial(_mixer_in_kernel, ctx_tiles=ctx_tiles),
        grid=(rows // tm, n // tn),
        in_specs=h_specs + [
            pl.BlockSpec((None, 6, d), lambda i, j: (cond_of_tile(i // sub), 0, 0)),
            pl.BlockSpec((1, d), lambda i, j: (0, 0)),
            pl.BlockSpec((None, d, tn), lambda i, j: (layer, 0, j)),
        ],
        out_specs=[
            pl.BlockSpec((tm, tn), lambda i, j: (i, j)),
            pl.BlockSpec((tm, d), lambda i, j: (i, 0)),
        ],
        out_shape=[
            jax.ShapeDtypeStruct((rows, n), F32),
            jax.ShapeDtypeStruct((rows, d), BF16),
        ],
        compiler_params=_cparams(("arbitrary", "arbitrary")),
        name="mixer_in",
    )(*h_args, mod, g, w)


_STATE_WIDTHS = (MLA_KV_RANK, MLA_ROPE, 256, 256, 1024, 1024)
_STATE_HEADS = (None, None, WIN_KV_HEADS, WIN_KV_HEADS, NAT_HEADS, NAT_HEADS)


def _ctx_attn_kernel(p_ref, qn_ref, kvn_ref, wq_ref, wkv_ref, sink_ref, *refs, stacked):
    n_state = len(_STATE_WIDTHS)
    if stacked:
        prev_refs, o_ref, state_refs = refs[:n_state], refs[n_state], refs[n_state + 1:]
    else:
        prev_refs, o_ref, state_refs = None, refs[0], refs[1:]

    def put_state(k, val, slot=DEPTH - 1):
        ref, heads = state_refs[k], _STATE_HEADS[k]
        if not stacked:
            ref[...] = val
        elif heads is None:
            ref[slot] = val
        else:
            for j in range(heads):
                ref[slot, :, j, :] = val[:, j * LANES:(j + 1) * LANES]

    if stacked:
        for k in range(n_state):
            put_state(k, prev_refs[k][...], slot=0)

    q = _dot(_rms(p_ref[:, P_QD:P_QD + MLA_Q_RANK], qn_ref[...]).astype(BF16), wq_ref[...])
    ckv = _rms(p_ref[:, P_KVD:P_KVD + MLA_KV_RANK], kvn_ref[...])
    put_state(0, ckv)
    kv = _dot(ckv.astype(BF16), wkv_ref[...]).astype(BF16)
    kr = p_ref[:, P_KR:P_KR + MLA_ROPE]
    put_state(1, kr)
    krb = kr.astype(BF16)
    scale_a = (MLA_NOPE + MLA_ROPE) ** -0.5
    nope_w = MLA_HEADS * MLA_NOPE

    def score_a(h):
        qn = q[:, h * MLA_NOPE:(h + 1) * MLA_NOPE].astype(BF16)
        qr = q[:, nope_w + h * MLA_ROPE:nope_w + (h + 1) * MLA_ROPE].astype(BF16)
        return (_dot_nt(qn, kv[:, h * 256:h * 256 + MLA_NOPE]) + _dot_nt(qr, krb)) * scale_a

    _attend_heads(MLA_HEADS, score_a, lambda h: kv[:, h * 256 + MLA_NOPE:(h + 1) * 256], o_ref, 0)
    put_state(2, p_ref[:, P_WK:P_WK + 256])
    put_state(3, p_ref[:, P_WV:P_WV + 256])
    scale_b = WIN_HEAD_DIM ** -0.5

    def score_b(h):
        kvh = h // WIN_GROUP
        qh = p_ref[:, P_WQ + h * 128:P_WQ + (h + 1) * 128].astype(BF16)
        return _dot_nt(qh, p_ref[:, P_WK + kvh * 128:P_WK + (kvh + 1) * 128].astype(BF16)) * scale_b

    def value_b(h):
        kvh = h // WIN_GROUP
        return p_ref[:, P_WV + kvh * 128:P_WV + (kvh + 1) * 128].astype(BF16)

    _attend_heads(WIN_HEADS, score_b, value_b, o_ref, BRANCH_W, lambda h: sink_ref[0:1, h:h + 1])
    put_state(4, p_ref[:, P_NK:P_NK + 1024])
    put_state(5, p_ref[:, P_NV:P_NV + 1024])
    scale_c = NAT_HEAD_DIM ** -0.5

    def score_c(h):
        qh = p_ref[:, P_NQ + h * 128:P_NQ + (h + 1) * 128].astype(BF16)
        return _dot_nt(qh, p_ref[:, P_NK + h * 128:P_NK + (h + 1) * 128].astype(BF16)) * scale_c

    _attend_heads(NAT_HEADS, score_c, lambda h: p_ref[:, P_NV + h * 128:P_NV + (h + 1) * 128].astype(BF16),
                  o_ref, 2 * BRANCH_W)


def _ctx_attention(p, nb, seq, qn, kvn, wq, wkv, sink, prev_states=None):
    full = lambda a: pl.BlockSpec(a.shape, lambda b: (0,) * a.ndim)
    row = lambda w: pl.BlockSpec((seq, w), lambda b: (b, 0))
    stacked = prev_states is not None
    in_specs = [pl.BlockSpec((seq, P_COLS), lambda b: (b, 0)),
                full(qn), full(kvn), full(wq), full(wkv), full(sink)]
    args = [p, qn, kvn, wq, wkv, sink]
    out_specs = [row(3 * BRANCH_W)]
    out_shape = [jax.ShapeDtypeStruct((nb * seq, 3 * BRANCH_W), BF16)]
    for w, heads in zip(_STATE_WIDTHS, _STATE_HEADS):
        if not stacked:
            out_specs.append(row(w))
            out_shape.append(jax.ShapeDtypeStruct((nb * seq, w), F32))
        else:
            tail = (w,) if heads is None else (heads, w // heads)
            out_specs.append(pl.BlockSpec((None, DEPTH, seq) + tail, lambda b, n=len(tail): (b,) + (0,) * (n + 2)))
            out_shape.append(jax.ShapeDtypeStruct((nb, DEPTH, seq) + tail, F32))
    if stacked:
        in_specs += [row(w) for w in _STATE_WIDTHS]
        args += list(prev_states)
    outs = pl.pallas_call(
        functools.partial(_ctx_attn_kernel, stacked=stacked),
        grid=(nb,),
        in_specs=in_specs, out_specs=out_specs, out_shape=out_shape,
        compiler_params=_cparams(("arbitrary",)),
        name="ctx_attention",
    )(*args)
    return outs[0], outs[1:]


MLA_QT = 512
MLA_KC = 256
MLA_DK = MLA_NOPE + MLA_ROPE


def _lat_mla_kernel(p_ref, cckv_ref, ckr_ref, qn_ref, kvn_ref, wq_ref, wkv_ref, tq_ref, tk_ref,
                    o_ref, k_scr, v_scr, *, past, seq):
    qt = pl.program_id(1)

    def put_keys(r0, n, kv, kr):
        krb = kr.astype(BF16)
        for h in range(MLA_HEADS):
            k_scr[h, r0:r0 + n, 0:MLA_NOPE] = kv[:, h * 256:h * 256 + MLA_NOPE].astype(BF16)
            k_scr[h, r0:r0 + n, MLA_NOPE:MLA_DK] = krb
            v_scr[r0:r0 + n, h * MLA_V:(h + 1) * MLA_V] = kv[:, h * 256 + MLA_NOPE:(h + 1) * 256].astype(BF16)

    @pl.when(qt == 0)
    def _():
        for c in range(past // MLA_KC):
            r0 = c * MLA_KC
            kv = _dot(cckv_ref[r0:r0 + MLA_KC, :].astype(BF16), wkv_ref[...])
            put_keys(r0, MLA_KC, kv, ckr_ref[r0:r0 + MLA_KC, :])
        for c in range(seq // MLA_KC):
            r0 = c * MLA_KC
            ckv = _rms(p_ref[r0:r0 + MLA_KC, MLA_Q_RANK:MLA_Q_RANK + MLA_KV_RANK], kvn_ref[...])
            kv = _dot(ckv.astype(BF16), wkv_ref[...])
            krp = p_ref[r0:r0 + MLA_KC, 2 * MLA_Q_RANK:2 * MLA_Q_RANK + LANES]
            kr = _rope(krp, tk_ref[r0:r0 + MLA_KC, :], MLA_ROPE // 4)
            put_keys(past + r0, MLA_KC, kv, kr[:, 0:MLA_ROPE])

    r0 = pl.multiple_of(qt * MLA_QT, MLA_QT)
    qd = p_ref[pl.ds(r0, MLA_QT), 0:MLA_Q_RANK]
    q = _dot(_rms(qd, qn_ref[...]).astype(BF16), wq_ref[...])
    nope_w = MLA_HEADS * MLA_NOPE
    q_rope = _rope(q[:, nope_w:], tq_ref[pl.ds(r0, MLA_QT), :], MLA_ROPE // 4)
    scale = MLA_DK ** -0.5

    def score(h):
        qh = jnp.concatenate([q[:, h * MLA_NOPE:(h + 1) * MLA_NOPE],
                              q_rope[:, h * MLA_ROPE:(h + 1) * MLA_ROPE]], axis=1).astype(BF16)
        return _dot_nt(qh, k_scr[h]) * scale

    _attend_heads(MLA_HEADS, score, lambda h: v_scr[:, h * MLA_V:(h + 1) * MLA_V], o_ref, 0, group=2)


def _lat_mla(p, row_blk0, nb, seq, cache_ckv, cache_kr, layer, qn, kvn, wq, wkv, tq, tk):
    past = cache_ckv.shape[2]
    full = lambda a: pl.BlockSpec(a.shape, lambda b, t: (0,) * a.ndim)
    nqt = seq // MLA_QT
    return pl.pallas_call(
        functools.partial(_lat_mla_kernel, past=past, seq=seq),
        grid=(nb, nqt),
        in_specs=[
            pl.BlockSpec((seq, P_TN), lambda b, t: (row_blk0 + b, P_QD // P_TN)),
            pl.BlockSpec((None, None, past, MLA_KV_RANK), lambda b, t: (b, layer, 0, 0)),
            pl.BlockSpec((None, None, past, MLA_ROPE), lambda b, t: (b, layer, 0, 0)),
            full(qn), full(kvn), full(wq), full(wkv), full(tq), full(tk),
        ],
        out_specs=pl.BlockSpec((MLA_QT, BRANCH_W), lambda b, t: (b * nqt + t, 0)),
        out_shape=jax.ShapeDtypeStruct((nb * seq, BRANCH_W), BF16),
        scratch_shapes=[pltpu.VMEM((MLA_HEADS, past + seq, MLA_DK), BF16),
                        pltpu.VMEM((past + seq, MLA_HEADS * MLA_V), BF16)],
        compiler_params=_cparams(("arbitrary", "arbitrary")),
        name="lat_mla",
    )(p, cache_ckv, cache_kr, qn, kvn, wq, wkv, tq, tk)


WIN_QB = 128


def _lat_win_kernel(q_ref, k0_ref, k1_ref, k2_ref, v0_ref, v1_ref, v2_ref, ck_ref, cv_ref,
                    t0_ref, t1_ref, t2_ref, sink_ref, o_ref, *, past, seq):
    qb = pl.program_id(1)
    scale = WIN_HEAD_DIM ** -0.5
    n_loc = 3 * WIN_QB
    rows = WIN_GROUP * WIN_QB
    qpos = qb * WIN_QB + lax.broadcasted_iota(jnp.int32, (rows, n_loc), 0) % WIN_QB
    kpos = (qb - 1) * WIN_QB + lax.broadcasted_iota(jnp.int32, (rows, n_loc), 1)
    valid = (kpos >= 0) & (kpos < seq) & (jnp.abs(qpos - kpos) <= WINDOW)
    t1 = t1_ref[...]
    q_heads = lambda kvh: range(kvh * WIN_GROUP, (kvh + 1) * WIN_GROUP)

    def score(kvh):
        cs = slice(kvh * 128, (kvh + 1) * 128)
        keys = jnp.concatenate([
            ck_ref[:, cs],
            _rope(k0_ref[:, cs], t0_ref[...], 32),
            _rope(k1_ref[:, cs], t1, 32),
            _rope(k2_ref[:, cs], t2_ref[...], 32)], axis=0).astype(BF16)
        q = jnp.concatenate([_rope(q_ref[:, h * 128:(h + 1) * 128], t1, 32) for h in q_heads(kvh)],
                            axis=0).astype(BF16)
        s = _dot_nt(q, keys) * scale
        return jnp.concatenate([s[:, :past], jnp.where(valid, s[:, past:], NEG_INF)], axis=1)

    def value(kvh):
        cs = slice(kvh * 128, (kvh + 1) * 128)
        return jnp.concatenate([cv_ref[:, cs], v0_ref[:, cs], v1_ref[:, cs], v2_ref[:, cs]], axis=0).astype(BF16)

    def sink(kvh):
        return jnp.concatenate([jnp.broadcast_to(sink_ref[0:1, h:h + 1], (WIN_QB, 1)) for h in q_heads(kvh)],
                               axis=0)

    def emit(kvh, out):
        for g, h in enumerate(q_heads(kvh)):
            o_ref[:, h * 128:(h + 1) * 128] = out[g * WIN_QB:(g + 1) * WIN_QB, :].astype(BF16)

    _attend_heads(WIN_KV_HEADS, score, value, o_ref, 0, sink=sink, emit=emit)


def _lat_win(p, row0, nb, seq, cache_k, cache_v, layer, tab, sink):
    past = cache_k.shape[2]
    nqb = seq // WIN_QB
    rb0 = row0 // WIN_QB

    def kblk(off, col_blk):
        return pl.BlockSpec(
            (WIN_QB, 256),
            lambda b, t: (rb0 + b * nqb + jnp.clip(t + off, 0, nqb - 1), col_blk))

    def tblk(off):
        return pl.BlockSpec((WIN_QB, 3 * 128), lambda b, t: (jnp.clip(t + off, 0, nqb - 1), 0))

    cache = pl.BlockSpec((None, None, past, 256), lambda b, t: (b, layer, 0, 0))
    return pl.pallas_call(
        functools.partial(_lat_win_kernel, past=past, seq=seq),
        grid=(nb, nqb),
        in_specs=[
            pl.BlockSpec((WIN_QB, 1024), lambda b, t: (rb0 + b * nqb + t, P_WQ // 1024)),
            kblk(-1, P_WK // 256), kblk(0, P_WK // 256), kblk(1, P_WK // 256),
            kblk(-1, P_WV // 256), kblk(0, P_WV // 256), kblk(1, P_WV // 256),
            cache, cache,
            tblk(-1), tblk(0), tblk(1),
            pl.BlockSpec(sink.shape, lambda b, t: (0, 0)),
        ],
        out_specs=pl.BlockSpec((WIN_QB, BRANCH_W), lambda b, t: (b * nqb + t, 0)),
        out_shape=jax.ShapeDtypeStruct((nb * seq, BRANCH_W), BF16),
        compiler_params=_cparams(("arbitrary", "arbitrary")),
        name="lat_window",
    )(p, p, p, p, p, p, p, cache_k, cache_v, tab, tab, tab, sink)


def _lat_nat_kernel(q_ref, k0_ref, k1_ref, k2_ref, v0_ref, v1_ref, v2_ref, ck_ref, cv_ref, bias_ref, o_ref,
                    *, grid_rows):
    g = pl.program_id(0)
    scale = NAT_HEAD_DIM ** -0.5
    past = ck_ref.shape[0]
    win_rows = min(NAT_ROWS, grid_rows)
    lane = lax.broadcasted_iota(jnp.int32, (GRID_W, 2 * GRID_W), 1)

    def rows_of(h, refs):
        cs = slice(h * 128, (h + 1) * 128)
        return jnp.concatenate([r[:, cs] for r in refs], axis=0).astype(BF16)

    def pair_plan(rq, rkp):
        r = NAT_QROWS * g + rq
        kr0 = NAT_QROWS * _nat_key_start(g) + 2 * rkp
        rs = jnp.clip(r - win_rows // 2, 0, grid_rows - win_rows)
        in_win = lambda kr: ((kr >= rs) & (kr < rs + win_rows)).astype(jnp.int32)
        ok = jnp.where(lane < GRID_W, in_win(kr0), in_win(kr0 + 1)) > 0
        return jnp.clip(kr0 - r + NAT_ROWS - 1, 0, 2 * NAT_ROWS - 1), ok

    plans = [[pair_plan(rq, rkp) for rkp in range(NAT_KROWS // 2)] for rq in range(NAT_QROWS)]

    def bias(h):
        return jnp.concatenate(
            [jnp.concatenate([jnp.where(ok, bias_ref[h, a], NEG_INF) for a, ok in row], axis=1) for row in plans],
            axis=0)

    def score(h):
        q = q_ref[:, h * 128:(h + 1) * 128].astype(BF16)
        s = _dot_nt(q, rows_of(h, (ck_ref, k0_ref, k1_ref, k2_ref))) * scale
        return jnp.concatenate([s[:, :past], s[:, past:] + bias(h)], axis=1)

    _attend_heads(NAT_HEADS, score, lambda h: rows_of(h, (cv_ref, v0_ref, v1_ref, v2_ref)), o_ref, 0, group=2)


def _nat_key_start(g):
    return g // 2


def _lat_nat(p, row0, nb, seq, cache_k, cache_v, layer, bias):
    past = cache_k.shape[2]
    qrows = NAT_QROWS * GRID_W
    ng = seq // qrows
    rb0 = row0 // qrows

    def kblk(off, col_blk):
        return pl.BlockSpec((qrows, 1024), lambda g, b: (rb0 + b * ng + _nat_key_start(g) + off, col_blk))

    cache = pl.BlockSpec((None, None, past, 1024), lambda g, b: (b, layer, 0, 0))
    return pl.pallas_call(
        functools.partial(_lat_nat_kernel, grid_rows=seq // GRID_W),
        grid=(ng, nb),
        in_specs=[
            pl.BlockSpec((qrows, 1024), lambda g, b: (rb0 + b * ng + g, P_NQ // 1024)),
            kblk(0, P_NK // 1024), kblk(1, P_NK // 1024), kblk(2, P_NK // 1024),
            kblk(0, P_NV // 1024), kblk(1, P_NV // 1024), kblk(2, P_NV // 1024),
            cache, cache,
            pl.BlockSpec(bias.shape, lambda g, b: (0, 0, 0, 0)),
        ],
        out_specs=pl.BlockSpec((qrows, BRANCH_W), lambda g, b: (b * ng + g, 0)),
        out_shape=jax.ShapeDtypeStruct((nb * seq, BRANCH_W), BF16),
        compiler_params=_cparams(("arbitrary", "arbitrary")),
        name="lat_neighbourhood",
    )(p, p, p, p, p, p, p, cache_k, cache_v, bias)


def _nat_bias_blocks(rpb):
    n_dr, n_dc = 2 * NAT_ROWS - 1, 2 * NAT_COLS - 1
    cc = np.arange(GRID_W)
    dc = cc[None, :] - cc[:, None] + NAT_COLS - 1
    cs = np.clip(cc - NAT_COLS // 2, 0, GRID_W - NAT_COLS)[:, None]
    col_ok = (cc[None, :] >= cs) & (cc[None, :] < cs + NAT_COLS)
    onehot = ((dc[None] == np.arange(n_dc)[:, None, None]) & col_ok[None]).astype(np.float32)
    toep = jnp.einsum('had,dck->hack', rpb.astype(F32), jnp.asarray(onehot), precision=lax.Precision.HIGHEST)
    toep = jnp.where(jnp.asarray(col_ok), toep, NEG_INF)
    fill = jnp.full((NAT_HEADS, 2 * NAT_ROWS + 1 - n_dr, GRID_W, GRID_W), NEG_INF, F32)
    toep = jnp.concatenate([toep, fill], axis=1)
    return jnp.concatenate([toep[:, :-1], toep[:, 1:]], axis=-1)


def _merge_kernel(hn_ref, octx_ref, oa_ref, ob_ref, oc_ref, wga_ref, wgb_ref, wgc_ref,
                  wba_ref, wbb_ref, wbc_ref, m_ref, *, ctx_tiles):
    i = pl.program_id(0)
    weights = ((wga_ref, wba_ref), (wgb_ref, wbb_ref), (wgc_ref, wbc_ref))

    def merged(branch):
        hn = hn_ref[...]
        acc = None
        for k, (wg, wb) in enumerate(weights):
            term = jax.nn.sigmoid(_dot(hn, wg[...])) * _dot(branch(k), wb[...])
            acc = term if acc is None else acc + term
        m_ref[...] = acc.astype(BF16)

    @pl.when(i < ctx_tiles)
    def _():
        merged(lambda k: octx_ref[:, k * BRANCH_W:(k + 1) * BRANCH_W])

    @pl.when(i >= ctx_tiles)
    def _():
        lat = (oa_ref, ob_ref, oc_ref)
        merged(lambda k: lat[k][...])


def _merge(hn, o_ctx, o_a, o_b, o_c, w_gate, w_br, layer):
    rows, d = hn.shape
    tm, tn = ROW_TILE, 256
    nj = d // tn
    ctx_tiles = o_ctx.shape[0] // tm
    gate = lambda k: pl.BlockSpec((None, d, tn), lambda i, j: (layer, 0, k * nj + j))
    br = lambda k: pl.BlockSpec((None, None, BRANCH_W, tn), lambda i, j: (layer, k, 0, j))
    lat = pl.BlockSpec((tm, BRANCH_W), lambda i, j: (jnp.maximum(i - ctx_tiles, 0), 0))
    return pl.pallas_call(
        functools.partial(_merge_kernel, ctx_tiles=ctx_tiles),
        grid=(rows // tm, nj),
        in_specs=[pl.BlockSpec((tm, d), lambda i, j: (i, 0)),
                  pl.BlockSpec((tm, 3 * BRANCH_W), lambda i, j: (jnp.minimum(i, ctx_tiles - 1), 0)),
                  lat, lat, lat,
                  gate(0), gate(1), gate(2), br(0), br(1), br(2)],
        out_specs=pl.BlockSpec((tm, tn), lambda i, j: (i, j)),
        out_shape=jax.ShapeDtypeStruct((rows, d), BF16),
        compiler_params=_cparams(("arbitrary", "arbitrary")),
        name="merge",
    )(hn, o_ctx, o_a, o_b, o_c, w_gate, w_gate, w_gate, w_br, w_br, w_br)


ROUTE_IDX = 8
ROUTE_W = 10


def _out_proj_kernel(m_ref, w_ref, mod_ref, g_ref, *rest, routed, ctx_tiles):
    nh = 1 if ctx_tiles is None else 2
    h_refs, rest = rest[:nh], rest[nh:]
    if routed:
        wr_ref, h1_ref, hn2_ref, route_ref = rest
    else:
        h1_ref, hn2_ref = rest
    h1 = _row_group_value(h_refs, ctx_tiles) + mod_ref[2:3, :] * _dot(m_ref[...], w_ref[...])
    h1_ref[...] = h1
    hn2 = _rms(h1, g_ref[...]) * (1.0 + mod_ref[4:5, :]) + mod_ref[3:4, :]
    hn2_ref[...] = hn2.astype(hn2_ref.dtype)
    if routed:
        logits = _dot(hn2.astype(BF16), wr_ref[...])
        lane = lax.broadcasted_iota(jnp.int32, logits.shape, 1).astype(F32)
        lg = jnp.where(lane < N_EXPERTS, logits, -jnp.inf)
        m1 = jnp.max(lg, axis=-1, keepdims=True)
        i1 = jnp.min(jnp.where(lg == m1, lane, float(LANES)), axis=-1, keepdims=True)
        lg2 = jnp.where(lane == i1, -jnp.inf, lg)
        m2 = jnp.max(lg2, axis=-1, keepdims=True)
        i2 = jnp.min(jnp.where(lg2 == m2, lane, float(LANES)), axis=-1, keepdims=True)
        e2 = jnp.exp(m2 - m1)
        w1 = 1.0 / (1.0 + e2)
        w2 = e2 / (1.0 + e2)
        route = (jnp.where(lane == ROUTE_IDX, i1, 0.0)
                 + jnp.where(lane == ROUTE_IDX + 1, i2, 0.0)
                 + jnp.where(lane == ROUTE_W, w1, 0.0)
                 + jnp.where(lane == ROUTE_W + 1, w2, 0.0))
        route_ref[...] = route


def _out_proj(m, w_out, h, mod, g, layer, cond_of_tile, w_router=None):
    rows, d = m.shape
    tm = 512
    sub = ROW_TILE // tm
    routed = w_router is not None
    h_args, h_specs, ctx_tiles = _split_rows_specs(h, tm)
    in_specs = [
        pl.BlockSpec((tm, d), lambda i: (i, 0)),
        pl.BlockSpec((None, d, d), lambda i: (layer, 0, 0)),
        pl.BlockSpec((None, 6, d), lambda i: (cond_of_tile(i // sub), 0, 0)),
        pl.BlockSpec((1, d), lambda i: (0, 0)),
    ] + h_specs
    out_specs = [pl.BlockSpec((tm, d), lambda i: (i, 0)), pl.BlockSpec((tm, d), lambda i: (i, 0))]
    out_shape = [jax.ShapeDtypeStruct((rows, d), F32), jax.ShapeDtypeStruct((rows, d), BF16)]
    args = [m, w_out, mod, g] + h_args
    if routed:
        out_shape[1] = jax.ShapeDtypeStruct((rows, d), F32)
        in_specs.append(pl.BlockSpec((d, LANES), lambda i: (0, 0)))
        out_specs.append(pl.BlockSpec((tm, LANES), lambda i: (i, 0)))
        out_shape.append(jax.ShapeDtypeStruct((rows, LANES), F32))
        args.append(w_router)
    return pl.pallas_call(
        functools.partial(_out_proj_kernel, routed=routed, ctx_tiles=ctx_tiles),
        grid=(rows // tm,),
        in_specs=in_specs, out_specs=out_specs, out_shape=out_shape,
        compiler_params=_cparams(("arbitrary",)),
        name="out_proj",
    )(*args)


def _swiglu_chunk(x, wg_ref, wu_ref, wd_ref):
    g = _dot(x, wg_ref[...])
    u = _dot(x, wu_ref[...])
    a = (g * jax.nn.sigmoid(g) * u).astype(BF16)
    return _dot(a, wd_ref[...])


def _ffn_kernel(x_ref, wg_ref, wu_ref, wd_ref, h_ref, mod_ref, o_ref):
    f = pl.program_id(1)

    @pl.when(f == 0)
    def _():
        o_ref[...] = jnp.zeros_like(o_ref)

    o_ref[...] += _swiglu_chunk(x_ref[...], wg_ref, wu_ref, wd_ref)

    @pl.when(f == pl.num_programs(1) - 1)
    def _():
        o_ref[...] = h_ref[...] + mod_ref[5:6, :] * o_ref[...]


def _ffn(x, wg, wu, wd, h, mod, cond_of_tile):
    rows, d = h.shape
    ff = wg.shape[1]
    tm, tf = FFN_TM, FFN_TF
    sub = ROW_TILE // tm
    return pl.pallas_call(
        _ffn_kernel,
        grid=(rows // tm, ff // tf),
        in_specs=[
            pl.BlockSpec((tm, d), lambda i, f: (i, 0)),
            pl.BlockSpec((d, tf), lambda i, f: (0, f)),
            pl.BlockSpec((d, tf), lambda i, f: (0, f)),
            pl.BlockSpec((tf, d), lambda i, f: (f, 0)),
            pl.BlockSpec((tm, d), lambda i, f: (i, 0)),
            pl.BlockSpec((None, 6, d), lambda i, f: (cond_of_tile(i // sub), 0, 0)),
        ],
        out_specs=pl.BlockSpec((tm, d), lambda i, f: (i, 0)),
        out_shape=jax.ShapeDtypeStruct((rows, d), F32),
        compiler_params=_cparams(("arbitrary", "arbitrary")),
        name="ffn_dense",
    )(x, wg, wu, wd, h, mod)


def _start_row_gather(src_ref, idx_ref, dst_ref, sem, n):
    def issue(r, carry):
        pltpu.make_async_copy(src_ref.at[pl.ds(idx_ref[0, r], 1), :], dst_ref.at[pl.ds(r, 1), :], sem).start()
        return carry

    lax.fori_loop(0, n, issue, 0, unroll=8)


def _wait_row_gather(src_ref, dst_ref, sem, n):
    pltpu.make_async_copy(src_ref.at[pl.ds(0, n), :], dst_ref, sem).wait()


def _gather_kernel(tr_ref, idx_ref, idx_next_ref, src_ref, o_ref, buf, sem):
    i = pl.program_id(0)
    n = o_ref.shape[0]
    per_tile = MOE_TM // n
    slot = i % 2

    def used(step):
        return tr_ref[step // per_tile] > (step % per_tile) * n

    @pl.when((i == 0) & used(0))
    def _():
        _start_row_gather(src_ref, idx_ref, buf.at[0], sem.at[0], n)

    nxt = jnp.minimum(i + 1, pl.num_programs(0) - 1)

    @pl.when((i + 1 < pl.num_programs(0)) & used(nxt))
    def _():
        _start_row_gather(src_ref, idx_next_ref, buf.at[1 - slot], sem.at[1 - slot], n)

    @pl.when(used(i))
    def _():
        _wait_row_gather(src_ref, buf.at[slot], sem.at[slot], n)
        o_ref[...] = buf[slot].astype(o_ref.dtype)

    @pl.when(jnp.logical_not(used(i)))
    def _():
        o_ref[...] = jnp.zeros_like(o_ref)


def _gather_rows(src, idx, tile_rows, out_dtype):
    n = idx.shape[0]
    d = src.shape[1]
    tg = GATHER_ROWS
    steps = n // tg
    idx3 = idx.reshape(steps, 1, tg)
    grid_spec = pltpu.PrefetchScalarGridSpec(
        num_scalar_prefetch=1,
        grid=(steps,),
        in_specs=[pl.BlockSpec((None, 1, tg), lambda i, tr: (i, 0, 0), memory_space=pltpu.SMEM),
                  pl.BlockSpec((None, 1, tg), lambda i, tr: (jnp.minimum(i + 1, steps - 1), 0, 0),
                               memory_space=pltpu.SMEM),
                  pl.BlockSpec(memory_space=pl.ANY)],
        out_specs=pl.BlockSpec((tg, d), lambda i, tr: (i, 0)),
        scratch_shapes=[pltpu.VMEM((2, tg, d), src.dtype), pltpu.SemaphoreType.DMA((2,))],
    )
    return pl.pallas_call(
        _gather_kernel,
        grid_spec=grid_spec,
        out_shape=jax.ShapeDtypeStruct((n, d), out_dtype),
        compiler_params=_cparams(("arbitrary",)),
        name="moe_gather",
    )(tile_rows, idx3, idx3, src)


def _moe_ffn_kernel(te_ref, tv_ref, x_ref, wg_ref, wu_ref, wd_ref, o_ref):
    i = pl.program_id(0)
    f = pl.program_id(1)
    rows_used = tv_ref[i]
    for s in range(MOE_TM // MOE_SLAB):
        rs = slice(s * MOE_SLAB, (s + 1) * MOE_SLAB)
        used = rows_used > s * MOE_SLAB

        @pl.when(f == 0)
        def _():
            o_ref[rs, :] = jnp.zeros((MOE_SLAB, o_ref.shape[1]), o_ref.dtype)

        @pl.when(used)
        def _():
            o_ref[rs, :] += _swiglu_chunk(x_ref[rs, :], wg_ref, wu_ref, wd_ref)


def _moe_ffn(xs, wg, wu, wd, tile_expert, tile_valid):
    n, d = xs.shape
    ff = wg.shape[2]
    tm, tf = MOE_TM, FFN_TF
    nf = ff // tf

    def fidx(i, f, tv):
        return jnp.where(tv[i] > 0, f, nf - 1)

    grid_spec = pltpu.PrefetchScalarGridSpec(
        num_scalar_prefetch=2,
        grid=(n // tm, nf),
        in_specs=[
            pl.BlockSpec((tm, d), lambda i, f, te, tv: (i, 0)),
            pl.BlockSpec((None, d, tf), lambda i, f, te, tv: (te[i], 0, fidx(i, f, tv))),
            pl.BlockSpec((None, d, tf), lambda i, f, te, tv: (te[i], 0, fidx(i, f, tv))),
            pl.BlockSpec((None, tf, d), lambda i, f, te, tv: (te[i], fidx(i, f, tv), 0)),
        ],
        out_specs=pl.BlockSpec((tm, d), lambda i, f, te, tv: (i, 0)),
    )
    return pl.pallas_call(
        _moe_ffn_kernel,
        grid_spec=grid_spec,
        out_shape=jax.ShapeDtypeStruct((n, d), F32),
        compiler_params=_cparams(("arbitrary", "arbitrary")),
        name="moe_ffn",
    )(tile_expert, tile_valid, xs, wg, wu, wd)


def _combine_kernel(p0_ref, p1_ref, p0_next_ref, p1_next_ref, ys_ref, route_ref, h_ref, mod_ref, fn_ref,
                    octx_ref, olat_ref, buf0, buf1, sem, *, ctx_steps):
    i = pl.program_id(0)
    n = octx_ref.shape[0]
    slot = i % 2

    def start(q0_ref, q1_ref, s):
        _start_row_gather(ys_ref, q0_ref, buf0.at[s], sem.at[s, 0], n)
        _start_row_gather(ys_ref, q1_ref, buf1.at[s], sem.at[s, 1], n)

    @pl.when(i == 0)
    def _():
        start(p0_ref, p1_ref, 0)

    @pl.when(i + 1 < pl.num_programs(0))
    def _():
        start(p0_next_ref, p1_next_ref, 1 - slot)

    _wait_row_gather(ys_ref, buf0.at[slot], sem.at[slot, 0], n)
    _wait_row_gather(ys_ref, buf1.at[slot], sem.at[slot, 1], n)
    w0 = route_ref[:, ROUTE_W:ROUTE_W + 1]
    w1 = route_ref[:, ROUTE_W + 1:ROUTE_W + 2]
    y = w0 * buf0[slot] + w1 * buf1[slot]
    h2 = h_ref[...] + mod_ref[5:6, :] * y
    out = _rms(h2, fn_ref[...])

    @pl.when(i < ctx_steps)
    def _():
        octx_ref[...] = out

    @pl.when(i >= ctx_steps)
    def _():
        olat_ref[...] = out


def _moe_combine(ys, pos0, pos1, route, h, mod, final_norm, cond_of_tile, rows_ctx):
    rows, d = h.shape
    tc = GATHER_ROWS
    sub = ROW_TILE // tc
    ctx_steps = rows_ctx // tc
    steps = rows // tc
    smem = lambda: pl.BlockSpec((None, 1, tc), lambda i: (i, 0, 0), memory_space=pltpu.SMEM)
    smem_next = lambda: pl.BlockSpec((None, 1, tc), lambda i: (jnp.minimum(i + 1, steps - 1), 0, 0),
                                     memory_space=pltpu.SMEM)
    pos0, pos1 = pos0.reshape(steps, 1, tc), pos1.reshape(steps, 1, tc)
    return pl.pallas_call(
        functools.partial(_combine_kernel, ctx_steps=ctx_steps),
        grid=(steps,),
        in_specs=[smem(), smem(), smem_next(), smem_next(),
                  pl.BlockSpec(memory_space=pl.ANY),
                  pl.BlockSpec((tc, LANES), lambda i: (i, 0)),
                  pl.BlockSpec((tc, d), lambda i: (i, 0)),
                  pl.BlockSpec((None, 6, d), lambda i: (cond_of_tile(i // sub), 0, 0)),
                  pl.BlockSpec((1, d), lambda i: (0, 0))],
        out_specs=[pl.BlockSpec((tc, d), lambda i: (jnp.minimum(i, ctx_steps - 1), 0)),
                   pl.BlockSpec((tc, d), lambda i: (jnp.maximum(i - ctx_steps, 0), 0))],
        out_shape=[jax.ShapeDtypeStruct((rows_ctx, d), F32), jax.ShapeDtypeStruct((rows - rows_ctx, d), F32)],
        scratch_shapes=[pltpu.VMEM((2, tc, d), F32), pltpu.VMEM((2, tc, d), F32),
                        pltpu.SemaphoreType.DMA((2, 2))],
        compiler_params=_cparams(("arbitrary",)),
        name="moe_combine",
    )(pos0, pos1, pos0, pos1, ys, route, h, mod, final_norm)


def _dispatch_plan(route, tm):
    rows = route.shape[0]
    ids = route[:, ROUTE_IDX:ROUTE_IDX + TOP_K].astype(jnp.int32)
    flat = ids.reshape(-1)
    onehot = (flat[:, None] == jnp.arange(N_EXPERTS)[None, :]).astype(jnp.int32)
    rank = jnp.sum((jnp.cumsum(onehot, axis=0) - onehot) * onehot, axis=1)
    counts = jnp.sum(onehot, axis=0)
    tiles = (counts + tm - 1) // tm
    tile_end = jnp.cumsum(tiles)
    start = (tile_end - tiles) * tm
    slot = start[flat] + rank
    n_tiles = (rows * TOP_K) // tm + N_EXPERTS
    n_slots = n_tiles * tm
    slot_token = (jnp.arange(n_slots, dtype=jnp.int32) % rows).at[slot].set(
        jnp.arange(rows * TOP_K, dtype=jnp.int32) // TOP_K)
    t = jnp.arange(n_tiles)
    tile_expert = jnp.minimum(jnp.sum((t[:, None] >= tile_end[None, :]).astype(jnp.int32), axis=1), N_EXPERTS - 1)
    tile_in_expert = t - (tile_end - tiles)[tile_expert]
    tile_rows = jnp.where(t < tile_end[-1], jnp.clip(counts[tile_expert] - tile_in_expert * tm, 0, tm), 0)
    last_expert = tile_expert[jnp.maximum(tile_end[-1] - 1, 0)]
    tile_expert = jnp.where(tile_rows > 0, tile_expert, last_expert).astype(jnp.int32)
    pos = slot.reshape(rows, TOP_K)
    return slot_token, tile_expert, tile_rows.astype(jnp.int32), pos[:, 0], pos[:, 1]


def _rope_table(seq, n):
    quarter = n // 4
    t = np.arange(seq)
    inv = jnp.power(ROPE_BASE, -jnp.arange(quarter, dtype=F32) * (2.0 / (n // 2)))
    ang_r = jnp.asarray(t // GRID_W, F32)[:, None] * inv[None, :]
    ang_c = jnp.asarray(t % GRID_W, F32)[:, None] * inv[None, :]
    zero = jnp.zeros((seq, quarter), F32)
    cos = jnp.concatenate([jnp.cos(ang_r)] * 2 + [jnp.cos(ang_c)] * 2, axis=1)
    up = jnp.concatenate([-jnp.sin(ang_r), zero, -jnp.sin(ang_c), zero], axis=1)
    dn = jnp.concatenate([zero, jnp.sin(ang_r), zero, jnp.sin(ang_c)], axis=1)
    return cos, up, dn


def _tile_cols(parts, reps, pad_to=None):
    out = [jnp.tile(p, (1, reps)) for p in parts]
    if pad_to is not None:
        out = [jnp.pad(p, ((0, 0), (0, pad_to - p.shape[1]))) for p in out]
    return jnp.concatenate(out, axis=1)


def kernel(x_prompt, x_sample, cache_mla_ckv, cache_mla_krope, cache_win_k, cache_win_v, cache_nat_k, cache_nat_v, c, c_ctx, w_mod, b_mod, norm1, norm2, w_in, mla_q_norm, mla_kv_norm, w_mla_q_up, w_mla_kv_up, win_sink, nat_rpb, w_br_mla, w_br_win, w_br_nat, w_out, w_ff_gate, w_ff_up, w_ff_down, w_router, w_ex_gate, w_ex_up, w_ex_down, final_norm):
    nb_ctx, seq_ctx, d = x_prompt.shape
    nb_lat, seq_lat, _ = x_sample.shape
    past = cache_mla_ckv.shape[2]
    rows_ctx = nb_ctx * seq_ctx
    rows_lat = nb_lat * seq_lat
    assert d == D_MODEL and seq_lat == ROW_TILE and rows_ctx % ROW_TILE == 0
    assert seq_lat // GRID_W == 4 * NAT_QROWS
    assert w_mod.shape[0] == DEPTH == 2
    ctx_tiles = rows_ctx // ROW_TILE

    def cond_of_tile(i):
        return jnp.where(i < ctx_tiles, 0, i - ctx_tiles + 1)

    n_cond = 16
    cond = jnp.zeros((n_cond, d), F32).at[0].set(c_ctx).at[1:1 + nb_lat].set(c)
    mod_all = _modulation(cond, w_mod, b_mod).reshape(DEPTH, n_cond, 6, d)

    w_p, w_gate = _w_in_prep(w_in.astype(BF16))
    wq_up = w_mla_q_up.reshape(DEPTH, MLA_Q_RANK, MLA_HEADS, MLA_NOPE + MLA_ROPE)
    wq_up = jnp.concatenate([wq_up[..., :MLA_NOPE].reshape(DEPTH, MLA_Q_RANK, -1),
                             wq_up[..., MLA_NOPE:].reshape(DEPTH, MLA_Q_RANK, -1)], axis=2).astype(BF16)
    wkv_up = w_mla_kv_up.astype(BF16)
    w_br = jnp.stack([w_br_mla, w_br_win, w_br_nat], axis=1).astype(BF16)
    w_out_b = w_out.astype(BF16)
    w_router_p = jnp.pad(w_router, ((0, 0), (0, 0), (0, LANES - N_EXPERTS))).astype(BF16)

    t128 = jnp.concatenate(_rope_table(seq_lat, 128), axis=1)
    t64 = _rope_table(seq_lat, MLA_ROPE)
    t_q = _tile_cols(t64, MLA_HEADS)
    t_k = _tile_cols(t64, 1, pad_to=LANES)

    ck_win = cache_win_k.reshape(nb_lat, DEPTH, past, 256)
    cv_win = cache_win_v.reshape(nb_lat, DEPTH, past, 256)
    ck_nat = cache_nat_k.reshape(nb_lat, DEPTH, past, 1024)
    cv_nat = cache_nat_v.reshape(nb_lat, DEPTH, past, 1024)


    h = (x_prompt.reshape(rows_ctx, d), x_sample.reshape(rows_lat, d))
    states = None
    for l in range(DEPTH):
        mod = mod_all[l]
        qn = mla_q_norm[l].reshape(1, -1)
        kvn = mla_kv_norm[l].reshape(1, -1)
        sink = win_sink[l].reshape(1, -1)
        p, hn = _mixer_in(h, mod, norm1[l].reshape(1, d), w_p, l, cond_of_tile)
        o_ctx, states = _ctx_attention(p, nb_ctx, seq_ctx, qn, kvn, wq_up[l], wkv_up[l], sink,
                                       prev_states=states)
        o_a = _lat_mla(p, rows_ctx // seq_lat, nb_lat, seq_lat, cache_mla_ckv, cache_mla_krope, l,
                       qn, kvn, wq_up[l], wkv_up[l], t_q, t_k)
        o_b = _lat_win(p, rows_ctx, nb_lat, seq_lat, ck_win, cv_win, l, t128, sink)
        o_c = _lat_nat(p, rows_ctx, nb_lat, seq_lat, ck_nat, cv_nat, l, _nat_bias_blocks(nat_rpb[l]))
        m = _merge(hn, o_ctx, o_a, o_b, o_c, w_gate, w_br, l)
        if l % 2 == 0:
            h1, hn2 = _out_proj(m, w_out_b, h, mod, norm2[l].reshape(1, d), l, cond_of_tile)
            h = _ffn(hn2, w_ff_gate[l // 2].astype(BF16), w_ff_up[l // 2].astype(BF16),
                     w_ff_down[l // 2].astype(BF16), h1, mod, cond_of_tile)
        else:
            h1, hn2, route = _out_proj(m, w_out_b, h, mod, norm2[l].reshape(1, d), l, cond_of_tile,
                                       w_router=w_router_p[l // 2])
            slot_token, tile_expert, tile_valid, pos0, pos1 = _dispatch_plan(route, MOE_TM)
            xs = _gather_rows(hn2, slot_token, tile_valid, BF16)
            ys = _moe_ffn(xs, w_ex_gate[l // 2].astype(BF16), w_ex_up[l // 2].astype(BF16),
                          w_ex_down[l // 2].astype(BF16), tile_expert, tile_valid)
            y_ctx, y_lat = _moe_combine(ys, pos0, pos1, route, h1, mod, final_norm.reshape(1, d),
                                        cond_of_tile, rows_ctx)

    y_prompt = y_ctx.reshape(nb_ctx, seq_ctx, d)
    y_sample = y_lat.reshape(nb_lat, seq_lat, d)
    return (y_prompt, y_sample) + tuple(states)
```

```python
import functools

import jax
import jax.numpy as jnp
import numpy as np
from jax import lax
from jax.experimental import pallas as pl
from jax.experimental.pallas import tpu as pltpu

F32 = jnp.float32
BF16 = jnp.bfloat16

D_MODEL = 2048
DEPTH = 2
GRID_W = 64
ROPE_BASE = 10000.0
EPS = 1e-6
NEG_INF = -1e30
MLA_HEADS = 8
MLA_Q_RANK = 512
MLA_KV_RANK = 512
MLA_NOPE = 128
MLA_ROPE = 64
MLA_V = 128
WIN_HEADS = 8
WIN_KV_HEADS = 2
WIN_GROUP = WIN_HEADS // WIN_KV_HEADS
WIN_HEAD_DIM = 128
WINDOW = 128
NAT_HEADS = 8
NAT_HEAD_DIM = 128
NAT_ROWS = 8
NAT_COLS = 16
BRANCH_W = 1024
D_FF = 5632
N_EXPERTS = 8
TOP_K = 2

P_WQ, P_NQ, P_NK, P_NV = 0, 1024, 2048, 3072
P_WK, P_WV = 4096, 4352
P_QD, P_KVD, P_KR = 4608, 5120, 5632
P_COLS = 5760
P_TN = 1152
LANES = 128

VMEM_LIMIT = 56 * 1024 * 1024

ROW_TILE = 1024
FFN_TM = 512
FFN_TF = 512
MOE_TM = 1024
MOE_SLAB = 512
MOE_TF = 256
GATHER_ROWS = 256
NAT_QROWS = 4
NAT_KROWS = 12


def _cparams(sem):
    return pltpu.CompilerParams(dimension_semantics=sem, vmem_limit_bytes=VMEM_LIMIT)


def _rms(x, g):
    ms = jnp.mean(x * x, axis=-1, keepdims=True)
    return x * lax.rsqrt(ms + EPS) * g


def _dot(a, b):
    return jnp.dot(a, b, preferred_element_type=F32)


def _dot_nt(a, b):
    return lax.dot_general(a, b, (((1,), (1,)), ((), ())), preferred_element_type=F32)


def _rope(x, tab_ref_or_val, shift):
    n = x.shape[-1]
    t = tab_ref_or_val
    c, s_up, s_dn = t[:, 0:n], t[:, n:2 * n], t[:, 2 * n:3 * n]
    up = pltpu.roll(x, n - shift, axis=1)
    dn = pltpu.roll(x, shift, axis=1)
    return x * c + up * s_up + dn * s_dn


def _softmax_pv(s, v, sink=None):
    m = jnp.max(s, axis=-1, keepdims=True)
    if sink is not None:
        m = jnp.maximum(m, sink)
    e = jnp.exp(s - m)
    den = jnp.sum(e, axis=-1, keepdims=True)
    if sink is not None:
        den = den + jnp.exp(sink - m)
    return _dot(e.astype(BF16), v) * (1.0 / den)


def _attend_heads(n_heads, score, value, o_ref, col0, sink=None, group=None, emit=None):
    group = group or n_heads
    for h0 in range(0, n_heads, group):
        heads = range(h0, h0 + group)
        s = [score(h) for h in heads]
        m = [jnp.max(x, axis=-1, keepdims=True) for x in s]
        if sink is not None:
            sk = [sink(h) for h in heads]
            m = [jnp.maximum(a, b) for a, b in zip(m, sk)]
        e = [jnp.exp(x - a) for x, a in zip(s, m)]
        den = [jnp.sum(x, axis=-1, keepdims=True) for x in e]
        if sink is not None:
            den = [d + jnp.exp(b - a) for d, a, b in zip(den, m, sk)]
        pv = [_dot(x.astype(BF16), value(h)) for x, h in zip(e, heads)]
        for i, h in enumerate(heads):
            out = pv[i] * (1.0 / den[i])
            if emit is not None:
                emit(h, out)
            else:
                o_ref[:, col0 + h * LANES:col0 + (h + 1) * LANES] = out.astype(o_ref.dtype)


def _mod_kernel(c_ref, w_ref, b_ref, o_ref):
    c = c_ref[...]
    s = (c * jax.nn.sigmoid(c)).astype(BF16)
    o_ref[...] = _dot(s, w_ref[...].astype(BF16)) + b_ref[...]


def _modulation(cond, w_mod, b_mod):
    depth, d, n = w_mod.shape
    nc = cond.shape[0]
    tn = 1024
    return pl.pallas_call(
        _mod_kernel,
        grid=(depth, n // tn),
        in_specs=[
            pl.BlockSpec((nc, d), lambda l, j: (0, 0)),
            pl.BlockSpec((None, d, tn), lambda l, j: (l, 0, j)),
            pl.BlockSpec((None, 1, tn), lambda l, j: (l, 0, j)),
        ],
        out_specs=pl.BlockSpec((None, nc, tn), lambda l, j: (l, 0, j)),
        out_shape=jax.ShapeDtypeStruct((depth, nc, n), F32),
        compiler_params=_cparams(("arbitrary", "arbitrary")),
        name="modulation",
    )(cond, w_mod, b_mod.reshape(depth, 1, n))


_W_IN_SEGMENTS = ((1088, 1024), (2624, 1024), (3648, 1024), (4672, 1024), (2112, 256), (2368, 256),
                  (0, 512), (512, 512), (1024, 64))
W_IN_GATES = 5696


def _w_in_prep_kernel(w_ref, p_ref, g_ref):
    col = 0
    for src, width in _W_IN_SEGMENTS:
        p_ref[:, col:col + width] = w_ref[:, src:src + width].astype(BF16)
        col += width
    p_ref[:, col:] = jnp.zeros((p_ref.shape[0], P_COLS - col), BF16)
    g_ref[...] = w_ref[:, W_IN_GATES:W_IN_GATES + g_ref.shape[1]].astype(BF16)


def _w_in_prep(w_in, n_cols):
    depth, d, n = w_in.shape
    tr = 128
    n_gate = n_cols - W_IN_GATES
    return pl.pallas_call(
        _w_in_prep_kernel,
        grid=(depth, d // tr),
        in_specs=[pl.BlockSpec((None, tr, n), lambda l, r: (l, r, 0))],
        out_specs=[pl.BlockSpec((None, tr, P_COLS), lambda l, r: (l, r, 0)),
                   pl.BlockSpec((None, tr, n_gate), lambda l, r: (l, r, 0))],
        out_shape=[jax.ShapeDtypeStruct((depth, d, P_COLS), BF16),
                   jax.ShapeDtypeStruct((depth, d, n_gate), BF16)],
        compiler_params=_cparams(("arbitrary", "arbitrary")),
        name="w_in_prep",
    )(w_in)


def _split_rows_specs(h, tm):
    if not isinstance(h, tuple):
        return [h], [pl.BlockSpec((tm, h.shape[1]), lambda i, *_: (i, 0))], None
    ctx_tiles = h[0].shape[0] // tm
    d = h[0].shape[1]
    return (list(h),
            [pl.BlockSpec((tm, d), lambda i, *_: (jnp.minimum(i, ctx_tiles - 1), 0)),
             pl.BlockSpec((tm, d), lambda i, *_: (jnp.maximum(i - ctx_tiles, 0), 0))],
            ctx_tiles)


def _row_group_value(h_refs, ctx_tiles):
    if ctx_tiles is None:
        return h_refs[0][...]
    return jnp.where(pl.program_id(0) < ctx_tiles, h_refs[0][...], h_refs[1][...])


def _mixer_in_kernel(*refs, ctx_tiles):
    nh = 1 if ctx_tiles is None else 2
    h_refs = refs[:nh]
    mod_ref, g_ref, w_ref, p_ref, hn_ref = refs[nh:]

    @pl.when(pl.program_id(1) == 0)
    def _():
        h = _row_group_value(h_refs, ctx_tiles)
        hn = _rms(h, g_ref[...]) * (1.0 + mod_ref[1:2, :]) + mod_ref[0:1, :]
        hn_ref[...] = hn.astype(BF16)

    p_ref[...] = _dot(hn_ref[...], w_ref[...])


def _mixer_in(h, mod, g, w, layer, cond_of_tile):
    tm, tn = (ROW_TILE // 2 if isinstance(h, tuple) else ROW_TILE), P_TN
    sub = ROW_TILE // tm
    h_args, h_specs, ctx_tiles = _split_rows_specs(h, tm)
    rows = sum(a.shape[0] for a in h_args)
    d = h_args[0].shape[1]
    n = w.shape[2]
    return pl.pallas_call(
        functools.partial(_mixer_in_kernel, ctx_tiles=ctx_tiles),
        grid=(rows // tm, n // tn),
        in_specs=h_specs + [
            pl.BlockSpec((None, 6, d), lambda i, j: (cond_of_tile(i // sub), 0, 0)),
            pl.BlockSpec((1, d), lambda i, j: (0, 0)),
            pl.BlockSpec((None, d, tn), lambda i, j: (layer, 0, j)),
        ],
        out_specs=[
            pl.BlockSpec((tm, tn), lambda i, j: (i, j)),
            pl.BlockSpec((tm, d), lambda i, j: (i, 0)),
        ],
        out_shape=[
            jax.ShapeDtypeStruct((rows, n), F32),
            jax.ShapeDtypeStruct((rows, d), BF16),
        ],
        compiler_params=_cparams(("arbitrary", "arbitrary")),
        name="mixer_in",
    )(*h_args, mod, g, w)


_STATE_WIDTHS = (MLA_KV_RANK, MLA_ROPE, 256, 256, 1024, 1024)
_STATE_HEADS = (None, None, WIN_KV_HEADS, WIN_KV_HEADS, NAT_HEADS, NAT_HEADS)


def _ctx_attn_kernel(p_ref, qn_ref, kvn_ref, wq_ref, wkv_ref, sink_ref, *refs, stacked):
    n_state = len(_STATE_WIDTHS)
    if stacked:
        prev_refs, o_ref, state_refs = refs[:n_state], refs[n_state], refs[n_state + 1:]
    else:
        prev_refs, o_ref, state_refs = None, refs[0], refs[1:]

    def put_state(k, val, slot=DEPTH - 1):
        ref, heads = state_refs[k], _STATE_HEADS[k]
        if not stacked:
            ref[...] = val
        elif heads is None:
            ref[slot] = val
        else:
            for j in range(heads):
                ref[slot, :, j, :] = val[:, j * LANES:(j + 1) * LANES]

    if stacked:
        for k in range(n_state):
            put_state(k, prev_refs[k][...], slot=0)

    q = _dot(_rms(p_ref[:, P_QD:P_QD + MLA_Q_RANK], qn_ref[...]).astype(BF16), wq_ref[...])
    ckv = _rms(p_ref[:, P_KVD:P_KVD + MLA_KV_RANK], kvn_ref[...])
    put_state(0, ckv)
    kv = _dot(ckv.astype(BF16), wkv_ref[...]).astype(BF16)
    kr = p_ref[:, P_KR:P_KR + MLA_ROPE]
    put_state(1, kr)
    krb = kr.astype(BF16)
    scale_a = (MLA_NOPE + MLA_ROPE) ** -0.5
    nope_w = MLA_HEADS * MLA_NOPE

    def score_a(h):
        qn = q[:, h * MLA_NOPE:(h + 1) * MLA_NOPE].astype(BF16)
        qr = q[:, nope_w + h * MLA_ROPE:nope_w + (h + 1) * MLA_ROPE].astype(BF16)
        return (_dot_nt(qn, kv[:, h * 256:h * 256 + MLA_NOPE]) + _dot_nt(qr, krb)) * scale_a

    _attend_heads(MLA_HEADS, score_a, lambda h: kv[:, h * 256 + MLA_NOPE:(h + 1) * 256], o_ref, 0)
    put_state(2, p_ref[:, P_WK:P_WK + 256])
    put_state(3, p_ref[:, P_WV:P_WV + 256])
    scale_b = WIN_HEAD_DIM ** -0.5

    def score_b(h):
        kvh = h // WIN_GROUP
        qh = p_ref[:, P_WQ + h * 128:P_WQ + (h + 1) * 128].astype(BF16)
        return _dot_nt(qh, p_ref[:, P_WK + kvh * 128:P_WK + (kvh + 1) * 128].astype(BF16)) * scale_b

    def value_b(h):
        kvh = h // WIN_GROUP
        return p_ref[:, P_WV + kvh * 128:P_WV + (kvh + 1) * 128].astype(BF16)

    _attend_heads(WIN_HEADS, score_b, value_b, o_ref, BRANCH_W, lambda h: sink_ref[0:1, h:h + 1])
    put_state(4, p_ref[:, P_NK:P_NK + 1024])
    put_state(5, p_ref[:, P_NV:P_NV + 1024])
    scale_c = NAT_HEAD_DIM ** -0.5

    def score_c(h):
        qh = p_ref[:, P_NQ + h * 128:P_NQ + (h + 1) * 128].astype(BF16)
        return _dot_nt(qh, p_ref[:, P_NK + h * 128:P_NK + (h + 1) * 128].astype(BF16)) * scale_c

    _attend_heads(NAT_HEADS, score_c, lambda h: p_ref[:, P_NV + h * 128:P_NV + (h + 1) * 128].astype(BF16),
                  o_ref, 2 * BRANCH_W)


def _ctx_attention(p, nb, seq, qn, kvn, wq, wkv, sink, prev_states=None):
    full = lambda a: pl.BlockSpec(a.shape, lambda b: (0,) * a.ndim)
    row = lambda w: pl.BlockSpec((seq, w), lambda b: (b, 0))
    stacked = prev_states is not None
    in_specs = [pl.BlockSpec((seq, P_COLS), lambda b: (b, 0)),
                full(qn), full(kvn), full(wq), full(wkv), full(sink)]
    args = [p, qn, kvn, wq, wkv, sink]
    out_specs = [row(3 * BRANCH_W)]
    out_shape = [jax.ShapeDtypeStruct((nb * seq, 3 * BRANCH_W), BF16)]
    for w, heads in zip(_STATE_WIDTHS, _STATE_HEADS):
        if not stacked:
            out_specs.append(row(w))
            out_shape.append(jax.ShapeDtypeStruct((nb * seq, w), F32))
        else:
            tail = (w,) if heads is None else (heads, w // heads)
            out_specs.append(pl.BlockSpec((None, DEPTH, seq) + tail, lambda b, n=len(tail): (b,) + (0,) * (n + 2)))
            out_shape.append(jax.ShapeDtypeStruct((nb, DEPTH, seq) + tail, F32))
    if stacked:
        in_specs += [row(w) for w in _STATE_WIDTHS]
        args += list(prev_states)
    outs = pl.pallas_call(
        functools.partial(_ctx_attn_kernel, stacked=stacked),
        grid=(nb,),
        in_specs=in_specs, out_specs=out_specs, out_shape=out_shape,
        compiler_params=_cparams(("arbitrary",)),
        name="ctx_attention",
    )(*args)
    return outs[0], outs[1:]


MLA_QT = 512
MLA_KC = 256
MLA_DK = MLA_NOPE + MLA_ROPE


def _lat_mla_kernel(p_ref, cckv_ref, ckr_ref, qn_ref, kvn_ref, wq_ref, wkv_ref, tq_ref, tk_ref,
                    o_ref, k_scr, v_scr, *, past, seq):
    qt = pl.program_id(1)

    def put_keys(r0, n, kv, kr):
        krb = kr.astype(BF16)
        for h in range(MLA_HEADS):
            k_scr[h, r0:r0 + n, 0:MLA_NOPE] = kv[:, h * 256:h * 256 + MLA_NOPE].astype(BF16)
            k_scr[h, r0:r0 + n, MLA_NOPE:MLA_DK] = krb
            v_scr[r0:r0 + n, h * MLA_V:(h + 1) * MLA_V] = kv[:, h * 256 + MLA_NOPE:(h + 1) * 256].astype(BF16)

    @pl.when(qt == 0)
    def _():
        for c in range(past // MLA_KC):
            r0 = c * MLA_KC
            kv = _dot(cckv_ref[r0:r0 + MLA_KC, :].astype(BF16), wkv_ref[...])
            put_keys(r0, MLA_KC, kv, ckr_ref[r0:r0 + MLA_KC, :])
        for c in range(seq // MLA_KC):
            r0 = c * MLA_KC
            ckv = _rms(p_ref[r0:r0 + MLA_KC, MLA_Q_RANK:MLA_Q_RANK + MLA_KV_RANK], kvn_ref[...])
            kv = _dot(ckv.astype(BF16), wkv_ref[...])
            krp = p_ref[r0:r0 + MLA_KC, 2 * MLA_Q_RANK:2 * MLA_Q_RANK + LANES]
            kr = _rope(krp, tk_ref[r0:r0 + MLA_KC, :], MLA_ROPE // 4)
            put_keys(past + r0, MLA_KC, kv, kr[:, 0:MLA_ROPE])

    r0 = pl.multiple_of(qt * MLA_QT, MLA_QT)
    qd = p_ref[pl.ds(r0, MLA_QT), 0:MLA_Q_RANK]
    q = _dot(_rms(qd, qn_ref[...]).astype(BF16), wq_ref[...])
    nope_w = MLA_HEADS * MLA_NOPE
    q_rope = _rope(q[:, nope_w:], tq_ref[pl.ds(r0, MLA_QT), :], MLA_ROPE // 4)
    scale = MLA_DK ** -0.5

    def score(h):
        qh = jnp.concatenate([q[:, h * MLA_NOPE:(h + 1) * MLA_NOPE],
                              q_rope[:, h * MLA_ROPE:(h + 1) * MLA_ROPE]], axis=1).astype(BF16)
        return _dot_nt(qh, k_scr[h]) * scale

    _attend_heads(MLA_HEADS, score, lambda h: v_scr[:, h * MLA_V:(h + 1) * MLA_V], o_ref, 0, group=2)


def _lat_mla(p, row_blk0, nb, seq, cache_ckv, cache_kr, layer, qn, kvn, wq, wkv, tq, tk):
    past = cache_ckv.shape[2]
    full = lambda a: pl.BlockSpec(a.shape, lambda b, t: (0,) * a.ndim)
    nqt = seq // MLA_QT
    return pl.pallas_call(
        functools.partial(_lat_mla_kernel, past=past, seq=seq),
        grid=(nb, nqt),
        in_specs=[
            pl.BlockSpec((seq, P_TN), lambda b, t: (row_blk0 + b, P_QD // P_TN)),
            pl.BlockSpec((None, None, past, MLA_KV_RANK), lambda b, t: (b, layer, 0, 0)),
            pl.BlockSpec((None, None, past, MLA_ROPE), lambda b, t: (b, layer, 0, 0)),
            full(qn), full(kvn), full(wq), full(wkv), full(tq), full(tk),
        ],
        out_specs=pl.BlockSpec((MLA_QT, BRANCH_W), lambda b, t: (b * nqt + t, 0)),
        out_shape=jax.ShapeDtypeStruct((nb * seq, BRANCH_W), BF16),
        scratch_shapes=[pltpu.VMEM((MLA_HEADS, past + seq, MLA_DK), BF16),
                        pltpu.VMEM((past + seq, MLA_HEADS * MLA_V), BF16)],
        compiler_params=_cparams(("arbitrary", "arbitrary")),
        name="lat_mla",
    )(p, cache_ckv, cache_kr, qn, kvn, wq, wkv, tq, tk)


WIN_QB = 128


def _lat_win_kernel(q_ref, k0_ref, k1_ref, k2_ref, v0_ref, v1_ref, v2_ref, ck_ref, cv_ref,
                    t0_ref, t1_ref, t2_ref, sink_ref, o_ref, *, past, seq):
    qb = pl.program_id(1)
    scale = WIN_HEAD_DIM ** -0.5
    n_loc = 3 * WIN_QB
    rows = WIN_GROUP * WIN_QB
    qpos = qb * WIN_QB + lax.broadcasted_iota(jnp.int32, (rows, n_loc), 0) % WIN_QB
    kpos = (qb - 1) * WIN_QB + lax.broadcasted_iota(jnp.int32, (rows, n_loc), 1)
    valid = (kpos >= 0) & (kpos < seq) & (jnp.abs(qpos - kpos) <= WINDOW)
    t1 = t1_ref[...]
    q_heads = lambda kvh: range(kvh * WIN_GROUP, (kvh + 1) * WIN_GROUP)

    def score(kvh):
        cs = slice(kvh * 128, (kvh + 1) * 128)
        keys = jnp.concatenate([
            ck_ref[:, cs],
            _rope(k0_ref[:, cs], t0_ref[...], 32),
            _rope(k1_ref[:, cs], t1, 32),
            _rope(k2_ref[:, cs], t2_ref[...], 32)], axis=0).astype(BF16)
        q = jnp.concatenate([_rope(q_ref[:, h * 128:(h + 1) * 128], t1, 32) for h in q_heads(kvh)],
                            axis=0).astype(BF16)
        s = _dot_nt(q, keys) * scale
        return jnp.concatenate([s[:, :past], jnp.where(valid, s[:, past:], NEG_INF)], axis=1)

    def value(kvh):
        cs = slice(kvh * 128, (kvh + 1) * 128)
        return jnp.concatenate([cv_ref[:, cs], v0_ref[:, cs], v1_ref[:, cs], v2_ref[:, cs]], axis=0).astype(BF16)

    def sink(kvh):
        return jnp.concatenate([jnp.broadcast_to(sink_ref[0:1, h:h + 1], (WIN_QB, 1)) for h in q_heads(kvh)],
                               axis=0)

    def emit(kvh, out):
        for g, h in enumerate(q_heads(kvh)):
            o_ref[:, h * 128:(h + 1) * 128] = out[g * WIN_QB:(g + 1) * WIN_QB, :].astype(BF16)

    _attend_heads(WIN_KV_HEADS, score, value, o_ref, 0, sink=sink, emit=emit)


def _lat_win(p, row0, nb, seq, cache_k, cache_v, layer, tab, sink):
    past = cache_k.shape[2]
    nqb = seq // WIN_QB
    rb0 = row0 // WIN_QB

    def kblk(off, col_blk):
        return pl.BlockSpec(
            (WIN_QB, 256),
            lambda b, t: (rb0 + b * nqb + jnp.clip(t + off, 0, nqb - 1), col_blk))

    def tblk(off):
        return pl.BlockSpec((WIN_QB, 3 * 128), lambda b, t: (jnp.clip(t + off, 0, nqb - 1), 0))

    cache = pl.BlockSpec((None, None, past, 256), lambda b, t: (b, layer, 0, 0))
    return pl.pallas_call(
        functools.partial(_lat_win_kernel, past=past, seq=seq),
        grid=(nb, nqb),
        in_specs=[
            pl.BlockSpec((WIN_QB, 1024), lambda b, t: (rb0 + b * nqb + t, P_WQ // 1024)),
            kblk(-1, P_WK // 256), kblk(0, P_WK // 256), kblk(1, P_WK // 256),
            kblk(-1, P_WV // 256), kblk(0, P_WV // 256), kblk(1, P_WV // 256),
            cache, cache,
            tblk(-1), tblk(0), tblk(1),
            pl.BlockSpec(sink.shape, lambda b, t: (0, 0)),
        ],
        out_specs=pl.BlockSpec((WIN_QB, BRANCH_W), lambda b, t: (b * nqb + t, 0)),
        out_shape=jax.ShapeDtypeStruct((nb * seq, BRANCH_W), BF16),
        compiler_params=_cparams(("arbitrary", "arbitrary")),
        name="lat_window",
    )(p, p, p, p, p, p, p, cache_k, cache_v, tab, tab, tab, sink)


def _lat_nat_kernel(q_ref, k0_ref, k1_ref, k2_ref, v0_ref, v1_ref, v2_ref, ck_ref, cv_ref, bias_ref, o_ref,
                    *, grid_rows):
    g = pl.program_id(0)
    scale = NAT_HEAD_DIM ** -0.5
    past = ck_ref.shape[0]
    win_rows = min(NAT_ROWS, grid_rows)
    lane = lax.broadcasted_iota(jnp.int32, (GRID_W, 2 * GRID_W), 1)

    def rows_of(h, refs):
        cs = slice(h * 128, (h + 1) * 128)
        return jnp.concatenate([r[:, cs] for r in refs], axis=0).astype(BF16)

    def pair_plan(rq, rkp):
        r = NAT_QROWS * g + rq
        kr0 = NAT_QROWS * _nat_key_start(g) + 2 * rkp
        rs = jnp.clip(r - win_rows // 2, 0, grid_rows - win_rows)
        in_win = lambda kr: ((kr >= rs) & (kr < rs + win_rows)).astype(jnp.int32)
        ok = jnp.where(lane < GRID_W, in_win(kr0), in_win(kr0 + 1)) > 0
        return jnp.clip(kr0 - r + NAT_ROWS - 1, 0, 2 * NAT_ROWS - 1), ok

    plans = [[pair_plan(rq, rkp) for rkp in range(NAT_KROWS // 2)] for rq in range(NAT_QROWS)]

    def bias(h):
        return jnp.concatenate(
            [jnp.concatenate([jnp.where(ok, bias_ref[h, a], NEG_INF) for a, ok in row], axis=1) for row in plans],
            axis=0)

    def score(h):
        q = q_ref[:, h * 128:(h + 1) * 128].astype(BF16)
        s = _dot_nt(q, rows_of(h, (ck_ref, k0_ref, k1_ref, k2_ref))) * scale
        return jnp.concatenate([s[:, :past], s[:, past:] + bias(h)], axis=1)

    _attend_heads(NAT_HEADS, score, lambda h: rows_of(h, (cv_ref, v0_ref, v1_ref, v2_ref)), o_ref, 0, group=2)


def _nat_key_start(g):
    return g // 2


def _lat_nat(p, row0, nb, seq, cache_k, cache_v, layer, bias):
    past = cache_k.shape[2]
    qrows = NAT_QROWS * GRID_W
    ng = seq // qrows
    rb0 = row0 // qrows

    def kblk(off, col_blk):
        return pl.BlockSpec((qrows, 1024), lambda g, b: (rb0 + b * ng + _nat_key_start(g) + off, col_blk))

    cache = pl.BlockSpec((None, None, past, 1024), lambda g, b: (b, layer, 0, 0))
    return pl.pallas_call(
        functools.partial(_lat_nat_kernel, grid_rows=seq // GRID_W),
        grid=(ng, nb),
        in_specs=[
            pl.BlockSpec((qrows, 1024), lambda g, b: (rb0 + b * ng + g, P_NQ // 1024)),
            kblk(0, P_NK // 1024), kblk(1, P_NK // 1024), kblk(2, P_NK // 1024),
            kblk(0, P_NV // 1024), kblk(1, P_NV // 1024), kblk(2, P_NV // 1024),
            cache, cache,
            pl.BlockSpec(bias.shape, lambda g, b: (0, 0, 0, 0)),
        ],
        out_specs=pl.BlockSpec((qrows, BRANCH_W), lambda g, b: (b * ng + g, 0)),
        out_shape=jax.ShapeDtypeStruct((nb * seq, BRANCH_W), BF16),
        compiler_params=_cparams(("arbitrary", "arbitrary")),
        name="lat_neighbourhood",
    )(p, p, p, p, p, p, p, cache_k, cache_v, bias)


def _nat_bias_blocks(rpb):
    n_dr, n_dc = 2 * NAT_ROWS - 1, 2 * NAT_COLS - 1
    cc = np.arange(GRID_W)
    dc = cc[None, :] - cc[:, None] + NAT_COLS - 1
    cs = np.clip(cc - NAT_COLS // 2, 0, GRID_W - NAT_COLS)[:, None]
    col_ok = (cc[None, :] >= cs) & (cc[None, :] < cs + NAT_COLS)
    onehot = ((dc[None] == np.arange(n_dc)[:, None, None]) & col_ok[None]).astype(np.float32)
    toep = jnp.einsum('had,dck->hack', rpb.astype(F32), jnp.asarray(onehot), precision=lax.Precision.HIGHEST)
    toep = jnp.where(jnp.asarray(col_ok), toep, NEG_INF)
    fill = jnp.full((NAT_HEADS, 2 * NAT_ROWS + 1 - n_dr, GRID_W, GRID_W), NEG_INF, F32)
    toep = jnp.concatenate([toep, fill], axis=1)
    return jnp.concatenate([toep[:, :-1], toep[:, 1:]], axis=-1)


def _merge_kernel(hn_ref, octx_ref, oa_ref, ob_ref, oc_ref, wga_ref, wgb_ref, wgc_ref,
                  wba_ref, wbb_ref, wbc_ref, m_ref, *, ctx_tiles):
    i = pl.program_id(0)
    weights = ((wga_ref, wba_ref), (wgb_ref, wbb_ref), (wgc_ref, wbc_ref))

    def merged(branch):
        hn = hn_ref[...]
        acc = None
        for k, (wg, wb) in enumerate(weights):
            term = jax.nn.sigmoid(_dot(hn, wg[...])) * _dot(branch(k), wb[...])
            acc = term if acc is None else acc + term
        m_ref[...] = acc.astype(BF16)

    @pl.when(i < ctx_tiles)
    def _():
        merged(lambda k: octx_ref[:, k * BRANCH_W:(k + 1) * BRANCH_W])

    @pl.when(i >= ctx_tiles)
    def _():
        lat = (oa_ref, ob_ref, oc_ref)
        merged(lambda k: lat[k][...])


def _merge(hn, o_ctx, o_a, o_b, o_c, w_gate, w_br, layer):
    rows, d = hn.shape
    tm, tn = ROW_TILE, 256
    nj = d // tn
    ctx_tiles = o_ctx.shape[0] // tm
    gate = lambda k: pl.BlockSpec((None, d, tn), lambda i, j: (layer, 0, k * nj + j))
    br = lambda k: pl.BlockSpec((None, None, BRANCH_W, tn), lambda i, j: (layer, k, 0, j))
    lat = pl.BlockSpec((tm, BRANCH_W), lambda i, j: (jnp.maximum(i - ctx_tiles, 0), 0))
    return pl.pallas_call(
        functools.partial(_merge_kernel, ctx_tiles=ctx_tiles),
        grid=(rows // tm, nj),
        in_specs=[pl.BlockSpec((tm, d), lambda i, j: (i, 0)),
                  pl.BlockSpec((tm, 3 * BRANCH_W), lambda i, j: (jnp.minimum(i, ctx_tiles - 1), 0)),
                  lat, lat, lat,
                  gate(0), gate(1), gate(2), br(0), br(1), br(2)],
        out_specs=pl.BlockSpec((tm, tn), lambda i, j: (i, j)),
        out_shape=jax.ShapeDtypeStruct((rows, d), BF16),
        compiler_params=_cparams(("arbitrary", "arbitrary")),
        name="merge",
    )(hn, o_ctx, o_a, o_b, o_c, w_gate, w_gate, w_gate, w_br, w_br, w_br)


ROUTE_IDX = 8
ROUTE_W = 10


def _out_proj_kernel(m_ref, w_ref, mod_ref, g_ref, *rest, routed, ctx_tiles):
    nh = 1 if ctx_tiles is None else 2
    h_refs, rest = rest[:nh], rest[nh:]
    if routed:
        wr_ref, h1_ref, hn2_ref, route_ref = rest
    else:
        h1_ref, hn2_ref = rest
    h1 = _row_group_value(h_refs, ctx_tiles) + mod_ref[2:3, :] * _dot(m_ref[...], w_ref[...])
    h1_ref[...] = h1
    hn2 = _rms(h1, g_ref[...]) * (1.0 + mod_ref[4:5, :]) + mod_ref[3:4, :]
    hn2_ref[...] = hn2.astype(hn2_ref.dtype)
    if routed:
        logits = _dot(hn2.astype(BF16), wr_ref[...])
        lane = lax.broadcasted_iota(jnp.int32, logits.shape, 1).astype(F32)
        lg = jnp.where(lane < N_EXPERTS, logits, -jnp.inf)
        m1 = jnp.max(lg, axis=-1, keepdims=True)
        i1 = jnp.min(jnp.where(lg == m1, lane, float(LANES)), axis=-1, keepdims=True)
        lg2 = jnp.where(lane == i1, -jnp.inf, lg)
        m2 = jnp.max(lg2, axis=-1, keepdims=True)
        i2 = jnp.min(jnp.where(lg2 == m2, lane, float(LANES)), axis=-1, keepdims=True)
        e2 = jnp.exp(m2 - m1)
        w1 = 1.0 / (1.0 + e2)
        w2 = e2 / (1.0 + e2)
        route = (jnp.where(lane == ROUTE_IDX, i1, 0.0)
                 + jnp.where(lane == ROUTE_IDX + 1, i2, 0.0)
                 + jnp.where(lane == ROUTE_W, w1, 0.0)
                 + jnp.where(lane == ROUTE_W + 1, w2, 0.0))
        route_ref[...] = route


def _out_proj(m, w_out, h, mod, g, layer, cond_of_tile, w_router=None):
    rows, d = m.shape
    tm = 512
    sub = ROW_TILE // tm
    routed = w_router is not None
    h_args, h_specs, ctx_tiles = _split_rows_specs(h, tm)
    in_specs = [
        pl.BlockSpec((tm, d), lambda i: (i, 0)),
        pl.BlockSpec((None, d, d), lambda i: (layer, 0, 0)),
        pl.BlockSpec((None, 6, d), lambda i: (cond_of_tile(i // sub), 0, 0)),
        pl.BlockSpec((1, d), lambda i: (0, 0)),
    ] + h_specs
    out_specs = [pl.BlockSpec((tm, d), lambda i: (i, 0)), pl.BlockSpec((tm, d), lambda i: (i, 0))]
    out_shape = [jax.ShapeDtypeStruct((rows, d), F32), jax.ShapeDtypeStruct((rows, d), BF16)]
    args = [m, w_out, mod, g] + h_args
    if routed:
        out_shape[1] = jax.ShapeDtypeStruct((rows, d), F32)
        in_specs.append(pl.BlockSpec((d, LANES), lambda i: (0, 0)))
        out_specs.append(pl.BlockSpec((tm, LANES), lambda i: (i, 0)))
        out_shape.append(jax.ShapeDtypeStruct((rows, LANES), F32))
        args.append(w_router)
    return pl.pallas_call(
        functools.partial(_out_proj_kernel, routed=routed, ctx_tiles=ctx_tiles),
        grid=(rows // tm,),
        in_specs=in_specs, out_specs=out_specs, out_shape=out_shape,
        compiler_params=_cparams(("arbitrary",)),
        name="out_proj",
    )(*args)


def _swiglu_chunk(x, wg_ref, wu_ref, wd_ref):
    g = _dot(x, wg_ref[...].astype(BF16))
    u = _dot(x, wu_ref[...].astype(BF16))
    a = (g * jax.nn.sigmoid(g) * u).astype(BF16)
    return _dot(a, wd_ref[...].astype(BF16))


def _ffn_kernel(x_ref, wg_ref, wu_ref, wd_ref, h_ref, mod_ref, o_ref):
    f = pl.program_id(1)

    @pl.when(f == 0)
    def _():
        o_ref[...] = jnp.zeros_like(o_ref)

    o_ref[...] += _swiglu_chunk(x_ref[...], wg_ref, wu_ref, wd_ref)

    @pl.when(f == pl.num_programs(1) - 1)
    def _():
        o_ref[...] = h_ref[...] + mod_ref[5:6, :] * o_ref[...]


def _ffn(x, wg, wu, wd, h, mod, cond_of_tile):
    rows, d = h.shape
    ff = wg.shape[1]
    tm, tf = FFN_TM, FFN_TF
    sub = ROW_TILE // tm
    return pl.pallas_call(
        _ffn_kernel,
        grid=(rows // tm, ff // tf),
        in_specs=[
            pl.BlockSpec((tm, d), lambda i, f: (i, 0)),
            pl.BlockSpec((d, tf), lambda i, f: (0, f)),
            pl.BlockSpec((d, tf), lambda i, f: (0, f)),
            pl.BlockSpec((tf, d), lambda i, f: (f, 0)),
            pl.BlockSpec((tm, d), lambda i, f: (i, 0)),
            pl.BlockSpec((None, 6, d), lambda i, f: (cond_of_tile(i // sub), 0, 0)),
        ],
        out_specs=pl.BlockSpec((tm, d), lambda i, f: (i, 0)),
        out_shape=jax.ShapeDtypeStruct((rows, d), F32),
        compiler_params=_cparams(("arbitrary", "arbitrary")),
        name="ffn_dense",
    )(x, wg, wu, wd, h, mod)


def _start_row_gather(src_ref, idx_ref, dst_ref, sem, n):
    def issue(r, carry):
        pltpu.make_async_copy(src_ref.at[pl.ds(idx_ref[0, r], 1), :], dst_ref.at[pl.ds(r, 1), :], sem).start()
        return carry

    lax.fori_loop(0, n, issue, 0, unroll=8)


def _wait_row_gather(src_ref, dst_ref, sem, n):
    pltpu.make_async_copy(src_ref.at[pl.ds(0, n), :], dst_ref, sem).wait()


def _gather_kernel(tr_ref, idx_ref, idx_next_ref, src_ref, o_ref, buf, sem):
    i = pl.program_id(0)
    n = o_ref.shape[0]
    per_tile = MOE_TM // n
    slot = i % 2

    def used(step):
        return tr_ref[step // per_tile] > (step % per_tile) * n

    @pl.when((i == 0) & used(0))
    def _():
        _start_row_gather(src_ref, idx_ref, buf.at[0], sem.at[0], n)

    nxt = jnp.minimum(i + 1, pl.num_programs(0) - 1)

    @pl.when((i + 1 < pl.num_programs(0)) & used(nxt))
    def _():
        _start_row_gather(src_ref, idx_next_ref, buf.at[1 - slot], sem.at[1 - slot], n)

    @pl.when(used(i))
    def _():
        _wait_row_gather(src_ref, buf.at[slot], sem.at[slot], n)
        o_ref[...] = buf[slot].astype(o_ref.dtype)

    @pl.when(jnp.logical_not(used(i)))
    def _():
        o_ref[...] = jnp.zeros_like(o_ref)


def _gather_rows(src, idx, tile_rows, out_dtype):
    n = idx.shape[0]
    d = src.shape[1]
    tg = GATHER_ROWS
    steps = n // tg
    idx3 = idx.reshape(steps, 1, tg)
    grid_spec = pltpu.PrefetchScalarGridSpec(
        num_scalar_prefetch=1,
        grid=(steps,),
        in_specs=[pl.BlockSpec((None, 1, tg), lambda i, tr: (i, 0, 0), memory_space=pltpu.SMEM),
                  pl.BlockSpec((None, 1, tg), lambda i, tr: (jnp.minimum(i + 1, steps - 1), 0, 0),
                               memory_space=pltpu.SMEM),
                  pl.BlockSpec(memory_space=pl.ANY)],
        out_specs=pl.BlockSpec((tg, d), lambda i, tr: (i, 0)),
        scratch_shapes=[pltpu.VMEM((2, tg, d), src.dtype), pltpu.SemaphoreType.DMA((2,))],
    )
    return pl.pallas_call(
        _gather_kernel,
        grid_spec=grid_spec,
        out_shape=jax.ShapeDtypeStruct((n, d), out_dtype),
        compiler_params=_cparams(("arbitrary",)),
        name="moe_gather",
    )(tile_rows, idx3, idx3, src)


def _moe_ffn_kernel(te_ref, tv_ref, x_ref, wg_ref, wu_ref, wd_ref, o_ref):
    i = pl.program_id(0)
    f = pl.program_id(1)
    rows_used = tv_ref[i]
    for s in range(MOE_TM // MOE_SLAB):
        rs = slice(s * MOE_SLAB, (s + 1) * MOE_SLAB)
        used = rows_used > s * MOE_SLAB

        @pl.when(f == 0)
        def _():
            o_ref[rs, :] = jnp.zeros((MOE_SLAB, o_ref.shape[1]), o_ref.dtype)

        @pl.when(used)
        def _():
            o_ref[rs, :] += _swiglu_chunk(x_ref[rs, :], wg_ref, wu_ref, wd_ref)


def _moe_ffn(xs, wg, wu, wd, tile_expert, tile_valid):
    n, d = xs.shape
    ff = wg.shape[2]
    tm, tf = MOE_TM, MOE_TF
    nf = ff // tf

    def fidx(i, f, tv):
        return jnp.where(tv[i] > 0, f, nf - 1)

    grid_spec = pltpu.PrefetchScalarGridSpec(
        num_scalar_prefetch=2,
        grid=(n // tm, nf),
        in_specs=[
            pl.BlockSpec((tm, d), lambda i, f, te, tv: (i, 0)),
            pl.BlockSpec((None, d, tf), lambda i, f, te, tv: (te[i], 0, fidx(i, f, tv))),
            pl.BlockSpec((None, d, tf), lambda i, f, te, tv: (te[i], 0, fidx(i, f, tv))),
            pl.BlockSpec((None, tf, d), lambda i, f, te, tv: (te[i], fidx(i, f, tv), 0)),
        ],
        out_specs=pl.BlockSpec((tm, d), lambda i, f, te, tv: (i, 0)),
    )
    return pl.pallas_call(
        _moe_ffn_kernel,
        grid_spec=grid_spec,
        out_shape=jax.ShapeDtypeStruct((n, d), F32),
        compiler_params=_cparams(("arbitrary", "arbitrary")),
        name="moe_ffn",
    )(tile_expert, tile_valid, xs, wg, wu, wd)


def _combine_kernel(p0_ref, p1_ref, p0_next_ref, p1_next_ref, ys_ref, route_ref, h_ref, mod_ref, fn_ref,
                    octx_ref, olat_ref, buf0, buf1, sem, *, ctx_steps):
    i = pl.program_id(0)
    n = octx_ref.shape[0]
    slot = i % 2

    def start(q0_ref, q1_ref, s):
        _start_row_gather(ys_ref, q0_ref, buf0.at[s], sem.at[s, 0], n)
        _start_row_gather(ys_ref, q1_ref, buf1.at[s], sem.at[s, 1], n)

    @pl.when(i == 0)
    def _():
        start(p0_ref, p1_ref, 0)

    @pl.when(i + 1 < pl.num_programs(0))
    def _():
        start(p0_next_ref, p1_next_ref, 1 - slot)

    _wait_row_gather(ys_ref, buf0.at[slot], sem.at[slot, 0], n)
    _wait_row_gather(ys_ref, buf1.at[slot], sem.at[slot, 1], n)
    w0 = route_ref[:, ROUTE_W:ROUTE_W + 1]
    w1 = route_ref[:, ROUTE_W + 1:ROUTE_W + 2]
    y = w0 * buf0[slot] + w1 * buf1[slot]
    h2 = h_ref[...] + mod_ref[5:6, :] * y
    out = _rms(h2, fn_ref[...])

    @pl.when(i < ctx_steps)
    def _():
        octx_ref[...] = out

    @pl.when(i >= ctx_steps)
    def _():
        olat_ref[...] = out


def _moe_combine(ys, pos0, pos1, route, h, mod, final_norm, cond_of_tile, rows_ctx):
    rows, d = h.shape
    tc = GATHER_ROWS
    sub = ROW_TILE // tc
    ctx_steps = rows_ctx // tc
    steps = rows // tc
    smem = lambda: pl.BlockSpec((None, 1, tc), lambda i: (i, 0, 0), memory_space=pltpu.SMEM)
    smem_next = lambda: pl.BlockSpec((None, 1, tc), lambda i: (jnp.minimum(i + 1, steps - 1), 0, 0),
                                     memory_space=pltpu.SMEM)
    pos0, pos1 = pos0.reshape(steps, 1, tc), pos1.reshape(steps, 1, tc)
    return pl.pallas_call(
        functools.partial(_combine_kernel, ctx_steps=ctx_steps),
        grid=(steps,),
        in_specs=[smem(), smem(), smem_next(), smem_next(),
                  pl.BlockSpec(memory_space=pl.ANY),
                  pl.BlockSpec((tc, LANES), lambda i: (i, 0)),
                  pl.BlockSpec((tc, d), lambda i: (i, 0)),
                  pl.BlockSpec((None, 6, d), lambda i: (cond_of_tile(i // sub), 0, 0)),
                  pl.BlockSpec((1, d), lambda i: (0, 0))],
        out_specs=[pl.BlockSpec((tc, d), lambda i: (jnp.minimum(i, ctx_steps - 1), 0)),
                   pl.BlockSpec((tc, d), lambda i: (jnp.maximum(i - ctx_steps, 0), 0))],
        out_shape=[jax.ShapeDtypeStruct((rows_ctx, d), F32), jax.ShapeDtypeStruct((rows - rows_ctx, d), F32)],
        scratch_shapes=[pltpu.VMEM((2, tc, d), F32), pltpu.VMEM((2, tc, d), F32),
                        pltpu.SemaphoreType.DMA((2, 2))],
        compiler_params=_cparams(("arbitrary",)),
        name="moe_combine",
    )(pos0, pos1, pos0, pos1, ys, route, h, mod, final_norm)


def _dispatch_plan(route, tm):
    rows = route.shape[0]
    ids = route[:, ROUTE_IDX:ROUTE_IDX + TOP_K].astype(jnp.int32)
    flat = ids.reshape(-1)
    onehot = (flat[:, None] == jnp.arange(N_EXPERTS)[None, :]).astype(jnp.int32)
    rank = jnp.sum((jnp.cumsum(onehot, axis=0) - onehot) * onehot, axis=1)
    counts = jnp.sum(onehot, axis=0)
    tiles = (counts + tm - 1) // tm
    tile_end = jnp.cumsum(tiles)
    start = (tile_end - tiles) * tm
    slot = start[flat] + rank
    n_tiles = (rows * TOP_K) // tm + N_EXPERTS
    n_slots = n_tiles * tm
    slot_token = (jnp.arange(n_slots, dtype=jnp.int32) % rows).at[slot].set(
        jnp.arange(rows * TOP_K, dtype=jnp.int32) // TOP_K)
    t = jnp.arange(n_tiles)
    tile_expert = jnp.minimum(jnp.sum((t[:, None] >= tile_end[None, :]).astype(jnp.int32), axis=1), N_EXPERTS - 1)
    tile_in_expert = t - (tile_end - tiles)[tile_expert]
    tile_rows = jnp.where(t < tile_end[-1], jnp.clip(counts[tile_expert] - tile_in_expert * tm, 0, tm), 0)
    last_expert = tile_expert[jnp.maximum(tile_end[-1] - 1, 0)]
    tile_expert = jnp.where(tile_rows > 0, tile_expert, last_expert).astype(jnp.int32)
    pos = slot.reshape(rows, TOP_K)
    return slot_token, tile_expert, tile_rows.astype(jnp.int32), pos[:, 0], pos[:, 1]


def _rope_table(seq, n):
    quarter = n // 4
    t = np.arange(seq)
    inv = jnp.power(ROPE_BASE, -jnp.arange(quarter, dtype=F32) * (2.0 / (n // 2)))
    ang_r = jnp.asarray(t // GRID_W, F32)[:, None] * inv[None, :]
    ang_c = jnp.asarray(t % GRID_W, F32)[:, None] * inv[None, :]
    zero = jnp.zeros((seq, quarter), F32)
    cos = jnp.concatenate([jnp.cos(ang_r)] * 2 + [jnp.cos(ang_c)] * 2, axis=1)
    up = jnp.concatenate([-jnp.sin(ang_r), zero, -jnp.sin(ang_c), zero], axis=1)
    dn = jnp.concatenate([zero, jnp.sin(ang_r), zero, jnp.sin(ang_c)], axis=1)
    return cos, up, dn


def _tile_cols(parts, reps, pad_to=None):
    out = [jnp.tile(p, (1, reps)) for p in parts]
    if pad_to is not None:
        out = [jnp.pad(p, ((0, 0), (0, pad_to - p.shape[1]))) for p in out]
    return jnp.concatenate(out, axis=1)


def kernel(x_prompt, x_sample, cache_mla_ckv, cache_mla_krope, cache_win_k, cache_win_v, cache_nat_k, cache_nat_v, c, c_ctx, w_mod, b_mod, norm1, norm2, w_in, mla_q_norm, mla_kv_norm, w_mla_q_up, w_mla_kv_up, win_sink, nat_rpb, w_br_mla, w_br_win, w_br_nat, w_out, w_ff_gate, w_ff_up, w_ff_down, w_router, w_ex_gate, w_ex_up, w_ex_down, final_norm):
    nb_ctx, seq_ctx, d = x_prompt.shape
    nb_lat, seq_lat, _ = x_sample.shape
    past = cache_mla_ckv.shape[2]
    rows_ctx = nb_ctx * seq_ctx
    rows_lat = nb_lat * seq_lat
    assert d == D_MODEL and seq_lat == ROW_TILE and rows_ctx % ROW_TILE == 0
    assert seq_lat // GRID_W == 4 * NAT_QROWS
    assert w_mod.shape[0] == DEPTH == 2
    ctx_tiles = rows_ctx // ROW_TILE

    def cond_of_tile(i):
        return jnp.where(i < ctx_tiles, 0, i - ctx_tiles + 1)

    n_cond = 16
    cond = jnp.zeros((n_cond, d), F32).at[0].set(c_ctx).at[1:1 + nb_lat].set(c)
    mod_all = _modulation(cond, w_mod, b_mod).reshape(DEPTH, n_cond, 6, d)

    n_in = w_in.shape[2]
    w_in_b = jnp.pad(w_in, ((0, 0), (0, 0), (0, -n_in % LANES))).astype(BF16)
    w_p, w_gate = _w_in_prep(w_in_b, n_in)
    wq_up = w_mla_q_up.reshape(DEPTH, MLA_Q_RANK, MLA_HEADS, MLA_NOPE + MLA_ROPE)
    wq_up = jnp.concatenate([wq_up[..., :MLA_NOPE].reshape(DEPTH, MLA_Q_RANK, -1),
                             wq_up[..., MLA_NOPE:].reshape(DEPTH, MLA_Q_RANK, -1)], axis=2).astype(BF16)
    wkv_up = w_mla_kv_up.astype(BF16)
    w_br = jnp.stack([w_br_mla, w_br_win, w_br_nat], axis=1).astype(BF16)
    w_out_b = w_out.astype(BF16)
    w_router_p = jnp.pad(w_router, ((0, 0), (0, 0), (0, LANES - N_EXPERTS))).astype(BF16)

    t128 = jnp.concatenate(_rope_table(seq_lat, 128), axis=1)
    t64 = _rope_table(seq_lat, MLA_ROPE)
    t_q = _tile_cols(t64, MLA_HEADS)
    t_k = _tile_cols(t64, 1, pad_to=LANES)

    ck_win = cache_win_k.reshape(nb_lat, DEPTH, past, 256)
    cv_win = cache_win_v.reshape(nb_lat, DEPTH, past, 256)
    ck_nat = cache_nat_k.reshape(nb_lat, DEPTH, past, 1024)
    cv_nat = cache_nat_v.reshape(nb_lat, DEPTH, past, 1024)


    h = (x_prompt.reshape(rows_ctx, d), x_sample.reshape(rows_lat, d))
    states = None
    for l in range(DEPTH):
        mod = mod_all[l]
        qn = mla_q_norm[l].reshape(1, -1)
        kvn = mla_kv_norm[l].reshape(1, -1)
        sink = win_sink[l].reshape(1, -1)
        p, hn = _mixer_in(h, mod, norm1[l].reshape(1, d), w_p, l, cond_of_tile)
        o_ctx, states = _ctx_attention(p, nb_ctx, seq_ctx, qn, kvn, wq_up[l], wkv_up[l], sink,
                                       prev_states=states)
        o_a = _lat_mla(p, rows_ctx // seq_lat, nb_lat, seq_lat, cache_mla_ckv, cache_mla_krope, l,
                       qn, kvn, wq_up[l], wkv_up[l], t_q, t_k)
        o_b = _lat_win(p, rows_ctx, nb_lat, seq_lat, ck_win, cv_win, l, t128, sink)
        o_c = _lat_nat(p, rows_ctx, nb_lat, seq_lat, ck_nat, cv_nat, l, _nat_bias_blocks(nat_rpb[l]))
        m = _merge(hn, o_ctx, o_a, o_b, o_c, w_gate, w_br, l)
        if l % 2 == 0:
            h1, hn2 = _out_proj(m, w_out_b, h, mod, norm2[l].reshape(1, d), l, cond_of_tile)
            h = _ffn(hn2, w_ff_gate[l // 2].astype(BF16), w_ff_up[l // 2].astype(BF16),
                     w_ff_down[l // 2].astype(BF16), h1, mod, cond_of_tile)
        else:
            h1, hn2, route = _out_proj(m, w_out_b, h, mod, norm2[l].reshape(1, d), l, cond_of_tile,
                                       w_router=w_router_p[l // 2])
            slot_token, tile_expert, tile_valid, pos0, pos1 = _dispatch_plan(route, MOE_TM)
            xs = _gather_rows(hn2, slot_token, tile_valid, BF16)
            ys = _moe_ffn(xs, w_ex_gate[l // 2], w_ex_up[l // 2], w_ex_down[l // 2], tile_expert, tile_valid)
            y_ctx, y_lat = _moe_combine(ys, pos0, pos1, route, h1, mod, final_norm.reshape(1, d),
                                        cond_of_tile, rows_ctx)

    y_prompt = y_ctx.reshape(nb_ctx, seq_ctx, d)
    y_sample = y_lat.reshape(nb_lat, seq_lat, d)
    return (y_prompt, y_sample) + tuple(states)
```

```python
import functools

import jax
import jax.numpy as jnp
import numpy as np
from jax import lax
from jax.experimental import pallas as pl
from jax.experimental.pallas import tpu as pltpu

F32 = jnp.float32
BF16 = jnp.bfloat16

D_MODEL = 2048
DEPTH = 2
GRID_W = 64
ROPE_BASE = 10000.0
EPS = 1e-6
NEG_INF = -1e30
MLA_HEADS = 8
MLA_Q_RANK = 512
MLA_KV_RANK = 512
MLA_NOPE = 128
MLA_ROPE = 64
MLA_V = 128
WIN_HEADS = 8
WIN_KV_HEADS = 2
WIN_GROUP = WIN_HEADS // WIN_KV_HEADS
WIN_HEAD_DIM = 128
WINDOW = 128
NAT_HEADS = 8
NAT_HEAD_DIM = 128
NAT_ROWS = 8
NAT_COLS = 16
BRANCH_W = 1024
D_FF = 5632
N_EXPERTS = 8
TOP_K = 2

P_WQ, P_NQ, P_NK, P_NV = 0, 1024, 2048, 3072
P_WK, P_WV = 4096, 4352
P_QD, P_KVD, P_KR = 4608, 5120, 5632
P_COLS = 5760
P_TN = 1152
LANES = 128

VMEM_LIMIT = 56 * 1024 * 1024

ROW_TILE = 1024
FFN_TM = 512
FFN_TF = 512
MOE_TM = 2048
MOE_SLAB = 512
MOE_TF = 256
GATHER_ROWS = 256
NAT_QROWS = 4
NAT_KROWS = 12


def _cparams(sem):
    return pltpu.CompilerParams(dimension_semantics=sem, vmem_limit_bytes=VMEM_LIMIT)


def _rms(x, g):
    ms = jnp.mean(x * x, axis=-1, keepdims=True)
    return x * lax.rsqrt(ms + EPS) * g


def _dot(a, b):
    return jnp.dot(a, b, preferred_element_type=F32)


def _dot_nt(a, b):
    return lax.dot_general(a, b, (((1,), (1,)), ((), ())), preferred_element_type=F32)


def _rope(x, tab_ref_or_val, shift):
    n = x.shape[-1]
    t = tab_ref_or_val
    c, s_up, s_dn = t[:, 0:n], t[:, n:2 * n], t[:, 2 * n:3 * n]
    up = pltpu.roll(x, n - shift, axis=1)
    dn = pltpu.roll(x, shift, axis=1)
    return x * c + up * s_up + dn * s_dn


def _attend_heads(n_heads, score, value, o_ref, col0, sink=None, group=None, emit=None):
    group = group or n_heads
    for h0 in range(0, n_heads, group):
        heads = range(h0, h0 + group)
        s = [score(h) for h in heads]
        m = [jnp.max(x, axis=-1, keepdims=True) for x in s]
        if sink is not None:
            sk = [sink(h) for h in heads]
            m = [jnp.maximum(a, b) for a, b in zip(m, sk)]
        e = [jnp.exp(x - a) for x, a in zip(s, m)]
        den = [jnp.sum(x, axis=-1, keepdims=True) for x in e]
        if sink is not None:
            den = [d + jnp.exp(b - a) for d, a, b in zip(den, m, sk)]
        pv = [_dot(x.astype(BF16), value(h)) for x, h in zip(e, heads)]
        for i, h in enumerate(heads):
            out = pv[i] * (1.0 / den[i])
            if emit is not None:
                emit(h, out)
            else:
                o_ref[:, col0 + h * LANES:col0 + (h + 1) * LANES] = out.astype(o_ref.dtype)


def _mod_kernel(c_ref, w_ref, b_ref, o_ref):
    c = c_ref[...]
    s = (c * jax.nn.sigmoid(c)).astype(BF16)
    o_ref[...] = _dot(s, w_ref[...].astype(BF16)) + b_ref[...]


def _modulation(cond, w_mod, b_mod):
    depth, d, n = w_mod.shape
    nc = cond.shape[0]
    tn = 1024
    return pl.pallas_call(
        _mod_kernel,
        grid=(depth, n // tn),
        in_specs=[
            pl.BlockSpec((nc, d), lambda l, j: (0, 0)),
            pl.BlockSpec((None, d, tn), lambda l, j: (l, 0, j)),
            pl.BlockSpec((None, 1, tn), lambda l, j: (l, 0, j)),
        ],
        out_specs=pl.BlockSpec((None, nc, tn), lambda l, j: (l, 0, j)),
        out_shape=jax.ShapeDtypeStruct((depth, nc, n), F32),
        compiler_params=_cparams(("arbitrary", "arbitrary")),
        name="modulation",
    )(cond, w_mod, b_mod.reshape(depth, 1, n))


_W_IN_SEGMENTS = ((1088, 1024), (2624, 1024), (3648, 1024), (4672, 1024), (2112, 256), (2368, 256),
                  (0, 512), (512, 512), (1024, 64))
W_IN_GATES = 5696


def _w_in_prep_kernel(w_ref, p_ref, g_ref):
    col = 0
    for src, width in _W_IN_SEGMENTS:
        p_ref[:, col:col + width] = w_ref[:, src:src + width].astype(BF16)
        col += width
    p_ref[:, col:] = jnp.zeros((p_ref.shape[0], P_COLS - col), BF16)
    g_ref[...] = w_ref[:, W_IN_GATES:W_IN_GATES + g_ref.shape[1]].astype(BF16)


def _w_in_prep(w_in, n_cols):
    depth, d, n = w_in.shape
    tr = 128
    n_gate = n_cols - W_IN_GATES
    return pl.pallas_call(
        _w_in_prep_kernel,
        grid=(depth, d // tr),
        in_specs=[pl.BlockSpec((None, tr, n), lambda l, r: (l, r, 0))],
        out_specs=[pl.BlockSpec((None, tr, P_COLS), lambda l, r: (l, r, 0)),
                   pl.BlockSpec((None, tr, n_gate), lambda l, r: (l, r, 0))],
        out_shape=[jax.ShapeDtypeStruct((depth, d, P_COLS), BF16),
                   jax.ShapeDtypeStruct((depth, d, n_gate), BF16)],
        compiler_params=_cparams(("arbitrary", "arbitrary")),
        name="w_in_prep",
    )(w_in)


def _split_rows_specs(h, tm):
    if not isinstance(h, tuple):
        return [h], [pl.BlockSpec((tm, h.shape[1]), lambda i, *_: (i, 0))], None
    ctx_tiles = h[0].shape[0] // tm
    d = h[0].shape[1]
    return (list(h),
            [pl.BlockSpec((tm, d), lambda i, *_: (jnp.minimum(i, ctx_tiles - 1), 0)),
             pl.BlockSpec((tm, d), lambda i, *_: (jnp.maximum(i - ctx_tiles, 0), 0))],
            ctx_tiles)


def _row_group_value(h_refs, ctx_tiles):
    if ctx_tiles is None:
        return h_refs[0][...]
    return jnp.where(pl.program_id(0) < ctx_tiles, h_refs[0][...], h_refs[1][...])


def _mixer_in_kernel(*refs, ctx_tiles):
    nh = 1 if ctx_tiles is None else 2
    h_refs = refs[:nh]
    mod_ref, g_ref, w_ref, p_ref, hn_ref = refs[nh:]

    @pl.when(pl.program_id(1) == 0)
    def _():
        h = _row_group_value(h_refs, ctx_tiles)
        hn = _rms(h, g_ref[...]) * (1.0 + mod_ref[1:2, :]) + mod_ref[0:1, :]
        hn_ref[...] = hn.astype(BF16)

    p_ref[...] = _dot(hn_ref[...], w_ref[...])


def _mixer_in(h, mod, g, w, layer, cond_of_tile):
    tm, tn = (ROW_TILE // 2 if isinstance(h, tuple) else ROW_TILE), P_TN
    sub = ROW_TILE // tm
    h_args, h_specs, ctx_tiles = _split_rows_specs(h, tm)
    rows = sum(a.shape[0] for a in h_args)
    d = h_args[0].shape[1]
    n = w.shape[2]
    return pl.pallas_call(
        functools.partial(_mixer_in_kernel, ctx_tiles=ctx_tiles),
        grid=(rows // tm, n // tn),
        in_specs=h_specs + [
            pl.BlockSpec((None, 6, d), lambda i, j: (cond_of_tile(i // sub), 0, 0)),
            pl.BlockSpec((1, d), lambda i, j: (0, 0)),
            pl.BlockSpec((None, d, tn), lambda i, j: (layer, 0, j)),
        ],
        out_specs=[
            pl.BlockSpec((tm, tn), lambda i, j: (i, j)),
            pl.BlockSpec((tm, d), lambda i, j: (i, 0)),
        ],
        out_shape=[
            jax.ShapeDtypeStruct((rows, n), F32),
            jax.ShapeDtypeStruct((rows, d), BF16),
        ],
        compiler_params=_cparams(("arbitrary", "arbitrary")),
        name="mixer_in",
    )(*h_args, mod, g, w)


_STATE_WIDTHS = (MLA_KV_RANK, MLA_ROPE, 256, 256, 1024, 1024)
_STATE_HEADS = (None, None, WIN_KV_HEADS, WIN_KV_HEADS, NAT_HEADS, NAT_HEADS)


def _ctx_attn_kernel(p_ref, qn_ref, kvn_ref, wq_ref, wkv_ref, sink_ref, *refs, stacked):
    n_state = len(_STATE_WIDTHS)
    if stacked:
        prev_refs, o_ref, state_refs = refs[:n_state], refs[n_state], refs[n_state + 1:]
    else:
        prev_refs, o_ref, state_refs = None, refs[0], refs[1:]

    def put_state(k, val, slot=DEPTH - 1):
        ref, heads = state_refs[k], _STATE_HEADS[k]
        if not stacked:
            ref[...] = val
        elif heads is None:
            ref[slot] = val
        else:
            for j in range(heads):
                ref[slot, :, j, :] = val[:, j * LANES:(j + 1) * LANES]

    if stacked:
        for k in range(n_state):
            put_state(k, prev_refs[k][...], slot=0)

    q = _dot(_rms(p_ref[:, P_QD:P_QD + MLA_Q_RANK], qn_ref[...]).astype(BF16), wq_ref[...])
    ckv = _rms(p_ref[:, P_KVD:P_KVD + MLA_KV_RANK], kvn_ref[...])
    put_state(0, ckv)
    kv = _dot(ckv.astype(BF16), wkv_ref[...]).astype(BF16)
    kr = p_ref[:, P_KR:P_KR + MLA_ROPE]
    put_state(1, kr)
    krb = kr.astype(BF16)
    scale_a = (MLA_NOPE + MLA_ROPE) ** -0.5
    nope_w = MLA_HEADS * MLA_NOPE

    def score_a(h):
        qn = q[:, h * MLA_NOPE:(h + 1) * MLA_NOPE].astype(BF16)
        qr = q[:, nope_w + h * MLA_ROPE:nope_w + (h + 1) * MLA_ROPE].astype(BF16)
        return (_dot_nt(qn, kv[:, h * 256:h * 256 + MLA_NOPE]) + _dot_nt(qr, krb)) * scale_a

    _attend_heads(MLA_HEADS, score_a, lambda h: kv[:, h * 256 + MLA_NOPE:(h + 1) * 256], o_ref, 0)
    put_state(2, p_ref[:, P_WK:P_WK + 256])
    put_state(3, p_ref[:, P_WV:P_WV + 256])
    scale_b = WIN_HEAD_DIM ** -0.5

    def score_b(h):
        kvh = h // WIN_GROUP
        qh = p_ref[:, P_WQ + h * 128:P_WQ + (h + 1) * 128].astype(BF16)
        return _dot_nt(qh, p_ref[:, P_WK + kvh * 128:P_WK + (kvh + 1) * 128].astype(BF16)) * scale_b

    def value_b(h):
        kvh = h // WIN_GROUP
        return p_ref[:, P_WV + kvh * 128:P_WV + (kvh + 1) * 128].astype(BF16)

    _attend_heads(WIN_HEADS, score_b, value_b, o_ref, BRANCH_W, lambda h: sink_ref[0:1, h:h + 1])
    put_state(4, p_ref[:, P_NK:P_NK + 1024])
    put_state(5, p_ref[:, P_NV:P_NV + 1024])
    scale_c = NAT_HEAD_DIM ** -0.5

    def score_c(h):
        qh = p_ref[:, P_NQ + h * 128:P_NQ + (h + 1) * 128].astype(BF16)
        return _dot_nt(qh, p_ref[:, P_NK + h * 128:P_NK + (h + 1) * 128].astype(BF16)) * scale_c

    _attend_heads(NAT_HEADS, score_c, lambda h: p_ref[:, P_NV + h * 128:P_NV + (h + 1) * 128].astype(BF16),
                  o_ref, 2 * BRANCH_W)


def _ctx_attention(p, nb, seq, qn, kvn, wq, wkv, sink, prev_states=None):
    full = lambda a: pl.BlockSpec(a.shape, lambda b: (0,) * a.ndim)
    row = lambda w: pl.BlockSpec((seq, w), lambda b: (b, 0))
    stacked = prev_states is not None
    in_specs = [pl.BlockSpec((seq, P_COLS), lambda b: (b, 0)),
                full(qn), full(kvn), full(wq), full(wkv), full(sink)]
    args = [p, qn, kvn, wq, wkv, sink]
    out_specs = [row(3 * BRANCH_W)]
    out_shape = [jax.ShapeDtypeStruct((nb * seq, 3 * BRANCH_W), BF16)]
    for w, heads in zip(_STATE_WIDTHS, _STATE_HEADS):
        if not stacked:
            out_specs.append(row(w))
            out_shape.append(jax.ShapeDtypeStruct((nb * seq, w), F32))
        else:
            tail = (w,) if heads is None else (heads, w // heads)
            out_specs.append(pl.BlockSpec((None, DEPTH, seq) + tail, lambda b, n=len(tail): (b,) + (0,) * (n + 2)))
            out_shape.append(jax.ShapeDtypeStruct((nb, DEPTH, seq) + tail, F32))
    if stacked:
        in_specs += [row(w) for w in _STATE_WIDTHS]
        args += list(prev_states)
    outs = pl.pallas_call(
        functools.partial(_ctx_attn_kernel, stacked=stacked),
        grid=(nb,),
        in_specs=in_specs, out_specs=out_specs, out_shape=out_shape,
        compiler_params=_cparams(("arbitrary",)),
        name="ctx_attention",
    )(*args)
    return outs[0], outs[1:]


MLA_QT = 512
MLA_KC = 256
MLA_DK = MLA_NOPE + MLA_ROPE


def _lat_mla_kernel(p_ref, cckv_ref, ckr_ref, qn_ref, kvn_ref, wq_ref, wkv_ref, tq_ref, tk_ref,
                    o_ref, k_scr, v_scr, *, past, seq):
    qt = pl.program_id(1)

    def put_keys(r0, n, kv, kr):
        krb = kr.astype(BF16)
        for h in range(MLA_HEADS):
            k_scr[h, r0:r0 + n, 0:MLA_NOPE] = kv[:, h * 256:h * 256 + MLA_NOPE].astype(BF16)
            k_scr[h, r0:r0 + n, MLA_NOPE:MLA_DK] = krb
            v_scr[r0:r0 + n, h * MLA_V:(h + 1) * MLA_V] = kv[:, h * 256 + MLA_NOPE:(h + 1) * 256].astype(BF16)

    @pl.when(qt == 0)
    def _():
        for c in range(past // MLA_KC):
            r0 = c * MLA_KC
            kv = _dot(cckv_ref[r0:r0 + MLA_KC, :].astype(BF16), wkv_ref[...])
            put_keys(r0, MLA_KC, kv, ckr_ref[r0:r0 + MLA_KC, :])
        for c in range(seq // MLA_KC):
            r0 = c * MLA_KC
            ckv = _rms(p_ref[r0:r0 + MLA_KC, MLA_Q_RANK:MLA_Q_RANK + MLA_KV_RANK], kvn_ref[...])
            kv = _dot(ckv.astype(BF16), wkv_ref[...])
            krp = p_ref[r0:r0 + MLA_KC, 2 * MLA_Q_RANK:2 * MLA_Q_RANK + LANES]
            kr = _rope(krp, tk_ref[r0:r0 + MLA_KC, :], MLA_ROPE // 4)
            put_keys(past + r0, MLA_KC, kv, kr[:, 0:MLA_ROPE])

    r0 = pl.multiple_of(qt * MLA_QT, MLA_QT)
    qd = p_ref[pl.ds(r0, MLA_QT), 0:MLA_Q_RANK]
    q = _dot(_rms(qd, qn_ref[...]).astype(BF16), wq_ref[...])
    nope_w = MLA_HEADS * MLA_NOPE
    q_rope = _rope(q[:, nope_w:], tq_ref[pl.ds(r0, MLA_QT), :], MLA_ROPE // 4)
    scale = MLA_DK ** -0.5

    def score(h):
        qh = jnp.concatenate([q[:, h * MLA_NOPE:(h + 1) * MLA_NOPE],
                              q_rope[:, h * MLA_ROPE:(h + 1) * MLA_ROPE]], axis=1).astype(BF16)
        return _dot_nt(qh, k_scr[h]) * scale

    _attend_heads(MLA_HEADS, score, lambda h: v_scr[:, h * MLA_V:(h + 1) * MLA_V], o_ref, 0, group=2)


def _lat_mla(p, row_blk0, nb, seq, cache_ckv, cache_kr, layer, qn, kvn, wq, wkv, tq, tk):
    past = cache_ckv.shape[2]
    full = lambda a: pl.BlockSpec(a.shape, lambda b, t: (0,) * a.ndim)
    nqt = seq // MLA_QT
    return pl.pallas_call(
        functools.partial(_lat_mla_kernel, past=past, seq=seq),
        grid=(nb, nqt),
        in_specs=[
            pl.BlockSpec((seq, P_TN), lambda b, t: (row_blk0 + b, P_QD // P_TN)),
            pl.BlockSpec((None, None, past, MLA_KV_RANK), lambda b, t: (b, layer, 0, 0)),
            pl.BlockSpec((None, None, past, MLA_ROPE), lambda b, t: (b, layer, 0, 0)),
            full(qn), full(kvn), full(wq), full(wkv), full(tq), full(tk),
        ],
        out_specs=pl.BlockSpec((MLA_QT, BRANCH_W), lambda b, t: (b * nqt + t, 0)),
        out_shape=jax.ShapeDtypeStruct((nb * seq, BRANCH_W), BF16),
        scratch_shapes=[pltpu.VMEM((MLA_HEADS, past + seq, MLA_DK), BF16),
                        pltpu.VMEM((past + seq, MLA_HEADS * MLA_V), BF16)],
        compiler_params=_cparams(("arbitrary", "arbitrary")),
        name="lat_mla",
    )(p, cache_ckv, cache_kr, qn, kvn, wq, wkv, tq, tk)


WIN_QB = 128


def _lat_win_kernel(q_ref, k0_ref, k1_ref, k2_ref, v0_ref, v1_ref, v2_ref, ck_ref, cv_ref,
                    t0_ref, t1_ref, t2_ref, sink_ref, o_ref, *, past, seq):
    qb = pl.program_id(1)
    scale = WIN_HEAD_DIM ** -0.5
    n_loc = 3 * WIN_QB
    rows = WIN_GROUP * WIN_QB
    qpos = qb * WIN_QB + lax.broadcasted_iota(jnp.int32, (rows, n_loc), 0) % WIN_QB
    kpos = (qb - 1) * WIN_QB + lax.broadcasted_iota(jnp.int32, (rows, n_loc), 1)
    valid = (kpos >= 0) & (kpos < seq) & (jnp.abs(qpos - kpos) <= WINDOW)
    t1 = t1_ref[...]
    q_heads = lambda kvh: range(kvh * WIN_GROUP, (kvh + 1) * WIN_GROUP)

    def score(kvh):
        cs = slice(kvh * 128, (kvh + 1) * 128)
        keys = jnp.concatenate([
            ck_ref[:, cs],
            _rope(k0_ref[:, cs], t0_ref[...], 32),
            _rope(k1_ref[:, cs], t1, 32),
            _rope(k2_ref[:, cs], t2_ref[...], 32)], axis=0).astype(BF16)
        q = jnp.concatenate([_rope(q_ref[:, h * 128:(h + 1) * 128], t1, 32) for h in q_heads(kvh)],
                            axis=0).astype(BF16)
        s = _dot_nt(q, keys) * scale
        return jnp.concatenate([s[:, :past], jnp.where(valid, s[:, past:], NEG_INF)], axis=1)

    def value(kvh):
        cs = slice(kvh * 128, (kvh + 1) * 128)
        return jnp.concatenate([cv_ref[:, cs], v0_ref[:, cs], v1_ref[:, cs], v2_ref[:, cs]], axis=0).astype(BF16)

    def sink(kvh):
        return jnp.concatenate([jnp.broadcast_to(sink_ref[0:1, h:h + 1], (WIN_QB, 1)) for h in q_heads(kvh)],
                               axis=0)

    def emit(kvh, out):
        for g, h in enumerate(q_heads(kvh)):
            o_ref[:, h * 128:(h + 1) * 128] = out[g * WIN_QB:(g + 1) * WIN_QB, :].astype(BF16)

    _attend_heads(WIN_KV_HEADS, score, value, o_ref, 0, sink=sink, emit=emit)


def _lat_win(p, row0, nb, seq, cache_k, cache_v, layer, tab, sink):
    past = cache_k.shape[2]
    nqb = seq // WIN_QB
    rb0 = row0 // WIN_QB

    def kblk(off, col_blk):
        return pl.BlockSpec(
            (WIN_QB, 256),
            lambda b, t: (rb0 + b * nqb + jnp.clip(t + off, 0, nqb - 1), col_blk))

    def tblk(off):
        return pl.BlockSpec((WIN_QB, 3 * 128), lambda b, t: (jnp.clip(t + off, 0, nqb - 1), 0))

    cache = pl.BlockSpec((None, None, past, 256), lambda b, t: (b, layer, 0, 0))
    return pl.pallas_call(
        functools.partial(_lat_win_kernel, past=past, seq=seq),
        grid=(nb, nqb),
        in_specs=[
            pl.BlockSpec((WIN_QB, 1024), lambda b, t: (rb0 + b * nqb + t, P_WQ // 1024)),
            kblk(-1, P_WK // 256), kblk(0, P_WK // 256), kblk(1, P_WK // 256),
            kblk(-1, P_WV // 256), kblk(0, P_WV // 256), kblk(1, P_WV // 256),
            cache, cache,
            tblk(-1), tblk(0), tblk(1),
            pl.BlockSpec(sink.shape, lambda b, t: (0, 0)),
        ],
        out_specs=pl.BlockSpec((WIN_QB, BRANCH_W), lambda b, t: (b * nqb + t, 0)),
        out_shape=jax.ShapeDtypeStruct((nb * seq, BRANCH_W), BF16),
        compiler_params=_cparams(("arbitrary", "arbitrary")),
        name="lat_window",
    )(p, p, p, p, p, p, p, cache_k, cache_v, tab, tab, tab, sink)


def _lat_nat_kernel(q_ref, k0_ref, k1_ref, k2_ref, v0_ref, v1_ref, v2_ref, ck_ref, cv_ref, bias_ref, o_ref,
                    *, grid_rows):
    g = pl.program_id(0)
    scale = NAT_HEAD_DIM ** -0.5
    past = ck_ref.shape[0]
    win_rows = min(NAT_ROWS, grid_rows)
    lane = lax.broadcasted_iota(jnp.int32, (GRID_W, 2 * GRID_W), 1)

    def rows_of(h, refs):
        cs = slice(h * 128, (h + 1) * 128)
        return jnp.concatenate([r[:, cs] for r in refs], axis=0).astype(BF16)

    def pair_plan(rq, rkp):
        r = NAT_QROWS * g + rq
        kr0 = NAT_QROWS * _nat_key_start(g) + 2 * rkp
        rs = jnp.clip(r - win_rows // 2, 0, grid_rows - win_rows)
        in_win = lambda kr: ((kr >= rs) & (kr < rs + win_rows)).astype(jnp.int32)
        ok = jnp.where(lane < GRID_W, in_win(kr0), in_win(kr0 + 1)) > 0
        return jnp.clip(kr0 - r + NAT_ROWS - 1, 0, 2 * NAT_ROWS - 1), ok

    plans = [[pair_plan(rq, rkp) for rkp in range(NAT_KROWS // 2)] for rq in range(NAT_QROWS)]

    def bias(h):
        return jnp.concatenate(
            [jnp.concatenate([jnp.where(ok, bias_ref[h, a], NEG_INF) for a, ok in row], axis=1) for row in plans],
            axis=0)

    def score(h):
        q = q_ref[:, h * 128:(h + 1) * 128].astype(BF16)
        s = _dot_nt(q, rows_of(h, (ck_ref, k0_ref, k1_ref, k2_ref))) * scale
        return jnp.concatenate([s[:, :past], s[:, past:] + bias(h)], axis=1)

    _attend_heads(NAT_HEADS, score, lambda h: rows_of(h, (cv_ref, v0_ref, v1_ref, v2_ref)), o_ref, 0, group=2)


def _nat_key_start(g):
    return g // 2


def _lat_nat(p, row0, nb, seq, cache_k, cache_v, layer, bias):
    past = cache_k.shape[2]
    qrows = NAT_QROWS * GRID_W
    ng = seq // qrows
    rb0 = row0 // qrows

    def kblk(off, col_blk):
        return pl.BlockSpec((qrows, 1024), lambda g, b: (rb0 + b * ng + _nat_key_start(g) + off, col_blk))

    cache = pl.BlockSpec((None, None, past, 1024), lambda g, b: (b, layer, 0, 0))
    return pl.pallas_call(
        functools.partial(_lat_nat_kernel, grid_rows=seq // GRID_W),
        grid=(ng, nb),
        in_specs=[
            pl.BlockSpec((qrows, 1024), lambda g, b: (rb0 + b * ng + g, P_NQ // 1024)),
            kblk(0, P_NK // 1024), kblk(1, P_NK // 1024), kblk(2, P_NK // 1024),
            kblk(0, P_NV // 1024), kblk(1, P_NV // 1024), kblk(2, P_NV // 1024),
            cache, cache,
            pl.BlockSpec(bias.shape, lambda g, b: (0, 0, 0, 0)),
        ],
        out_specs=pl.BlockSpec((qrows, BRANCH_W), lambda g, b: (b * ng + g, 0)),
        out_shape=jax.ShapeDtypeStruct((nb * seq, BRANCH_W), BF16),
        compiler_params=_cparams(("arbitrary", "arbitrary")),
        name="lat_neighbourhood",
    )(p, p, p, p, p, p, p, cache_k, cache_v, bias)


def _nat_bias_blocks(rpb):
    n_dr, n_dc = 2 * NAT_ROWS - 1, 2 * NAT_COLS - 1
    cc = np.arange(GRID_W)
    dc = cc[None, :] - cc[:, None] + NAT_COLS - 1
    cs = np.clip(cc - NAT_COLS // 2, 0, GRID_W - NAT_COLS)[:, None]
    col_ok = (cc[None, :] >= cs) & (cc[None, :] < cs + NAT_COLS)
    onehot = ((dc[None] == np.arange(n_dc)[:, None, None]) & col_ok[None]).astype(np.float32)
    toep = jnp.einsum('had,dck->hack', rpb.astype(F32), jnp.asarray(onehot), precision=lax.Precision.HIGHEST)
    toep = jnp.where(jnp.asarray(col_ok), toep, NEG_INF)
    fill = jnp.full((NAT_HEADS, 2 * NAT_ROWS + 1 - n_dr, GRID_W, GRID_W), NEG_INF, F32)
    toep = jnp.concatenate([toep, fill], axis=1)
    return jnp.concatenate([toep[:, :-1], toep[:, 1:]], axis=-1)


def _merge_kernel(hn_ref, octx_ref, oa_ref, ob_ref, oc_ref, wga_ref, wgb_ref, wgc_ref,
                  wba_ref, wbb_ref, wbc_ref, m_ref, *, ctx_tiles):
    i = pl.program_id(0)
    weights = ((wga_ref, wba_ref), (wgb_ref, wbb_ref), (wgc_ref, wbc_ref))

    def merged(branch):
        hn = hn_ref[...]
        acc = None
        for k, (wg, wb) in enumerate(weights):
            term = jax.nn.sigmoid(_dot(hn, wg[...])) * _dot(branch(k), wb[...])
            acc = term if acc is None else acc + term
        m_ref[...] = acc.astype(BF16)

    @pl.when(i < ctx_tiles)
    def _():
        merged(lambda k: octx_ref[:, k * BRANCH_W:(k + 1) * BRANCH_W])

    @pl.when(i >= ctx_tiles)
    def _():
        lat = (oa_ref, ob_ref, oc_ref)
        merged(lambda k: lat[k][...])


def _merge(hn, o_ctx, o_a, o_b, o_c, w_gate, w_br, layer):
    rows, d = hn.shape
    tm, tn = ROW_TILE, 256
    nj = d // tn
    ctx_tiles = o_ctx.shape[0] // tm
    gate = lambda k: pl.BlockSpec((None, d, tn), lambda i, j: (layer, 0, k * nj + j))
    br = lambda k: pl.BlockSpec((None, None, BRANCH_W, tn), lambda i, j: (layer, k, 0, j))
    lat = pl.BlockSpec((tm, BRANCH_W), lambda i, j: (jnp.maximum(i - ctx_tiles, 0), 0))
    return pl.pallas_call(
        functools.partial(_merge_kernel, ctx_tiles=ctx_tiles),
        grid=(rows // tm, nj),
        in_specs=[pl.BlockSpec((tm, d), lambda i, j: (i, 0)),
                  pl.BlockSpec((tm, 3 * BRANCH_W), lambda i, j: (jnp.minimum(i, ctx_tiles - 1), 0)),
                  lat, lat, lat,
                  gate(0), gate(1), gate(2), br(0), br(1), br(2)],
        out_specs=pl.BlockSpec((tm, tn), lambda i, j: (i, j)),
        out_shape=jax.ShapeDtypeStruct((rows, d), BF16),
        compiler_params=_cparams(("arbitrary", "arbitrary")),
        name="merge",
    )(hn, o_ctx, o_a, o_b, o_c, w_gate, w_gate, w_gate, w_br, w_br, w_br)


ROUTE_IDX = 8
ROUTE_W = 10


def _out_proj_kernel(m_ref, w_ref, mod_ref, g_ref, *rest, routed, ctx_tiles):
    nh = 1 if ctx_tiles is None else 2
    h_refs, rest = rest[:nh], rest[nh:]
    if routed:
        wr_ref, h1_ref, hn2_ref, route_ref = rest
    else:
        h1_ref, hn2_ref = rest
    h1 = _row_group_value(h_refs, ctx_tiles) + mod_ref[2:3, :] * _dot(m_ref[...], w_ref[...])
    h1_ref[...] = h1
    hn2 = _rms(h1, g_ref[...]) * (1.0 + mod_ref[4:5, :]) + mod_ref[3:4, :]
    hn2_ref[...] = hn2.astype(hn2_ref.dtype)
    if routed:
        logits = _dot(hn2.astype(BF16), wr_ref[...])
        lane = lax.broadcasted_iota(jnp.int32, logits.shape, 1).astype(F32)
        lg = jnp.where(lane < N_EXPERTS, logits, -jnp.inf)
        m1 = jnp.max(lg, axis=-1, keepdims=True)
        i1 = jnp.min(jnp.where(lg == m1, lane, float(LANES)), axis=-1, keepdims=True)
        lg2 = jnp.where(lane == i1, -jnp.inf, lg)
        m2 = jnp.max(lg2, axis=-1, keepdims=True)
        i2 = jnp.min(jnp.where(lg2 == m2, lane, float(LANES)), axis=-1, keepdims=True)
        e2 = jnp.exp(m2 - m1)
        w1 = 1.0 / (1.0 + e2)
        w2 = e2 / (1.0 + e2)
        route = (jnp.where(lane == ROUTE_IDX, i1, 0.0)
                 + jnp.where(lane == ROUTE_IDX + 1, i2, 0.0)
                 + jnp.where(lane == ROUTE_W, w1, 0.0)
                 + jnp.where(lane == ROUTE_W + 1, w2, 0.0))
        route_ref[...] = route


def _out_proj(m, w_out, h, mod, g, layer, cond_of_tile, w_router=None):
    rows, d = m.shape
    tm = 512
    sub = ROW_TILE // tm
    routed = w_router is not None
    h_args, h_specs, ctx_tiles = _split_rows_specs(h, tm)
    in_specs = [
        pl.BlockSpec((tm, d), lambda i: (i, 0)),
        pl.BlockSpec((None, d, d), lambda i: (layer, 0, 0)),
        pl.BlockSpec((None, 6, d), lambda i: (cond_of_tile(i // sub), 0, 0)),
        pl.BlockSpec((1, d), lambda i: (0, 0)),
    ] + h_specs
    out_specs = [pl.BlockSpec((tm, d), lambda i: (i, 0)), pl.BlockSpec((tm, d), lambda i: (i, 0))]
    out_shape = [jax.ShapeDtypeStruct((rows, d), F32), jax.ShapeDtypeStruct((rows, d), BF16)]
    args = [m, w_out, mod, g] + h_args
    if routed:
        out_shape[1] = jax.ShapeDtypeStruct((rows, d), F32)
        in_specs.append(pl.BlockSpec((d, LANES), lambda i: (0, 0)))
        out_specs.append(pl.BlockSpec((tm, LANES), lambda i: (i, 0)))
        out_shape.append(jax.ShapeDtypeStruct((rows, LANES), F32))
        args.append(w_router)
    return pl.pallas_call(
        functools.partial(_out_proj_kernel, routed=routed, ctx_tiles=ctx_tiles),
        grid=(rows // tm,),
        in_specs=in_specs, out_specs=out_specs, out_shape=out_shape,
        compiler_params=_cparams(("arbitrary",)),
        name="out_proj",
    )(*args)


def _swiglu_chunk(x, wg_ref, wu_ref, wd_ref):
    g = _dot(x, wg_ref[...].astype(BF16))
    u = _dot(x, wu_ref[...].astype(BF16))
    a = (g * jax.nn.sigmoid(g) * u).astype(BF16)
    return _dot(a, wd_ref[...].astype(BF16))


def _ffn_kernel(x_ref, wg_ref, wu_ref, wd_ref, h_ref, mod_ref, o_ref):
    f = pl.program_id(1)

    @pl.when(f == 0)
    def _():
        o_ref[...] = jnp.zeros_like(o_ref)

    o_ref[...] += _swiglu_chunk(x_ref[...], wg_ref, wu_ref, wd_ref)

    @pl.when(f == pl.num_programs(1) - 1)
    def _():
        o_ref[...] = h_ref[...] + mod_ref[5:6, :] * o_ref[...]


def _ffn(x, wg, wu, wd, h, mod, cond_of_tile):
    rows, d = h.shape
    ff = wg.shape[1]
    tm, tf = FFN_TM, FFN_TF
    sub = ROW_TILE // tm
    return pl.pallas_call(
        _ffn_kernel,
        grid=(rows // tm, ff // tf),
        in_specs=[
            pl.BlockSpec((tm, d), lambda i, f: (i, 0)),
            pl.BlockSpec((d, tf), lambda i, f: (0, f)),
            pl.BlockSpec((d, tf), lambda i, f: (0, f)),
            pl.BlockSpec((tf, d), lambda i, f: (f, 0)),
            pl.BlockSpec((tm, d), lambda i, f: (i, 0)),
            pl.BlockSpec((None, 6, d), lambda i, f: (cond_of_tile(i // sub), 0, 0)),
        ],
        out_specs=pl.BlockSpec((tm, d), lambda i, f: (i, 0)),
        out_shape=jax.ShapeDtypeStruct((rows, d), F32),
        compiler_params=_cparams(("arbitrary", "arbitrary")),
        name="ffn_dense",
    )(x, wg, wu, wd, h, mod)


def _start_row_gather(src_ref, idx_ref, dst_ref, sem, n):
    def issue(r, carry):
        pltpu.make_async_copy(src_ref.at[pl.ds(idx_ref[0, r], 1), :], dst_ref.at[pl.ds(r, 1), :], sem).start()
        return carry

    lax.fori_loop(0, n, issue, 0, unroll=8)


def _wait_row_gather(src_ref, dst_ref, sem, n):
    pltpu.make_async_copy(src_ref.at[pl.ds(0, n), :], dst_ref, sem).wait()


def _gather_kernel(tr_ref, idx_ref, idx_next_ref, src_ref, o_ref, buf, sem):
    i = pl.program_id(0)
    n = o_ref.shape[0]
    per_tile = MOE_TM // n
    slot = i % 2

    def used(step):
        return tr_ref[step // per_tile] > (step % per_tile) * n

    @pl.when((i == 0) & used(0))
    def _():
        _start_row_gather(src_ref, idx_ref, buf.at[0], sem.at[0], n)

    nxt = jnp.minimum(i + 1, pl.num_programs(0) - 1)

    @pl.when((i + 1 < pl.num_programs(0)) & used(nxt))
    def _():
        _start_row_gather(src_ref, idx_next_ref, buf.at[1 - slot], sem.at[1 - slot], n)

    @pl.when(used(i))
    def _():
        _wait_row_gather(src_ref, buf.at[slot], sem.at[slot], n)
        o_ref[...] = buf[slot].astype(o_ref.dtype)

    @pl.when(jnp.logical_not(used(i)))
    def _():
        o_ref[...] = jnp.zeros_like(o_ref)


def _gather_rows(src, idx, tile_rows, out_dtype):
    n = idx.shape[0]
    d = src.shape[1]
    tg = GATHER_ROWS
    steps = n // tg
    idx3 = idx.reshape(steps, 1, tg)
    grid_spec = pltpu.PrefetchScalarGridSpec(
        num_scalar_prefetch=1,
        grid=(steps,),
        in_specs=[pl.BlockSpec((None, 1, tg), lambda i, tr: (i, 0, 0), memory_space=pltpu.SMEM),
                  pl.BlockSpec((None, 1, tg), lambda i, tr: (jnp.minimum(i + 1, steps - 1), 0, 0),
                               memory_space=pltpu.SMEM),
                  pl.BlockSpec(memory_space=pl.ANY)],
        out_specs=pl.BlockSpec((tg, d), lambda i, tr: (i, 0)),
        scratch_shapes=[pltpu.VMEM((2, tg, d), src.dtype), pltpu.SemaphoreType.DMA((2,))],
    )
    return pl.pallas_call(
        _gather_kernel,
        grid_spec=grid_spec,
        out_shape=jax.ShapeDtypeStruct((n, d), out_dtype),
        compiler_params=_cparams(("arbitrary",)),
        name="moe_gather",
    )(tile_rows, idx3, idx3, src)


def _moe_ffn_kernel(te_ref, tv_ref, x_ref, wg_ref, wu_ref, wd_ref, o_ref, acc_ref, sem):
    i = pl.program_id(0)
    f = pl.program_id(1)
    last_f = pl.num_programs(1) - 1
    rows_used = tv_ref[i]

    def writeback(tile, s):
        rows = pl.ds(pl.multiple_of(tile * MOE_TM + s * MOE_SLAB, MOE_SLAB), MOE_SLAB)
        return pltpu.make_async_copy(acc_ref.at[pl.ds(s * MOE_SLAB, MOE_SLAB), :], o_ref.at[rows, :], sem.at[s])

    for s in range(MOE_TM // MOE_SLAB):
        rs = slice(s * MOE_SLAB, (s + 1) * MOE_SLAB)
        used = rows_used > s * MOE_SLAB

        @pl.when((f == 0) & (i > 0))
        def _():
            writeback(i - 1, s).wait()

        @pl.when(f == 0)
        def _():
            acc_ref[rs, :] = jnp.zeros((MOE_SLAB, acc_ref.shape[1]), acc_ref.dtype)

        @pl.when(used)
        def _():
            acc_ref[rs, :] += _swiglu_chunk(x_ref[rs, :], wg_ref, wu_ref, wd_ref)

        @pl.when(f == last_f)
        def _():
            writeback(i, s).start()

    @pl.when((f == last_f) & (i == pl.num_programs(0) - 1))
    def _():
        for s in range(MOE_TM // MOE_SLAB):
            writeback(i, s).wait()


def _moe_ffn(xs, wg, wu, wd, tile_expert, tile_valid):
    n, d = xs.shape
    ff = wg.shape[2]
    tm, tf = MOE_TM, MOE_TF
    nf = ff // tf

    def fidx(i, f, tv):
        return jnp.where(tv[i] > 0, f, nf - 1)

    grid_spec = pltpu.PrefetchScalarGridSpec(
        num_scalar_prefetch=2,
        grid=(n // tm, nf),
        in_specs=[
            pl.BlockSpec((tm, d), lambda i, f, te, tv: (i, 0)),
            pl.BlockSpec((None, d, tf), lambda i, f, te, tv: (te[i], 0, fidx(i, f, tv))),
            pl.BlockSpec((None, d, tf), lambda i, f, te, tv: (te[i], 0, fidx(i, f, tv))),
            pl.BlockSpec((None, tf, d), lambda i, f, te, tv: (te[i], fidx(i, f, tv), 0)),
        ],
        out_specs=pl.BlockSpec(memory_space=pl.ANY),
        scratch_shapes=[pltpu.VMEM((tm, d), F32), pltpu.SemaphoreType.DMA((tm // MOE_SLAB,))],
    )
    return pl.pallas_call(
        _moe_ffn_kernel,
        grid_spec=grid_spec,
        out_shape=jax.ShapeDtypeStruct((n, d), F32),
        compiler_params=_cparams(("arbitrary", "arbitrary")),
        name="moe_ffn",
    )(tile_expert, tile_valid, xs, wg, wu, wd)


def _combine_kernel(p0_ref, p1_ref, p0_next_ref, p1_next_ref, ys_ref, route_ref, h_ref, mod_ref, fn_ref,
                    octx_ref, olat_ref, buf0, buf1, sem, *, ctx_steps):
    i = pl.program_id(0)
    n = octx_ref.shape[0]
    slot = i % 2

    def start(q0_ref, q1_ref, s):
        _start_row_gather(ys_ref, q0_ref, buf0.at[s], sem.at[s, 0], n)
        _start_row_gather(ys_ref, q1_ref, buf1.at[s], sem.at[s, 1], n)

    @pl.when(i == 0)
    def _():
        start(p0_ref, p1_ref, 0)

    @pl.when(i + 1 < pl.num_programs(0))
    def _():
        start(p0_next_ref, p1_next_ref, 1 - slot)

    _wait_row_gather(ys_ref, buf0.at[slot], sem.at[slot, 0], n)
    _wait_row_gather(ys_ref, buf1.at[slot], sem.at[slot, 1], n)
    w0 = route_ref[:, ROUTE_W:ROUTE_W + 1]
    w1 = route_ref[:, ROUTE_W + 1:ROUTE_W + 2]
    y = w0 * buf0[slot] + w1 * buf1[slot]
    h2 = h_ref[...] + mod_ref[5:6, :] * y
    out = _rms(h2, fn_ref[...])

    @pl.when(i < ctx_steps)
    def _():
        octx_ref[...] = out

    @pl.when(i >= ctx_steps)
    def _():
        olat_ref[...] = out


def _moe_combine(ys, pos0, pos1, route, h, mod, final_norm, cond_of_tile, rows_ctx):
    rows, d = h.shape
    tc = GATHER_ROWS
    sub = ROW_TILE // tc
    ctx_steps = rows_ctx // tc
    steps = rows // tc
    smem = lambda: pl.BlockSpec((None, 1, tc), lambda i: (i, 0, 0), memory_space=pltpu.SMEM)
    smem_next = lambda: pl.BlockSpec((None, 1, tc), lambda i: (jnp.minimum(i + 1, steps - 1), 0, 0),
                                     memory_space=pltpu.SMEM)
    pos0, pos1 = pos0.reshape(steps, 1, tc), pos1.reshape(steps, 1, tc)
    return pl.pallas_call(
        functools.partial(_combine_kernel, ctx_steps=ctx_steps),
        grid=(steps,),
        in_specs=[smem(), smem(), smem_next(), smem_next(),
                  pl.BlockSpec(memory_space=pl.ANY),
                  pl.BlockSpec((tc, LANES), lambda i: (i, 0)),
                  pl.BlockSpec((tc, d), lambda i: (i, 0)),
                  pl.BlockSpec((None, 6, d), lambda i: (cond_of_tile(i // sub), 0, 0)),
                  pl.BlockSpec((1, d), lambda i: (0, 0))],
        out_specs=[pl.BlockSpec((tc, d), lambda i: (jnp.minimum(i, ctx_steps - 1), 0)),
                   pl.BlockSpec((tc, d), lambda i: (jnp.maximum(i - ctx_steps, 0), 0))],
        out_shape=[jax.ShapeDtypeStruct((rows_ctx, d), F32), jax.ShapeDtypeStruct((rows - rows_ctx, d), F32)],
        scratch_shapes=[pltpu.VMEM((2, tc, d), F32), pltpu.VMEM((2, tc, d), F32),
                        pltpu.SemaphoreType.DMA((2, 2))],
        compiler_params=_cparams(("arbitrary",)),
        name="moe_combine",
    )(pos0, pos1, pos0, pos1, ys, route, h, mod, final_norm)


def _dispatch_plan(route, tm):
    rows = route.shape[0]
    ids = route[:, ROUTE_IDX:ROUTE_IDX + TOP_K].astype(jnp.int32)
    flat = ids.reshape(-1)
    onehot = (flat[:, None] == jnp.arange(N_EXPERTS)[None, :]).astype(jnp.int32)
    rank = jnp.sum((jnp.cumsum(onehot, axis=0) - onehot) * onehot, axis=1)
    counts = jnp.sum(onehot, axis=0)
    tiles = (counts + tm - 1) // tm
    tile_end = jnp.cumsum(tiles)
    start = (tile_end - tiles) * tm
    slot = start[flat] + rank
    n_tiles = (rows * TOP_K) // tm + N_EXPERTS
    n_slots = n_tiles * tm
    slot_token = (jnp.arange(n_slots, dtype=jnp.int32) % rows).at[slot].set(
        jnp.arange(rows * TOP_K, dtype=jnp.int32) // TOP_K)
    t = jnp.arange(n_tiles)
    tile_expert = jnp.minimum(jnp.sum((t[:, None] >= tile_end[None, :]).astype(jnp.int32), axis=1), N_EXPERTS - 1)
    tile_in_expert = t - (tile_end - tiles)[tile_expert]
    tile_rows = jnp.where(t < tile_end[-1], jnp.clip(counts[tile_expert] - tile_in_expert * tm, 0, tm), 0)
    last_expert = tile_expert[jnp.maximum(tile_end[-1] - 1, 0)]
    tile_expert = jnp.where(tile_rows > 0, tile_expert, last_expert).astype(jnp.int32)
    pos = slot.reshape(rows, TOP_K)
    return slot_token, tile_expert, tile_rows.astype(jnp.int32), pos[:, 0], pos[:, 1]


def _rope_table(seq, n):
    quarter = n // 4
    t = np.arange(seq)
    inv = jnp.power(ROPE_BASE, -jnp.arange(quarter, dtype=F32) * (2.0 / (n // 2)))
    ang_r = jnp.asarray(t // GRID_W, F32)[:, None] * inv[None, :]
    ang_c = jnp.asarray(t % GRID_W, F32)[:, None] * inv[None, :]
    zero = jnp.zeros((seq, quarter), F32)
    cos = jnp.concatenate([jnp.cos(ang_r)] * 2 + [jnp.cos(ang_c)] * 2, axis=1)
    up = jnp.concatenate([-jnp.sin(ang_r), zero, -jnp.sin(ang_c), zero], axis=1)
    dn = jnp.concatenate([zero, jnp.sin(ang_r), zero, jnp.sin(ang_c)], axis=1)
    return cos, up, dn


def _tile_cols(parts, reps, pad_to=None):
    out = [jnp.tile(p, (1, reps)) for p in parts]
    if pad_to is not None:
        out = [jnp.pad(p, ((0, 0), (0, pad_to - p.shape[1]))) for p in out]
    return jnp.concatenate(out, axis=1)


def kernel(x_prompt, x_sample, cache_mla_ckv, cache_mla_krope, cache_win_k, cache_win_v, cache_nat_k, cache_nat_v, c, c_ctx, w_mod, b_mod, norm1, norm2, w_in, mla_q_norm, mla_kv_norm, w_mla_q_up, w_mla_kv_up, win_sink, nat_rpb, w_br_mla, w_br_win, w_br_nat, w_out, w_ff_gate, w_ff_up, w_ff_down, w_router, w_ex_gate, w_ex_up, w_ex_down, final_norm):
    nb_ctx, seq_ctx, d = x_prompt.shape
    nb_lat, seq_lat, _ = x_sample.shape
    past = cache_mla_ckv.shape[2]
    rows_ctx = nb_ctx * seq_ctx
    rows_lat = nb_lat * seq_lat
    assert d == D_MODEL and seq_lat == ROW_TILE and rows_ctx % ROW_TILE == 0
    assert seq_lat // GRID_W == 4 * NAT_QROWS
    assert w_mod.shape[0] == DEPTH == 2
    ctx_tiles = rows_ctx // ROW_TILE

    def cond_of_tile(i):
        return jnp.where(i < ctx_tiles, 0, i - ctx_tiles + 1)

    n_cond = 16
    cond = jnp.zeros((n_cond, d), F32).at[0].set(c_ctx).at[1:1 + nb_lat].set(c)
    mod_all = _modulation(cond, w_mod, b_mod).reshape(DEPTH, n_cond, 6, d)

    n_in = w_in.shape[2]
    w_in_b = jnp.pad(w_in, ((0, 0), (0, 0), (0, -n_in % LANES))).astype(BF16)
    w_p, w_gate = _w_in_prep(w_in_b, n_in)
    wq_up = w_mla_q_up.reshape(DEPTH, MLA_Q_RANK, MLA_HEADS, MLA_NOPE + MLA_ROPE)
    wq_up = jnp.concatenate([wq_up[..., :MLA_NOPE].reshape(DEPTH, MLA_Q_RANK, -1),
                             wq_up[..., MLA_NOPE:].reshape(DEPTH, MLA_Q_RANK, -1)], axis=2).astype(BF16)
    wkv_up = w_mla_kv_up.astype(BF16)
    w_br = jnp.stack([w_br_mla, w_br_win, w_br_nat], axis=1).astype(BF16)
    w_out_b = w_out.astype(BF16)
    w_router_p = jnp.pad(w_router, ((0, 0), (0, 0), (0, LANES - N_EXPERTS))).astype(BF16)

    t128 = jnp.concatenate(_rope_table(seq_lat, 128), axis=1)
    t64 = _rope_table(seq_lat, MLA_ROPE)
    t_q = _tile_cols(t64, MLA_HEADS)
    t_k = _tile_cols(t64, 1, pad_to=LANES)

    ck_win = cache_win_k.reshape(nb_lat, DEPTH, past, 256)
    cv_win = cache_win_v.reshape(nb_lat, DEPTH, past, 256)
    ck_nat = cache_nat_k.reshape(nb_lat, DEPTH, past, 1024)
    cv_nat = cache_nat_v.reshape(nb_lat, DEPTH, past, 1024)


    h = (x_prompt.reshape(rows_ctx, d), x_sample.reshape(rows_lat, d))
    states = None
    for l in range(DEPTH):
        mod = mod_all[l]
        qn = mla_q_norm[l].reshape(1, -1)
        kvn = mla_kv_norm[l].reshape(1, -1)
        sink = win_sink[l].reshape(1, -1)
        p, hn = _mixer_in(h, mod, norm1[l].reshape(1, d), w_p, l, cond_of_tile)
        o_ctx, states = _ctx_attention(p, nb_ctx, seq_ctx, qn, kvn, wq_up[l], wkv_up[l], sink,
                                       prev_states=states)
        o_a = _lat_mla(p, rows_ctx // seq_lat, nb_lat, seq_lat, cache_mla_ckv, cache_mla_krope, l,
                       qn, kvn, wq_up[l], wkv_up[l], t_q, t_k)
        o_b = _lat_win(p, rows_ctx, nb_lat, seq_lat, ck_win, cv_win, l, t128, sink)
        o_c = _lat_nat(p, rows_ctx, nb_lat, seq_lat, ck_nat, cv_nat, l, _nat_bias_blocks(nat_rpb[l]))
        m = _merge(hn, o_ctx, o_a, o_b, o_c, w_gate, w_br, l)
        if l % 2 == 0:
            h1, hn2 = _out_proj(m, w_out_b, h, mod, norm2[l].reshape(1, d), l, cond_of_tile)
            h = _ffn(hn2, w_ff_gate[l // 2].astype(BF16), w_ff_up[l // 2].astype(BF16),
                     w_ff_down[l // 2].astype(BF16), h1, mod, cond_of_tile)
        else:
            h1, hn2, route = _out_proj(m, w_out_b, h, mod, norm2[l].reshape(1, d), l, cond_of_tile,
                                       w_router=w_router_p[l // 2])
            slot_token, tile_expert, tile_valid, pos0, pos1 = _dispatch_plan(route, MOE_TM)
            xs = _gather_rows(hn2, slot_token, tile_valid, BF16)
            ys = _moe_ffn(xs, w_ex_gate[l // 2], w_ex_up[l // 2], w_ex_down[l // 2], tile_expert, tile_valid)
            y_ctx, y_lat = _moe_combine(ys, pos0, pos1, route, h1, mod, final_norm.reshape(1, d),
                                        cond_of_tile, rows_ctx)

    y_prompt = y_ctx.reshape(nb_ctx, seq_ctx, d)
    y_sample = y_lat.reshape(nb_lat, seq_lat, d)
    return (y_prompt, y_sample) + tuple(states)
```

```python
import functools

import jax
import jax.numpy as jnp
import numpy as np
from jax import lax
from jax.experimental import pallas as pl
from jax.experimental.pallas import tpu as pltpu

F32 = jnp.float32
BF16 = jnp.bfloat16

D_MODEL = 2048
DEPTH = 2
GRID_W = 64
ROPE_BASE = 10000.0
EPS = 1e-6
NEG_INF = -1e30
MLA_HEADS = 8
MLA_Q_RANK = 512
MLA_KV_RANK = 512
MLA_NOPE = 128
MLA_ROPE = 64
MLA_V = 128
WIN_HEADS = 8
WIN_KV_HEADS = 2
WIN_GROUP = WIN_HEADS // WIN_KV_HEADS
WIN_HEAD_DIM = 128
WINDOW = 128
NAT_HEADS = 8
NAT_HEAD_DIM = 128
NAT_ROWS = 8
NAT_COLS = 16
BRANCH_W = 1024
D_FF = 5632
N_EXPERTS = 8
TOP_K = 2

P_WQ, P_NQ, P_NK, P_NV = 0, 1024, 2048, 3072
P_WK, P_WV = 4096, 4352
P_QD, P_KVD, P_KR = 4608, 5120, 5632
P_COLS = 5760
P_TN = 1152
MIXER_TN = 1280
LANES = 128

VMEM_LIMIT = 56 * 1024 * 1024

ROW_TILE = 1024
FFN_TM = 512
FFN_TF = 512
MOE_TM = 2048
MOE_SLAB = 512
MOE_TF = 256
GATHER_ROWS = 512
NAT_QROWS = 4
NAT_KROWS = 12


def _cparams(sem):
    return pltpu.CompilerParams(dimension_semantics=sem, vmem_limit_bytes=VMEM_LIMIT)


def _rms(x, g):
    ms = jnp.mean(x * x, axis=-1, keepdims=True)
    return x * lax.rsqrt(ms + EPS) * g


def _dot(a, b):
    return jnp.dot(a, b, preferred_element_type=F32)


def _dot_nt(a, b):
    return lax.dot_general(a, b, (((1,), (1,)), ((), ())), preferred_element_type=F32)


def _rope(x, tab_ref_or_val, shift):
    n = x.shape[-1]
    t = tab_ref_or_val
    c, s_up, s_dn = t[:, 0:n], t[:, n:2 * n], t[:, 2 * n:3 * n]
    up = pltpu.roll(x, n - shift, axis=1)
    dn = pltpu.roll(x, shift, axis=1)
    return x * c + up * s_up + dn * s_dn


def _attend_heads(n_heads, score, value, o_ref, col0, sink=None, group=None, emit=None):
    group = group or n_heads
    for h0 in range(0, n_heads, group):
        heads = range(h0, h0 + group)
        s = [score(h) for h in heads]
        m = [jnp.max(x, axis=-1, keepdims=True) for x in s]
        if sink is not None:
            sk = [sink(h) for h in heads]
            m = [jnp.maximum(a, b) for a, b in zip(m, sk)]
        e = [jnp.exp(x - a) for x, a in zip(s, m)]
        den = [jnp.sum(x, axis=-1, keepdims=True) for x in e]
        if sink is not None:
            den = [d + jnp.exp(b - a) for d, a, b in zip(den, m, sk)]
        pv = [_dot(x.astype(BF16), value(h)) for x, h in zip(e, heads)]
        for i, h in enumerate(heads):
            out = pv[i] * (1.0 / den[i])
            if emit is not None:
                emit(h, out)
            else:
                o_ref[:, col0 + h * LANES:col0 + (h + 1) * LANES] = out.astype(o_ref.dtype)


def _mod_kernel(c_ref, w_ref, b_ref, o_ref):
    c = c_ref[...]
    s = (c * jax.nn.sigmoid(c)).astype(BF16)
    o_ref[...] = _dot(s, w_ref[...].astype(BF16)) + b_ref[...]


def _modulation(cond, w_mod, b_mod):
    depth, d, n = w_mod.shape
    nc = cond.shape[0]
    tn = 1024
    return pl.pallas_call(
        _mod_kernel,
        grid=(depth, n // tn),
        in_specs=[
            pl.BlockSpec((nc, d), lambda l, j: (0, 0)),
            pl.BlockSpec((None, d, tn), lambda l, j: (l, 0, j)),
            pl.BlockSpec((None, 1, tn), lambda l, j: (l, 0, j)),
        ],
        out_specs=pl.BlockSpec((None, nc, tn), lambda l, j: (l, 0, j)),
        out_shape=jax.ShapeDtypeStruct((depth, nc, n), F32),
        compiler_params=_cparams(("arbitrary", "arbitrary")),
        name="modulation",
    )(cond, w_mod, b_mod.reshape(depth, 1, n))


_W_IN_SEGMENTS = ((1088, 1024), (2624, 1024), (3648, 1024), (4672, 1024), (2112, 256), (2368, 256),
                  (0, 512), (512, 512), (1024, 64))
W_IN_GATES = 5696


def _w_in_prep_kernel(w_ref, p_ref, g_ref):
    col = 0
    for src, width in _W_IN_SEGMENTS:
        p_ref[:, col:col + width] = w_ref[:, src:src + width].astype(BF16)
        col += width
    p_ref[:, col:] = jnp.zeros((p_ref.shape[0], P_COLS - col), BF16)
    g_ref[...] = w_ref[:, W_IN_GATES:W_IN_GATES + g_ref.shape[1]].astype(BF16)


def _w_in_prep(w_in, n_cols):
    depth, d, n = w_in.shape
    tr = 128
    n_gate = n_cols - W_IN_GATES
    return pl.pallas_call(
        _w_in_prep_kernel,
        grid=(depth, d // tr),
        in_specs=[pl.BlockSpec((None, tr, n), lambda l, r: (l, r, 0))],
        out_specs=[pl.BlockSpec((None, tr, P_COLS), lambda l, r: (l, r, 0)),
                   pl.BlockSpec((None, tr, n_gate), lambda l, r: (l, r, 0))],
        out_shape=[jax.ShapeDtypeStruct((depth, d, P_COLS), BF16),
                   jax.ShapeDtypeStruct((depth, d, n_gate), BF16)],
        compiler_params=_cparams(("arbitrary", "arbitrary")),
        name="w_in_prep",
    )(w_in)


def _split_rows_specs(h, tm, single_buffer=False):
    if not isinstance(h, tuple):
        return [h], [pl.BlockSpec((tm, h.shape[1]), lambda i, *_: (i, 0))], None
    ctx_tiles = h[0].shape[0] // tm
    d = h[0].shape[1]
    mode = dict(pipeline_mode=pl.Buffered(1)) if single_buffer else {}
    return (list(h),
            [pl.BlockSpec((tm, d), lambda i, *_: (jnp.minimum(i, ctx_tiles - 1), 0), **mode),
             pl.BlockSpec((tm, d), lambda i, *_: (jnp.maximum(i - ctx_tiles, 0), 0), **mode)],
            ctx_tiles)


def _row_group_value(h_refs, ctx_tiles):
    if ctx_tiles is None:
        return h_refs[0][...]
    return jnp.where(pl.program_id(0) < ctx_tiles, h_refs[0][...], h_refs[1][...])


def _mixer_in_kernel(*refs, ctx_tiles, n_cols):
    nh = 1 if ctx_tiles is None else 2
    h_refs = refs[:nh]
    mod_ref, g_ref, w_ref, p_ref, hn_ref = refs[nh:]

    @pl.when(pl.program_id(1) == 0)
    def _():
        h = _row_group_value(h_refs, ctx_tiles)
        hn = _rms(h, g_ref[...]) * (1.0 + mod_ref[1:2, :]) + mod_ref[0:1, :]
        hn_ref[...] = hn.astype(BF16)

    j = pl.program_id(1)
    tn = p_ref.shape[1]
    tail = n_cols % tn

    @pl.when((j + 1) * tn <= n_cols)
    def _():
        p_ref[...] = _dot(hn_ref[...], w_ref[...])

    if tail:
        @pl.when((j + 1) * tn > n_cols)
        def _():
            p_ref[:, :tail] = _dot(hn_ref[...], w_ref[:, :tail])


def _mixer_in(h, mod, g, w, layer, cond_of_tile):
    tm, tn = ROW_TILE, MIXER_TN
    sub = ROW_TILE // tm
    h_args, h_specs, ctx_tiles = _split_rows_specs(h, tm, single_buffer=True)
    rows = sum(a.shape[0] for a in h_args)
    d = h_args[0].shape[1]
    n = w.shape[2]
    return pl.pallas_call(
        functools.partial(_mixer_in_kernel, ctx_tiles=ctx_tiles, n_cols=n),
        grid=(rows // tm, pl.cdiv(n, tn)),
        in_specs=h_specs + [
            pl.BlockSpec((None, 6, d), lambda i, j: (cond_of_tile(i // sub), 0, 0)),
            pl.BlockSpec((1, d), lambda i, j: (0, 0)),
            pl.BlockSpec((None, d, tn), lambda i, j: (layer, 0, j)),
        ],
        out_specs=[
            pl.BlockSpec((tm, tn), lambda i, j: (i, j)),
            pl.BlockSpec((tm, d), lambda i, j: (i, 0)),
        ],
        out_shape=[
            jax.ShapeDtypeStruct((rows, n), F32),
            jax.ShapeDtypeStruct((rows, d), BF16),
        ],
        compiler_params=_cparams(("arbitrary", "arbitrary")),
        name="mixer_in",
    )(*h_args, mod, g, w)


_STATE_WIDTHS = (MLA_KV_RANK, MLA_ROPE, 256, 256, 1024, 1024)
_STATE_HEADS = (None, None, WIN_KV_HEADS, WIN_KV_HEADS, NAT_HEADS, NAT_HEADS)


def _ctx_attn_kernel(p_ref, qn_ref, kvn_ref, wq_ref, wkv_ref, sink_ref, *refs, stacked):
    n_state = len(_STATE_WIDTHS)
    if stacked:
        prev_refs, o_ref, state_refs = refs[:n_state], refs[n_state], refs[n_state + 1:]
    else:
        prev_refs, o_ref, state_refs = None, refs[0], refs[1:]

    def put_state(k, val, slot=DEPTH - 1):
        ref, heads = state_refs[k], _STATE_HEADS[k]
        if not stacked:
            ref[...] = val
        elif heads is None:
            ref[slot] = val
        else:
            for j in range(heads):
                ref[slot, :, j, :] = val[:, j * LANES:(j + 1) * LANES]

    if stacked:
        for k in range(n_state):
            put_state(k, prev_refs[k][...], slot=0)

    q = _dot(_rms(p_ref[:, P_QD:P_QD + MLA_Q_RANK], qn_ref[...]).astype(BF16), wq_ref[...])
    ckv = _rms(p_ref[:, P_KVD:P_KVD + MLA_KV_RANK], kvn_ref[...])
    put_state(0, ckv)
    kv = _dot(ckv.astype(BF16), wkv_ref[...]).astype(BF16)
    kr = p_ref[:, P_KR:P_KR + MLA_ROPE]
    put_state(1, kr)
    krb = kr.astype(BF16)
    scale_a = (MLA_NOPE + MLA_ROPE) ** -0.5
    nope_w = MLA_HEADS * MLA_NOPE

    def score_a(h):
        qn = q[:, h * MLA_NOPE:(h + 1) * MLA_NOPE].astype(BF16)
        qr = q[:, nope_w + h * MLA_ROPE:nope_w + (h + 1) * MLA_ROPE].astype(BF16)
        return (_dot_nt(qn, kv[:, h * 256:h * 256 + MLA_NOPE]) + _dot_nt(qr, krb)) * scale_a

    _attend_heads(MLA_HEADS, score_a, lambda h: kv[:, h * 256 + MLA_NOPE:(h + 1) * 256], o_ref, 0)
    put_state(2, p_ref[:, P_WK:P_WK + 256])
    put_state(3, p_ref[:, P_WV:P_WV + 256])
    scale_b = WIN_HEAD_DIM ** -0.5

    def score_b(h):
        kvh = h // WIN_GROUP
        qh = p_ref[:, P_WQ + h * 128:P_WQ + (h + 1) * 128].astype(BF16)
        return _dot_nt(qh, p_ref[:, P_WK + kvh * 128:P_WK + (kvh + 1) * 128].astype(BF16)) * scale_b

    def value_b(h):
        kvh = h // WIN_GROUP
        return p_ref[:, P_WV + kvh * 128:P_WV + (kvh + 1) * 128].astype(BF16)

    _attend_heads(WIN_HEADS, score_b, value_b, o_ref, BRANCH_W, lambda h: sink_ref[0:1, h:h + 1])
    put_state(4, p_ref[:, P_NK:P_NK + 1024])
    put_state(5, p_ref[:, P_NV:P_NV + 1024])
    scale_c = NAT_HEAD_DIM ** -0.5

    def score_c(h):
        qh = p_ref[:, P_NQ + h * 128:P_NQ + (h + 1) * 128].astype(BF16)
        return _dot_nt(qh, p_ref[:, P_NK + h * 128:P_NK + (h + 1) * 128].astype(BF16)) * scale_c

    _attend_heads(NAT_HEADS, score_c, lambda h: p_ref[:, P_NV + h * 128:P_NV + (h + 1) * 128].astype(BF16),
                  o_ref, 2 * BRANCH_W)


def _ctx_attention(p, nb, seq, qn, kvn, wq, wkv, sink, prev_states=None):
    full = lambda a: pl.BlockSpec(a.shape, lambda b: (0,) * a.ndim)
    row = lambda w: pl.BlockSpec((seq, w), lambda b: (b, 0))
    stacked = prev_states is not None
    in_specs = [pl.BlockSpec((seq, P_COLS), lambda b: (b, 0)),
                full(qn), full(kvn), full(wq), full(wkv), full(sink)]
    args = [p, qn, kvn, wq, wkv, sink]
    out_specs = [row(3 * BRANCH_W)]
    out_shape = [jax.ShapeDtypeStruct((nb * seq, 3 * BRANCH_W), BF16)]
    for w, heads in zip(_STATE_WIDTHS, _STATE_HEADS):
        if not stacked:
            out_specs.append(row(w))
            out_shape.append(jax.ShapeDtypeStruct((nb * seq, w), F32))
        else:
            tail = (w,) if heads is None else (heads, w // heads)
            out_specs.append(pl.BlockSpec((None, DEPTH, seq) + tail, lambda b, n=len(tail): (b,) + (0,) * (n + 2)))
            out_shape.append(jax.ShapeDtypeStruct((nb, DEPTH, seq) + tail, F32))
    if stacked:
        in_specs += [row(w) for w in _STATE_WIDTHS]
        args += list(prev_states)
    outs = pl.pallas_call(
        functools.partial(_ctx_attn_kernel, stacked=stacked),
        grid=(nb,),
        in_specs=in_specs, out_specs=out_specs, out_shape=out_shape,
        compiler_params=_cparams(("arbitrary",)),
        name="ctx_attention",
    )(*args)
    return outs[0], outs[1:]


MLA_QT = 512
MLA_KC = 256
MLA_DK = MLA_NOPE + MLA_ROPE


def _lat_mla_kernel(p_ref, cckv_ref, ckr_ref, qn_ref, kvn_ref, wq_ref, wkv_ref, tq_ref, tk_ref,
                    o_ref, k_scr, v_scr, *, past, seq):
    qt = pl.program_id(1)

    def put_keys(r0, n, kv, kr):
        krb = kr.astype(BF16)
        for h in range(MLA_HEADS):
            k_scr[h, r0:r0 + n, 0:MLA_NOPE] = kv[:, h * 256:h * 256 + MLA_NOPE].astype(BF16)
            k_scr[h, r0:r0 + n, MLA_NOPE:MLA_DK] = krb
            v_scr[r0:r0 + n, h * MLA_V:(h + 1) * MLA_V] = kv[:, h * 256 + MLA_NOPE:(h + 1) * 256].astype(BF16)

    @pl.when(qt == 0)
    def _():
        for c in range(past // MLA_KC):
            r0 = c * MLA_KC
            kv = _dot(cckv_ref[r0:r0 + MLA_KC, :].astype(BF16), wkv_ref[...])
            put_keys(r0, MLA_KC, kv, ckr_ref[r0:r0 + MLA_KC, :])
        for c in range(seq // MLA_KC):
            r0 = c * MLA_KC
            ckv = _rms(p_ref[r0:r0 + MLA_KC, MLA_Q_RANK:MLA_Q_RANK + MLA_KV_RANK], kvn_ref[...])
            kv = _dot(ckv.astype(BF16), wkv_ref[...])
            krp = p_ref[r0:r0 + MLA_KC, 2 * MLA_Q_RANK:2 * MLA_Q_RANK + LANES]
            kr = _rope(krp, tk_ref[r0:r0 + MLA_KC, :], MLA_ROPE // 4)
            put_keys(past + r0, MLA_KC, kv, kr[:, 0:MLA_ROPE])

    r0 = pl.multiple_of(qt * MLA_QT, MLA_QT)
    qd = p_ref[pl.ds(r0, MLA_QT), 0:MLA_Q_RANK]
    q = _dot(_rms(qd, qn_ref[...]).astype(BF16), wq_ref[...])
    nope_w = MLA_HEADS * MLA_NOPE
    q_rope = _rope(q[:, nope_w:], tq_ref[pl.ds(r0, MLA_QT), :], MLA_ROPE // 4)
    scale = MLA_DK ** -0.5

    def score(h):
        qh = jnp.concatenate([q[:, h * MLA_NOPE:(h + 1) * MLA_NOPE],
                              q_rope[:, h * MLA_ROPE:(h + 1) * MLA_ROPE]], axis=1).astype(BF16)
        return _dot_nt(qh, k_scr[h]) * scale

    _attend_heads(MLA_HEADS, score, lambda h: v_scr[:, h * MLA_V:(h + 1) * MLA_V], o_ref, 0, group=2)


def _lat_mla(p, row_blk0, nb, seq, cache_ckv, cache_kr, layer, qn, kvn, wq, wkv, tq, tk):
    past = cache_ckv.shape[2]
    full = lambda a: pl.BlockSpec(a.shape, lambda b, t: (0,) * a.ndim)
    nqt = seq // MLA_QT
    return pl.pallas_call(
        functools.partial(_lat_mla_kernel, past=past, seq=seq),
        grid=(nb, nqt),
        in_specs=[
            pl.BlockSpec((seq, P_TN), lambda b, t: (row_blk0 + b, P_QD // P_TN)),
            pl.BlockSpec((None, None, past, MLA_KV_RANK), lambda b, t: (b, layer, 0, 0)),
            pl.BlockSpec((None, None, past, MLA_ROPE), lambda b, t: (b, layer, 0, 0)),
            full(qn), full(kvn), full(wq), full(wkv), full(tq), full(tk),
        ],
        out_specs=pl.BlockSpec((MLA_QT, BRANCH_W), lambda b, t: (b * nqt + t, 0)),
        out_shape=jax.ShapeDtypeStruct((nb * seq, BRANCH_W), BF16),
        scratch_shapes=[pltpu.VMEM((MLA_HEADS, past + seq, MLA_DK), BF16),
                        pltpu.VMEM((past + seq, MLA_HEADS * MLA_V), BF16)],
        compiler_params=_cparams(("arbitrary", "arbitrary")),
        name="lat_mla",
    )(p, cache_ckv, cache_kr, qn, kvn, wq, wkv, tq, tk)


WIN_QB = 128


def _lat_win_kernel(q_ref, k0_ref, k1_ref, k2_ref, v0_ref, v1_ref, v2_ref, ck_ref, cv_ref,
                    t0_ref, t1_ref, t2_ref, sink_ref, o_ref, *, past, seq):
    qb = pl.program_id(1)
    scale = WIN_HEAD_DIM ** -0.5
    n_loc = 3 * WIN_QB
    rows = WIN_GROUP * WIN_QB
    qpos = qb * WIN_QB + lax.broadcasted_iota(jnp.int32, (rows, n_loc), 0) % WIN_QB
    kpos = (qb - 1) * WIN_QB + lax.broadcasted_iota(jnp.int32, (rows, n_loc), 1)
    valid = (kpos >= 0) & (kpos < seq) & (jnp.abs(qpos - kpos) <= WINDOW)
    t1 = t1_ref[...]
    q_heads = lambda kvh: range(kvh * WIN_GROUP, (kvh + 1) * WIN_GROUP)

    def score(kvh):
        cs = slice(kvh * 128, (kvh + 1) * 128)
        keys = jnp.concatenate([
            ck_ref[:, cs],
            _rope(k0_ref[:, cs], t0_ref[...], 32),
            _rope(k1_ref[:, cs], t1, 32),
            _rope(k2_ref[:, cs], t2_ref[...], 32)], axis=0).astype(BF16)
        q = jnp.concatenate([_rope(q_ref[:, h * 128:(h + 1) * 128], t1, 32) for h in q_heads(kvh)],
                            axis=0).astype(BF16)
        s = _dot_nt(q, keys) * scale
        return jnp.concatenate([s[:, :past], jnp.where(valid, s[:, past:], NEG_INF)], axis=1)

    def value(kvh):
        cs = slice(kvh * 128, (kvh + 1) * 128)
        return jnp.concatenate([cv_ref[:, cs], v0_ref[:, cs], v1_ref[:, cs], v2_ref[:, cs]], axis=0).astype(BF16)

    def sink(kvh):
        return jnp.concatenate([jnp.broadcast_to(sink_ref[0:1, h:h + 1], (WIN_QB, 1)) for h in q_heads(kvh)],
                               axis=0)

    def emit(kvh, out):
        for g, h in enumerate(q_heads(kvh)):
            o_ref[:, h * 128:(h + 1) * 128] = out[g * WIN_QB:(g + 1) * WIN_QB, :].astype(BF16)

    _attend_heads(WIN_KV_HEADS, score, value, o_ref, 0, sink=sink, emit=emit)


def _lat_win(p, row0, nb, seq, cache_k, cache_v, layer, tab, sink):
    past = cache_k.shape[2]
    nqb = seq // WIN_QB
    rb0 = row0 // WIN_QB

    def kblk(off, col_blk):
        return pl.BlockSpec(
            (WIN_QB, 256),
            lambda b, t: (rb0 + b * nqb + jnp.clip(t + off, 0, nqb - 1), col_blk))

    def tblk(off):
        return pl.BlockSpec((WIN_QB, 3 * 128), lambda b, t: (jnp.clip(t + off, 0, nqb - 1), 0))

    cache = pl.BlockSpec((None, None, past, 256), lambda b, t: (b, layer, 0, 0))
    return pl.pallas_call(
        functools.partial(_lat_win_kernel, past=past, seq=seq),
        grid=(nb, nqb),
        in_specs=[
            pl.BlockSpec((WIN_QB, 1024), lambda b, t: (rb0 + b * nqb + t, P_WQ // 1024)),
            kblk(-1, P_WK // 256), kblk(0, P_WK // 256), kblk(1, P_WK // 256),
            kblk(-1, P_WV // 256), kblk(0, P_WV // 256), kblk(1, P_WV // 256),
            cache, cache,
            tblk(-1), tblk(0), tblk(1),
            pl.BlockSpec(sink.shape, lambda b, t: (0, 0)),
        ],
        out_specs=pl.BlockSpec((WIN_QB, BRANCH_W), lambda b, t: (b * nqb + t, 0)),
        out_shape=jax.ShapeDtypeStruct((nb * seq, BRANCH_W), BF16),
        compiler_params=_cparams(("arbitrary", "arbitrary")),
        name="lat_window",
    )(p, p, p, p, p, p, p, cache_k, cache_v, tab, tab, tab, sink)


def _lat_nat_kernel(q_ref, k0_ref, k1_ref, k2_ref, v0_ref, v1_ref, v2_ref, ck_ref, cv_ref, bias_ref, o_ref,
                    *, grid_rows):
    g = pl.program_id(0)
    scale = NAT_HEAD_DIM ** -0.5
    past = ck_ref.shape[0]
    win_rows = min(NAT_ROWS, grid_rows)
    lane = lax.broadcasted_iota(jnp.int32, (GRID_W, 2 * GRID_W), 1)

    def rows_of(h, refs):
        cs = slice(h * 128, (h + 1) * 128)
        return jnp.concatenate([r[:, cs] for r in refs], axis=0).astype(BF16)

    def pair_plan(rq, rkp):
        r = NAT_QROWS * g + rq
        kr0 = NAT_QROWS * _nat_key_start(g) + 2 * rkp
        rs = jnp.clip(r - win_rows // 2, 0, grid_rows - win_rows)
        in_win = lambda kr: ((kr >= rs) & (kr < rs + win_rows)).astype(jnp.int32)
        ok = jnp.where(lane < GRID_W, in_win(kr0), in_win(kr0 + 1)) > 0
        return jnp.clip(kr0 - r + NAT_ROWS - 1, 0, 2 * NAT_ROWS - 1), ok

    plans = [[pair_plan(rq, rkp) for rkp in range(NAT_KROWS // 2)] for rq in range(NAT_QROWS)]

    def bias(h):
        return jnp.concatenate(
            [jnp.concatenate([jnp.where(ok, bias_ref[h, a], NEG_INF) for a, ok in row], axis=1) for row in plans],
            axis=0)

    def score(h):
        q = q_ref[:, h * 128:(h + 1) * 128].astype(BF16)
        s = _dot_nt(q, rows_of(h, (ck_ref, k0_ref, k1_ref, k2_ref))) * scale
        return jnp.concatenate([s[:, :past], s[:, past:] + bias(h)], axis=1)

    _attend_heads(NAT_HEADS, score, lambda h: rows_of(h, (cv_ref, v0_ref, v1_ref, v2_ref)), o_ref, 0, group=2)


def _nat_key_start(g):
    return g // 2


def _lat_nat(p, row0, nb, seq, cache_k, cache_v, layer, bias):
    past = cache_k.shape[2]
    qrows = NAT_QROWS * GRID_W
    ng = seq // qrows
    rb0 = row0 // qrows

    def kblk(off, col_blk):
        return pl.BlockSpec((qrows, 1024), lambda g, b: (rb0 + b * ng + _nat_key_start(g) + off, col_blk))

    cache = pl.BlockSpec((None, None, past, 1024), lambda g, b: (b, layer, 0, 0))
    return pl.pallas_call(
        functools.partial(_lat_nat_kernel, grid_rows=seq // GRID_W),
        grid=(ng, nb),
        in_specs=[
            pl.BlockSpec((qrows, 1024), lambda g, b: (rb0 + b * ng + g, P_NQ // 1024)),
            kblk(0, P_NK // 1024), kblk(1, P_NK // 1024), kblk(2, P_NK // 1024),
            kblk(0, P_NV // 1024), kblk(1, P_NV // 1024), kblk(2, P_NV // 1024),
            cache, cache,
            pl.BlockSpec(bias.shape, lambda g, b: (0, 0, 0, 0)),
        ],
        out_specs=pl.BlockSpec((qrows, BRANCH_W), lambda g, b: (b * ng + g, 0)),
        out_shape=jax.ShapeDtypeStruct((nb * seq, BRANCH_W), BF16),
        compiler_params=_cparams(("arbitrary", "arbitrary")),
        name="lat_neighbourhood",
    )(p, p, p, p, p, p, p, cache_k, cache_v, bias)


def _nat_bias_blocks(rpb):
    n_dr, n_dc = 2 * NAT_ROWS - 1, 2 * NAT_COLS - 1
    cc = np.arange(GRID_W)
    dc = cc[None, :] - cc[:, None] + NAT_COLS - 1
    cs = np.clip(cc - NAT_COLS // 2, 0, GRID_W - NAT_COLS)[:, None]
    col_ok = (cc[None, :] >= cs) & (cc[None, :] < cs + NAT_COLS)
    onehot = ((dc[None] == np.arange(n_dc)[:, None, None]) & col_ok[None]).astype(np.float32)
    toep = jnp.einsum('had,dck->hack', rpb.astype(F32), jnp.asarray(onehot), precision=lax.Precision.HIGHEST)
    toep = jnp.where(jnp.asarray(col_ok), toep, NEG_INF)
    fill = jnp.full((NAT_HEADS, 2 * NAT_ROWS + 1 - n_dr, GRID_W, GRID_W), NEG_INF, F32)
    toep = jnp.concatenate([toep, fill], axis=1)
    return jnp.concatenate([toep[:, :-1], toep[:, 1:]], axis=-1)


def _merge_kernel(hn_ref, octx_ref, oa_ref, ob_ref, oc_ref, wga_ref, wgb_ref, wgc_ref,
                  wba_ref, wbb_ref, wbc_ref, m_ref, *, ctx_tiles):
    i = pl.program_id(0)
    weights = ((wga_ref, wba_ref), (wgb_ref, wbb_ref), (wgc_ref, wbc_ref))

    def merged(branch):
        hn = hn_ref[...]
        acc = None
        for k, (wg, wb) in enumerate(weights):
            term = jax.nn.sigmoid(_dot(hn, wg[...])) * _dot(branch(k), wb[...])
            acc = term if acc is None else acc + term
        m_ref[...] = acc.astype(BF16)

    @pl.when(i < ctx_tiles)
    def _():
        merged(lambda k: octx_ref[:, k * BRANCH_W:(k + 1) * BRANCH_W])

    @pl.when(i >= ctx_tiles)
    def _():
        lat = (oa_ref, ob_ref, oc_ref)
        merged(lambda k: lat[k][...])


def _merge(hn, o_ctx, o_a, o_b, o_c, w_gate, w_br, layer):
    rows, d = hn.shape
    tm, tn = ROW_TILE, 256
    nj = d // tn
    ctx_tiles = o_ctx.shape[0] // tm
    gate = lambda k: pl.BlockSpec((None, d, tn), lambda i, j: (layer, 0, k * nj + j))
    br = lambda k: pl.BlockSpec((None, None, BRANCH_W, tn), lambda i, j: (layer, k, 0, j))
    lat = pl.BlockSpec((tm, BRANCH_W), lambda i, j: (jnp.maximum(i - ctx_tiles, 0), 0))
    return pl.pallas_call(
        functools.partial(_merge_kernel, ctx_tiles=ctx_tiles),
        grid=(rows // tm, nj),
        in_specs=[pl.BlockSpec((tm, d), lambda i, j: (i, 0)),
                  pl.BlockSpec((tm, 3 * BRANCH_W), lambda i, j: (jnp.minimum(i, ctx_tiles - 1), 0)),
                  lat, lat, lat,
                  gate(0), gate(1), gate(2), br(0), br(1), br(2)],
        out_specs=pl.BlockSpec((tm, tn), lambda i, j: (i, j)),
        out_shape=jax.ShapeDtypeStruct((rows, d), BF16),
        compiler_params=_cparams(("arbitrary", "arbitrary")),
        name="merge",
    )(hn, o_ctx, o_a, o_b, o_c, w_gate, w_gate, w_gate, w_br, w_br, w_br)


ROUTE_IDX = 8
ROUTE_W = 10


def _out_proj_kernel(m_ref, w_ref, mod_ref, g_ref, *rest, routed, ctx_tiles):
    nh = 1 if ctx_tiles is None else 2
    h_refs, rest = rest[:nh], rest[nh:]
    if routed:
        wr_ref, h1_ref, hn2_ref, route_ref = rest
    else:
        h1_ref, hn2_ref = rest
    h1 = _row_group_value(h_refs, ctx_tiles) + mod_ref[2:3, :] * _dot(m_ref[...], w_ref[...])
    h1_ref[...] = h1
    hn2 = _rms(h1, g_ref[...]) * (1.0 + mod_ref[4:5, :]) + mod_ref[3:4, :]
    hn2_ref[...] = hn2.astype(hn2_ref.dtype)
    if routed:
        logits = _dot(hn2.astype(BF16), wr_ref[...])
        lane = lax.broadcasted_iota(jnp.int32, logits.shape, 1).astype(F32)
        lg = jnp.where(lane < N_EXPERTS, logits, -jnp.inf)
        m1 = jnp.max(lg, axis=-1, keepdims=True)
        i1 = jnp.min(jnp.where(lg == m1, lane, float(LANES)), axis=-1, keepdims=True)
        lg2 = jnp.where(lane == i1, -jnp.inf, lg)
        m2 = jnp.max(lg2, axis=-1, keepdims=True)
        i2 = jnp.min(jnp.where(lg2 == m2, lane, float(LANES)), axis=-1, keepdims=True)
        e2 = jnp.exp(m2 - m1)
        w1 = 1.0 / (1.0 + e2)
        w2 = e2 / (1.0 + e2)
        route = (jnp.where(lane == ROUTE_IDX, i1, 0.0)
                 + jnp.where(lane == ROUTE_IDX + 1, i2, 0.0)
                 + jnp.where(lane == ROUTE_W, w1, 0.0)
                 + jnp.where(lane == ROUTE_W + 1, w2, 0.0))
        route_ref[...] = route


def _out_proj(m, w_out, h, mod, g, layer, cond_of_tile, w_router=None):
    rows, d = m.shape
    tm = 512
    sub = ROW_TILE // tm
    routed = w_router is not None
    h_args, h_specs, ctx_tiles = _split_rows_specs(h, tm)
    in_specs = [
        pl.BlockSpec((tm, d), lambda i: (i, 0)),
        pl.BlockSpec((None, d, d), lambda i: (layer, 0, 0)),
        pl.BlockSpec((None, 6, d), lambda i: (cond_of_tile(i // sub), 0, 0)),
        pl.BlockSpec((1, d), lambda i: (0, 0)),
    ] + h_specs
    out_specs = [pl.BlockSpec((tm, d), lambda i: (i, 0)), pl.BlockSpec((tm, d), lambda i: (i, 0))]
    out_shape = [jax.ShapeDtypeStruct((rows, d), F32), jax.ShapeDtypeStruct((rows, d), BF16)]
    args = [m, w_out, mod, g] + h_args
    if routed:
        out_shape[1] = jax.ShapeDtypeStruct((rows, d), F32)
        in_specs.append(pl.BlockSpec((d, LANES), lambda i: (0, 0)))
        out_specs.append(pl.BlockSpec((tm, LANES), lambda i: (i, 0)))
        out_shape.append(jax.ShapeDtypeStruct((rows, LANES), F32))
        args.append(w_router)
    return pl.pallas_call(
        functools.partial(_out_proj_kernel, routed=routed, ctx_tiles=ctx_tiles),
        grid=(rows // tm,),
        in_specs=in_specs, out_specs=out_specs, out_shape=out_shape,
        compiler_params=_cparams(("arbitrary",)),
        name="out_proj",
    )(*args)


def _swiglu_chunk(x, wg_ref, wu_ref, wd_ref):
    g = _dot(x, wg_ref[...].astype(BF16))
    u = _dot(x, wu_ref[...].astype(BF16))
    a = (g * jax.nn.sigmoid(g) * u).astype(BF16)
    return _dot(a, wd_ref[...].astype(BF16))


def _ffn_kernel(x_ref, wg_ref, wu_ref, wd_ref, h_ref, mod_ref, o_ref):
    f = pl.program_id(1)

    @pl.when(f == 0)
    def _():
        o_ref[...] = jnp.zeros_like(o_ref)

    o_ref[...] += _swiglu_chunk(x_ref[...], wg_ref, wu_ref, wd_ref)

    @pl.when(f == pl.num_programs(1) - 1)
    def _():
        o_ref[...] = h_ref[...] + mod_ref[5:6, :] * o_ref[...]


def _ffn(x, wg, wu, wd, h, mod, cond_of_tile):
    rows, d = h.shape
    ff = wg.shape[1]
    tm, tf = FFN_TM, FFN_TF
    sub = ROW_TILE // tm
    return pl.pallas_call(
        _ffn_kernel,
        grid=(rows // tm, ff // tf),
        in_specs=[
            pl.BlockSpec((tm, d), lambda i, f: (i, 0)),
            pl.BlockSpec((d, tf), lambda i, f: (0, f)),
            pl.BlockSpec((d, tf), lambda i, f: (0, f)),
            pl.BlockSpec((tf, d), lambda i, f: (f, 0)),
            pl.BlockSpec((tm, d), lambda i, f: (i, 0)),
            pl.BlockSpec((None, 6, d), lambda i, f: (cond_of_tile(i // sub), 0, 0)),
        ],
        out_specs=pl.BlockSpec((tm, d), lambda i, f: (i, 0)),
        out_shape=jax.ShapeDtypeStruct((rows, d), F32),
        compiler_params=_cparams(("arbitrary", "arbitrary")),
        name="ffn_dense",
    )(x, wg, wu, wd, h, mod)


def _start_row_gather(src_ref, idx_ref, dst_ref, sem, n):
    def issue(r, carry):
        pltpu.make_async_copy(src_ref.at[pl.ds(idx_ref[0, r], 1), :], dst_ref.at[pl.ds(r, 1), :], sem).start()
        return carry

    lax.fori_loop(0, n, issue, 0, unroll=8)


def _wait_row_gather(src_ref, dst_ref, sem, n):
    pltpu.make_async_copy(src_ref.at[pl.ds(0, n), :], dst_ref, sem).wait()


def _gather_kernel(tr_ref, idx_ref, idx_next_ref, src_ref, o_ref, buf, sem):
    i = pl.program_id(0)
    n = o_ref.shape[0]
    per_tile = MOE_TM // n
    slot = i % 2

    def used(step):
        return tr_ref[step // per_tile] > (step % per_tile) * n

    @pl.when((i == 0) & used(0))
    def _():
        _start_row_gather(src_ref, idx_ref, buf.at[0], sem.at[0], n)

    nxt = jnp.minimum(i + 1, pl.num_programs(0) - 1)

    @pl.when((i + 1 < pl.num_programs(0)) & used(nxt))
    def _():
        _start_row_gather(src_ref, idx_next_ref, buf.at[1 - slot], sem.at[1 - slot], n)

    @pl.when(used(i))
    def _():
        _wait_row_gather(src_ref, buf.at[slot], sem.at[slot], n)
        o_ref[...] = buf[slot].astype(o_ref.dtype)

    @pl.when(jnp.logical_not(used(i)))
    def _():
        o_ref[...] = jnp.zeros_like(o_ref)


def _gather_rows(src, idx, tile_rows, out_dtype):
    n = idx.shape[0]
    d = src.shape[1]
    tg = GATHER_ROWS
    steps = n // tg
    idx3 = idx.reshape(steps, 1, tg)
    grid_spec = pltpu.PrefetchScalarGridSpec(
        num_scalar_prefetch=1,
        grid=(steps,),
        in_specs=[pl.BlockSpec((None, 1, tg), lambda i, tr: (i, 0, 0), memory_space=pltpu.SMEM),
                  pl.BlockSpec((None, 1, tg), lambda i, tr: (jnp.minimum(i + 1, steps - 1), 0, 0),
                               memory_space=pltpu.SMEM),
                  pl.BlockSpec(memory_space=pl.ANY)],
        out_specs=pl.BlockSpec((tg, d), lambda i, tr: (i, 0)),
        scratch_shapes=[pltpu.VMEM((2, tg, d), src.dtype), pltpu.SemaphoreType.DMA((2,))],
    )
    return pl.pallas_call(
        _gather_kernel,
        grid_spec=grid_spec,
        out_shape=jax.ShapeDtypeStruct((n, d), out_dtype),
        compiler_params=_cparams(("arbitrary",)),
        name="moe_gather",
    )(tile_rows, idx3, idx3, src)


def _moe_ffn_kernel(te_ref, tv_ref, x_ref, wg_ref, wu_ref, wd_ref, o_ref, acc_ref, sem):
    i = pl.program_id(0)
    f = pl.program_id(1)
    last_f = pl.num_programs(1) - 1
    rows_used = tv_ref[i]

    def writeback(tile, s):
        rows = pl.ds(pl.multiple_of(tile * MOE_TM + s * MOE_SLAB, MOE_SLAB), MOE_SLAB)
        return pltpu.make_async_copy(acc_ref.at[pl.ds(s * MOE_SLAB, MOE_SLAB), :], o_ref.at[rows, :], sem.at[s])

    for s in range(MOE_TM // MOE_SLAB):
        rs = slice(s * MOE_SLAB, (s + 1) * MOE_SLAB)
        used = rows_used > s * MOE_SLAB

        @pl.when((f == 0) & (i > 0))
        def _():
            writeback(i - 1, s).wait()

        @pl.when(f == 0)
        def _():
            acc_ref[rs, :] = jnp.zeros((MOE_SLAB, acc_ref.shape[1]), acc_ref.dtype)

        @pl.when(used)
        def _():
            acc_ref[rs, :] += _swiglu_chunk(x_ref[rs, :], wg_ref, wu_ref, wd_ref)

        @pl.when(f == last_f)
        def _():
            writeback(i, s).start()

    @pl.when((f == last_f) & (i == pl.num_programs(0) - 1))
    def _():
        for s in range(MOE_TM // MOE_SLAB):
            writeback(i, s).wait()


def _moe_ffn(xs, wg, wu, wd, tile_expert, tile_valid):
    n, d = xs.shape
    ff = wg.shape[2]
    tm, tf = MOE_TM, MOE_TF
    nf = ff // tf

    def fidx(i, f, tv):
        return jnp.where(tv[i] > 0, f, nf - 1)

    grid_spec = pltpu.PrefetchScalarGridSpec(
        num_scalar_prefetch=2,
        grid=(n // tm, nf),
        in_specs=[
            pl.BlockSpec((tm, d), lambda i, f, te, tv: (i, 0)),
            pl.BlockSpec((None, d, tf), lambda i, f, te, tv: (te[i], 0, fidx(i, f, tv))),
            pl.BlockSpec((None, d, tf), lambda i, f, te, tv: (te[i], 0, fidx(i, f, tv))),
            pl.BlockSpec((None, tf, d), lambda i, f, te, tv: (te[i], fidx(i, f, tv), 0)),
        ],
        out_specs=pl.BlockSpec(memory_space=pl.ANY),
        scratch_shapes=[pltpu.VMEM((tm, d), F32), pltpu.SemaphoreType.DMA((tm // MOE_SLAB,))],
    )
    return pl.pallas_call(
        _moe_ffn_kernel,
        grid_spec=grid_spec,
        out_shape=jax.ShapeDtypeStruct((n, d), F32),
        compiler_params=_cparams(("arbitrary", "arbitrary")),
        name="moe_ffn",
    )(tile_expert, tile_valid, xs, wg, wu, wd)


def _combine_kernel(p0_ref, p1_ref, p0_next_ref, p1_next_ref, ys_ref, route_ref, h_ref, mod_ref, fn_ref,
                    octx_ref, olat_ref, buf0, buf1, sem, *, ctx_steps):
    i = pl.program_id(0)
    n = octx_ref.shape[0]
    slot = i % 2

    def start(q0_ref, q1_ref, s):
        _start_row_gather(ys_ref, q0_ref, buf0.at[s], sem.at[s, 0], n)
        _start_row_gather(ys_ref, q1_ref, buf1.at[s], sem.at[s, 1], n)

    @pl.when(i == 0)
    def _():
        start(p0_ref, p1_ref, 0)

    @pl.when(i + 1 < pl.num_programs(0))
    def _():
        start(p0_next_ref, p1_next_ref, 1 - slot)

    _wait_row_gather(ys_ref, buf0.at[slot], sem.at[slot, 0], n)
    _wait_row_gather(ys_ref, buf1.at[slot], sem.at[slot, 1], n)
    w0 = route_ref[:, ROUTE_W:ROUTE_W + 1]
    w1 = route_ref[:, ROUTE_W + 1:ROUTE_W + 2]
    y = w0 * buf0[slot] + w1 * buf1[slot]
    h2 = h_ref[...] + mod_ref[5:6, :] * y
    out = _rms(h2, fn_ref[...])

    @pl.when(i < ctx_steps)
    def _():
        octx_ref[...] = out

    @pl.when(i >= ctx_steps)
    def _():
        olat_ref[...] = out


def _moe_combine(ys, pos0, pos1, route, h, mod, final_norm, cond_of_tile, rows_ctx):
    rows, d = h.shape
    tc = GATHER_ROWS
    sub = ROW_TILE // tc
    ctx_steps = rows_ctx // tc
    steps = rows // tc
    smem = lambda: pl.BlockSpec((None, 1, tc), lambda i: (i, 0, 0), memory_space=pltpu.SMEM)
    smem_next = lambda: pl.BlockSpec((None, 1, tc), lambda i: (jnp.minimum(i + 1, steps - 1), 0, 0),
                                     memory_space=pltpu.SMEM)
    pos0, pos1 = pos0.reshape(steps, 1, tc), pos1.reshape(steps, 1, tc)
    return pl.pallas_call(
        functools.partial(_combine_kernel, ctx_steps=ctx_steps),
        grid=(steps,),
        in_specs=[smem(), smem(), smem_next(), smem_next(),
                  pl.BlockSpec(memory_space=pl.ANY),
                  pl.BlockSpec((tc, LANES), lambda i: (i, 0)),
                  pl.BlockSpec((tc, d), lambda i: (i, 0)),
                  pl.BlockSpec((None, 6, d), lambda i: (cond_of_tile(i // sub), 0, 0)),
                  pl.BlockSpec((1, d), lambda i: (0, 0))],
        out_specs=[pl.BlockSpec((tc, d), lambda i: (jnp.minimum(i, ctx_steps - 1), 0)),
                   pl.BlockSpec((tc, d), lambda i: (jnp.maximum(i - ctx_steps, 0), 0))],
        out_shape=[jax.ShapeDtypeStruct((rows_ctx, d), F32), jax.ShapeDtypeStruct((rows - rows_ctx, d), F32)],
        scratch_shapes=[pltpu.VMEM((2, tc, d), F32), pltpu.VMEM((2, tc, d), F32),
                        pltpu.SemaphoreType.DMA((2, 2))],
        compiler_params=_cparams(("arbitrary",)),
        name="moe_combine",
    )(pos0, pos1, pos0, pos1, ys, route, h, mod, final_norm)


def _dispatch_plan(route, tm):
    rows = route.shape[0]
    ids = route[:, ROUTE_IDX:ROUTE_IDX + TOP_K].astype(jnp.int32)
    flat = ids.reshape(-1)
    onehot = (flat[:, None] == jnp.arange(N_EXPERTS)[None, :]).astype(jnp.int32)
    rank = jnp.sum((jnp.cumsum(onehot, axis=0) - onehot) * onehot, axis=1)
    counts = jnp.sum(onehot, axis=0)
    tiles = (counts + tm - 1) // tm
    tile_end = jnp.cumsum(tiles)
    start = (tile_end - tiles) * tm
    slot = start[flat] + rank
    n_tiles = (rows * TOP_K) // tm + N_EXPERTS
    n_slots = n_tiles * tm
    slot_token = (jnp.arange(n_slots, dtype=jnp.int32) % rows).at[slot].set(
        jnp.arange(rows * TOP_K, dtype=jnp.int32) // TOP_K)
    t = jnp.arange(n_tiles)
    tile_expert = jnp.minimum(jnp.sum((t[:, None] >= tile_end[None, :]).astype(jnp.int32), axis=1), N_EXPERTS - 1)
    tile_in_expert = t - (tile_end - tiles)[tile_expert]
    tile_rows = jnp.where(t < tile_end[-1], jnp.clip(counts[tile_expert] - tile_in_expert * tm, 0, tm), 0)
    last_expert = tile_expert[jnp.maximum(tile_end[-1] - 1, 0)]
    tile_expert = jnp.where(tile_rows > 0, tile_expert, last_expert).astype(jnp.int32)
    pos = slot.reshape(rows, TOP_K)
    return slot_token, tile_expert, tile_rows.astype(jnp.int32), pos[:, 0], pos[:, 1]


def _rope_table(seq, n):
    quarter = n // 4
    t = np.arange(seq)
    inv = jnp.power(ROPE_BASE, -jnp.arange(quarter, dtype=F32) * (2.0 / (n // 2)))
    ang_r = jnp.asarray(t // GRID_W, F32)[:, None] * inv[None, :]
    ang_c = jnp.asarray(t % GRID_W, F32)[:, None] * inv[None, :]
    zero = jnp.zeros((seq, quarter), F32)
    cos = jnp.concatenate([jnp.cos(ang_r)] * 2 + [jnp.cos(ang_c)] * 2, axis=1)
    up = jnp.concatenate([-jnp.sin(ang_r), zero, -jnp.sin(ang_c), zero], axis=1)
    dn = jnp.concatenate([zero, jnp.sin(ang_r), zero, jnp.sin(ang_c)], axis=1)
    return cos, up, dn


def _tile_cols(parts, reps, pad_to=None):
    out = [jnp.tile(p, (1, reps)) for p in parts]
    if pad_to is not None:
        out = [jnp.pad(p, ((0, 0), (0, pad_to - p.shape[1]))) for p in out]
    return jnp.concatenate(out, axis=1)


def kernel(x_prompt, x_sample, cache_mla_ckv, cache_mla_krope, cache_win_k, cache_win_v, cache_nat_k, cache_nat_v, c, c_ctx, w_mod, b_mod, norm1, norm2, w_in, mla_q_norm, mla_kv_norm, w_mla_q_up, w_mla_kv_up, win_sink, nat_rpb, w_br_mla, w_br_win, w_br_nat, w_out, w_ff_gate, w_ff_up, w_ff_down, w_router, w_ex_gate, w_ex_up, w_ex_down, final_norm):
    nb_ctx, seq_ctx, d = x_prompt.shape
    nb_lat, seq_lat, _ = x_sample.shape
    past = cache_mla_ckv.shape[2]
    rows_ctx = nb_ctx * seq_ctx
    rows_lat = nb_lat * seq_lat
    assert d == D_MODEL and seq_lat == ROW_TILE and rows_ctx % ROW_TILE == 0
    assert seq_lat // GRID_W == 4 * NAT_QROWS
    assert w_mod.shape[0] == DEPTH == 2
    ctx_tiles = rows_ctx // ROW_TILE

    def cond_of_tile(i):
        return jnp.where(i < ctx_tiles, 0, i - ctx_tiles + 1)

    n_cond = 16
    cond = jnp.zeros((n_cond, d), F32).at[0].set(c_ctx).at[1:1 + nb_lat].set(c)
    mod_all = _modulation(cond, w_mod, b_mod).reshape(DEPTH, n_cond, 6, d)

    n_in = w_in.shape[2]
    w_in_b = jnp.pad(w_in, ((0, 0), (0, 0), (0, -n_in % LANES))).astype(BF16)
    w_p, w_gate = _w_in_prep(w_in_b, n_in)
    wq_up = w_mla_q_up.reshape(DEPTH, MLA_Q_RANK, MLA_HEADS, MLA_NOPE + MLA_ROPE)
    wq_up = jnp.concatenate([wq_up[..., :MLA_NOPE].reshape(DEPTH, MLA_Q_RANK, -1),
                             wq_up[..., MLA_NOPE:].reshape(DEPTH, MLA_Q_RANK, -1)], axis=2).astype(BF16)
    wkv_up = w_mla_kv_up.astype(BF16)
    w_br = jnp.stack([w_br_mla, w_br_win, w_br_nat], axis=1).astype(BF16)
    w_out_b = w_out.astype(BF16)
    w_router_p = jnp.pad(w_router, ((0, 0), (0, 0), (0, LANES - N_EXPERTS))).astype(BF16)

    t128 = jnp.concatenate(_rope_table(seq_lat, 128), axis=1)
    t64 = _rope_table(seq_lat, MLA_ROPE)
    t_q = _tile_cols(t64, MLA_HEADS)
    t_k = _tile_cols(t64, 1, pad_to=LANES)

    ck_win = cache_win_k.reshape(nb_lat, DEPTH, past, 256)
    cv_win = cache_win_v.reshape(nb_lat, DEPTH, past, 256)
    ck_nat = cache_nat_k.reshape(nb_lat, DEPTH, past, 1024)
    cv_nat = cache_nat_v.reshape(nb_lat, DEPTH, past, 1024)


    h = (x_prompt.reshape(rows_ctx, d), x_sample.reshape(rows_lat, d))
    states = None
    for l in range(DEPTH):
        mod = mod_all[l]
        qn = mla_q_norm[l].reshape(1, -1)
        kvn = mla_kv_norm[l].reshape(1, -1)
        sink = win_sink[l].reshape(1, -1)
        p, hn = _mixer_in(h, mod, norm1[l].reshape(1, d), w_p, l, cond_of_tile)
        o_ctx, states = _ctx_attention(p, nb_ctx, seq_ctx, qn, kvn, wq_up[l], wkv_up[l], sink,
                                       prev_states=states)
        o_a = _lat_mla(p, rows_ctx // seq_lat, nb_lat, seq_lat, cache_mla_ckv, cache_mla_krope, l,
                       qn, kvn, wq_up[l], wkv_up[l], t_q, t_k)
        o_b = _lat_win(p, rows_ctx, nb_lat, seq_lat, ck_win, cv_win, l, t128, sink)
        o_c = _lat_nat(p, rows_ctx, nb_lat, seq_lat, ck_nat, cv_nat, l, _nat_bias_blocks(nat_rpb[l]))
        m = _merge(hn, o_ctx, o_a, o_b, o_c, w_gate, w_br, l)
        if l % 2 == 0:
            h1, hn2 = _out_proj(m, w_out_b, h, mod, norm2[l].reshape(1, d), l, cond_of_tile)
            h = _ffn(hn2, w_ff_gate[l // 2].astype(BF16), w_ff_up[l // 2].astype(BF16),
                     w_ff_down[l // 2].astype(BF16), h1, mod, cond_of_tile)
        else:
            h1, hn2, route = _out_proj(m, w_out_b, h, mod, norm2[l].reshape(1, d), l, cond_of_tile,
                                       w_router=w_router_p[l // 2])
            slot_token, tile_expert, tile_valid, pos0, pos1 = _dispatch_plan(route, MOE_TM)
            xs = _gather_rows(hn2, slot_token, tile_valid, BF16)
            ys = _moe_ffn(xs, w_ex_gate[l // 2], w_ex_up[l // 2], w_ex_down[l // 2], tile_expert, tile_valid)
            y_ctx, y_lat = _moe_combine(ys, pos0, pos1, route, h1, mod, final_norm.reshape(1, d),
                                        cond_of_tile, rows_ctx)

    y_prompt = y_ctx.reshape(nb_ctx, seq_ctx, d)
    y_sample = y_lat.reshape(nb_lat, seq_lat, d)
    return (y_prompt, y_sample) + tuple(states)
```

```python
import functools

import jax
import jax.numpy as jnp
import numpy as np
from jax import lax
from jax.experimental import pallas as pl
from jax.experimental.pallas import tpu as pltpu

F32 = jnp.float32
BF16 = jnp.bfloat16

D_MODEL = 2048
DEPTH = 2
GRID_W = 64
ROPE_BASE = 10000.0
EPS = 1e-6
NEG_INF = -1e30
MLA_HEADS = 8
MLA_Q_RANK = 512
MLA_KV_RANK = 512
MLA_NOPE = 128
MLA_ROPE = 64
MLA_V = 128
WIN_HEADS = 8
WIN_KV_HEADS = 2
WIN_GROUP = WIN_HEADS // WIN_KV_HEADS
WIN_HEAD_DIM = 128
WINDOW = 128
NAT_HEADS = 8
NAT_HEAD_DIM = 128
NAT_ROWS = 8
NAT_COLS = 16
BRANCH_W = 1024
D_FF = 5632
N_EXPERTS = 8
TOP_K = 2

P_WQ, P_NQ, P_NK, P_NV = 0, 1024, 2048, 3072
P_WK, P_WV = 4096, 4352
P_QD, P_KVD, P_KR = 4608, 5120, 5632
P_COLS = 5760
P_TN = 1152
MIXER_TN = 1280
LANES = 128

VMEM_LIMIT = 56 * 1024 * 1024

ROW_TILE = 1024
FFN_TM = 1024
FFN_TF = 512
MOE_TM = 2048
MOE_SLAB = 512
MOE_TF = 256
GATHER_ROWS = 512
NAT_QROWS = 4
NAT_KROWS = 12


def _cparams(sem):
    return pltpu.CompilerParams(dimension_semantics=sem, vmem_limit_bytes=VMEM_LIMIT)


def _rms(x, g):
    ms = jnp.mean(x * x, axis=-1, keepdims=True)
    return x * lax.rsqrt(ms + EPS) * g


def _dot(a, b):
    return jnp.dot(a, b, preferred_element_type=F32)


def _dot_nt(a, b):
    return lax.dot_general(a, b, (((1,), (1,)), ((), ())), preferred_element_type=F32)


def _rope(x, tab_ref_or_val, shift):
    n = x.shape[-1]
    t = tab_ref_or_val
    c, s_up, s_dn = t[:, 0:n], t[:, n:2 * n], t[:, 2 * n:3 * n]
    up = pltpu.roll(x, n - shift, axis=1)
    dn = pltpu.roll(x, shift, axis=1)
    return x * c + up * s_up + dn * s_dn


def _attend_heads(n_heads, score, value, o_ref, col0, sink=None, group=None, emit=None):
    group = group or n_heads
    for h0 in range(0, n_heads, group):
        heads = range(h0, h0 + group)
        s = [score(h) for h in heads]
        m = [jnp.max(x, axis=-1, keepdims=True) for x in s]
        if sink is not None:
            sk = [sink(h) for h in heads]
            m = [jnp.maximum(a, b) for a, b in zip(m, sk)]
        e = [jnp.exp(x - a) for x, a in zip(s, m)]
        den = [jnp.sum(x, axis=-1, keepdims=True) for x in e]
        if sink is not None:
            den = [d + jnp.exp(b - a) for d, a, b in zip(den, m, sk)]
        pv = [_dot(x.astype(BF16), value(h)) for x, h in zip(e, heads)]
        for i, h in enumerate(heads):
            out = pv[i] * (1.0 / den[i])
            if emit is not None:
                emit(h, out)
            else:
                o_ref[:, col0 + h * LANES:col0 + (h + 1) * LANES] = out.astype(o_ref.dtype)


def _mod_kernel(c_ref, w_ref, b_ref, o_ref):
    c = c_ref[...]
    s = (c * jax.nn.sigmoid(c)).astype(BF16)
    o_ref[...] = _dot(s, w_ref[...].astype(BF16)) + b_ref[...]


def _modulation(cond, w_mod, b_mod):
    depth, d, n = w_mod.shape
    nc = cond.shape[0]
    tn = 1024
    return pl.pallas_call(
        _mod_kernel,
        grid=(depth, n // tn),
        in_specs=[
            pl.BlockSpec((nc, d), lambda l, j: (0, 0)),
            pl.BlockSpec((None, d, tn), lambda l, j: (l, 0, j)),
            pl.BlockSpec((None, 1, tn), lambda l, j: (l, 0, j)),
        ],
        out_specs=pl.BlockSpec((None, nc, tn), lambda l, j: (l, 0, j)),
        out_shape=jax.ShapeDtypeStruct((depth, nc, n), F32),
        compiler_params=_cparams(("arbitrary", "arbitrary")),
        name="modulation",
    )(cond, w_mod, b_mod.reshape(depth, 1, n))


_W_IN_SEGMENTS = ((1088, 1024), (2624, 1024), (3648, 1024), (4672, 1024), (2112, 256), (2368, 256),
                  (0, 512), (512, 512), (1024, 64))
W_IN_GATES = 5696


def _w_in_prep_kernel(w_ref, p_ref, g_ref):
    row = 0
    for src, width in _W_IN_SEGMENTS:
        p_ref[row:row + width, :] = w_ref[src:src + width, :].astype(BF16)
        row += width
    p_ref[row:, :] = jnp.zeros((P_COLS - row, p_ref.shape[1]), BF16)
    g_ref[...] = w_ref[W_IN_GATES:, :].astype(BF16)


def _w_in_prep(w_in_t):
    depth, n, d = w_in_t.shape
    tk = 256
    n_gate = n - W_IN_GATES
    return pl.pallas_call(
        _w_in_prep_kernel,
        grid=(depth, d // tk),
        in_specs=[pl.BlockSpec((None, n, tk), lambda l, r: (l, 0, r))],
        out_specs=[pl.BlockSpec((None, P_COLS, tk), lambda l, r: (l, 0, r)),
                   pl.BlockSpec((None, n_gate, tk), lambda l, r: (l, 0, r))],
        out_shape=[jax.ShapeDtypeStruct((depth, P_COLS, d), BF16),
                   jax.ShapeDtypeStruct((depth, n_gate, d), BF16)],
        compiler_params=_cparams(("arbitrary", "arbitrary")),
        name="w_in_prep",
    )(w_in_t)


def _split_rows_specs(h, tm, single_buffer=False):
    if not isinstance(h, tuple):
        return [h], [pl.BlockSpec((tm, h.shape[1]), lambda i, *_: (i, 0))], None
    ctx_tiles = h[0].shape[0] // tm
    d = h[0].shape[1]
    mode = dict(pipeline_mode=pl.Buffered(1)) if single_buffer else {}
    return (list(h),
            [pl.BlockSpec((tm, d), lambda i, *_: (jnp.minimum(i, ctx_tiles - 1), 0), **mode),
             pl.BlockSpec((tm, d), lambda i, *_: (jnp.maximum(i - ctx_tiles, 0), 0), **mode)],
            ctx_tiles)


def _row_group_value(h_refs, ctx_tiles):
    if ctx_tiles is None:
        return h_refs[0][...]
    return jnp.where(pl.program_id(0) < ctx_tiles, h_refs[0][...], h_refs[1][...])


def _mixer_in_kernel(*refs, ctx_tiles, n_cols):
    nh = 1 if ctx_tiles is None else 2
    h_refs = refs[:nh]
    mod_ref, g_ref, w_ref, p_ref, hn_ref = refs[nh:]

    @pl.when(pl.program_id(1) == 0)
    def _():
        h = _row_group_value(h_refs, ctx_tiles)
        hn = _rms(h, g_ref[...]) * (1.0 + mod_ref[1:2, :]) + mod_ref[0:1, :]
        hn_ref[...] = hn.astype(BF16)

    j = pl.program_id(1)
    tn = p_ref.shape[1]
    tail = n_cols % tn

    @pl.when((j + 1) * tn <= n_cols)
    def _():
        p_ref[...] = _dot_nt(hn_ref[...], w_ref[...])

    if tail:
        @pl.when((j + 1) * tn > n_cols)
        def _():
            p_ref[:, :tail] = _dot_nt(hn_ref[...], w_ref[:tail, :])


def _mixer_in(h, mod, g, w, layer, cond_of_tile):
    tm, tn = ROW_TILE, MIXER_TN
    sub = ROW_TILE // tm
    h_args, h_specs, ctx_tiles = _split_rows_specs(h, tm, single_buffer=True)
    rows = sum(a.shape[0] for a in h_args)
    d = h_args[0].shape[1]
    n = w.shape[1]
    return pl.pallas_call(
        functools.partial(_mixer_in_kernel, ctx_tiles=ctx_tiles, n_cols=n),
        grid=(rows // tm, pl.cdiv(n, tn)),
        in_specs=h_specs + [
            pl.BlockSpec((None, 6, d), lambda i, j: (cond_of_tile(i // sub), 0, 0)),
            pl.BlockSpec((1, d), lambda i, j: (0, 0)),
            pl.BlockSpec((None, tn, d), lambda i, j: (layer, j, 0)),
        ],
        out_specs=[
            pl.BlockSpec((tm, tn), lambda i, j: (i, j)),
            pl.BlockSpec((tm, d), lambda i, j: (i, 0)),
        ],
        out_shape=[
            jax.ShapeDtypeStruct((rows, n), F32),
            jax.ShapeDtypeStruct((rows, d), BF16),
        ],
        compiler_params=_cparams(("arbitrary", "arbitrary")),
        name="mixer_in",
    )(*h_args, mod, g, w)


_STATE_WIDTHS = (MLA_KV_RANK, MLA_ROPE, 256, 256, 1024, 1024)
_STATE_HEADS = (None, None, WIN_KV_HEADS, WIN_KV_HEADS, NAT_HEADS, NAT_HEADS)


def _ctx_attn_kernel(p_ref, qn_ref, kvn_ref, wq_ref, wkv_ref, sink_ref, *refs, stacked):
    n_state = len(_STATE_WIDTHS)
    if stacked:
        prev_refs, o_ref, state_refs = refs[:n_state], refs[n_state], refs[n_state + 1:]
    else:
        prev_refs, o_ref, state_refs = None, refs[0], refs[1:]

    def put_state(k, val, slot=DEPTH - 1):
        ref, heads = state_refs[k], _STATE_HEADS[k]
        if not stacked:
            ref[...] = val
        elif heads is None:
            ref[slot] = val
        else:
            for j in range(heads):
                ref[slot, :, j, :] = val[:, j * LANES:(j + 1) * LANES]

    if stacked:
        for k in range(n_state):
            put_state(k, prev_refs[k][...], slot=0)

    q = _dot(_rms(p_ref[:, P_QD:P_QD + MLA_Q_RANK], qn_ref[...]).astype(BF16), wq_ref[...])
    ckv = _rms(p_ref[:, P_KVD:P_KVD + MLA_KV_RANK], kvn_ref[...])
    put_state(0, ckv)
    kv = _dot(ckv.astype(BF16), wkv_ref[...]).astype(BF16)
    kr = p_ref[:, P_KR:P_KR + MLA_ROPE]
    put_state(1, kr)
    krb = kr.astype(BF16)
    scale_a = (MLA_NOPE + MLA_ROPE) ** -0.5
    nope_w = MLA_HEADS * MLA_NOPE

    def score_a(h):
        qn = q[:, h * MLA_NOPE:(h + 1) * MLA_NOPE].astype(BF16)
        qr = q[:, nope_w + h * MLA_ROPE:nope_w + (h + 1) * MLA_ROPE].astype(BF16)
        return (_dot_nt(qn, kv[:, h * 256:h * 256 + MLA_NOPE]) + _dot_nt(qr, krb)) * scale_a

    _attend_heads(MLA_HEADS, score_a, lambda h: kv[:, h * 256 + MLA_NOPE:(h + 1) * 256], o_ref, 0)
    put_state(2, p_ref[:, P_WK:P_WK + 256])
    put_state(3, p_ref[:, P_WV:P_WV + 256])
    scale_b = WIN_HEAD_DIM ** -0.5

    def score_b(h):
        kvh = h // WIN_GROUP
        qh = p_ref[:, P_WQ + h * 128:P_WQ + (h + 1) * 128].astype(BF16)
        return _dot_nt(qh, p_ref[:, P_WK + kvh * 128:P_WK + (kvh + 1) * 128].astype(BF16)) * scale_b

    def value_b(h):
        kvh = h // WIN_GROUP
        return p_ref[:, P_WV + kvh * 128:P_WV + (kvh + 1) * 128].astype(BF16)

    _attend_heads(WIN_HEADS, score_b, value_b, o_ref, BRANCH_W, lambda h: sink_ref[0:1, h:h + 1])
    put_state(4, p_ref[:, P_NK:P_NK + 1024])
    put_state(5, p_ref[:, P_NV:P_NV + 1024])
    scale_c = NAT_HEAD_DIM ** -0.5

    def score_c(h):
        qh = p_ref[:, P_NQ + h * 128:P_NQ + (h + 1) * 128].astype(BF16)
        return _dot_nt(qh, p_ref[:, P_NK + h * 128:P_NK + (h + 1) * 128].astype(BF16)) * scale_c

    _attend_heads(NAT_HEADS, score_c, lambda h: p_ref[:, P_NV + h * 128:P_NV + (h + 1) * 128].astype(BF16),
                  o_ref, 2 * BRANCH_W)


def _ctx_attention(p, nb, seq, qn, kvn, wq, wkv, sink, prev_states=None):
    full = lambda a: pl.BlockSpec(a.shape, lambda b: (0,) * a.ndim)
    row = lambda w: pl.BlockSpec((seq, w), lambda b: (b, 0))
    stacked = prev_states is not None
    in_specs = [pl.BlockSpec((seq, P_COLS), lambda b: (b, 0)),
                full(qn), full(kvn), full(wq), full(wkv), full(sink)]
    args = [p, qn, kvn, wq, wkv, sink]
    out_specs = [row(3 * BRANCH_W)]
    out_shape = [jax.ShapeDtypeStruct((nb * seq, 3 * BRANCH_W), BF16)]
    for w, heads in zip(_STATE_WIDTHS, _STATE_HEADS):
        if not stacked:
            out_specs.append(row(w))
            out_shape.append(jax.ShapeDtypeStruct((nb * seq, w), F32))
        else:
            tail = (w,) if heads is None else (heads, w // heads)
            out_specs.append(pl.BlockSpec((None, DEPTH, seq) + tail, lambda b, n=len(tail): (b,) + (0,) * (n + 2)))
            out_shape.append(jax.ShapeDtypeStruct((nb, DEPTH, seq) + tail, F32))
    if stacked:
        in_specs += [row(w) for w in _STATE_WIDTHS]
        args += list(prev_states)
    outs = pl.pallas_call(
        functools.partial(_ctx_attn_kernel, stacked=stacked),
        grid=(nb,),
        in_specs=in_specs, out_specs=out_specs, out_shape=out_shape,
        compiler_params=_cparams(("arbitrary",)),
        name="ctx_attention",
    )(*args)
    return outs[0], outs[1:]


MLA_QT = 512
MLA_KC = 256
MLA_DK = MLA_NOPE + MLA_ROPE


def _lat_mla_kernel(p_ref, cckv_ref, ckr_ref, qn_ref, kvn_ref, wq_ref, wkv_ref, tq_ref, tk_ref,
                    o_ref, k_scr, v_scr, *, past, seq):
    qt = pl.program_id(1)

    def put_keys(r0, n, kv, kr):
        krb = kr.astype(BF16)
        for h in range(MLA_HEADS):
            k_scr[h, r0:r0 + n, 0:MLA_NOPE] = kv[:, h * 256:h * 256 + MLA_NOPE].astype(BF16)
            k_scr[h, r0:r0 + n, MLA_NOPE:MLA_DK] = krb
            v_scr[r0:r0 + n, h * MLA_V:(h + 1) * MLA_V] = kv[:, h * 256 + MLA_NOPE:(h + 1) * 256].astype(BF16)

    @pl.when(qt == 0)
    def _():
        for c in range(past // MLA_KC):
            r0 = c * MLA_KC
            kv = _dot(cckv_ref[r0:r0 + MLA_KC, :].astype(BF16), wkv_ref[...])
            put_keys(r0, MLA_KC, kv, ckr_ref[r0:r0 + MLA_KC, :])
        for c in range(seq // MLA_KC):
            r0 = c * MLA_KC
            ckv = _rms(p_ref[r0:r0 + MLA_KC, MLA_Q_RANK:MLA_Q_RANK + MLA_KV_RANK], kvn_ref[...])
            kv = _dot(ckv.astype(BF16), wkv_ref[...])
            krp = p_ref[r0:r0 + MLA_KC, 2 * MLA_Q_RANK:2 * MLA_Q_RANK + LANES]
            kr = _rope(krp, tk_ref[r0:r0 + MLA_KC, :], MLA_ROPE // 4)
            put_keys(past + r0, MLA_KC, kv, kr[:, 0:MLA_ROPE])

    r0 = pl.multiple_of(qt * MLA_QT, MLA_QT)
    qd = p_ref[pl.ds(r0, MLA_QT), 0:MLA_Q_RANK]
    q = _dot(_rms(qd, qn_ref[...]).astype(BF16), wq_ref[...])
    nope_w = MLA_HEADS * MLA_NOPE
    q_rope = _rope(q[:, nope_w:], tq_ref[pl.ds(r0, MLA_QT), :], MLA_ROPE // 4)
    scale = MLA_DK ** -0.5

    def score(h):
        qh = jnp.concatenate([q[:, h * MLA_NOPE:(h + 1) * MLA_NOPE],
                              q_rope[:, h * MLA_ROPE:(h + 1) * MLA_ROPE]], axis=1).astype(BF16)
        return _dot_nt(qh, k_scr[h]) * scale

    _attend_heads(MLA_HEADS, score, lambda h: v_scr[:, h * MLA_V:(h + 1) * MLA_V], o_ref, 0, group=2)


def _lat_mla(p, row_blk0, nb, seq, cache_ckv, cache_kr, layer, qn, kvn, wq, wkv, tq, tk):
    past = cache_ckv.shape[2]
    full = lambda a: pl.BlockSpec(a.shape, lambda b, t: (0,) * a.ndim)
    nqt = seq // MLA_QT
    return pl.pallas_call(
        functools.partial(_lat_mla_kernel, past=past, seq=seq),
        grid=(nb, nqt),
        in_specs=[
            pl.BlockSpec((seq, P_TN), lambda b, t: (row_blk0 + b, P_QD // P_TN)),
            pl.BlockSpec((None, None, past, MLA_KV_RANK), lambda b, t: (b, layer, 0, 0)),
            pl.BlockSpec((None, None, past, MLA_ROPE), lambda b, t: (b, layer, 0, 0)),
            full(qn), full(kvn), full(wq), full(wkv), full(tq), full(tk),
        ],
        out_specs=pl.BlockSpec((MLA_QT, BRANCH_W), lambda b, t: (b * nqt + t, 0)),
        out_shape=jax.ShapeDtypeStruct((nb * seq, BRANCH_W), BF16),
        scratch_shapes=[pltpu.VMEM((MLA_HEADS, past + seq, MLA_DK), BF16),
                        pltpu.VMEM((past + seq, MLA_HEADS * MLA_V), BF16)],
        compiler_params=_cparams(("arbitrary", "arbitrary")),
        name="lat_mla",
    )(p, cache_ckv, cache_kr, qn, kvn, wq, wkv, tq, tk)


WIN_QB = 128


def _lat_win_kernel(q_ref, k0_ref, k1_ref, k2_ref, v0_ref, v1_ref, v2_ref, ck_ref, cv_ref,
                    t0_ref, t1_ref, t2_ref, sink_ref, o_ref, *, past, seq):
    qb = pl.program_id(1)
    scale = WIN_HEAD_DIM ** -0.5
    n_loc = 3 * WIN_QB
    rows = WIN_GROUP * WIN_QB
    qpos = qb * WIN_QB + lax.broadcasted_iota(jnp.int32, (rows, n_loc), 0) % WIN_QB
    kpos = (qb - 1) * WIN_QB + lax.broadcasted_iota(jnp.int32, (rows, n_loc), 1)
    valid = (kpos >= 0) & (kpos < seq) & (jnp.abs(qpos - kpos) <= WINDOW)
    t1 = t1_ref[...]
    q_heads = lambda kvh: range(kvh * WIN_GROUP, (kvh + 1) * WIN_GROUP)

    def score(kvh):
        cs = slice(kvh * 128, (kvh + 1) * 128)
        keys = jnp.concatenate([
            ck_ref[:, cs],
            _rope(k0_ref[:, cs], t0_ref[...], 32),
            _rope(k1_ref[:, cs], t1, 32),
            _rope(k2_ref[:, cs], t2_ref[...], 32)], axis=0).astype(BF16)
        q = jnp.concatenate([_rope(q_ref[:, h * 128:(h + 1) * 128], t1, 32) for h in q_heads(kvh)],
                            axis=0).astype(BF16)
        s = _dot_nt(q, keys) * scale
        return jnp.concatenate([s[:, :past], jnp.where(valid, s[:, past:], NEG_INF)], axis=1)

    def value(kvh):
        cs = slice(kvh * 128, (kvh + 1) * 128)
        return jnp.concatenate([cv_ref[:, cs], v0_ref[:, cs], v1_ref[:, cs], v2_ref[:, cs]], axis=0).astype(BF16)

    def sink(kvh):
        return jnp.concatenate([jnp.broadcast_to(sink_ref[0:1, h:h + 1], (WIN_QB, 1)) for h in q_heads(kvh)],
                               axis=0)

    def emit(kvh, out):
        for g, h in enumerate(q_heads(kvh)):
            o_ref[:, h * 128:(h + 1) * 128] = out[g * WIN_QB:(g + 1) * WIN_QB, :].astype(BF16)

    _attend_heads(WIN_KV_HEADS, score, value, o_ref, 0, sink=sink, emit=emit)


def _lat_win(p, row0, nb, seq, cache_k, cache_v, layer, tab, sink):
    past = cache_k.shape[2]
    nqb = seq // WIN_QB
    rb0 = row0 // WIN_QB

    def kblk(off, col_blk):
        return pl.BlockSpec(
            (WIN_QB, 256),
            lambda b, t: (rb0 + b * nqb + jnp.clip(t + off, 0, nqb - 1), col_blk))

    def tblk(off):
        return pl.BlockSpec((WIN_QB, 3 * 128), lambda b, t: (jnp.clip(t + off, 0, nqb - 1), 0))

    cache = pl.BlockSpec((None, None, past, 256), lambda b, t: (b, layer, 0, 0))
    return pl.pallas_call(
        functools.partial(_lat_win_kernel, past=past, seq=seq),
        grid=(nb, nqb),
        in_specs=[
            pl.BlockSpec((WIN_QB, 1024), lambda b, t: (rb0 + b * nqb + t, P_WQ // 1024)),
            kblk(-1, P_WK // 256), kblk(0, P_WK // 256), kblk(1, P_WK // 256),
            kblk(-1, P_WV // 256), kblk(0, P_WV // 256), kblk(1, P_WV // 256),
            cache, cache,
            tblk(-1), tblk(0), tblk(1),
            pl.BlockSpec(sink.shape, lambda b, t: (0, 0)),
        ],
        out_specs=pl.BlockSpec((WIN_QB, BRANCH_W), lambda b, t: (b * nqb + t, 0)),
        out_shape=jax.ShapeDtypeStruct((nb * seq, BRANCH_W), BF16),
        compiler_params=_cparams(("arbitrary", "arbitrary")),
        name="lat_window",
    )(p, p, p, p, p, p, p, cache_k, cache_v, tab, tab, tab, sink)


def _lat_nat_kernel(q_ref, k0_ref, k1_ref, k2_ref, v0_ref, v1_ref, v2_ref, ck_ref, cv_ref, bias_ref, o_ref,
                    *, grid_rows):
    g = pl.program_id(0)
    scale = NAT_HEAD_DIM ** -0.5
    past = ck_ref.shape[0]
    win_rows = min(NAT_ROWS, grid_rows)
    lane = lax.broadcasted_iota(jnp.int32, (GRID_W, 2 * GRID_W), 1)

    def rows_of(h, refs):
        cs = slice(h * 128, (h + 1) * 128)
        return jnp.concatenate([r[:, cs] for r in refs], axis=0).astype(BF16)

    def pair_plan(rq, rkp):
        r = NAT_QROWS * g + rq
        kr0 = NAT_QROWS * _nat_key_start(g) + 2 * rkp
        rs = jnp.clip(r - win_rows // 2, 0, grid_rows - win_rows)
        in_win = lambda kr: ((kr >= rs) & (kr < rs + win_rows)).astype(jnp.int32)
        ok = jnp.where(lane < GRID_W, in_win(kr0), in_win(kr0 + 1)) > 0
        return jnp.clip(kr0 - r + NAT_ROWS - 1, 0, 2 * NAT_ROWS - 1), ok

    plans = [[pair_plan(rq, rkp) for rkp in range(NAT_KROWS // 2)] for rq in range(NAT_QROWS)]

    def bias(h):
        return jnp.concatenate(
            [jnp.concatenate([jnp.where(ok, bias_ref[h, a], NEG_INF) for a, ok in row], axis=1) for row in plans],
            axis=0)

    def score(h):
        q = q_ref[:, h * 128:(h + 1) * 128].astype(BF16)
        s = _dot_nt(q, rows_of(h, (ck_ref, k0_ref, k1_ref, k2_ref))) * scale
        return jnp.concatenate([s[:, :past], s[:, past:] + bias(h)], axis=1)

    _attend_heads(NAT_HEADS, score, lambda h: rows_of(h, (cv_ref, v0_ref, v1_ref, v2_ref)), o_ref, 0, group=2)


def _nat_key_start(g):
    return g // 2


def _lat_nat(p, row0, nb, seq, cache_k, cache_v, layer, bias):
    past = cache_k.shape[2]
    qrows = NAT_QROWS * GRID_W
    ng = seq // qrows
    rb0 = row0 // qrows

    def kblk(off, col_blk):
        return pl.BlockSpec((qrows, 1024), lambda g, b: (rb0 + b * ng + _nat_key_start(g) + off, col_blk))

    cache = pl.BlockSpec((None, None, past, 1024), lambda g, b: (b, layer, 0, 0))
    return pl.pallas_call(
        functools.partial(_lat_nat_kernel, grid_rows=seq // GRID_W),
        grid=(ng, nb),
        in_specs=[
            pl.BlockSpec((qrows, 1024), lambda g, b: (rb0 + b * ng + g, P_NQ // 1024)),
            kblk(0, P_NK // 1024), kblk(1, P_NK // 1024), kblk(2, P_NK // 1024),
            kblk(0, P_NV // 1024), kblk(1, P_NV // 1024), kblk(2, P_NV // 1024),
            cache, cache,
            pl.BlockSpec(bias.shape, lambda g, b: (0, 0, 0, 0)),
        ],
        out_specs=pl.BlockSpec((qrows, BRANCH_W), lambda g, b: (b * ng + g, 0)),
        out_shape=jax.ShapeDtypeStruct((nb * seq, BRANCH_W), BF16),
        compiler_params=_cparams(("arbitrary", "arbitrary")),
        name="lat_neighbourhood",
    )(p, p, p, p, p, p, p, cache_k, cache_v, bias)


def _nat_bias_blocks(rpb):
    n_dr, n_dc = 2 * NAT_ROWS - 1, 2 * NAT_COLS - 1
    cc = np.arange(GRID_W)
    dc = cc[None, :] - cc[:, None] + NAT_COLS - 1
    cs = np.clip(cc - NAT_COLS // 2, 0, GRID_W - NAT_COLS)[:, None]
    col_ok = (cc[None, :] >= cs) & (cc[None, :] < cs + NAT_COLS)
    onehot = ((dc[None] == np.arange(n_dc)[:, None, None]) & col_ok[None]).astype(np.float32)
    toep = jnp.einsum('had,dck->hack', rpb.astype(F32), jnp.asarray(onehot), precision=lax.Precision.HIGHEST)
    toep = jnp.where(jnp.asarray(col_ok), toep, NEG_INF)
    fill = jnp.full((NAT_HEADS, 2 * NAT_ROWS + 1 - n_dr, GRID_W, GRID_W), NEG_INF, F32)
    toep = jnp.concatenate([toep, fill], axis=1)
    return jnp.concatenate([toep[:, :-1], toep[:, 1:]], axis=-1)


def _merge_kernel(hn_ref, octx_ref, oa_ref, ob_ref, oc_ref, wga_ref, wgb_ref, wgc_ref,
                  wba_ref, wbb_ref, wbc_ref, m_ref, *, ctx_tiles):
    i = pl.program_id(0)
    weights = ((wga_ref, wba_ref), (wgb_ref, wbb_ref), (wgc_ref, wbc_ref))

    def merged(branch):
        hn = hn_ref[...]
        acc = None
        for k, (wg, wb) in enumerate(weights):
            term = jax.nn.sigmoid(_dot_nt(hn, wg[...])) * _dot(branch(k), wb[...])
            acc = term if acc is None else acc + term
        m_ref[...] = acc.astype(BF16)

    @pl.when(i < ctx_tiles)
    def _():
        merged(lambda k: octx_ref[:, k * BRANCH_W:(k + 1) * BRANCH_W])

    @pl.when(i >= ctx_tiles)
    def _():
        lat = (oa_ref, ob_ref, oc_ref)
        merged(lambda k: lat[k][...])


def _merge(hn, o_ctx, o_a, o_b, o_c, w_gate, w_br, layer):
    rows, d = hn.shape
    tm, tn = ROW_TILE, 256
    nj = d // tn
    ctx_tiles = o_ctx.shape[0] // tm
    gate = lambda k: pl.BlockSpec((None, tn, d), lambda i, j: (layer, k * nj + j, 0))
    br = lambda k: pl.BlockSpec((None, None, BRANCH_W, tn), lambda i, j: (layer, k, 0, j))
    lat = pl.BlockSpec((tm, BRANCH_W), lambda i, j: (jnp.maximum(i - ctx_tiles, 0), 0))
    return pl.pallas_call(
        functools.partial(_merge_kernel, ctx_tiles=ctx_tiles),
        grid=(rows // tm, nj),
        in_specs=[pl.BlockSpec((tm, d), lambda i, j: (i, 0)),
                  pl.BlockSpec((tm, 3 * BRANCH_W), lambda i, j: (jnp.minimum(i, ctx_tiles - 1), 0)),
                  lat, lat, lat,
                  gate(0), gate(1), gate(2), br(0), br(1), br(2)],
        out_specs=pl.BlockSpec((tm, tn), lambda i, j: (i, j)),
        out_shape=jax.ShapeDtypeStruct((rows, d), BF16),
        compiler_params=_cparams(("arbitrary", "arbitrary")),
        name="merge",
    )(hn, o_ctx, o_a, o_b, o_c, w_gate, w_gate, w_gate, w_br, w_br, w_br)


ROUTE_IDX = 8
ROUTE_W = 10


def _out_proj_kernel(m_ref, w_ref, mod_ref, g_ref, *rest, routed, ctx_tiles):
    nh = 1 if ctx_tiles is None else 2
    h_refs, rest = rest[:nh], rest[nh:]
    if routed:
        wr_ref, h1_ref, hn2_ref, route_ref = rest
    else:
        h1_ref, hn2_ref = rest
    h1 = _row_group_value(h_refs, ctx_tiles) + mod_ref[2:3, :] * _dot(m_ref[...], w_ref[...])
    h1_ref[...] = h1
    hn2 = _rms(h1, g_ref[...]) * (1.0 + mod_ref[4:5, :]) + mod_ref[3:4, :]
    hn2_ref[...] = hn2.astype(hn2_ref.dtype)
    if routed:
        logits = _dot(hn2.astype(BF16), wr_ref[...])
        lane = lax.broadcasted_iota(jnp.int32, logits.shape, 1).astype(F32)
        lg = jnp.where(lane < N_EXPERTS, logits, -jnp.inf)
        m1 = jnp.max(lg, axis=-1, keepdims=True)
        i1 = jnp.min(jnp.where(lg == m1, lane, float(LANES)), axis=-1, keepdims=True)
        lg2 = jnp.where(lane == i1, -jnp.inf, lg)
        m2 = jnp.max(lg2, axis=-1, keepdims=True)
        i2 = jnp.min(jnp.where(lg2 == m2, lane, float(LANES)), axis=-1, keepdims=True)
        e2 = jnp.exp(m2 - m1)
        w1 = 1.0 / (1.0 + e2)
        w2 = e2 / (1.0 + e2)
        route = (jnp.where(lane == ROUTE_IDX, i1, 0.0)
                 + jnp.where(lane == ROUTE_IDX + 1, i2, 0.0)
                 + jnp.where(lane == ROUTE_W, w1, 0.0)
                 + jnp.where(lane == ROUTE_W + 1, w2, 0.0))
        route_ref[...] = route


def _out_proj(m, w_out, h, mod, g, layer, cond_of_tile, w_router=None):
    rows, d = m.shape
    tm = 512
    sub = ROW_TILE // tm
    routed = w_router is not None
    h_args, h_specs, ctx_tiles = _split_rows_specs(h, tm)
    in_specs = [
        pl.BlockSpec((tm, d), lambda i: (i, 0)),
        pl.BlockSpec((None, d, d), lambda i: (layer, 0, 0)),
        pl.BlockSpec((None, 6, d), lambda i: (cond_of_tile(i // sub), 0, 0)),
        pl.BlockSpec((1, d), lambda i: (0, 0)),
    ] + h_specs
    out_specs = [pl.BlockSpec((tm, d), lambda i: (i, 0)), pl.BlockSpec((tm, d), lambda i: (i, 0))]
    out_shape = [jax.ShapeDtypeStruct((rows, d), F32), jax.ShapeDtypeStruct((rows, d), BF16)]
    args = [m, w_out, mod, g] + h_args
    if routed:
        out_shape[1] = jax.ShapeDtypeStruct((rows, d), F32)
        in_specs.append(pl.BlockSpec((d, LANES), lambda i: (0, 0)))
        out_specs.append(pl.BlockSpec((tm, LANES), lambda i: (i, 0)))
        out_shape.append(jax.ShapeDtypeStruct((rows, LANES), F32))
        args.append(w_router)
    return pl.pallas_call(
        functools.partial(_out_proj_kernel, routed=routed, ctx_tiles=ctx_tiles),
        grid=(rows // tm,),
        in_specs=in_specs, out_specs=out_specs, out_shape=out_shape,
        compiler_params=_cparams(("arbitrary",)),
        name="out_proj",
    )(*args)


def _swiglu_chunk(x, wg_ref, wu_ref, wd_ref):
    g = _dot(x, wg_ref[...].astype(BF16))
    u = _dot(x, wu_ref[...].astype(BF16))
    a = (g * jax.nn.sigmoid(g) * u).astype(BF16)
    return _dot(a, wd_ref[...].astype(BF16))


FFN_PREFETCH_STEP = 5


def _ffn_kernel(x_ref, wg_ref, wu_ref, wd_ref, h_ref, mod_ref, o_ref, acc_ref, h_sem, wb_sem):
    i = pl.program_id(0)
    f = pl.program_id(1)
    nt = pl.num_programs(0)
    last_f = pl.num_programs(1) - 1
    tm = acc_ref.shape[1]
    slot = i % 2

    def rows_of(tile):
        return pl.ds(pl.multiple_of(tile * tm, tm), tm)

    def h_load(tile, s):
        return pltpu.make_async_copy(h_ref.at[rows_of(tile), :], acc_ref.at[s], h_sem.at[s])

    def writeback(tile, s):
        return pltpu.make_async_copy(acc_ref.at[s], o_ref.at[rows_of(tile), :], wb_sem.at[s])

    @pl.when((f == 0) & (i == 0))
    def _():
        h_load(0, 0).start()

    @pl.when(f == 0)
    def _():
        h_load(i, slot).wait()

    @pl.when((f == FFN_PREFETCH_STEP) & (i >= 1))
    def _():
        writeback(i - 1, 1 - slot).wait()

    @pl.when((f == FFN_PREFETCH_STEP) & (i + 1 < nt))
    def _():
        h_load(i + 1, 1 - slot).start()

    acc_ref[slot] += mod_ref[5:6, :] * _swiglu_chunk(x_ref[...], wg_ref, wu_ref, wd_ref)

    @pl.when(f == last_f)
    def _():
        writeback(i, slot).start()

    @pl.when((f == last_f) & (i == nt - 1))
    def _():
        writeback(i, slot).wait()


def _ffn(x, wg, wu, wd, h, mod, cond_of_tile):
    rows, d = h.shape
    ff = wg.shape[1]
    tm, tf = FFN_TM, FFN_TF
    sub = ROW_TILE // tm
    assert ff // tf > FFN_PREFETCH_STEP
    return pl.pallas_call(
        _ffn_kernel,
        grid=(rows // tm, ff // tf),
        in_specs=[
            pl.BlockSpec((tm, d), lambda i, f: (i, 0)),
            pl.BlockSpec((d, tf), lambda i, f: (0, f)),
            pl.BlockSpec((d, tf), lambda i, f: (0, f)),
            pl.BlockSpec((tf, d), lambda i, f: (f, 0)),
            pl.BlockSpec(memory_space=pl.ANY),
            pl.BlockSpec((None, 6, d), lambda i, f: (cond_of_tile(i // sub), 0, 0)),
        ],
        out_specs=pl.BlockSpec(memory_space=pl.ANY),
        out_shape=jax.ShapeDtypeStruct((rows, d), F32),
        scratch_shapes=[pltpu.VMEM((2, tm, d), F32), pltpu.SemaphoreType.DMA((2,)),
                        pltpu.SemaphoreType.DMA((2,))],
        compiler_params=_cparams(("arbitrary", "arbitrary")),
        name="ffn_dense",
    )(x, wg, wu, wd, h, mod)


def _start_row_gather(src_ref, idx_ref, dst_ref, sem, n):
    def issue(r, carry):
        pltpu.make_async_copy(src_ref.at[pl.ds(idx_ref[0, r], 1), :], dst_ref.at[pl.ds(r, 1), :], sem).start()
        return carry

    lax.fori_loop(0, n, issue, 0, unroll=8)


def _wait_row_gather(src_ref, dst_ref, sem, n):
    pltpu.make_async_copy(src_ref.at[pl.ds(0, n), :], dst_ref, sem).wait()


def _gather_kernel(tr_ref, idx_ref, idx_next_ref, src_ref, o_ref, buf, sem):
    i = pl.program_id(0)
    n = o_ref.shape[0]
    per_tile = MOE_TM // n
    slot = i % 2

    def used(step):
        return tr_ref[step // per_tile] > (step % per_tile) * n

    @pl.when((i == 0) & used(0))
    def _():
        _start_row_gather(src_ref, idx_ref, buf.at[0], sem.at[0], n)

    nxt = jnp.minimum(i + 1, pl.num_programs(0) - 1)

    @pl.when((i + 1 < pl.num_programs(0)) & used(nxt))
    def _():
        _start_row_gather(src_ref, idx_next_ref, buf.at[1 - slot], sem.at[1 - slot], n)

    @pl.when(used(i))
    def _():
        _wait_row_gather(src_ref, buf.at[slot], sem.at[slot], n)
        o_ref[...] = buf[slot].astype(o_ref.dtype)

    @pl.when(jnp.logical_not(used(i)))
    def _():
        o_ref[...] = jnp.zeros_like(o_ref)


def _gather_rows(src, idx, tile_rows, out_dtype):
    n = idx.shape[0]
    d = src.shape[1]
    tg = GATHER_ROWS
    steps = n // tg
    idx3 = idx.reshape(steps, 1, tg)
    grid_spec = pltpu.PrefetchScalarGridSpec(
        num_scalar_prefetch=1,
        grid=(steps,),
        in_specs=[pl.BlockSpec((None, 1, tg), lambda i, tr: (i, 0, 0), memory_space=pltpu.SMEM),
                  pl.BlockSpec((None, 1, tg), lambda i, tr: (jnp.minimum(i + 1, steps - 1), 0, 0),
                               memory_space=pltpu.SMEM),
                  pl.BlockSpec(memory_space=pl.ANY)],
        out_specs=pl.BlockSpec((tg, d), lambda i, tr: (i, 0)),
        scratch_shapes=[pltpu.VMEM((2, tg, d), src.dtype), pltpu.SemaphoreType.DMA((2,))],
    )
    return pl.pallas_call(
        _gather_kernel,
        grid_spec=grid_spec,
        out_shape=jax.ShapeDtypeStruct((n, d), out_dtype),
        compiler_params=_cparams(("arbitrary",)),
        name="moe_gather",
    )(tile_rows, idx3, idx3, src)


def _moe_ffn_kernel(te_ref, tv_ref, x_ref, wg_ref, wu_ref, wd_ref, o_ref, acc_ref, sem):
    i = pl.program_id(0)
    f = pl.program_id(1)
    last_f = pl.num_programs(1) - 1
    rows_used = tv_ref[i]

    def writeback(tile, s):
        rows = pl.ds(pl.multiple_of(tile * MOE_TM + s * MOE_SLAB, MOE_SLAB), MOE_SLAB)
        return pltpu.make_async_copy(acc_ref.at[pl.ds(s * MOE_SLAB, MOE_SLAB), :], o_ref.at[rows, :], sem.at[s])

    for s in range(MOE_TM // MOE_SLAB):
        rs = slice(s * MOE_SLAB, (s + 1) * MOE_SLAB)
        used = rows_used > s * MOE_SLAB

        @pl.when((f == 0) & (i > 0))
        def _():
            writeback(i - 1, s).wait()

        @pl.when(f == 0)
        def _():
            acc_ref[rs, :] = jnp.zeros((MOE_SLAB, acc_ref.shape[1]), acc_ref.dtype)

        @pl.when(used)
        def _():
            acc_ref[rs, :] += _swiglu_chunk(x_ref[rs, :], wg_ref, wu_ref, wd_ref)

        @pl.when(f == last_f)
        def _():
            writeback(i, s).start()

    @pl.when((f == last_f) & (i == pl.num_programs(0) - 1))
    def _():
        for s in range(MOE_TM // MOE_SLAB):
            writeback(i, s).wait()


def _moe_ffn(xs, wg, wu, wd, tile_expert, tile_valid):
    n, d = xs.shape
    ff = wg.shape[2]
    tm, tf = MOE_TM, MOE_TF
    nf = ff // tf

    def fidx(i, f, tv):
        return jnp.where(tv[i] > 0, f, nf - 1)

    grid_spec = pltpu.PrefetchScalarGridSpec(
        num_scalar_prefetch=2,
        grid=(n // tm, nf),
        in_specs=[
            pl.BlockSpec((tm, d), lambda i, f, te, tv: (i, 0)),
            pl.BlockSpec((None, d, tf), lambda i, f, te, tv: (te[i], 0, fidx(i, f, tv))),
            pl.BlockSpec((None, d, tf), lambda i, f, te, tv: (te[i], 0, fidx(i, f, tv))),
            pl.BlockSpec((None, tf, d), lambda i, f, te, tv: (te[i], fidx(i, f, tv), 0)),
        ],
        out_specs=pl.BlockSpec(memory_space=pl.ANY),
        scratch_shapes=[pltpu.VMEM((tm, d), F32), pltpu.SemaphoreType.DMA((tm // MOE_SLAB,))],
    )
    return pl.pallas_call(
        _moe_ffn_kernel,
        grid_spec=grid_spec,
        out_shape=jax.ShapeDtypeStruct((n, d), F32),
        compiler_params=_cparams(("arbitrary", "arbitrary")),
        name="moe_ffn",
    )(tile_expert, tile_valid, xs, wg, wu, wd)


def _combine_kernel(p0_ref, p1_ref, p0_next_ref, p1_next_ref, ys_ref, route_ref, h_ref, mod_ref, fn_ref,
                    octx_ref, olat_ref, buf0, buf1, sem, *, ctx_steps):
    i = pl.program_id(0)
    n = octx_ref.shape[0]
    slot = i % 2

    def start(q0_ref, q1_ref, s):
        _start_row_gather(ys_ref, q0_ref, buf0.at[s], sem.at[s, 0], n)
        _start_row_gather(ys_ref, q1_ref, buf1.at[s], sem.at[s, 1], n)

    @pl.when(i == 0)
    def _():
        start(p0_ref, p1_ref, 0)

    @pl.when(i + 1 < pl.num_programs(0))
    def _():
        start(p0_next_ref, p1_next_ref, 1 - slot)

    _wait_row_gather(ys_ref, buf0.at[slot], sem.at[slot, 0], n)
    _wait_row_gather(ys_ref, buf1.at[slot], sem.at[slot, 1], n)
    w0 = route_ref[:, ROUTE_W:ROUTE_W + 1]
    w1 = route_ref[:, ROUTE_W + 1:ROUTE_W + 2]
    y = w0 * buf0[slot] + w1 * buf1[slot]
    h2 = h_ref[...] + mod_ref[5:6, :] * y
    out = _rms(h2, fn_ref[...])

    @pl.when(i < ctx_steps)
    def _():
        octx_ref[...] = out

    @pl.when(i >= ctx_steps)
    def _():
        olat_ref[...] = out


def _moe_combine(ys, pos0, pos1, route, h, mod, final_norm, cond_of_tile, rows_ctx):
    rows, d = h.shape
    tc = GATHER_ROWS
    sub = ROW_TILE // tc
    ctx_steps = rows_ctx // tc
    steps = rows // tc
    smem = lambda: pl.BlockSpec((None, 1, tc), lambda i: (i, 0, 0), memory_space=pltpu.SMEM)
    smem_next = lambda: pl.BlockSpec((None, 1, tc), lambda i: (jnp.minimum(i + 1, steps - 1), 0, 0),
                                     memory_space=pltpu.SMEM)
    pos0, pos1 = pos0.reshape(steps, 1, tc), pos1.reshape(steps, 1, tc)
    return pl.pallas_call(
        functools.partial(_combine_kernel, ctx_steps=ctx_steps),
        grid=(steps,),
        in_specs=[smem(), smem(), smem_next(), smem_next(),
                  pl.BlockSpec(memory_space=pl.ANY),
                  pl.BlockSpec((tc, LANES), lambda i: (i, 0)),
                  pl.BlockSpec((tc, d), lambda i: (i, 0)),
                  pl.BlockSpec((None, 6, d), lambda i: (cond_of_tile(i // sub), 0, 0)),
                  pl.BlockSpec((1, d), lambda i: (0, 0))],
        out_specs=[pl.BlockSpec((tc, d), lambda i: (jnp.minimum(i, ctx_steps - 1), 0)),
                   pl.BlockSpec((tc, d), lambda i: (jnp.maximum(i - ctx_steps, 0), 0))],
        out_shape=[jax.ShapeDtypeStruct((rows_ctx, d), F32), jax.ShapeDtypeStruct((rows - rows_ctx, d), F32)],
        scratch_shapes=[pltpu.VMEM((2, tc, d), F32), pltpu.VMEM((2, tc, d), F32),
                        pltpu.SemaphoreType.DMA((2, 2))],
        compiler_params=_cparams(("arbitrary",)),
        name="moe_combine",
    )(pos0, pos1, pos0, pos1, ys, route, h, mod, final_norm)


def _dispatch_plan(route, tm):
    rows = route.shape[0]
    ids = route[:, ROUTE_IDX:ROUTE_IDX + TOP_K].astype(jnp.int32)
    flat = ids.reshape(-1)
    onehot = (flat[:, None] == jnp.arange(N_EXPERTS)[None, :]).astype(jnp.int32)
    rank = jnp.sum((jnp.cumsum(onehot, axis=0) - onehot) * onehot, axis=1)
    counts = jnp.sum(onehot, axis=0)
    tiles = (counts + tm - 1) // tm
    tile_end = jnp.cumsum(tiles)
    start = (tile_end - tiles) * tm
    slot = start[flat] + rank
    n_tiles = (rows * TOP_K) // tm + N_EXPERTS
    n_slots = n_tiles * tm
    slot_token = (jnp.arange(n_slots, dtype=jnp.int32) % rows).at[slot].set(
        jnp.arange(rows * TOP_K, dtype=jnp.int32) // TOP_K)
    t = jnp.arange(n_tiles)
    tile_expert = jnp.minimum(jnp.sum((t[:, None] >= tile_end[None, :]).astype(jnp.int32), axis=1), N_EXPERTS - 1)
    tile_in_expert = t - (tile_end - tiles)[tile_expert]
    tile_rows = jnp.where(t < tile_end[-1], jnp.clip(counts[tile_expert] - tile_in_expert * tm, 0, tm), 0)
    last_expert = tile_expert[jnp.maximum(tile_end[-1] - 1, 0)]
    tile_expert = jnp.where(tile_rows > 0, tile_expert, last_expert).astype(jnp.int32)
    pos = slot.reshape(rows, TOP_K)
    return slot_token, tile_expert, tile_rows.astype(jnp.int32), pos[:, 0], pos[:, 1]


def _rope_table(seq, n):
    quarter = n // 4
    t = np.arange(seq)
    inv = jnp.power(ROPE_BASE, -jnp.arange(quarter, dtype=F32) * (2.0 / (n // 2)))
    ang_r = jnp.asarray(t // GRID_W, F32)[:, None] * inv[None, :]
    ang_c = jnp.asarray(t % GRID_W, F32)[:, None] * inv[None, :]
    zero = jnp.zeros((seq, quarter), F32)
    cos = jnp.concatenate([jnp.cos(ang_r)] * 2 + [jnp.cos(ang_c)] * 2, axis=1)
    up = jnp.concatenate([-jnp.sin(ang_r), zero, -jnp.sin(ang_c), zero], axis=1)
    dn = jnp.concatenate([zero, jnp.sin(ang_r), zero, jnp.sin(ang_c)], axis=1)
    return cos, up, dn


def _tile_cols(parts, reps, pad_to=None):
    out = [jnp.tile(p, (1, reps)) for p in parts]
    if pad_to is not None:
        out = [jnp.pad(p, ((0, 0), (0, pad_to - p.shape[1]))) for p in out]
    return jnp.concatenate(out, axis=1)


def kernel(x_prompt, x_sample, cache_mla_ckv, cache_mla_krope, cache_win_k, cache_win_v, cache_nat_k, cache_nat_v, c, c_ctx, w_mod, b_mod, norm1, norm2, w_in, mla_q_norm, mla_kv_norm, w_mla_q_up, w_mla_kv_up, win_sink, nat_rpb, w_br_mla, w_br_win, w_br_nat, w_out, w_ff_gate, w_ff_up, w_ff_down, w_router, w_ex_gate, w_ex_up, w_ex_down, final_norm):
    nb_ctx, seq_ctx, d = x_prompt.shape
    nb_lat, seq_lat, _ = x_sample.shape
    past = cache_mla_ckv.shape[2]
    rows_ctx = nb_ctx * seq_ctx
    rows_lat = nb_lat * seq_lat
    assert d == D_MODEL and seq_lat == ROW_TILE and rows_ctx % ROW_TILE == 0
    assert seq_lat // GRID_W == 4 * NAT_QROWS
    assert w_mod.shape[0] == DEPTH == 2
    ctx_tiles = rows_ctx // ROW_TILE

    def cond_of_tile(i):
        return jnp.where(i < ctx_tiles, 0, i - ctx_tiles + 1)

    n_cond = 16
    cond = jnp.zeros((n_cond, d), F32).at[0].set(c_ctx).at[1:1 + nb_lat].set(c)
    mod_all = _modulation(cond, w_mod, b_mod).reshape(DEPTH, n_cond, 6, d)

    w_p, w_gate = _w_in_prep(jnp.swapaxes(w_in, 1, 2))
    wq_up = w_mla_q_up.reshape(DEPTH, MLA_Q_RANK, MLA_HEADS, MLA_NOPE + MLA_ROPE)
    wq_up = jnp.concatenate([wq_up[..., :MLA_NOPE].reshape(DEPTH, MLA_Q_RANK, -1),
                             wq_up[..., MLA_NOPE:].reshape(DEPTH, MLA_Q_RANK, -1)], axis=2).astype(BF16)
    wkv_up = w_mla_kv_up.astype(BF16)
    w_br = jnp.stack([w_br_mla, w_br_win, w_br_nat], axis=1).astype(BF16)
    w_out_b = w_out.astype(BF16)
    w_router_p = jnp.pad(w_router, ((0, 0), (0, 0), (0, LANES - N_EXPERTS))).astype(BF16)

    t128 = jnp.concatenate(_rope_table(seq_lat, 128), axis=1)
    t64 = _rope_table(seq_lat, MLA_ROPE)
    t_q = _tile_cols(t64, MLA_HEADS)
    t_k = _tile_cols(t64, 1, pad_to=LANES)

    ck_win = cache_win_k.reshape(nb_lat, DEPTH, past, 256)
    cv_win = cache_win_v.reshape(nb_lat, DEPTH, past, 256)
    ck_nat = cache_nat_k.reshape(nb_lat, DEPTH, past, 1024)
    cv_nat = cache_nat_v.reshape(nb_lat, DEPTH, past, 1024)


    h = (x_prompt.reshape(rows_ctx, d), x_sample.reshape(rows_lat, d))
    states = None
    for l in range(DEPTH):
        mod = mod_all[l]
        qn = mla_q_norm[l].reshape(1, -1)
        kvn = mla_kv_norm[l].reshape(1, -1)
        sink = win_sink[l].reshape(1, -1)
        p, hn = _mixer_in(h, mod, norm1[l].reshape(1, d), w_p, l, cond_of_tile)
        o_ctx, states = _ctx_attention(p, nb_ctx, seq_ctx, qn, kvn, wq_up[l], wkv_up[l], sink,
                                       prev_states=states)
        o_a = _lat_mla(p, rows_ctx // seq_lat, nb_lat, seq_lat, cache_mla_ckv, cache_mla_krope, l,
                       qn, kvn, wq_up[l], wkv_up[l], t_q, t_k)
        o_b = _lat_win(p, rows_ctx, nb_lat, seq_lat, ck_win, cv_win, l, t128, sink)
        o_c = _lat_nat(p, rows_ctx, nb_lat, seq_lat, ck_nat, cv_nat, l, _nat_bias_blocks(nat_rpb[l]))
        m = _merge(hn, o_ctx, o_a, o_b, o_c, w_gate, w_br, l)
        if l % 2 == 0:
            h1, hn2 = _out_proj(m, w_out_b, h, mod, norm2[l].reshape(1, d), l, cond_of_tile)
            h = _ffn(hn2, w_ff_gate[l // 2].astype(BF16), w_ff_up[l // 2].astype(BF16),
                     w_ff_down[l // 2].astype(BF16), h1, mod, cond_of_tile)
        else:
            h1, hn2, route = _out_proj(m, w_out_b, h, mod, norm2[l].reshape(1, d), l, cond_of_tile,
                                       w_router=w_router_p[l // 2])
            slot_token, tile_expert, tile_valid, pos0, pos1 = _dispatch_plan(route, MOE_TM)
            xs = _gather_rows(hn2, slot_token, tile_valid, BF16)
            ys = _moe_ffn(xs, w_ex_gate[l // 2], w_ex_up[l // 2], w_ex_down[l // 2], tile_expert, tile_valid)
            y_ctx, y_lat = _moe_combine(ys, pos0, pos1, route, h1, mod, final_norm.reshape(1, d),
                                        cond_of_tile, rows_ctx)

    y_prompt = y_ctx.reshape(nb_ctx, seq_ctx, d)
    y_sample = y_lat.reshape(nb_lat, seq_lat, d)
    return (y_prompt, y_sample) + tuple(states)
```

```python
import functools

import jax
import jax.numpy as jnp
import numpy as np
from jax import lax
from jax.experimental import pallas as pl
from jax.experimental.pallas import tpu as pltpu

F32 = jnp.float32
BF16 = jnp.bfloat16

D_MODEL = 2048
DEPTH = 2
GRID_W = 64
ROPE_BASE = 10000.0
EPS = 1e-6
NEG_INF = -1e30
MLA_HEADS = 8
MLA_Q_RANK = 512
MLA_KV_RANK = 512
MLA_NOPE = 128
MLA_ROPE = 64
MLA_V = 128
WIN_HEADS = 8
WIN_KV_HEADS = 2
WIN_GROUP = WIN_HEADS // WIN_KV_HEADS
WIN_HEAD_DIM = 128
WINDOW = 128
NAT_HEADS = 8
NAT_HEAD_DIM = 128
NAT_ROWS = 8
NAT_COLS = 16
BRANCH_W = 1024
D_FF = 5632
N_EXPERTS = 8
TOP_K = 2

P_WQ, P_NQ, P_NK, P_NV = 0, 1024, 2048, 3072
P_WK, P_WV = 4096, 4352
P_QD, P_KVD, P_KR = 4608, 5120, 5632
P_COLS = 5760
P_TN = 1152
MIXER_TN = 1280
LANES = 128

VMEM_LIMIT = 56 * 1024 * 1024

ROW_TILE = 1024
FFN_TM = 1024
FFN_TF = 512
MOE_TM = 2048
MOE_SLAB = 512
MOE_TF = 256
GATHER_ROWS = 512
NAT_QROWS = 4
NAT_KROWS = 12


def _cparams(sem):
    return pltpu.CompilerParams(dimension_semantics=sem, vmem_limit_bytes=VMEM_LIMIT)


def _rms(x, g):
    ms = jnp.mean(x * x, axis=-1, keepdims=True)
    return x * lax.rsqrt(ms + EPS) * g


def _dot(a, b):
    return jnp.dot(a, b, preferred_element_type=F32)


def _dot_nt(a, b):
    return lax.dot_general(a, b, (((1,), (1,)), ((), ())), preferred_element_type=F32)


def _rope(x, tab_ref_or_val, shift):
    n = x.shape[-1]
    t = tab_ref_or_val
    c, s_up, s_dn = t[:, 0:n], t[:, n:2 * n], t[:, 2 * n:3 * n]
    up = pltpu.roll(x, n - shift, axis=1)
    dn = pltpu.roll(x, shift, axis=1)
    return x * c + up * s_up + dn * s_dn


def _attend_heads(n_heads, score, value, o_ref, col0, sink=None, group=None, emit=None):
    group = group or n_heads
    for h0 in range(0, n_heads, group):
        heads = range(h0, h0 + group)
        s = [score(h) for h in heads]
        m = [jnp.max(x, axis=-1, keepdims=True) for x in s]
        if sink is not None:
            sk = [sink(h) for h in heads]
            m = [jnp.maximum(a, b) for a, b in zip(m, sk)]
        e = [jnp.exp(x - a) for x, a in zip(s, m)]
        den = [jnp.sum(x, axis=-1, keepdims=True) for x in e]
        if sink is not None:
            den = [d + jnp.exp(b - a) for d, a, b in zip(den, m, sk)]
        pv = [_dot(x.astype(BF16), value(h)) for x, h in zip(e, heads)]
        for i, h in enumerate(heads):
            out = pv[i] * (1.0 / den[i])
            if emit is not None:
                emit(h, out)
            else:
                o_ref[:, col0 + h * LANES:col0 + (h + 1) * LANES] = out.astype(o_ref.dtype)


def _mod_kernel(c_ref, w_ref, b_ref, o_ref):
    c = c_ref[...]
    s = (c * jax.nn.sigmoid(c)).astype(BF16)
    o_ref[...] = _dot(s, w_ref[...].astype(BF16)) + b_ref[...]


def _modulation(cond, w_mod, b_mod):
    depth, d, n = w_mod.shape
    nc = cond.shape[0]
    tn = 1024
    return pl.pallas_call(
        _mod_kernel,
        grid=(depth, n // tn),
        in_specs=[
            pl.BlockSpec((nc, d), lambda l, j: (0, 0)),
            pl.BlockSpec((None, d, tn), lambda l, j: (l, 0, j)),
            pl.BlockSpec((None, 1, tn), lambda l, j: (l, 0, j)),
        ],
        out_specs=pl.BlockSpec((None, nc, tn), lambda l, j: (l, 0, j)),
        out_shape=jax.ShapeDtypeStruct((depth, nc, n), F32),
        compiler_params=_cparams(("arbitrary", "arbitrary")),
        name="modulation",
    )(cond, w_mod, b_mod.reshape(depth, 1, n))


_W_IN_SEGMENTS = ((1088, 1024), (2624, 1024), (3648, 1024), (4672, 1024), (2112, 256), (2368, 256),
                  (0, 512), (512, 512), (1024, 64))
W_IN_GATES = 5696


def _w_in_prep_kernel(w_ref, p_ref, g_ref):
    row = 0
    for src, width in _W_IN_SEGMENTS:
        p_ref[row:row + width, :] = w_ref[src:src + width, :].astype(BF16)
        row += width
    p_ref[row:, :] = jnp.zeros((P_COLS - row, p_ref.shape[1]), BF16)
    g_ref[...] = w_ref[W_IN_GATES:, :].astype(BF16)


def _w_in_prep(w_in_t):
    depth, n, d = w_in_t.shape
    tk = 256
    n_gate = n - W_IN_GATES
    return pl.pallas_call(
        _w_in_prep_kernel,
        grid=(depth, d // tk),
        in_specs=[pl.BlockSpec((None, n, tk), lambda l, r: (l, 0, r))],
        out_specs=[pl.BlockSpec((None, P_COLS, tk), lambda l, r: (l, 0, r)),
                   pl.BlockSpec((None, n_gate, tk), lambda l, r: (l, 0, r))],
        out_shape=[jax.ShapeDtypeStruct((depth, P_COLS, d), BF16),
                   jax.ShapeDtypeStruct((depth, n_gate, d), BF16)],
        compiler_params=_cparams(("arbitrary", "arbitrary")),
        name="w_in_prep",
    )(w_in_t)


def _split_rows_specs(h, tm, single_buffer=False):
    if not isinstance(h, tuple):
        return [h], [pl.BlockSpec((tm, h.shape[1]), lambda i, *_: (i, 0))], None
    ctx_tiles = h[0].shape[0] // tm
    d = h[0].shape[1]
    mode = dict(pipeline_mode=pl.Buffered(1)) if single_buffer else {}
    return (list(h),
            [pl.BlockSpec((tm, d), lambda i, *_: (jnp.minimum(i, ctx_tiles - 1), 0), **mode),
             pl.BlockSpec((tm, d), lambda i, *_: (jnp.maximum(i - ctx_tiles, 0), 0), **mode)],
            ctx_tiles)


def _row_group_value(h_refs, ctx_tiles):
    if ctx_tiles is None:
        return h_refs[0][...]
    return jnp.where(pl.program_id(0) < ctx_tiles, h_refs[0][...], h_refs[1][...])


def _mixer_in_kernel(*refs, ctx_tiles, n_cols):
    nh = 1 if ctx_tiles is None else 2
    h_refs = refs[:nh]
    mod_ref, g_ref, w_ref, p_ref, hn_ref = refs[nh:]

    @pl.when(pl.program_id(1) == 0)
    def _():
        h = _row_group_value(h_refs, ctx_tiles)
        hn = _rms(h, g_ref[...]) * (1.0 + mod_ref[1:2, :]) + mod_ref[0:1, :]
        hn_ref[...] = hn.astype(BF16)

    j = pl.program_id(1)
    tn = p_ref.shape[1]
    tail = n_cols % tn

    @pl.when((j + 1) * tn <= n_cols)
    def _():
        p_ref[...] = _dot_nt(hn_ref[...], w_ref[...])

    if tail:
        @pl.when((j + 1) * tn > n_cols)
        def _():
            p_ref[:, :tail] = _dot_nt(hn_ref[...], w_ref[:tail, :])


def _mixer_in(h, mod, g, w, layer, cond_of_tile):
    tm, tn = ROW_TILE, MIXER_TN
    sub = ROW_TILE // tm
    h_args, h_specs, ctx_tiles = _split_rows_specs(h, tm, single_buffer=True)
    rows = sum(a.shape[0] for a in h_args)
    d = h_args[0].shape[1]
    n = w.shape[1]
    return pl.pallas_call(
        functools.partial(_mixer_in_kernel, ctx_tiles=ctx_tiles, n_cols=n),
        grid=(rows // tm, pl.cdiv(n, tn)),
        in_specs=h_specs + [
            pl.BlockSpec((None, 6, d), lambda i, j: (cond_of_tile(i // sub), 0, 0)),
            pl.BlockSpec((1, d), lambda i, j: (0, 0)),
            pl.BlockSpec((None, tn, d), lambda i, j: (layer, j, 0)),
        ],
        out_specs=[
            pl.BlockSpec((tm, tn), lambda i, j: (i, j)),
            pl.BlockSpec((tm, d), lambda i, j: (i, 0)),
        ],
        out_shape=[
            jax.ShapeDtypeStruct((rows, n), F32),
            jax.ShapeDtypeStruct((rows, d), BF16),
        ],
        compiler_params=_cparams(("arbitrary", "arbitrary")),
        name="mixer_in",
    )(*h_args, mod, g, w)


_STATE_WIDTHS = (MLA_KV_RANK, MLA_ROPE, 256, 256, 1024, 1024)
_STATE_HEADS = (None, None, WIN_KV_HEADS, WIN_KV_HEADS, NAT_HEADS, NAT_HEADS)


def _ctx_attn_kernel(p_ref, qn_ref, kvn_ref, wq_ref, wkv_ref, sink_ref, *refs, stacked):
    n_state = len(_STATE_WIDTHS)
    if stacked:
        prev_refs, o_ref, state_refs = refs[:n_state], refs[n_state], refs[n_state + 1:]
    else:
        prev_refs, o_ref, state_refs = None, refs[0], refs[1:]

    def put_state(k, val, slot=DEPTH - 1):
        ref, heads = state_refs[k], _STATE_HEADS[k]
        if not stacked:
            ref[...] = val
        elif heads is None:
            ref[slot] = val
        else:
            for j in range(heads):
                ref[slot, :, j, :] = val[:, j * LANES:(j + 1) * LANES]

    if stacked:
        for k in range(n_state):
            put_state(k, prev_refs[k][...], slot=0)

    q = _dot(_rms(p_ref[:, P_QD:P_QD + MLA_Q_RANK], qn_ref[...]).astype(BF16), wq_ref[...])
    ckv = _rms(p_ref[:, P_KVD:P_KVD + MLA_KV_RANK], kvn_ref[...])
    put_state(0, ckv)
    kv = _dot(ckv.astype(BF16), wkv_ref[...]).astype(BF16)
    kr = p_ref[:, P_KR:P_KR + MLA_ROPE]
    put_state(1, kr)
    krb = kr.astype(BF16)
    scale_a = (MLA_NOPE + MLA_ROPE) ** -0.5
    nope_w = MLA_HEADS * MLA_NOPE

    def score_a(h):
        qn = q[:, h * MLA_NOPE:(h + 1) * MLA_NOPE].astype(BF16)
        qr = q[:, nope_w + h * MLA_ROPE:nope_w + (h + 1) * MLA_ROPE].astype(BF16)
        return (_dot_nt(qn, kv[:, h * 256:h * 256 + MLA_NOPE]) + _dot_nt(qr, krb)) * scale_a

    _attend_heads(MLA_HEADS, score_a, lambda h: kv[:, h * 256 + MLA_NOPE:(h + 1) * 256], o_ref, 0)
    put_state(2, p_ref[:, P_WK:P_WK + 256])
    put_state(3, p_ref[:, P_WV:P_WV + 256])
    scale_b = WIN_HEAD_DIM ** -0.5

    def score_b(h):
        kvh = h // WIN_GROUP
        qh = p_ref[:, P_WQ + h * 128:P_WQ + (h + 1) * 128].astype(BF16)
        return _dot_nt(qh, p_ref[:, P_WK + kvh * 128:P_WK + (kvh + 1) * 128].astype(BF16)) * scale_b

    def value_b(h):
        kvh = h // WIN_GROUP
        return p_ref[:, P_WV + kvh * 128:P_WV + (kvh + 1) * 128].astype(BF16)

    _attend_heads(WIN_HEADS, score_b, value_b, o_ref, BRANCH_W, lambda h: sink_ref[0:1, h:h + 1])
    put_state(4, p_ref[:, P_NK:P_NK + 1024])
    put_state(5, p_ref[:, P_NV:P_NV + 1024])
    scale_c = NAT_HEAD_DIM ** -0.5

    def score_c(h):
        qh = p_ref[:, P_NQ + h * 128:P_NQ + (h + 1) * 128].astype(BF16)
        return _dot_nt(qh, p_ref[:, P_NK + h * 128:P_NK + (h + 1) * 128].astype(BF16)) * scale_c

    _attend_heads(NAT_HEADS, score_c, lambda h: p_ref[:, P_NV + h * 128:P_NV + (h + 1) * 128].astype(BF16),
                  o_ref, 2 * BRANCH_W)


def _ctx_attention(p, nb, seq, qn, kvn, wq, wkv, sink, prev_states=None):
    full = lambda a: pl.BlockSpec(a.shape, lambda b: (0,) * a.ndim)
    row = lambda w: pl.BlockSpec((seq, w), lambda b: (b, 0))
    stacked = prev_states is not None
    in_specs = [pl.BlockSpec((seq, P_COLS), lambda b: (b, 0)),
                full(qn), full(kvn), full(wq), full(wkv), full(sink)]
    args = [p, qn, kvn, wq, wkv, sink]
    out_specs = [row(3 * BRANCH_W)]
    out_shape = [jax.ShapeDtypeStruct((nb * seq, 3 * BRANCH_W), BF16)]
    for w, heads in zip(_STATE_WIDTHS, _STATE_HEADS):
        if not stacked:
            out_specs.append(row(w))
            out_shape.append(jax.ShapeDtypeStruct((nb * seq, w), F32))
        else:
            tail = (w,) if heads is None else (heads, w // heads)
            out_specs.append(pl.BlockSpec((None, DEPTH, seq) + tail, lambda b, n=len(tail): (b,) + (0,) * (n + 2)))
            out_shape.append(jax.ShapeDtypeStruct((nb, DEPTH, seq) + tail, F32))
    if stacked:
        in_specs += [row(w) for w in _STATE_WIDTHS]
        args += list(prev_states)
    outs = pl.pallas_call(
        functools.partial(_ctx_attn_kernel, stacked=stacked),
        grid=(nb,),
        in_specs=in_specs, out_specs=out_specs, out_shape=out_shape,
        compiler_params=_cparams(("arbitrary",)),
        name="ctx_attention",
    )(*args)
    return outs[0], outs[1:]


MLA_QT = 512
MLA_KC = 256
MLA_DK = MLA_NOPE + MLA_ROPE


def _lat_mla_kernel(p_ref, cckv_ref, ckr_ref, qn_ref, kvn_ref, wq_ref, wkv_ref, tq_ref, tk_ref,
                    o_ref, k_scr, v_scr, *, past, seq):
    qt = pl.program_id(1)

    def put_keys(r0, n, kv, kr):
        krb = kr.astype(BF16)
        for h in range(MLA_HEADS):
            k_scr[h, r0:r0 + n, 0:MLA_NOPE] = kv[:, h * 256:h * 256 + MLA_NOPE].astype(BF16)
            k_scr[h, r0:r0 + n, MLA_NOPE:MLA_DK] = krb
            v_scr[r0:r0 + n, h * MLA_V:(h + 1) * MLA_V] = kv[:, h * 256 + MLA_NOPE:(h + 1) * 256].astype(BF16)

    @pl.when(qt == 0)
    def _():
        for c in range(past // MLA_KC):
            r0 = c * MLA_KC
            kv = _dot(cckv_ref[r0:r0 + MLA_KC, :].astype(BF16), wkv_ref[...])
            put_keys(r0, MLA_KC, kv, ckr_ref[r0:r0 + MLA_KC, :])
        for c in range(seq // MLA_KC):
            r0 = c * MLA_KC
            ckv = _rms(p_ref[r0:r0 + MLA_KC, MLA_Q_RANK:MLA_Q_RANK + MLA_KV_RANK], kvn_ref[...])
            kv = _dot(ckv.astype(BF16), wkv_ref[...])
            krp = p_ref[r0:r0 + MLA_KC, 2 * MLA_Q_RANK:2 * MLA_Q_RANK + LANES]
            kr = _rope(krp, tk_ref[r0:r0 + MLA_KC, :], MLA_ROPE // 4)
            put_keys(past + r0, MLA_KC, kv, kr[:, 0:MLA_ROPE])

    r0 = pl.multiple_of(qt * MLA_QT, MLA_QT)
    qd = p_ref[pl.ds(r0, MLA_QT), 0:MLA_Q_RANK]
    q = _dot(_rms(qd, qn_ref[...]).astype(BF16), wq_ref[...])
    nope_w = MLA_HEADS * MLA_NOPE
    q_rope = _rope(q[:, nope_w:], tq_ref[pl.ds(r0, MLA_QT), :], MLA_ROPE // 4)
    scale = MLA_DK ** -0.5

    def score(h):
        qh = jnp.concatenate([q[:, h * MLA_NOPE:(h + 1) * MLA_NOPE],
                              q_rope[:, h * MLA_ROPE:(h + 1) * MLA_ROPE]], axis=1).astype(BF16)
        return _dot_nt(qh, k_scr[h]) * scale

    _attend_heads(MLA_HEADS, score, lambda h: v_scr[:, h * MLA_V:(h + 1) * MLA_V], o_ref, 0, group=2)


def _lat_mla(p, row_blk0, nb, seq, cache_ckv, cache_kr, layer, qn, kvn, wq, wkv, tq, tk):
    past = cache_ckv.shape[2]
    full = lambda a: pl.BlockSpec(a.shape, lambda b, t: (0,) * a.ndim)
    nqt = seq // MLA_QT
    return pl.pallas_call(
        functools.partial(_lat_mla_kernel, past=past, seq=seq),
        grid=(nb, nqt),
        in_specs=[
            pl.BlockSpec((seq, P_TN), lambda b, t: (row_blk0 + b, P_QD // P_TN)),
            pl.BlockSpec((None, None, past, MLA_KV_RANK), lambda b, t: (b, layer, 0, 0)),
            pl.BlockSpec((None, None, past, MLA_ROPE), lambda b, t: (b, layer, 0, 0)),
            full(qn), full(kvn), full(wq), full(wkv), full(tq), full(tk),
        ],
        out_specs=pl.BlockSpec((MLA_QT, BRANCH_W), lambda b, t: (b * nqt + t, 0)),
        out_shape=jax.ShapeDtypeStruct((nb * seq, BRANCH_W), BF16),
        scratch_shapes=[pltpu.VMEM((MLA_HEADS, past + seq, MLA_DK), BF16),
                        pltpu.VMEM((past + seq, MLA_HEADS * MLA_V), BF16)],
        compiler_params=_cparams(("arbitrary", "arbitrary")),
        name="lat_mla",
    )(p, cache_ckv, cache_kr, qn, kvn, wq, wkv, tq, tk)


WIN_QB = 128


def _lat_win_kernel(q_ref, k0_ref, k1_ref, k2_ref, v0_ref, v1_ref, v2_ref, ck_ref, cv_ref,
                    t0_ref, t1_ref, t2_ref, sink_ref, o_ref, *, past, seq):
    qb = pl.program_id(1)
    scale = WIN_HEAD_DIM ** -0.5
    n_loc = 3 * WIN_QB
    rows = WIN_GROUP * WIN_QB
    qpos = qb * WIN_QB + lax.broadcasted_iota(jnp.int32, (rows, n_loc), 0) % WIN_QB
    kpos = (qb - 1) * WIN_QB + lax.broadcasted_iota(jnp.int32, (rows, n_loc), 1)
    valid = (kpos >= 0) & (kpos < seq) & (jnp.abs(qpos - kpos) <= WINDOW)
    t1 = t1_ref[...]
    q_heads = lambda kvh: range(kvh * WIN_GROUP, (kvh + 1) * WIN_GROUP)

    def score(kvh):
        cs = slice(kvh * 128, (kvh + 1) * 128)
        keys = jnp.concatenate([
            ck_ref[:, cs],
            _rope(k0_ref[:, cs], t0_ref[...], 32),
            _rope(k1_ref[:, cs], t1, 32),
            _rope(k2_ref[:, cs], t2_ref[...], 32)], axis=0).astype(BF16)
        q = jnp.concatenate([_rope(q_ref[:, h * 128:(h + 1) * 128], t1, 32) for h in q_heads(kvh)],
                            axis=0).astype(BF16)
        s = _dot_nt(q, keys) * scale
        return jnp.concatenate([s[:, :past], jnp.where(valid, s[:, past:], NEG_INF)], axis=1)

    def value(kvh):
        cs = slice(kvh * 128, (kvh + 1) * 128)
        return jnp.concatenate([cv_ref[:, cs], v0_ref[:, cs], v1_ref[:, cs], v2_ref[:, cs]], axis=0).astype(BF16)

    def sink(kvh):
        return jnp.concatenate([jnp.broadcast_to(sink_ref[0:1, h:h + 1], (WIN_QB, 1)) for h in q_heads(kvh)],
                               axis=0)

    def emit(kvh, out):
        for g, h in enumerate(q_heads(kvh)):
            o_ref[:, h * 128:(h + 1) * 128] = out[g * WIN_QB:(g + 1) * WIN_QB, :].astype(BF16)

    _attend_heads(WIN_KV_HEADS, score, value, o_ref, 0, sink=sink, emit=emit)


def _lat_win(p, row0, nb, seq, cache_k, cache_v, layer, tab, sink):
    past = cache_k.shape[2]
    nqb = seq // WIN_QB
    rb0 = row0 // WIN_QB

    def kblk(off, col_blk):
        return pl.BlockSpec(
            (WIN_QB, 256),
            lambda b, t: (rb0 + b * nqb + jnp.clip(t + off, 0, nqb - 1), col_blk))

    def tblk(off):
        return pl.BlockSpec((WIN_QB, 3 * 128), lambda b, t: (jnp.clip(t + off, 0, nqb - 1), 0))

    cache = pl.BlockSpec((None, None, past, 256), lambda b, t: (b, layer, 0, 0))
    return pl.pallas_call(
        functools.partial(_lat_win_kernel, past=past, seq=seq),
        grid=(nb, nqb),
        in_specs=[
            pl.BlockSpec((WIN_QB, 1024), lambda b, t: (rb0 + b * nqb + t, P_WQ // 1024)),
            kblk(-1, P_WK // 256), kblk(0, P_WK // 256), kblk(1, P_WK // 256),
            kblk(-1, P_WV // 256), kblk(0, P_WV // 256), kblk(1, P_WV // 256),
            cache, cache,
            tblk(-1), tblk(0), tblk(1),
            pl.BlockSpec(sink.shape, lambda b, t: (0, 0)),
        ],
        out_specs=pl.BlockSpec((WIN_QB, BRANCH_W), lambda b, t: (b * nqb + t, 0)),
        out_shape=jax.ShapeDtypeStruct((nb * seq, BRANCH_W), BF16),
        compiler_params=_cparams(("arbitrary", "arbitrary")),
        name="lat_window",
    )(p, p, p, p, p, p, p, cache_k, cache_v, tab, tab, tab, sink)


def _lat_nat_kernel(q_ref, k0_ref, k1_ref, k2_ref, v0_ref, v1_ref, v2_ref, ck_ref, cv_ref, bias_ref, o_ref,
                    *, grid_rows):
    g = pl.program_id(0)
    scale = NAT_HEAD_DIM ** -0.5
    past = ck_ref.shape[0]
    win_rows = min(NAT_ROWS, grid_rows)
    lane = lax.broadcasted_iota(jnp.int32, (GRID_W, 2 * GRID_W), 1)

    def rows_of(h, refs):
        cs = slice(h * 128, (h + 1) * 128)
        return jnp.concatenate([r[:, cs] for r in refs], axis=0).astype(BF16)

    def pair_plan(rq, rkp):
        r = NAT_QROWS * g + rq
        kr0 = NAT_QROWS * _nat_key_start(g) + 2 * rkp
        rs = jnp.clip(r - win_rows // 2, 0, grid_rows - win_rows)
        in_win = lambda kr: ((kr >= rs) & (kr < rs + win_rows)).astype(jnp.int32)
        ok = jnp.where(lane < GRID_W, in_win(kr0), in_win(kr0 + 1)) > 0
        return jnp.clip(kr0 - r + NAT_ROWS - 1, 0, 2 * NAT_ROWS - 1), ok

    plans = [[pair_plan(rq, rkp) for rkp in range(NAT_KROWS // 2)] for rq in range(NAT_QROWS)]

    def bias(h):
        return jnp.concatenate(
            [jnp.concatenate([jnp.where(ok, bias_ref[h, a], NEG_INF) for a, ok in row], axis=1) for row in plans],
            axis=0)

    def score(h):
        q = q_ref[:, h * 128:(h + 1) * 128].astype(BF16)
        s = _dot_nt(q, rows_of(h, (ck_ref, k0_ref, k1_ref, k2_ref))) * scale
        return jnp.concatenate([s[:, :past], s[:, past:] + bias(h)], axis=1)

    _attend_heads(NAT_HEADS, score, lambda h: rows_of(h, (cv_ref, v0_ref, v1_ref, v2_ref)), o_ref, 0, group=4)


def _nat_key_start(g):
    return g // 2


def _lat_nat(p, row0, nb, seq, cache_k, cache_v, layer, bias):
    past = cache_k.shape[2]
    qrows = NAT_QROWS * GRID_W
    ng = seq // qrows
    rb0 = row0 // qrows

    def kblk(off, col_blk):
        return pl.BlockSpec((qrows, 1024), lambda g, b: (rb0 + b * ng + _nat_key_start(g) + off, col_blk))

    cache = pl.BlockSpec((None, None, past, 1024), lambda g, b: (b, layer, 0, 0))
    return pl.pallas_call(
        functools.partial(_lat_nat_kernel, grid_rows=seq // GRID_W),
        grid=(ng, nb),
        in_specs=[
            pl.BlockSpec((qrows, 1024), lambda g, b: (rb0 + b * ng + g, P_NQ // 1024)),
            kblk(0, P_NK // 1024), kblk(1, P_NK // 1024), kblk(2, P_NK // 1024),
            kblk(0, P_NV // 1024), kblk(1, P_NV // 1024), kblk(2, P_NV // 1024),
            cache, cache,
            pl.BlockSpec(bias.shape, lambda g, b: (0, 0, 0, 0)),
        ],
        out_specs=pl.BlockSpec((qrows, BRANCH_W), lambda g, b: (b * ng + g, 0)),
        out_shape=jax.ShapeDtypeStruct((nb * seq, BRANCH_W), BF16),
        compiler_params=_cparams(("arbitrary", "arbitrary")),
        name="lat_neighbourhood",
    )(p, p, p, p, p, p, p, cache_k, cache_v, bias)


def _nat_bias_blocks(rpb):
    n_dr, n_dc = 2 * NAT_ROWS - 1, 2 * NAT_COLS - 1
    cc = np.arange(GRID_W)
    dc = cc[None, :] - cc[:, None] + NAT_COLS - 1
    cs = np.clip(cc - NAT_COLS // 2, 0, GRID_W - NAT_COLS)[:, None]
    col_ok = (cc[None, :] >= cs) & (cc[None, :] < cs + NAT_COLS)
    onehot = ((dc[None] == np.arange(n_dc)[:, None, None]) & col_ok[None]).astype(np.float32)
    toep = jnp.einsum('had,dck->hack', rpb.astype(F32), jnp.asarray(onehot), precision=lax.Precision.HIGHEST)
    toep = jnp.where(jnp.asarray(col_ok), toep, NEG_INF)
    fill = jnp.full((NAT_HEADS, 2 * NAT_ROWS + 1 - n_dr, GRID_W, GRID_W), NEG_INF, F32)
    toep = jnp.concatenate([toep, fill], axis=1)
    return jnp.concatenate([toep[:, :-1], toep[:, 1:]], axis=-1)


def _merge_kernel(hn_ref, octx_ref, oa_ref, ob_ref, oc_ref, wga_ref, wgb_ref, wgc_ref,
                  wba_ref, wbb_ref, wbc_ref, m_ref, *, ctx_tiles):
    i = pl.program_id(0)
    weights = ((wga_ref, wba_ref), (wgb_ref, wbb_ref), (wgc_ref, wbc_ref))

    def merged(branch):
        hn = hn_ref[...]
        acc = None
        for k, (wg, wb) in enumerate(weights):
            term = jax.nn.sigmoid(_dot_nt(hn, wg[...])) * _dot(branch(k), wb[...])
            acc = term if acc is None else acc + term
        m_ref[...] = acc.astype(BF16)

    @pl.when(i < ctx_tiles)
    def _():
        merged(lambda k: octx_ref[:, k * BRANCH_W:(k + 1) * BRANCH_W])

    @pl.when(i >= ctx_tiles)
    def _():
        lat = (oa_ref, ob_ref, oc_ref)
        merged(lambda k: lat[k][...])


def _merge(hn, o_ctx, o_a, o_b, o_c, w_gate, w_br, layer):
    rows, d = hn.shape
    tm, tn = ROW_TILE, 256
    nj = d // tn
    ctx_tiles = o_ctx.shape[0] // tm
    gate = lambda k: pl.BlockSpec((None, tn, d), lambda i, j: (layer, k * nj + j, 0))
    br = lambda k: pl.BlockSpec((None, None, BRANCH_W, tn), lambda i, j: (layer, k, 0, j))
    lat = pl.BlockSpec((tm, BRANCH_W), lambda i, j: (jnp.maximum(i - ctx_tiles, 0), 0))
    return pl.pallas_call(
        functools.partial(_merge_kernel, ctx_tiles=ctx_tiles),
        grid=(rows // tm, nj),
        in_specs=[pl.BlockSpec((tm, d), lambda i, j: (i, 0)),
                  pl.BlockSpec((tm, 3 * BRANCH_W), lambda i, j: (jnp.minimum(i, ctx_tiles - 1), 0)),
                  lat, lat, lat,
                  gate(0), gate(1), gate(2), br(0), br(1), br(2)],
        out_specs=pl.BlockSpec((tm, tn), lambda i, j: (i, j)),
        out_shape=jax.ShapeDtypeStruct((rows, d), BF16),
        compiler_params=_cparams(("arbitrary", "arbitrary")),
        name="merge",
    )(hn, o_ctx, o_a, o_b, o_c, w_gate, w_gate, w_gate, w_br, w_br, w_br)


ROUTE_IDX = 8
ROUTE_W = 10


def _out_proj_kernel(m_ref, w_ref, mod_ref, g_ref, *rest, routed, ctx_tiles):
    nh = 1 if ctx_tiles is None else 2
    h_refs, rest = rest[:nh], rest[nh:]
    if routed:
        wr_ref, h1_ref, hn2_ref, route_ref = rest
    else:
        h1_ref, hn2_ref = rest
    h1 = _row_group_value(h_refs, ctx_tiles) + mod_ref[2:3, :] * _dot(m_ref[...], w_ref[...])
    h1_ref[...] = h1
    hn2 = _rms(h1, g_ref[...]) * (1.0 + mod_ref[4:5, :]) + mod_ref[3:4, :]
    hn2_ref[...] = hn2.astype(hn2_ref.dtype)
    if routed:
        logits = _dot(hn2.astype(BF16), wr_ref[...])
        lane = lax.broadcasted_iota(jnp.int32, logits.shape, 1).astype(F32)
        lg = jnp.where(lane < N_EXPERTS, logits, -jnp.inf)
        m1 = jnp.max(lg, axis=-1, keepdims=True)
        i1 = jnp.min(jnp.where(lg == m1, lane, float(LANES)), axis=-1, keepdims=True)
        lg2 = jnp.where(lane == i1, -jnp.inf, lg)
        m2 = jnp.max(lg2, axis=-1, keepdims=True)
        i2 = jnp.min(jnp.where(lg2 == m2, lane, float(LANES)), axis=-1, keepdims=True)
        e2 = jnp.exp(m2 - m1)
        w1 = 1.0 / (1.0 + e2)
        w2 = e2 / (1.0 + e2)
        route = (jnp.where(lane == ROUTE_IDX, i1, 0.0)
                 + jnp.where(lane == ROUTE_IDX + 1, i2, 0.0)
                 + jnp.where(lane == ROUTE_W, w1, 0.0)
                 + jnp.where(lane == ROUTE_W + 1, w2, 0.0))
        route_ref[...] = route


def _out_proj(m, w_out, h, mod, g, layer, cond_of_tile, w_router=None):
    rows, d = m.shape
    tm = 512
    sub = ROW_TILE // tm
    routed = w_router is not None
    h_args, h_specs, ctx_tiles = _split_rows_specs(h, tm)
    in_specs = [
        pl.BlockSpec((tm, d), lambda i: (i, 0)),
        pl.BlockSpec((None, d, d), lambda i: (layer, 0, 0)),
        pl.BlockSpec((None, 6, d), lambda i: (cond_of_tile(i // sub), 0, 0)),
        pl.BlockSpec((1, d), lambda i: (0, 0)),
    ] + h_specs
    out_specs = [pl.BlockSpec((tm, d), lambda i: (i, 0)), pl.BlockSpec((tm, d), lambda i: (i, 0))]
    out_shape = [jax.ShapeDtypeStruct((rows, d), F32), jax.ShapeDtypeStruct((rows, d), BF16)]
    args = [m, w_out, mod, g] + h_args
    if routed:
        out_shape[1] = jax.ShapeDtypeStruct((rows, d), F32)
        in_specs.append(pl.BlockSpec((d, LANES), lambda i: (0, 0)))
        out_specs.append(pl.BlockSpec((tm, LANES), lambda i: (i, 0)))
        out_shape.append(jax.ShapeDtypeStruct((rows, LANES), F32))
        args.append(w_router)
    return pl.pallas_call(
        functools.partial(_out_proj_kernel, routed=routed, ctx_tiles=ctx_tiles),
        grid=(rows // tm,),
        in_specs=in_specs, out_specs=out_specs, out_shape=out_shape,
        compiler_params=_cparams(("arbitrary",)),
        name="out_proj",
    )(*args)


def _swiglu_chunk(x, wg_ref, wu_ref, wd_ref):
    g = _dot(x, wg_ref[...].astype(BF16))
    u = _dot(x, wu_ref[...].astype(BF16))
    a = (g * jax.nn.sigmoid(g) * u).astype(BF16)
    return _dot(a, wd_ref[...].astype(BF16))


FFN_PREFETCH_STEP = 5


def _ffn_kernel(x_ref, wg_ref, wu_ref, wd_ref, h_ref, mod_ref, o_ref, acc_ref, h_sem, wb_sem):
    i = pl.program_id(0)
    f = pl.program_id(1)
    nt = pl.num_programs(0)
    last_f = pl.num_programs(1) - 1
    tm = acc_ref.shape[1]
    slot = i % 2

    def rows_of(tile):
        return pl.ds(pl.multiple_of(tile * tm, tm), tm)

    def h_load(tile, s):
        return pltpu.make_async_copy(h_ref.at[rows_of(tile), :], acc_ref.at[s], h_sem.at[s])

    def writeback(tile, s):
        return pltpu.make_async_copy(acc_ref.at[s], o_ref.at[rows_of(tile), :], wb_sem.at[s])

    @pl.when((f == 0) & (i == 0))
    def _():
        h_load(0, 0).start()

    @pl.when(f == 0)
    def _():
        h_load(i, slot).wait()

    @pl.when((f == FFN_PREFETCH_STEP) & (i >= 1))
    def _():
        writeback(i - 1, 1 - slot).wait()

    @pl.when((f == FFN_PREFETCH_STEP) & (i + 1 < nt))
    def _():
        h_load(i + 1, 1 - slot).start()

    acc_ref[slot] += mod_ref[5:6, :] * _swiglu_chunk(x_ref[...], wg_ref, wu_ref, wd_ref)

    @pl.when(f == last_f)
    def _():
        writeback(i, slot).start()

    @pl.when((f == last_f) & (i == nt - 1))
    def _():
        writeback(i, slot).wait()


def _ffn(x, wg, wu, wd, h, mod, cond_of_tile):
    rows, d = h.shape
    ff = wg.shape[1]
    tm, tf = FFN_TM, FFN_TF
    sub = ROW_TILE // tm
    assert ff // tf > FFN_PREFETCH_STEP
    return pl.pallas_call(
        _ffn_kernel,
        grid=(rows // tm, ff // tf),
        in_specs=[
            pl.BlockSpec((tm, d), lambda i, f: (i, 0)),
            pl.BlockSpec((d, tf), lambda i, f: (0, f)),
            pl.BlockSpec((d, tf), lambda i, f: (0, f)),
            pl.BlockSpec((tf, d), lambda i, f: (f, 0)),
            pl.BlockSpec(memory_space=pl.ANY),
            pl.BlockSpec((None, 6, d), lambda i, f: (cond_of_tile(i // sub), 0, 0)),
        ],
        out_specs=pl.BlockSpec(memory_space=pl.ANY),
        out_shape=jax.ShapeDtypeStruct((rows, d), F32),
        scratch_shapes=[pltpu.VMEM((2, tm, d), F32), pltpu.SemaphoreType.DMA((2,)),
                        pltpu.SemaphoreType.DMA((2,))],
        compiler_params=_cparams(("arbitrary", "arbitrary")),
        name="ffn_dense",
    )(x, wg, wu, wd, h, mod)


def _start_row_gather(src_ref, idx_ref, dst_ref, sem, n):
    def issue(r, carry):
        pltpu.make_async_copy(src_ref.at[pl.ds(idx_ref[0, r], 1), :], dst_ref.at[pl.ds(r, 1), :], sem).start()
        return carry

    lax.fori_loop(0, n, issue, 0, unroll=8)


def _wait_row_gather(src_ref, dst_ref, sem, n):
    pltpu.make_async_copy(src_ref.at[pl.ds(0, n), :], dst_ref, sem).wait()


def _gather_kernel(tr_ref, idx_ref, idx_next_ref, src_ref, o_ref, buf, sem):
    i = pl.program_id(0)
    n = o_ref.shape[0]
    per_tile = MOE_TM // n
    slot = i % 2

    def used(step):
        return tr_ref[step // per_tile] > (step % per_tile) * n

    @pl.when((i == 0) & used(0))
    def _():
        _start_row_gather(src_ref, idx_ref, buf.at[0], sem.at[0], n)

    nxt = jnp.minimum(i + 1, pl.num_programs(0) - 1)

    @pl.when((i + 1 < pl.num_programs(0)) & used(nxt))
    def _():
        _start_row_gather(src_ref, idx_next_ref, buf.at[1 - slot], sem.at[1 - slot], n)

    @pl.when(used(i))
    def _():
        _wait_row_gather(src_ref, buf.at[slot], sem.at[slot], n)
        o_ref[...] = buf[slot].astype(o_ref.dtype)

    @pl.when(jnp.logical_not(used(i)))
    def _():
        o_ref[...] = jnp.zeros_like(o_ref)


def _gather_rows(src, idx, tile_rows, out_dtype):
    n = idx.shape[0]
    d = src.shape[1]
    tg = GATHER_ROWS
    steps = n // tg
    idx3 = idx.reshape(steps, 1, tg)
    grid_spec = pltpu.PrefetchScalarGridSpec(
        num_scalar_prefetch=1,
        grid=(steps,),
        in_specs=[pl.BlockSpec((None, 1, tg), lambda i, tr: (i, 0, 0), memory_space=pltpu.SMEM),
                  pl.BlockSpec((None, 1, tg), lambda i, tr: (jnp.minimum(i + 1, steps - 1), 0, 0),
                               memory_space=pltpu.SMEM),
                  pl.BlockSpec(memory_space=pl.ANY)],
        out_specs=pl.BlockSpec((tg, d), lambda i, tr: (i, 0)),
        scratch_shapes=[pltpu.VMEM((2, tg, d), src.dtype), pltpu.SemaphoreType.DMA((2,))],
    )
    return pl.pallas_call(
        _gather_kernel,
        grid_spec=grid_spec,
        out_shape=jax.ShapeDtypeStruct((n, d), out_dtype),
        compiler_params=_cparams(("arbitrary",)),
        name="moe_gather",
    )(tile_rows, idx3, idx3, src)


def _moe_ffn_kernel(te_ref, tv_ref, x_ref, wg_ref, wu_ref, wd_ref, o_ref, acc_ref, sem):
    i = pl.program_id(0)
    f = pl.program_id(1)
    last_f = pl.num_programs(1) - 1
    rows_used = tv_ref[i]

    def writeback(tile, s):
        rows = pl.ds(pl.multiple_of(tile * MOE_TM + s * MOE_SLAB, MOE_SLAB), MOE_SLAB)
        return pltpu.make_async_copy(acc_ref.at[pl.ds(s * MOE_SLAB, MOE_SLAB), :], o_ref.at[rows, :], sem.at[s])

    for s in range(MOE_TM // MOE_SLAB):
        rs = slice(s * MOE_SLAB, (s + 1) * MOE_SLAB)
        used = rows_used > s * MOE_SLAB

        @pl.when((f == 0) & (i > 0))
        def _():
            writeback(i - 1, s).wait()

        @pl.when(f == 0)
        def _():
            acc_ref[rs, :] = jnp.zeros((MOE_SLAB, acc_ref.shape[1]), acc_ref.dtype)

        @pl.when(used)
        def _():
            acc_ref[rs, :] += _swiglu_chunk(x_ref[rs, :], wg_ref, wu_ref, wd_ref)

        @pl.when(f == last_f)
        def _():
            writeback(i, s).start()

    @pl.when((f == last_f) & (i == pl.num_programs(0) - 1))
    def _():
        for s in range(MOE_TM // MOE_SLAB):
            writeback(i, s).wait()


def _moe_ffn(xs, wg, wu, wd, tile_expert, tile_valid):
    n, d = xs.shape
    ff = wg.shape[2]
    tm, tf = MOE_TM, MOE_TF
    nf = ff // tf

    def fidx(i, f, tv):
        return jnp.where(tv[i] > 0, f, nf - 1)

    grid_spec = pltpu.PrefetchScalarGridSpec(
        num_scalar_prefetch=2,
        grid=(n // tm, nf),
        in_specs=[
            pl.BlockSpec((tm, d), lambda i, f, te, tv: (i, 0)),
            pl.BlockSpec((None, d, tf), lambda i, f, te, tv: (te[i], 0, fidx(i, f, tv))),
            pl.BlockSpec((None, d, tf), lambda i, f, te, tv: (te[i], 0, fidx(i, f, tv))),
            pl.BlockSpec((None, tf, d), lambda i, f, te, tv: (te[i], fidx(i, f, tv), 0)),
        ],
        out_specs=pl.BlockSpec(memory_space=pl.ANY),
        scratch_shapes=[pltpu.VMEM((tm, d), F32), pltpu.SemaphoreType.DMA((tm // MOE_SLAB,))],
    )
    return pl.pallas_call(
        _moe_ffn_kernel,
        grid_spec=grid_spec,
        out_shape=jax.ShapeDtypeStruct((n, d), F32),
        compiler_params=_cparams(("arbitrary", "arbitrary")),
        name="moe_ffn",
    )(tile_expert, tile_valid, xs, wg, wu, wd)


def _combine_kernel(p0_ref, p1_ref, p0_next_ref, p1_next_ref, ys_ref, route_ref, h_ref, mod_ref, fn_ref,
                    octx_ref, olat_ref, buf0, buf1, sem, *, ctx_steps):
    i = pl.program_id(0)
    n = octx_ref.shape[0]
    slot = i % 2

    def start(q0_ref, q1_ref, s):
        _start_row_gather(ys_ref, q0_ref, buf0.at[s], sem.at[s, 0], n)
        _start_row_gather(ys_ref, q1_ref, buf1.at[s], sem.at[s, 1], n)

    @pl.when(i == 0)
    def _():
        start(p0_ref, p1_ref, 0)

    @pl.when(i + 1 < pl.num_programs(0))
    def _():
        start(p0_next_ref, p1_next_ref, 1 - slot)

    _wait_row_gather(ys_ref, buf0.at[slot], sem.at[slot, 0], n)
    _wait_row_gather(ys_ref, buf1.at[slot], sem.at[slot, 1], n)
    w0 = route_ref[:, ROUTE_W:ROUTE_W + 1]
    w1 = route_ref[:, ROUTE_W + 1:ROUTE_W + 2]
    y = w0 * buf0[slot] + w1 * buf1[slot]
    h2 = h_ref[...] + mod_ref[5:6, :] * y
    out = _rms(h2, fn_ref[...])

    @pl.when(i < ctx_steps)
    def _():
        octx_ref[...] = out

    @pl.when(i >= ctx_steps)
    def _():
        olat_ref[...] = out


def _moe_combine(ys, pos0, pos1, route, h, mod, final_norm, cond_of_tile, rows_ctx):
    rows, d = h.shape
    tc = GATHER_ROWS
    sub = ROW_TILE // tc
    ctx_steps = rows_ctx // tc
    steps = rows // tc
    smem = lambda: pl.BlockSpec((None, 1, tc), lambda i: (i, 0, 0), memory_space=pltpu.SMEM)
    smem_next = lambda: pl.BlockSpec((None, 1, tc), lambda i: (jnp.minimum(i + 1, steps - 1), 0, 0),
                                     memory_space=pltpu.SMEM)
    pos0, pos1 = pos0.reshape(steps, 1, tc), pos1.reshape(steps, 1, tc)
    return pl.pallas_call(
        functools.partial(_combine_kernel, ctx_steps=ctx_steps),
        grid=(steps,),
        in_specs=[smem(), smem(), smem_next(), smem_next(),
                  pl.BlockSpec(memory_space=pl.ANY),
                  pl.BlockSpec((tc, LANES), lambda i: (i, 0)),
                  pl.BlockSpec((tc, d), lambda i: (i, 0)),
                  pl.BlockSpec((None, 6, d), lambda i: (cond_of_tile(i // sub), 0, 0)),
                  pl.BlockSpec((1, d), lambda i: (0, 0))],
        out_specs=[pl.BlockSpec((tc, d), lambda i: (jnp.minimum(i, ctx_steps - 1), 0)),
                   pl.BlockSpec((tc, d), lambda i: (jnp.maximum(i - ctx_steps, 0), 0))],
        out_shape=[jax.ShapeDtypeStruct((rows_ctx, d), F32), jax.ShapeDtypeStruct((rows - rows_ctx, d), F32)],
        scratch_shapes=[pltpu.VMEM((2, tc, d), F32), pltpu.VMEM((2, tc, d), F32),
                        pltpu.SemaphoreType.DMA((2, 2))],
        compiler_params=_cparams(("arbitrary",)),
        name="moe_combine",
    )(pos0, pos1, pos0, pos1, ys, route, h, mod, final_norm)


def _dispatch_plan(route, tm):
    rows = route.shape[0]
    ids = route[:, ROUTE_IDX:ROUTE_IDX + TOP_K].astype(jnp.int32)
    flat = ids.reshape(-1)
    onehot = (flat[:, None] == jnp.arange(N_EXPERTS)[None, :]).astype(jnp.int32)
    rank = jnp.sum((jnp.cumsum(onehot, axis=0) - onehot) * onehot, axis=1)
    counts = jnp.sum(onehot, axis=0)
    tiles = (counts + tm - 1) // tm
    tile_end = jnp.cumsum(tiles)
    start = (tile_end - tiles) * tm
    slot = start[flat] + rank
    n_tiles = (rows * TOP_K) // tm + N_EXPERTS
    n_slots = n_tiles * tm
    slot_token = (jnp.arange(n_slots, dtype=jnp.int32) % rows).at[slot].set(
        jnp.arange(rows * TOP_K, dtype=jnp.int32) // TOP_K, unique_indices=True)
    t = jnp.arange(n_tiles)
    tile_expert = jnp.minimum(jnp.sum((t[:, None] >= tile_end[None, :]).astype(jnp.int32), axis=1), N_EXPERTS - 1)
    tile_in_expert = t - (tile_end - tiles)[tile_expert]
    tile_rows = jnp.where(t < tile_end[-1], jnp.clip(counts[tile_expert] - tile_in_expert * tm, 0, tm), 0)
    last_expert = tile_expert[jnp.maximum(tile_end[-1] - 1, 0)]
    tile_expert = jnp.where(tile_rows > 0, tile_expert, last_expert).astype(jnp.int32)
    pos = slot.reshape(rows, TOP_K)
    return slot_token, tile_expert, tile_rows.astype(jnp.int32), pos[:, 0], pos[:, 1]


def _rope_table(seq, n):
    quarter = n // 4
    t = np.arange(seq)
    inv = jnp.power(ROPE_BASE, -jnp.arange(quarter, dtype=F32) * (2.0 / (n // 2)))
    ang_r = jnp.asarray(t // GRID_W, F32)[:, None] * inv[None, :]
    ang_c = jnp.asarray(t % GRID_W, F32)[:, None] * inv[None, :]
    zero = jnp.zeros((seq, quarter), F32)
    cos = jnp.concatenate([jnp.cos(ang_r)] * 2 + [jnp.cos(ang_c)] * 2, axis=1)
    up = jnp.concatenate([-jnp.sin(ang_r), zero, -jnp.sin(ang_c), zero], axis=1)
    dn = jnp.concatenate([zero, jnp.sin(ang_r), zero, jnp.sin(ang_c)], axis=1)
    return cos, up, dn


def _tile_cols(parts, reps, pad_to=None):
    out = [jnp.tile(p, (1, reps)) for p in parts]
    if pad_to is not None:
        out = [jnp.pad(p, ((0, 0), (0, pad_to - p.shape[1]))) for p in out]
    return jnp.concatenate(out, axis=1)


def kernel(x_prompt, x_sample, cache_mla_ckv, cache_mla_krope, cache_win_k, cache_win_v, cache_nat_k, cache_nat_v, c, c_ctx, w_mod, b_mod, norm1, norm2, w_in, mla_q_norm, mla_kv_norm, w_mla_q_up, w_mla_kv_up, win_sink, nat_rpb, w_br_mla, w_br_win, w_br_nat, w_out, w_ff_gate, w_ff_up, w_ff_down, w_router, w_ex_gate, w_ex_up, w_ex_down, final_norm):
    nb_ctx, seq_ctx, d = x_prompt.shape
    nb_lat, seq_lat, _ = x_sample.shape
    past = cache_mla_ckv.shape[2]
    rows_ctx = nb_ctx * seq_ctx
    rows_lat = nb_lat * seq_lat
    assert d == D_MODEL and seq_lat == ROW_TILE and rows_ctx % ROW_TILE == 0
    assert seq_lat // GRID_W == 4 * NAT_QROWS
    assert w_mod.shape[0] == DEPTH == 2
    ctx_tiles = rows_ctx // ROW_TILE

    def cond_of_tile(i):
        return jnp.where(i < ctx_tiles, 0, i - ctx_tiles + 1)

    n_cond = 16
    cond = jnp.zeros((n_cond, d), F32).at[0].set(c_ctx).at[1:1 + nb_lat].set(c)
    mod_all = _modulation(cond, w_mod, b_mod).reshape(DEPTH, n_cond, 6, d)

    w_p, w_gate = _w_in_prep(jnp.swapaxes(w_in, 1, 2))
    wq_up = w_mla_q_up.reshape(DEPTH, MLA_Q_RANK, MLA_HEADS, MLA_NOPE + MLA_ROPE)
    wq_up = jnp.concatenate([wq_up[..., :MLA_NOPE].reshape(DEPTH, MLA_Q_RANK, -1),
                             wq_up[..., MLA_NOPE:].reshape(DEPTH, MLA_Q_RANK, -1)], axis=2).astype(BF16)
    wkv_up = w_mla_kv_up.astype(BF16)
    w_br = jnp.stack([w_br_mla, w_br_win, w_br_nat], axis=1).astype(BF16)
    w_out_b = w_out.astype(BF16)
    w_router_p = jnp.pad(w_router, ((0, 0), (0, 0), (0, LANES - N_EXPERTS))).astype(BF16)

    t128 = jnp.concatenate(_rope_table(seq_lat, 128), axis=1)
    t64 = _rope_table(seq_lat, MLA_ROPE)
    t_q = _tile_cols(t64, MLA_HEADS)
    t_k = _tile_cols(t64, 1, pad_to=LANES)

    ck_win = cache_win_k.reshape(nb_lat, DEPTH, past, 256)
    cv_win = cache_win_v.reshape(nb_lat, DEPTH, past, 256)
    ck_nat = cache_nat_k.reshape(nb_lat, DEPTH, past, 1024)
    cv_nat = cache_nat_v.reshape(nb_lat, DEPTH, past, 1024)


    h = (x_prompt.reshape(rows_ctx, d), x_sample.reshape(rows_lat, d))
    states = None
    for l in range(DEPTH):
        mod = mod_all[l]
        qn = mla_q_norm[l].reshape(1, -1)
        kvn = mla_kv_norm[l].reshape(1, -1)
        sink = win_sink[l].reshape(1, -1)
        p, hn = _mixer_in(h, mod, norm1[l].reshape(1, d), w_p, l, cond_of_tile)
        o_ctx, states = _ctx_attention(p, nb_ctx, seq_ctx, qn, kvn, wq_up[l], wkv_up[l], sink,
                                       prev_states=states)
        o_a = _lat_mla(p, rows_ctx // seq_lat, nb_lat, seq_lat, cache_mla_ckv, cache_mla_krope, l,
                       qn, kvn, wq_up[l], wkv_up[l], t_q, t_k)
        o_b = _lat_win(p, rows_ctx, nb_lat, seq_lat, ck_win, cv_win, l, t128, sink)
        o_c = _lat_nat(p, rows_ctx, nb_lat, seq_lat, ck_nat, cv_nat, l, _nat_bias_blocks(nat_rpb[l]))
        m = _merge(hn, o_ctx, o_a, o_b, o_c, w_gate, w_br, l)
        if l % 2 == 0:
            h1, hn2 = _out_proj(m, w_out_b, h, mod, norm2[l].reshape(1, d), l, cond_of_tile)
            h = _ffn(hn2, w_ff_gate[l // 2].astype(BF16), w_ff_up[l // 2].astype(BF16),
                     w_ff_down[l // 2].astype(BF16), h1, mod, cond_of_tile)
        else:
            h1, hn2, route = _out_proj(m, w_out_b, h, mod, norm2[l].reshape(1, d), l, cond_of_tile,
                                       w_router=w_router_p[l // 2])
            slot_token, tile_expert, tile_valid, pos0, pos1 = _dispatch_plan(route, MOE_TM)
            xs = _gather_rows(hn2, slot_token, tile_valid, BF16)
            ys = _moe_ffn(xs, w_ex_gate[l // 2], w_ex_up[l // 2], w_ex_down[l // 2], tile_expert, tile_valid)
            y_ctx, y_lat = _moe_combine(ys, pos0, pos1, route, h1, mod, final_norm.reshape(1, d),
                                        cond_of_tile, rows_ctx)

    y_prompt = y_ctx.reshape(nb_ctx, seq_ctx, d)
    y_sample = y_lat.reshape(nb_lat, seq_lat, d)
    return (y_prompt, y_sample) + tuple(states)
```

```python
import functools

import jax
import jax.numpy as jnp
import numpy as np
from jax import lax
from jax.experimental import pallas as pl
from jax.experimental.pallas import tpu as pltpu

F32 = jnp.float32
BF16 = jnp.bfloat16

D_MODEL = 2048
DEPTH = 2
GRID_W = 64
ROPE_BASE = 10000.0
EPS = 1e-6
NEG_INF = -1e30
MLA_HEADS = 8
MLA_Q_RANK = 512
MLA_KV_RANK = 512
MLA_NOPE = 128
MLA_ROPE = 64
MLA_V = 128
WIN_HEADS = 8
WIN_KV_HEADS = 2
WIN_GROUP = WIN_HEADS // WIN_KV_HEADS
WIN_HEAD_DIM = 128
WINDOW = 128
NAT_HEADS = 8
NAT_HEAD_DIM = 128
NAT_ROWS = 8
NAT_COLS = 16
BRANCH_W = 1024
D_FF = 5632
N_EXPERTS = 8
TOP_K = 2

P_WQ, P_NQ, P_NK, P_NV = 0, 1024, 2048, 3072
P_WK, P_WV = 4096, 4352
P_QD, P_KVD, P_KR = 4608, 5120, 5632
P_COLS = 5760
P_TN = 1152
MIXER_TN = 1280
LANES = 128

VMEM_LIMIT = 56 * 1024 * 1024

ROW_TILE = 1024
FFN_TM = 1024
FFN_TF = 512
MOE_TM = 2048
MOE_SLAB = 512
MOE_TF = 256
GATHER_ROWS = 512
NAT_QROWS = 4
NAT_KROWS = 12


def _cparams(sem):
    return pltpu.CompilerParams(dimension_semantics=sem, vmem_limit_bytes=VMEM_LIMIT)


def _rms(x, g):
    ms = jnp.mean(x * x, axis=-1, keepdims=True)
    return x * lax.rsqrt(ms + EPS) * g


def _dot(a, b):
    return jnp.dot(a, b, preferred_element_type=F32)


def _dot_nt(a, b):
    return lax.dot_general(a, b, (((1,), (1,)), ((), ())), preferred_element_type=F32)


def _rope(x, tab_ref_or_val, shift):
    n = x.shape[-1]
    t = tab_ref_or_val
    c, s_up, s_dn = t[:, 0:n], t[:, n:2 * n], t[:, 2 * n:3 * n]
    up = pltpu.roll(x, n - shift, axis=1)
    dn = pltpu.roll(x, shift, axis=1)
    return x * c + up * s_up + dn * s_dn


def _attend_heads(n_heads, score, value, o_ref, col0, sink=None, group=None, emit=None):
    group = group or n_heads
    for h0 in range(0, n_heads, group):
        heads = range(h0, h0 + group)
        s = [score(h) for h in heads]
        m = [jnp.max(x, axis=-1, keepdims=True) for x in s]
        if sink is not None:
            sk = [sink(h) for h in heads]
            m = [jnp.maximum(a, b) for a, b in zip(m, sk)]
        e = [jnp.exp(x - a) for x, a in zip(s, m)]
        den = [jnp.sum(x, axis=-1, keepdims=True) for x in e]
        if sink is not None:
            den = [d + jnp.exp(b - a) for d, a, b in zip(den, m, sk)]
        pv = [_dot(x.astype(BF16), value(h)) for x, h in zip(e, heads)]
        for i, h in enumerate(heads):
            out = pv[i] * (1.0 / den[i])
            if emit is not None:
                emit(h, out)
            else:
                o_ref[:, col0 + h * LANES:col0 + (h + 1) * LANES] = out.astype(o_ref.dtype)


def _mod_kernel(c_ref, w_ref, b_ref, o_ref):
    c = c_ref[...]
    s = (c * jax.nn.sigmoid(c)).astype(BF16)
    o_ref[...] = _dot(s, w_ref[...].astype(BF16)) + b_ref[...]


def _modulation(cond, w_mod, b_mod):
    depth, d, n = w_mod.shape
    nc = cond.shape[0]
    tn = 1024
    return pl.pallas_call(
        _mod_kernel,
        grid=(depth, n // tn),
        in_specs=[
            pl.BlockSpec((nc, d), lambda l, j: (0, 0)),
            pl.BlockSpec((None, d, tn), lambda l, j: (l, 0, j)),
            pl.BlockSpec((None, 1, tn), lambda l, j: (l, 0, j)),
        ],
        out_specs=pl.BlockSpec((None, nc, tn), lambda l, j: (l, 0, j)),
        out_shape=jax.ShapeDtypeStruct((depth, nc, n), F32),
        compiler_params=_cparams(("arbitrary", "arbitrary")),
        name="modulation",
    )(cond, w_mod, b_mod.reshape(depth, 1, n))


_W_IN_SEGMENTS = ((1088, 1024), (2624, 1024), (3648, 1024), (4672, 1024), (2112, 256), (2368, 256),
                  (0, 512), (512, 512), (1024, 64))
W_IN_GATES = 5696


def _w_in_prep_kernel(w_ref, p_ref, g_ref):
    row = 0
    for src, width in _W_IN_SEGMENTS:
        p_ref[row:row + width, :] = w_ref[src:src + width, :].astype(BF16)
        row += width
    p_ref[row:, :] = jnp.zeros((P_COLS - row, p_ref.shape[1]), BF16)
    g_ref[...] = w_ref[W_IN_GATES:, :].astype(BF16)


def _w_in_prep(w_in_t):
    depth, n, d = w_in_t.shape
    tk = 256
    n_gate = n - W_IN_GATES
    return pl.pallas_call(
        _w_in_prep_kernel,
        grid=(depth, d // tk),
        in_specs=[pl.BlockSpec((None, n, tk), lambda l, r: (l, 0, r))],
        out_specs=[pl.BlockSpec((None, P_COLS, tk), lambda l, r: (l, 0, r)),
                   pl.BlockSpec((None, n_gate, tk), lambda l, r: (l, 0, r))],
        out_shape=[jax.ShapeDtypeStruct((depth, P_COLS, d), BF16),
                   jax.ShapeDtypeStruct((depth, n_gate, d), BF16)],
        compiler_params=_cparams(("arbitrary", "arbitrary")),
        name="w_in_prep",
    )(w_in_t)


def _split_rows_specs(h, tm, single_buffer=False):
    if not isinstance(h, tuple):
        return [h], [pl.BlockSpec((tm, h.shape[1]), lambda i, *_: (i, 0))], None
    ctx_tiles = h[0].shape[0] // tm
    d = h[0].shape[1]
    mode = dict(pipeline_mode=pl.Buffered(1)) if single_buffer else {}
    return (list(h),
            [pl.BlockSpec((tm, d), lambda i, *_: (jnp.minimum(i, ctx_tiles - 1), 0), **mode),
             pl.BlockSpec((tm, d), lambda i, *_: (jnp.maximum(i - ctx_tiles, 0), 0), **mode)],
            ctx_tiles)


def _row_group_value(h_refs, ctx_tiles):
    if ctx_tiles is None:
        return h_refs[0][...]
    return jnp.where(pl.program_id(0) < ctx_tiles, h_refs[0][...], h_refs[1][...])


def _mixer_in_kernel(*refs, ctx_tiles, n_cols):
    nh = 1 if ctx_tiles is None else 2
    h_refs = refs[:nh]
    mod_ref, g_ref, w_ref, p_ref, hn_ref = refs[nh:]

    @pl.when(pl.program_id(1) == 0)
    def _():
        h = _row_group_value(h_refs, ctx_tiles)
        hn = _rms(h, g_ref[...]) * (1.0 + mod_ref[1:2, :]) + mod_ref[0:1, :]
        hn_ref[...] = hn.astype(BF16)

    j = pl.program_id(1)
    tn = p_ref.shape[1]
    tail = n_cols % tn

    @pl.when((j + 1) * tn <= n_cols)
    def _():
        p_ref[...] = _dot_nt(hn_ref[...], w_ref[...])

    if tail:
        @pl.when((j + 1) * tn > n_cols)
        def _():
            p_ref[:, :tail] = _dot_nt(hn_ref[...], w_ref[:tail, :])


def _mixer_in(h, mod, g, w, layer, cond_of_tile):
    tm, tn = ROW_TILE, MIXER_TN
    sub = ROW_TILE // tm
    h_args, h_specs, ctx_tiles = _split_rows_specs(h, tm, single_buffer=True)
    rows = sum(a.shape[0] for a in h_args)
    d = h_args[0].shape[1]
    n = w.shape[1]
    return pl.pallas_call(
        functools.partial(_mixer_in_kernel, ctx_tiles=ctx_tiles, n_cols=n),
        grid=(rows // tm, pl.cdiv(n, tn)),
        in_specs=h_specs + [
            pl.BlockSpec((None, 6, d), lambda i, j: (cond_of_tile(i // sub), 0, 0)),
            pl.BlockSpec((1, d), lambda i, j: (0, 0)),
            pl.BlockSpec((None, tn, d), lambda i, j: (layer, j, 0)),
        ],
        out_specs=[
            pl.BlockSpec((tm, tn), lambda i, j: (i, j)),
            pl.BlockSpec((tm, d), lambda i, j: (i, 0)),
        ],
        out_shape=[
            jax.ShapeDtypeStruct((rows, n), F32),
            jax.ShapeDtypeStruct((rows, d), BF16),
        ],
        compiler_params=_cparams(("arbitrary", "arbitrary")),
        name="mixer_in",
    )(*h_args, mod, g, w)


_STATE_WIDTHS = (MLA_KV_RANK, MLA_ROPE, 256, 256, 1024, 1024)
_STATE_HEADS = (None, None, WIN_KV_HEADS, WIN_KV_HEADS, NAT_HEADS, NAT_HEADS)


def _ctx_attn_kernel(p_ref, qn_ref, kvn_ref, wq_ref, wkv_ref, sink_ref, *refs, stacked):
    n_state = len(_STATE_WIDTHS)
    if stacked:
        prev_refs, o_ref, state_refs = refs[:n_state], refs[n_state], refs[n_state + 1:]
    else:
        prev_refs, o_ref, state_refs = None, refs[0], refs[1:]

    def put_state(k, val, slot=DEPTH - 1):
        ref, heads = state_refs[k], _STATE_HEADS[k]
        if not stacked:
            ref[...] = val
        elif heads is None:
            ref[slot] = val
        else:
            for j in range(heads):
                ref[slot, :, j, :] = val[:, j * LANES:(j + 1) * LANES]

    if stacked:
        for k in range(n_state):
            put_state(k, prev_refs[k][...], slot=0)

    q = _dot(_rms(p_ref[:, P_QD:P_QD + MLA_Q_RANK], qn_ref[...]).astype(BF16), wq_ref[...])
    ckv = _rms(p_ref[:, P_KVD:P_KVD + MLA_KV_RANK], kvn_ref[...])
    put_state(0, ckv)
    kv = _dot(ckv.astype(BF16), wkv_ref[...]).astype(BF16)
    kr = p_ref[:, P_KR:P_KR + MLA_ROPE]
    put_state(1, kr)
    krb = kr.astype(BF16)
    scale_a = (MLA_NOPE + MLA_ROPE) ** -0.5
    nope_w = MLA_HEADS * MLA_NOPE

    def score_a(h):
        qn = q[:, h * MLA_NOPE:(h + 1) * MLA_NOPE].astype(BF16)
        qr = q[:, nope_w + h * MLA_ROPE:nope_w + (h + 1) * MLA_ROPE].astype(BF16)
        return (_dot_nt(qn, kv[:, h * 256:h * 256 + MLA_NOPE]) + _dot_nt(qr, krb)) * scale_a

    _attend_heads(MLA_HEADS, score_a, lambda h: kv[:, h * 256 + MLA_NOPE:(h + 1) * 256], o_ref, 0)
    put_state(2, p_ref[:, P_WK:P_WK + 256])
    put_state(3, p_ref[:, P_WV:P_WV + 256])
    scale_b = WIN_HEAD_DIM ** -0.5

    def score_b(h):
        kvh = h // WIN_GROUP
        qh = p_ref[:, P_WQ + h * 128:P_WQ + (h + 1) * 128].astype(BF16)
        return _dot_nt(qh, p_ref[:, P_WK + kvh * 128:P_WK + (kvh + 1) * 128].astype(BF16)) * scale_b

    def value_b(h):
        kvh = h // WIN_GROUP
        return p_ref[:, P_WV + kvh * 128:P_WV + (kvh + 1) * 128].astype(BF16)

    _attend_heads(WIN_HEADS, score_b, value_b, o_ref, BRANCH_W, lambda h: sink_ref[0:1, h:h + 1])
    put_state(4, p_ref[:, P_NK:P_NK + 1024])
    put_state(5, p_ref[:, P_NV:P_NV + 1024])
    scale_c = NAT_HEAD_DIM ** -0.5

    def score_c(h):
        qh = p_ref[:, P_NQ + h * 128:P_NQ + (h + 1) * 128].astype(BF16)
        return _dot_nt(qh, p_ref[:, P_NK + h * 128:P_NK + (h + 1) * 128].astype(BF16)) * scale_c

    _attend_heads(NAT_HEADS, score_c, lambda h: p_ref[:, P_NV + h * 128:P_NV + (h + 1) * 128].astype(BF16),
                  o_ref, 2 * BRANCH_W)


def _ctx_attention(p, nb, seq, qn, kvn, wq, wkv, sink, prev_states=None):
    full = lambda a: pl.BlockSpec(a.shape, lambda b: (0,) * a.ndim)
    row = lambda w: pl.BlockSpec((seq, w), lambda b: (b, 0))
    stacked = prev_states is not None
    in_specs = [pl.BlockSpec((seq, P_COLS), lambda b: (b, 0)),
                full(qn), full(kvn), full(wq), full(wkv), full(sink)]
    args = [p, qn, kvn, wq, wkv, sink]
    out_specs = [row(3 * BRANCH_W)]
    out_shape = [jax.ShapeDtypeStruct((nb * seq, 3 * BRANCH_W), BF16)]
    for w, heads in zip(_STATE_WIDTHS, _STATE_HEADS):
        if not stacked:
            out_specs.append(row(w))
            out_shape.append(jax.ShapeDtypeStruct((nb * seq, w), F32))
        else:
            tail = (w,) if heads is None else (heads, w // heads)
            out_specs.append(pl.BlockSpec((None, DEPTH, seq) + tail, lambda b, n=len(tail): (b,) + (0,) * (n + 2)))
            out_shape.append(jax.ShapeDtypeStruct((nb, DEPTH, seq) + tail, F32))
    if stacked:
        in_specs += [row(w) for w in _STATE_WIDTHS]
        args += list(prev_states)
    outs = pl.pallas_call(
        functools.partial(_ctx_attn_kernel, stacked=stacked),
        grid=(nb,),
        in_specs=in_specs, out_specs=out_specs, out_shape=out_shape,
        compiler_params=_cparams(("arbitrary",)),
        name="ctx_attention",
    )(*args)
    return outs[0], outs[1:]


MLA_QT = 512
MLA_KC = 256
MLA_DK = MLA_NOPE + MLA_ROPE


def _lat_mla_kernel(p_ref, cckv_ref, ckr_ref, qn_ref, kvn_ref, wq_ref, wkv_ref, tq_ref, tk_ref,
                    o_ref, k_scr, v_scr, *, past, seq):
    qt = pl.program_id(1)

    def put_keys(r0, n, kv, kr):
        krb = kr.astype(BF16)
        for h in range(MLA_HEADS):
            k_scr[h, r0:r0 + n, 0:MLA_NOPE] = kv[:, h * 256:h * 256 + MLA_NOPE].astype(BF16)
            k_scr[h, r0:r0 + n, MLA_NOPE:MLA_DK] = krb
            v_scr[r0:r0 + n, h * MLA_V:(h + 1) * MLA_V] = kv[:, h * 256 + MLA_NOPE:(h + 1) * 256].astype(BF16)

    @pl.when(qt == 0)
    def _():
        for c in range(past // MLA_KC):
            r0 = c * MLA_KC
            kv = _dot(cckv_ref[r0:r0 + MLA_KC, :].astype(BF16), wkv_ref[...])
            put_keys(r0, MLA_KC, kv, ckr_ref[r0:r0 + MLA_KC, :])
        for c in range(seq // MLA_KC):
            r0 = c * MLA_KC
            ckv = _rms(p_ref[r0:r0 + MLA_KC, MLA_Q_RANK:MLA_Q_RANK + MLA_KV_RANK], kvn_ref[...])
            kv = _dot(ckv.astype(BF16), wkv_ref[...])
            krp = p_ref[r0:r0 + MLA_KC, 2 * MLA_Q_RANK:2 * MLA_Q_RANK + LANES]
            kr = _rope(krp, tk_ref[r0:r0 + MLA_KC, :], MLA_ROPE // 4)
            put_keys(past + r0, MLA_KC, kv, kr[:, 0:MLA_ROPE])

    r0 = pl.multiple_of(qt * MLA_QT, MLA_QT)
    qd = p_ref[pl.ds(r0, MLA_QT), 0:MLA_Q_RANK]
    q = _dot(_rms(qd, qn_ref[...]).astype(BF16), wq_ref[...])
    nope_w = MLA_HEADS * MLA_NOPE
    q_rope = _rope(q[:, nope_w:], tq_ref[pl.ds(r0, MLA_QT), :], MLA_ROPE // 4)
    scale = MLA_DK ** -0.5

    def score(h):
        qh = jnp.concatenate([q[:, h * MLA_NOPE:(h + 1) * MLA_NOPE],
                              q_rope[:, h * MLA_ROPE:(h + 1) * MLA_ROPE]], axis=1).astype(BF16)
        return _dot_nt(qh, k_scr[h]) * scale

    _attend_heads(MLA_HEADS, score, lambda h: v_scr[:, h * MLA_V:(h + 1) * MLA_V], o_ref, 0, group=2)


def _lat_mla(p, row_blk0, nb, seq, cache_ckv, cache_kr, layer, qn, kvn, wq, wkv, tq, tk):
    past = cache_ckv.shape[2]
    full = lambda a: pl.BlockSpec(a.shape, lambda b, t: (0,) * a.ndim)
    nqt = seq // MLA_QT
    return pl.pallas_call(
        functools.partial(_lat_mla_kernel, past=past, seq=seq),
        grid=(nb, nqt),
        in_specs=[
            pl.BlockSpec((seq, P_TN), lambda b, t: (row_blk0 + b, P_QD // P_TN)),
            pl.BlockSpec((None, None, past, MLA_KV_RANK), lambda b, t: (b, layer, 0, 0)),
            pl.BlockSpec((None, None, past, MLA_ROPE), lambda b, t: (b, layer, 0, 0)),
            full(qn), full(kvn), full(wq), full(wkv), full(tq), full(tk),
        ],
        out_specs=pl.BlockSpec((MLA_QT, BRANCH_W), lambda b, t: (b * nqt + t, 0)),
        out_shape=jax.ShapeDtypeStruct((nb * seq, BRANCH_W), BF16),
        scratch_shapes=[pltpu.VMEM((MLA_HEADS, past + seq, MLA_DK), BF16),
                        pltpu.VMEM((past + seq, MLA_HEADS * MLA_V), BF16)],
        compiler_params=_cparams(("arbitrary", "arbitrary")),
        name="lat_mla",
    )(p, cache_ckv, cache_kr, qn, kvn, wq, wkv, tq, tk)


WIN_QB = 128


def _lat_win_kernel(q_ref, k0_ref, k1_ref, k2_ref, v0_ref, v1_ref, v2_ref, ck_ref, cv_ref,
                    t0_ref, t1_ref, t2_ref, sink_ref, o_ref, *, past, seq):
    qb = pl.program_id(1)
    scale = WIN_HEAD_DIM ** -0.5
    n_loc = 3 * WIN_QB
    rows = WIN_GROUP * WIN_QB
    qpos = qb * WIN_QB + lax.broadcasted_iota(jnp.int32, (rows, n_loc), 0) % WIN_QB
    kpos = (qb - 1) * WIN_QB + lax.broadcasted_iota(jnp.int32, (rows, n_loc), 1)
    valid = (kpos >= 0) & (kpos < seq) & (jnp.abs(qpos - kpos) <= WINDOW)
    t1 = t1_ref[...]
    q_heads = lambda kvh: range(kvh * WIN_GROUP, (kvh + 1) * WIN_GROUP)

    def score(kvh):
        cs = slice(kvh * 128, (kvh + 1) * 128)
        keys = jnp.concatenate([
            ck_ref[:, cs],
            _rope(k0_ref[:, cs], t0_ref[...], 32),
            _rope(k1_ref[:, cs], t1, 32),
            _rope(k2_ref[:, cs], t2_ref[...], 32)], axis=0).astype(BF16)
        q = jnp.concatenate([_rope(q_ref[:, h * 128:(h + 1) * 128], t1, 32) for h in q_heads(kvh)],
                            axis=0).astype(BF16)
        s = _dot_nt(q, keys) * scale
        return jnp.concatenate([s[:, :past], jnp.where(valid, s[:, past:], NEG_INF)], axis=1)

    def value(kvh):
        cs = slice(kvh * 128, (kvh + 1) * 128)
        return jnp.concatenate([cv_ref[:, cs], v0_ref[:, cs], v1_ref[:, cs], v2_ref[:, cs]], axis=0).astype(BF16)

    def sink(kvh):
        return jnp.concatenate([jnp.broadcast_to(sink_ref[0:1, h:h + 1], (WIN_QB, 1)) for h in q_heads(kvh)],
                               axis=0)

    def emit(kvh, out):
        for g, h in enumerate(q_heads(kvh)):
            o_ref[:, h * 128:(h + 1) * 128] = out[g * WIN_QB:(g + 1) * WIN_QB, :].astype(BF16)

    _attend_heads(WIN_KV_HEADS, score, value, o_ref, 0, sink=sink, emit=emit)


def _lat_win(p, row0, nb, seq, cache_k, cache_v, layer, tab, sink):
    past = cache_k.shape[2]
    nqb = seq // WIN_QB
    rb0 = row0 // WIN_QB

    def kblk(off, col_blk):
        return pl.BlockSpec(
            (WIN_QB, 256),
            lambda b, t: (rb0 + b * nqb + jnp.clip(t + off, 0, nqb - 1), col_blk))

    def tblk(off):
        return pl.BlockSpec((WIN_QB, 3 * 128), lambda b, t: (jnp.clip(t + off, 0, nqb - 1), 0))

    cache = pl.BlockSpec((None, None, past, 256), lambda b, t: (b, layer, 0, 0))
    return pl.pallas_call(
        functools.partial(_lat_win_kernel, past=past, seq=seq),
        grid=(nb, nqb),
        in_specs=[
            pl.BlockSpec((WIN_QB, 1024), lambda b, t: (rb0 + b * nqb + t, P_WQ // 1024)),
            kblk(-1, P_WK // 256), kblk(0, P_WK // 256), kblk(1, P_WK // 256),
            kblk(-1, P_WV // 256), kblk(0, P_WV // 256), kblk(1, P_WV // 256),
            cache, cache,
            tblk(-1), tblk(0), tblk(1),
            pl.BlockSpec(sink.shape, lambda b, t: (0, 0)),
        ],
        out_specs=pl.BlockSpec((WIN_QB, BRANCH_W), lambda b, t: (b * nqb + t, 0)),
        out_shape=jax.ShapeDtypeStruct((nb * seq, BRANCH_W), BF16),
        compiler_params=_cparams(("arbitrary", "arbitrary")),
        name="lat_window",
    )(p, p, p, p, p, p, p, cache_k, cache_v, tab, tab, tab, sink)


def _lat_nat_kernel(q_ref, k0_ref, k1_ref, k2_ref, v0_ref, v1_ref, v2_ref, ck_ref, cv_ref, bias_ref, o_ref,
                    *, grid_rows):
    g = pl.program_id(0)
    scale = NAT_HEAD_DIM ** -0.5
    past = ck_ref.shape[0]
    win_rows = min(NAT_ROWS, grid_rows)
    lane = lax.broadcasted_iota(jnp.int32, (GRID_W, 2 * GRID_W), 1)

    def rows_of(h, refs):
        cs = slice(h * 128, (h + 1) * 128)
        return jnp.concatenate([r[:, cs] for r in refs], axis=0).astype(BF16)

    def pair_plan(rq, rkp):
        r = NAT_QROWS * g + rq
        kr0 = NAT_QROWS * _nat_key_start(g) + 2 * rkp
        rs = jnp.clip(r - win_rows // 2, 0, grid_rows - win_rows)
        in_win = lambda kr: ((kr >= rs) & (kr < rs + win_rows)).astype(jnp.int32)
        ok = jnp.where(lane < GRID_W, in_win(kr0), in_win(kr0 + 1)) > 0
        return jnp.clip(kr0 - r + NAT_ROWS - 1, 0, 2 * NAT_ROWS - 1), ok

    plans = [[pair_plan(rq, rkp) for rkp in range(NAT_KROWS // 2)] for rq in range(NAT_QROWS)]

    def bias(h):
        return jnp.concatenate(
            [jnp.concatenate([jnp.where(ok, bias_ref[h, a], NEG_INF) for a, ok in row], axis=1) for row in plans],
            axis=0)

    def score(h):
        q = q_ref[:, h * 128:(h + 1) * 128].astype(BF16)
        s = _dot_nt(q, rows_of(h, (ck_ref, k0_ref, k1_ref, k2_ref))) * scale
        return jnp.concatenate([s[:, :past], s[:, past:] + bias(h)], axis=1)

    _attend_heads(NAT_HEADS, score, lambda h: rows_of(h, (cv_ref, v0_ref, v1_ref, v2_ref)), o_ref, 0, group=4)


def _nat_key_start(g):
    return g // 2


def _lat_nat(p, row0, nb, seq, cache_k, cache_v, layer, bias):
    past = cache_k.shape[2]
    qrows = NAT_QROWS * GRID_W
    ng = seq // qrows
    rb0 = row0 // qrows

    def kblk(off, col_blk):
        return pl.BlockSpec((qrows, 1024), lambda g, b: (rb0 + b * ng + _nat_key_start(g) + off, col_blk))

    cache = pl.BlockSpec((None, None, past, 1024), lambda g, b: (b, layer, 0, 0))
    return pl.pallas_call(
        functools.partial(_lat_nat_kernel, grid_rows=seq // GRID_W),
        grid=(ng, nb),
        in_specs=[
            pl.BlockSpec((qrows, 1024), lambda g, b: (rb0 + b * ng + g, P_NQ // 1024)),
            kblk(0, P_NK // 1024), kblk(1, P_NK // 1024), kblk(2, P_NK // 1024),
            kblk(0, P_NV // 1024), kblk(1, P_NV // 1024), kblk(2, P_NV // 1024),
            cache, cache,
            pl.BlockSpec(bias.shape, lambda g, b: (0, 0, 0, 0)),
        ],
        out_specs=pl.BlockSpec((qrows, BRANCH_W), lambda g, b: (b * ng + g, 0)),
        out_shape=jax.ShapeDtypeStruct((nb * seq, BRANCH_W), BF16),
        compiler_params=_cparams(("arbitrary", "arbitrary")),
        name="lat_neighbourhood",
    )(p, p, p, p, p, p, p, cache_k, cache_v, bias)


def _nat_bias_blocks(rpb):
    n_dr, n_dc = 2 * NAT_ROWS - 1, 2 * NAT_COLS - 1
    cc = np.arange(GRID_W)
    dc = cc[None, :] - cc[:, None] + NAT_COLS - 1
    cs = np.clip(cc - NAT_COLS // 2, 0, GRID_W - NAT_COLS)[:, None]
    col_ok = (cc[None, :] >= cs) & (cc[None, :] < cs + NAT_COLS)
    onehot = ((dc[None] == np.arange(n_dc)[:, None, None]) & col_ok[None]).astype(np.float32)
    toep = jnp.einsum('had,dck->hack', rpb.astype(F32), jnp.asarray(onehot), precision=lax.Precision.HIGHEST)
    toep = jnp.where(jnp.asarray(col_ok), toep, NEG_INF)
    fill = jnp.full((NAT_HEADS, 2 * NAT_ROWS + 1 - n_dr, GRID_W, GRID_W), NEG_INF, F32)
    toep = jnp.concatenate([toep, fill], axis=1)
    return jnp.concatenate([toep[:, :-1], toep[:, 1:]], axis=-1)


def _merge_kernel(hn_ref, octx_ref, oa_ref, ob_ref, oc_ref, wga_ref, wgb_ref, wgc_ref,
                  wba_ref, wbb_ref, wbc_ref, m_ref, *, ctx_tiles):
    i = pl.program_id(0)
    weights = ((wga_ref, wba_ref), (wgb_ref, wbb_ref), (wgc_ref, wbc_ref))

    def merged(branch):
        hn = hn_ref[...]
        acc = None
        for k, (wg, wb) in enumerate(weights):
            term = jax.nn.sigmoid(_dot_nt(hn, wg[...])) * _dot(branch(k), wb[...])
            acc = term if acc is None else acc + term
        m_ref[...] = acc.astype(BF16)

    @pl.when(i < ctx_tiles)
    def _():
        merged(lambda k: octx_ref[:, k * BRANCH_W:(k + 1) * BRANCH_W])

    @pl.when(i >= ctx_tiles)
    def _():
        lat = (oa_ref, ob_ref, oc_ref)
        merged(lambda k: lat[k][...])


def _merge(hn, o_ctx, o_a, o_b, o_c, w_gate, w_br, layer):
    rows, d = hn.shape
    tm, tn = ROW_TILE, 256
    nj = d // tn
    ctx_tiles = o_ctx.shape[0] // tm
    gate = lambda k: pl.BlockSpec((None, tn, d), lambda i, j: (layer, k * nj + j, 0))
    br = lambda k: pl.BlockSpec((None, None, BRANCH_W, tn), lambda i, j: (layer, k, 0, j))
    lat = pl.BlockSpec((tm, BRANCH_W), lambda i, j: (jnp.maximum(i - ctx_tiles, 0), 0))
    return pl.pallas_call(
        functools.partial(_merge_kernel, ctx_tiles=ctx_tiles),
        grid=(rows // tm, nj),
        in_specs=[pl.BlockSpec((tm, d), lambda i, j: (i, 0)),
                  pl.BlockSpec((tm, 3 * BRANCH_W), lambda i, j: (jnp.minimum(i, ctx_tiles - 1), 0)),
                  lat, lat, lat,
                  gate(0), gate(1), gate(2), br(0), br(1), br(2)],
        out_specs=pl.BlockSpec((tm, tn), lambda i, j: (i, j)),
        out_shape=jax.ShapeDtypeStruct((rows, d), BF16),
        compiler_params=_cparams(("arbitrary", "arbitrary")),
        name="merge",
    )(hn, o_ctx, o_a, o_b, o_c, w_gate, w_gate, w_gate, w_br, w_br, w_br)


ROUTE_IDX = 8
ROUTE_W = 10


def _out_proj_kernel(m_ref, w_ref, mod_ref, g_ref, *rest, routed, ctx_tiles):
    nh = 1 if ctx_tiles is None else 2
    h_refs, rest = rest[:nh], rest[nh:]
    if routed:
        wr_ref, h1_ref, hn2_ref, route_ref = rest
    else:
        h1_ref, hn2_ref = rest
    h1 = _row_group_value(h_refs, ctx_tiles) + mod_ref[2:3, :] * _dot(m_ref[...], w_ref[...])
    h1_ref[...] = h1
    hn2 = _rms(h1, g_ref[...]) * (1.0 + mod_ref[4:5, :]) + mod_ref[3:4, :]
    hn2_ref[...] = hn2.astype(hn2_ref.dtype)
    if routed:
        logits = _dot(hn2.astype(BF16), wr_ref[...])
        lane = lax.broadcasted_iota(jnp.int32, logits.shape, 1).astype(F32)
        lg = jnp.where(lane < N_EXPERTS, logits, -jnp.inf)
        m1 = jnp.max(lg, axis=-1, keepdims=True)
        i1 = jnp.min(jnp.where(lg == m1, lane, float(LANES)), axis=-1, keepdims=True)
        lg2 = jnp.where(lane == i1, -jnp.inf, lg)
        m2 = jnp.max(lg2, axis=-1, keepdims=True)
        i2 = jnp.min(jnp.where(lg2 == m2, lane, float(LANES)), axis=-1, keepdims=True)
        e2 = jnp.exp(m2 - m1)
        w1 = 1.0 / (1.0 + e2)
        w2 = e2 / (1.0 + e2)
        route = (jnp.where(lane == ROUTE_IDX, i1, 0.0)
                 + jnp.where(lane == ROUTE_IDX + 1, i2, 0.0)
                 + jnp.where(lane == ROUTE_W, w1, 0.0)
                 + jnp.where(lane == ROUTE_W + 1, w2, 0.0))
        route_ref[...] = route


def _out_proj(m, w_out, h, mod, g, layer, cond_of_tile, w_router=None):
    rows, d = m.shape
    tm = 512
    sub = ROW_TILE // tm
    routed = w_router is not None
    h_args, h_specs, ctx_tiles = _split_rows_specs(h, tm)
    in_specs = [
        pl.BlockSpec((tm, d), lambda i: (i, 0)),
        pl.BlockSpec((None, d, d), lambda i: (layer, 0, 0)),
        pl.BlockSpec((None, 6, d), lambda i: (cond_of_tile(i // sub), 0, 0)),
        pl.BlockSpec((1, d), lambda i: (0, 0)),
    ] + h_specs
    out_specs = [pl.BlockSpec((tm, d), lambda i: (i, 0)), pl.BlockSpec((tm, d), lambda i: (i, 0))]
    out_shape = [jax.ShapeDtypeStruct((rows, d), F32), jax.ShapeDtypeStruct((rows, d), BF16)]
    args = [m, w_out, mod, g] + h_args
    if routed:
        out_shape[1] = jax.ShapeDtypeStruct((rows, d), F32)
        in_specs.append(pl.BlockSpec((d, LANES), lambda i: (0, 0)))
        out_specs.append(pl.BlockSpec((tm, LANES), lambda i: (i, 0)))
        out_shape.append(jax.ShapeDtypeStruct((rows, LANES), F32))
        args.append(w_router)
    return pl.pallas_call(
        functools.partial(_out_proj_kernel, routed=routed, ctx_tiles=ctx_tiles),
        grid=(rows // tm,),
        in_specs=in_specs, out_specs=out_specs, out_shape=out_shape,
        compiler_params=_cparams(("arbitrary",)),
        name="out_proj",
    )(*args)


def _swiglu_chunk(x, wg_ref, wu_ref, wd_ref):
    g = _dot(x, wg_ref[...].astype(BF16))
    u = _dot(x, wu_ref[...].astype(BF16))
    a = (g * jax.nn.sigmoid(g) * u).astype(BF16)
    return _dot(a, wd_ref[...].astype(BF16))


FFN_PREFETCH_STEP = 5


def _ffn_kernel(x_ref, wg_ref, wu_ref, wd_ref, h_ref, mod_ref, o_ref, acc_ref, h_sem, wb_sem):
    i = pl.program_id(0)
    f = pl.program_id(1)
    nt = pl.num_programs(0)
    last_f = pl.num_programs(1) - 1
    tm = acc_ref.shape[1]
    slot = i % 2

    def rows_of(tile):
        return pl.ds(pl.multiple_of(tile * tm, tm), tm)

    def h_load(tile, s):
        return pltpu.make_async_copy(h_ref.at[rows_of(tile), :], acc_ref.at[s], h_sem.at[s])

    def writeback(tile, s):
        return pltpu.make_async_copy(acc_ref.at[s], o_ref.at[rows_of(tile), :], wb_sem.at[s])

    @pl.when((f == 0) & (i == 0))
    def _():
        h_load(0, 0).start()

    @pl.when(f == 0)
    def _():
        h_load(i, slot).wait()

    @pl.when((f == FFN_PREFETCH_STEP) & (i >= 1))
    def _():
        writeback(i - 1, 1 - slot).wait()

    @pl.when((f == FFN_PREFETCH_STEP) & (i + 1 < nt))
    def _():
        h_load(i + 1, 1 - slot).start()

    acc_ref[slot] += mod_ref[5:6, :] * _swiglu_chunk(x_ref[...], wg_ref, wu_ref, wd_ref)

    @pl.when(f == last_f)
    def _():
        writeback(i, slot).start()

    @pl.when((f == last_f) & (i == nt - 1))
    def _():
        writeback(i, slot).wait()


def _ffn(x, wg, wu, wd, h, mod, cond_of_tile):
    rows, d = h.shape
    ff = wg.shape[1]
    tm, tf = FFN_TM, FFN_TF
    sub = ROW_TILE // tm
    assert ff // tf > FFN_PREFETCH_STEP
    return pl.pallas_call(
        _ffn_kernel,
        grid=(rows // tm, ff // tf),
        in_specs=[
            pl.BlockSpec((tm, d), lambda i, f: (i, 0)),
            pl.BlockSpec((d, tf), lambda i, f: (0, f)),
            pl.BlockSpec((d, tf), lambda i, f: (0, f)),
            pl.BlockSpec((tf, d), lambda i, f: (f, 0)),
            pl.BlockSpec(memory_space=pl.ANY),
            pl.BlockSpec((None, 6, d), lambda i, f: (cond_of_tile(i // sub), 0, 0)),
        ],
        out_specs=pl.BlockSpec(memory_space=pl.ANY),
        out_shape=jax.ShapeDtypeStruct((rows, d), F32),
        scratch_shapes=[pltpu.VMEM((2, tm, d), F32), pltpu.SemaphoreType.DMA((2,)),
                        pltpu.SemaphoreType.DMA((2,))],
        compiler_params=_cparams(("arbitrary", "arbitrary")),
        name="ffn_dense",
    )(x, wg, wu, wd, h, mod)


def _start_row_gather(src_ref, idx_ref, dst_ref, sem, n):
    def issue(k, carry):
        for p in range(2):
            r = 2 * k + p
            pltpu.make_async_copy(src_ref.at[pl.ds(idx_ref[0, r], 1), :], dst_ref.at[pl.ds(r, 1), :],
                                  sem).start(priority=p)
        return carry

    lax.fori_loop(0, n // 2, issue, 0, unroll=4)


def _wait_row_gather(src_ref, dst_ref, sem, n):
    pltpu.make_async_copy(src_ref.at[pl.ds(0, n), :], dst_ref, sem).wait()


def _gather_kernel(tr_ref, idx_ref, idx_next_ref, src_ref, o_ref, buf, sem):
    i = pl.program_id(0)
    n = o_ref.shape[0]
    per_tile = MOE_TM // n
    slot = i % 2

    def used(step):
        return tr_ref[step // per_tile] > (step % per_tile) * n

    @pl.when((i == 0) & used(0))
    def _():
        _start_row_gather(src_ref, idx_ref, buf.at[0], sem.at[0], n)

    nxt = jnp.minimum(i + 1, pl.num_programs(0) - 1)

    @pl.when((i + 1 < pl.num_programs(0)) & used(nxt))
    def _():
        _start_row_gather(src_ref, idx_next_ref, buf.at[1 - slot], sem.at[1 - slot], n)

    @pl.when(used(i))
    def _():
        _wait_row_gather(src_ref, buf.at[slot], sem.at[slot], n)
        o_ref[...] = buf[slot].astype(o_ref.dtype)

    @pl.when(jnp.logical_not(used(i)))
    def _():
        o_ref[...] = jnp.zeros_like(o_ref)


def _gather_rows(src, idx, tile_rows, out_dtype):
    n = idx.shape[0]
    d = src.shape[1]
    tg = GATHER_ROWS
    steps = n // tg
    idx3 = idx.reshape(steps, 1, tg)
    grid_spec = pltpu.PrefetchScalarGridSpec(
        num_scalar_prefetch=1,
        grid=(steps,),
        in_specs=[pl.BlockSpec((None, 1, tg), lambda i, tr: (i, 0, 0), memory_space=pltpu.SMEM),
                  pl.BlockSpec((None, 1, tg), lambda i, tr: (jnp.minimum(i + 1, steps - 1), 0, 0),
                               memory_space=pltpu.SMEM),
                  pl.BlockSpec(memory_space=pl.ANY)],
        out_specs=pl.BlockSpec((tg, d), lambda i, tr: (i, 0)),
        scratch_shapes=[pltpu.VMEM((2, tg, d), src.dtype), pltpu.SemaphoreType.DMA((2,))],
    )
    return pl.pallas_call(
        _gather_kernel,
        grid_spec=grid_spec,
        out_shape=jax.ShapeDtypeStruct((n, d), out_dtype),
        compiler_params=_cparams(("arbitrary",)),
        name="moe_gather",
    )(tile_rows, idx3, idx3, src)


def _moe_ffn_kernel(te_ref, tv_ref, x_ref, wg_ref, wu_ref, wd_ref, o_ref, acc_ref, sem):
    i = pl.program_id(0)
    f = pl.program_id(1)
    last_f = pl.num_programs(1) - 1
    rows_used = tv_ref[i]

    def writeback(tile, s):
        rows = pl.ds(pl.multiple_of(tile * MOE_TM + s * MOE_SLAB, MOE_SLAB), MOE_SLAB)
        return pltpu.make_async_copy(acc_ref.at[pl.ds(s * MOE_SLAB, MOE_SLAB), :], o_ref.at[rows, :], sem.at[s])

    for s in range(MOE_TM // MOE_SLAB):
        rs = slice(s * MOE_SLAB, (s + 1) * MOE_SLAB)
        used = rows_used > s * MOE_SLAB

        @pl.when((f == 0) & (i > 0))
        def _():
            writeback(i - 1, s).wait()

        @pl.when(f == 0)
        def _():
            acc_ref[rs, :] = jnp.zeros((MOE_SLAB, acc_ref.shape[1]), acc_ref.dtype)

        @pl.when(used)
        def _():
            acc_ref[rs, :] += _swiglu_chunk(x_ref[rs, :], wg_ref, wu_ref, wd_ref)

        @pl.when(f == last_f)
        def _():
            writeback(i, s).start()

    @pl.when((f == last_f) & (i == pl.num_programs(0) - 1))
    def _():
        for s in range(MOE_TM // MOE_SLAB):
            writeback(i, s).wait()


def _moe_ffn(xs, wg, wu, wd, tile_expert, tile_valid):
    n, d = xs.shape
    ff = wg.shape[2]
    tm, tf = MOE_TM, MOE_TF
    nf = ff // tf

    def fidx(i, f, tv):
        return jnp.where(tv[i] > 0, f, nf - 1)

    grid_spec = pltpu.PrefetchScalarGridSpec(
        num_scalar_prefetch=2,
        grid=(n // tm, nf),
        in_specs=[
            pl.BlockSpec((tm, d), lambda i, f, te, tv: (i, 0)),
            pl.BlockSpec((None, d, tf), lambda i, f, te, tv: (te[i], 0, fidx(i, f, tv))),
            pl.BlockSpec((None, d, tf), lambda i, f, te, tv: (te[i], 0, fidx(i, f, tv))),
            pl.BlockSpec((None, tf, d), lambda i, f, te, tv: (te[i], fidx(i, f, tv), 0)),
        ],
        out_specs=pl.BlockSpec(memory_space=pl.ANY),
        scratch_shapes=[pltpu.VMEM((tm, d), F32), pltpu.SemaphoreType.DMA((tm // MOE_SLAB,))],
    )
    return pl.pallas_call(
        _moe_ffn_kernel,
        grid_spec=grid_spec,
        out_shape=jax.ShapeDtypeStruct((n, d), F32),
        compiler_params=_cparams(("arbitrary", "arbitrary")),
        name="moe_ffn",
    )(tile_expert, tile_valid, xs, wg, wu, wd)


def _combine_kernel(p0_ref, p1_ref, p0_next_ref, p1_next_ref, ys_ref, route_ref, h_ref, mod_ref, fn_ref,
                    octx_ref, olat_ref, buf0, buf1, sem, *, ctx_steps):
    i = pl.program_id(0)
    n = octx_ref.shape[0]
    slot = i % 2

    def start(q0_ref, q1_ref, s):
        _start_row_gather(ys_ref, q0_ref, buf0.at[s], sem.at[s, 0], n)
        _start_row_gather(ys_ref, q1_ref, buf1.at[s], sem.at[s, 1], n)

    @pl.when(i == 0)
    def _():
        start(p0_ref, p1_ref, 0)

    @pl.when(i + 1 < pl.num_programs(0))
    def _():
        start(p0_next_ref, p1_next_ref, 1 - slot)

    _wait_row_gather(ys_ref, buf0.at[slot], sem.at[slot, 0], n)
    _wait_row_gather(ys_ref, buf1.at[slot], sem.at[slot, 1], n)
    w0 = route_ref[:, ROUTE_W:ROUTE_W + 1]
    w1 = route_ref[:, ROUTE_W + 1:ROUTE_W + 2]
    y = w0 * buf0[slot] + w1 * buf1[slot]
    h2 = h_ref[...] + mod_ref[5:6, :] * y
    out = _rms(h2, fn_ref[...])

    @pl.when(i < ctx_steps)
    def _():
        octx_ref[...] = out

    @pl.when(i >= ctx_steps)
    def _():
        olat_ref[...] = out


def _moe_combine(ys, pos0, pos1, route, h, mod, final_norm, cond_of_tile, rows_ctx):
    rows, d = h.shape
    tc = GATHER_ROWS
    sub = ROW_TILE // tc
    ctx_steps = rows_ctx // tc
    steps = rows // tc
    smem = lambda: pl.BlockSpec((None, 1, tc), lambda i: (i, 0, 0), memory_space=pltpu.SMEM)
    smem_next = lambda: pl.BlockSpec((None, 1, tc), lambda i: (jnp.minimum(i + 1, steps - 1), 0, 0),
                                     memory_space=pltpu.SMEM)
    pos0, pos1 = pos0.reshape(steps, 1, tc), pos1.reshape(steps, 1, tc)
    return pl.pallas_call(
        functools.partial(_combine_kernel, ctx_steps=ctx_steps),
        grid=(steps,),
        in_specs=[smem(), smem(), smem_next(), smem_next(),
                  pl.BlockSpec(memory_space=pl.ANY),
                  pl.BlockSpec((tc, LANES), lambda i: (i, 0)),
                  pl.BlockSpec((tc, d), lambda i: (i, 0)),
                  pl.BlockSpec((None, 6, d), lambda i: (cond_of_tile(i // sub), 0, 0)),
                  pl.BlockSpec((1, d), lambda i: (0, 0))],
        out_specs=[pl.BlockSpec((tc, d), lambda i: (jnp.minimum(i, ctx_steps - 1), 0)),
                   pl.BlockSpec((tc, d), lambda i: (jnp.maximum(i - ctx_steps, 0), 0))],
        out_shape=[jax.ShapeDtypeStruct((rows_ctx, d), F32), jax.ShapeDtypeStruct((rows - rows_ctx, d), F32)],
        scratch_shapes=[pltpu.VMEM((2, tc, d), F32), pltpu.VMEM((2, tc, d), F32),
                        pltpu.SemaphoreType.DMA((2, 2))],
        compiler_params=_cparams(("arbitrary",)),
        name="moe_combine",
    )(pos0, pos1, pos0, pos1, ys, route, h, mod, final_norm)


def _dispatch_plan(route, tm):
    rows = route.shape[0]
    ids = route[:, ROUTE_IDX:ROUTE_IDX + TOP_K].astype(jnp.int32)
    flat = ids.reshape(-1)
    onehot = (flat[:, None] == jnp.arange(N_EXPERTS)[None, :]).astype(jnp.int32)
    rank = jnp.sum((jnp.cumsum(onehot, axis=0) - onehot) * onehot, axis=1)
    counts = jnp.sum(onehot, axis=0)
    tiles = (counts + tm - 1) // tm
    tile_end = jnp.cumsum(tiles)
    start = (tile_end - tiles) * tm
    slot = start[flat] + rank
    n_tiles = (rows * TOP_K) // tm + N_EXPERTS
    n_slots = n_tiles * tm
    slot_token = (jnp.arange(n_slots, dtype=jnp.int32) % rows).at[slot].set(
        jnp.arange(rows * TOP_K, dtype=jnp.int32) // TOP_K, unique_indices=True)
    t = jnp.arange(n_tiles)
    tile_expert = jnp.minimum(jnp.sum((t[:, None] >= tile_end[None, :]).astype(jnp.int32), axis=1), N_EXPERTS - 1)
    tile_in_expert = t - (tile_end - tiles)[tile_expert]
    tile_rows = jnp.where(t < tile_end[-1], jnp.clip(counts[tile_expert] - tile_in_expert * tm, 0, tm), 0)
    last_expert = tile_expert[jnp.maximum(tile_end[-1] - 1, 0)]
    tile_expert = jnp.where(tile_rows > 0, tile_expert, last_expert).astype(jnp.int32)
    pos = slot.reshape(rows, TOP_K)
    return slot_token, tile_expert, tile_rows.astype(jnp.int32), pos[:, 0], pos[:, 1]


def _rope_table(seq, n):
    quarter = n // 4
    t = np.arange(seq)
    inv = jnp.power(ROPE_BASE, -jnp.arange(quarter, dtype=F32) * (2.0 / (n // 2)))
    ang_r = jnp.asarray(t // GRID_W, F32)[:, None] * inv[None, :]
    ang_c = jnp.asarray(t % GRID_W, F32)[:, None] * inv[None, :]
    zero = jnp.zeros((seq, quarter), F32)
    cos = jnp.concatenate([jnp.cos(ang_r)] * 2 + [jnp.cos(ang_c)] * 2, axis=1)
    up = jnp.concatenate([-jnp.sin(ang_r), zero, -jnp.sin(ang_c), zero], axis=1)
    dn = jnp.concatenate([zero, jnp.sin(ang_r), zero, jnp.sin(ang_c)], axis=1)
    return cos, up, dn


def _tile_cols(parts, reps, pad_to=None):
    out = [jnp.tile(p, (1, reps)) for p in parts]
    if pad_to is not None:
        out = [jnp.pad(p, ((0, 0), (0, pad_to - p.shape[1]))) for p in out]
    return jnp.concatenate(out, axis=1)


def kernel(x_prompt, x_sample, cache_mla_ckv, cache_mla_krope, cache_win_k, cache_win_v, cache_nat_k, cache_nat_v, c, c_ctx, w_mod, b_mod, norm1, norm2, w_in, mla_q_norm, mla_kv_norm, w_mla_q_up, w_mla_kv_up, win_sink, nat_rpb, w_br_mla, w_br_win, w_br_nat, w_out, w_ff_gate, w_ff_up, w_ff_down, w_router, w_ex_gate, w_ex_up, w_ex_down, final_norm):
    nb_ctx, seq_ctx, d = x_prompt.shape
    nb_lat, seq_lat, _ = x_sample.shape
    past = cache_mla_ckv.shape[2]
    rows_ctx = nb_ctx * seq_ctx
    rows_lat = nb_lat * seq_lat
    assert d == D_MODEL and seq_lat == ROW_TILE and rows_ctx % ROW_TILE == 0
    assert seq_lat // GRID_W == 4 * NAT_QROWS
    assert w_mod.shape[0] == DEPTH == 2
    ctx_tiles = rows_ctx // ROW_TILE

    def cond_of_tile(i):
        return jnp.where(i < ctx_tiles, 0, i - ctx_tiles + 1)

    n_cond = 16
    cond = jnp.zeros((n_cond, d), F32).at[0].set(c_ctx).at[1:1 + nb_lat].set(c)
    mod_all = _modulation(cond, w_mod, b_mod).reshape(DEPTH, n_cond, 6, d)

    w_p, w_gate = _w_in_prep(jnp.swapaxes(w_in, 1, 2))
    wq_up = w_mla_q_up.reshape(DEPTH, MLA_Q_RANK, MLA_HEADS, MLA_NOPE + MLA_ROPE)
    wq_up = jnp.concatenate([wq_up[..., :MLA_NOPE].reshape(DEPTH, MLA_Q_RANK, -1),
                             wq_up[..., MLA_NOPE:].reshape(DEPTH, MLA_Q_RANK, -1)], axis=2).astype(BF16)
    wkv_up = w_mla_kv_up.astype(BF16)
    w_br = jnp.stack([w_br_mla, w_br_win, w_br_nat], axis=1).astype(BF16)
    w_out_b = w_out.astype(BF16)
    w_router_p = jnp.pad(w_router, ((0, 0), (0, 0), (0, LANES - N_EXPERTS))).astype(BF16)

    t128 = jnp.concatenate(_rope_table(seq_lat, 128), axis=1)
    t64 = _rope_table(seq_lat, MLA_ROPE)
    t_q = _tile_cols(t64, MLA_HEADS)
    t_k = _tile_cols(t64, 1, pad_to=LANES)

    ck_win = cache_win_k.reshape(nb_lat, DEPTH, past, 256)
    cv_win = cache_win_v.reshape(nb_lat, DEPTH, past, 256)
    ck_nat = cache_nat_k.reshape(nb_lat, DEPTH, past, 1024)
    cv_nat = cache_nat_v.reshape(nb_lat, DEPTH, past, 1024)


    h = (x_prompt.reshape(rows_ctx, d), x_sample.reshape(rows_lat, d))
    states = None
    for l in range(DEPTH):
        mod = mod_all[l]
        qn = mla_q_norm[l].reshape(1, -1)
        kvn = mla_kv_norm[l].reshape(1, -1)
        sink = win_sink[l].reshape(1, -1)
        p, hn = _mixer_in(h, mod, norm1[l].reshape(1, d), w_p, l, cond_of_tile)
        o_ctx, states = _ctx_attention(p, nb_ctx, seq_ctx, qn, kvn, wq_up[l], wkv_up[l], sink,
                                       prev_states=states)
        o_a = _lat_mla(p, rows_ctx // seq_lat, nb_lat, seq_lat, cache_mla_ckv, cache_mla_krope, l,
                       qn, kvn, wq_up[l], wkv_up[l], t_q, t_k)
        o_b = _lat_win(p, rows_ctx, nb_lat, seq_lat, ck_win, cv_win, l, t128, sink)
        o_c = _lat_nat(p, rows_ctx, nb_lat, seq_lat, ck_nat, cv_nat, l, _nat_bias_blocks(nat_rpb[l]))
        m = _merge(hn, o_ctx, o_a, o_b, o_c, w_gate, w_br, l)
        if l % 2 == 0:
            h1, hn2 = _out_proj(m, w_out_b, h, mod, norm2[l].reshape(1, d), l, cond_of_tile)
            h = _ffn(hn2, w_ff_gate[l // 2].astype(BF16), w_ff_up[l // 2].astype(BF16),
                     w_ff_down[l // 2].astype(BF16), h1, mod, cond_of_tile)
        else:
            h1, hn2, route = _out_proj(m, w_out_b, h, mod, norm2[l].reshape(1, d), l, cond_of_tile,
                                       w_router=w_router_p[l // 2])
            slot_token, tile_expert, tile_valid, pos0, pos1 = _dispatch_plan(route, MOE_TM)
            xs = _gather_rows(hn2, slot_token, tile_valid, BF16)
            ys = _moe_ffn(xs, w_ex_gate[l // 2], w_ex_up[l // 2], w_ex_down[l // 2], tile_expert, tile_valid)
            y_ctx, y_lat = _moe_combine(ys, pos0, pos1, route, h1, mod, final_norm.reshape(1, d),
                                        cond_of_tile, rows_ctx)

    y_prompt = y_ctx.reshape(nb_ctx, seq_ctx, d)
    y_sample = y_lat.reshape(nb_lat, seq_lat, d)
    return (y_prompt, y_sample) + tuple(states)
```
